```python
import math, functools
import jax, jax.numpy as jnp
from jax import lax
import numpy as np

D_MODEL = 1024
BATCH = 2
SEQ = 16384
DEPTH = 4

N_MIXERS = 3
HEAD_DIM = 64
N_HEADS = D_MODEL // HEAD_DIM
REL_BUCKETS = 32
REL_MAX_DIST = 2048
Q_BLOCK = 128
LN_EPS = 1e-5
GLA_HEADS = 4
GLA_DK = D_MODEL // 2 // GLA_HEADS
GLA_DV = D_MODEL // GLA_HEADS
GLA_GATE_RANK = 16
GLA_TAU = 16.0
GLA_CHUNK = 64
GLA_IN = 2 * GLA_HEADS * GLA_DK + 2 * GLA_HEADS * GLA_DV + GLA_GATE_RANK
NSA_KV_GROUPS = 4
NSA_Q_PER_KV = N_HEADS // NSA_KV_GROUPS
CMP_LEN = 32
CMP_STRIDE = 16
CMP_HIDDEN = 256
SEL_LEN = 64
SEL_TOP = 16
WIN_LEN = 512
NSA_IN = N_HEADS * HEAD_DIM + 6 * NSA_KV_GROUPS * HEAD_DIM + 3 * N_HEADS
DIL_GROUPS = ((128, 1), (512, 4), (2048, 16))
DIL_IN = len(DIL_GROUPS) * 3 * N_HEADS * HEAD_DIM
D_FF = 2816
CONV_W = 3
DN_ALPHA = (2 * DEPTH) ** 0.25
DN_BETA = (8 * DEPTH) ** -0.25

kernel_name = "hybrid_gla_nsa_dilated_deepnorm"


def layer_norm(x, g, b):
    xf = x.astype(jnp.float32)
    mu = jnp.mean(xf, -1, keepdims=True)
    var = jnp.mean(jnp.square(xf - mu), -1, keepdims=True)
    return ((xf - mu) * lax.rsqrt(var + LN_EPS) * g + b).astype(x.dtype)


def rel_bucket(dist):
    n = jnp.maximum(dist, 0)
    exact = REL_BUCKETS // 2
    logn = jnp.log(jnp.maximum(n, 1).astype(jnp.float32) / exact)
    large = exact + (logn / math.log(REL_MAX_DIST / exact) * (REL_BUCKETS - exact)).astype(jnp.int32)
    large = jnp.minimum(large, REL_BUCKETS - 1)
    return jnp.where(n < exact, n, large)


def masked_softmax(s, mask):
    s = jnp.where(mask, s.astype(jnp.float32), -jnp.inf)
    m = jnp.max(s, axis=-1, keepdims=True)
    m = jnp.where(jnp.isfinite(m), m, 0.0)
    e = jnp.where(mask, jnp.exp(s - m), 0.0)
    den = jnp.sum(e, axis=-1, keepdims=True)
    p = e / jnp.where(den > 0, den, 1.0)
    return p, (m + jnp.log(den))[..., 0]


def banded_attention(q, k, v, max_dist, dist_scale, rel_table):
    n, l, g, r, dh = q.shape
    nblk = -(-l // Q_BLOCK)
    lp = nblk * Q_BLOCK
    nb = -(-max_dist // Q_BLOCK)
    span = (nb + 1) * Q_BLOCK
    qp = jnp.pad(q, ((0, 0), (0, lp - l), (0, 0), (0, 0), (0, 0)))
    kv_pad = ((0, 0), (nb * Q_BLOCK, lp - l), (0, 0), (0, 0))
    kp = jnp.pad(k, kv_pad)
    vp = jnp.pad(v, kv_pad)
    dist = jnp.arange(Q_BLOCK)[:, None] - jnp.arange(span)[None, :] + nb * Q_BLOCK
    band = (dist >= 0) & (dist <= max_dist)
    bias = rel_table[rel_bucket(dist * dist_scale)].reshape(Q_BLOCK, span, g, r)
    bias = bias.transpose(2, 3, 0, 1).astype(jnp.float32)
    scale = dh ** -0.5

    def block(i):
        q0 = i * Q_BLOCK
        qi = lax.dynamic_slice_in_dim(qp, q0, Q_BLOCK, axis=1)
        ki = lax.dynamic_slice_in_dim(kp, q0, span, axis=1)
        vi = lax.dynamic_slice_in_dim(vp, q0, span, axis=1)
        kpos = q0 - nb * Q_BLOCK + jnp.arange(span)
        mask = band & (kpos >= 0)[None, :]
        s = jnp.einsum('nqgrd,nkgd->ngrqk', qi, ki).astype(jnp.float32) * scale + bias
        p, lse = masked_softmax(s, mask)
        o = jnp.einsum('ngrqk,nkgd->nqgrd', p.astype(vi.dtype), vi)
        return o, lse

    o, lse = lax.map(block, jnp.arange(nblk))
    o = jnp.moveaxis(o, 0, 1).reshape(n, lp, g, r, dh)[:, :l]
    lse = lse.transpose(1, 0, 4, 2, 3).reshape(n, lp, g, r)[:, :l]
    return o, lse


def gla_mixer(h, w_in, w_a2, b_a, norm_g, w_o):
    b, s, _ = h.shape
    qk, dv = GLA_HEADS * GLA_DK, GLA_HEADS * GLA_DV
    q, k, v, r, a_lr = jnp.split(h @ w_in, [qk, 2 * qk, 2 * qk + dv, 2 * qk + 2 * dv], axis=-1)
    log_a = jax.nn.log_sigmoid((a_lr @ w_a2 + b_a).astype(jnp.float32)) / GLA_TAU
    nc = s // GLA_CHUNK

    def chunks(t, d):
        return t.astype(jnp.float32).reshape(b, nc, GLA_CHUNK, GLA_HEADS, d).transpose(0, 3, 1, 2, 4)

    qf = chunks(q, GLA_DK) * GLA_DK ** -0.5
    kf = chunks(k, GLA_DK)
    vf = chunks(v, GLA_DV)
    bcum = jnp.cumsum(chunks(log_a, GLA_DK), axis=3)
    blast = bcum[:, :, :, -1:, :]
    q_dec = qf * jnp.exp(bcum)
    k_inv = kf * jnp.exp(-bcum)
    k_dec = kf * jnp.exp(blast - bcum)
    causal = jnp.tril(jnp.ones((GLA_CHUNK, GLA_CHUNK), dtype=bool))
    att = jnp.where(causal, jnp.einsum('bhnid,bhnjd->bhnij', q_dec, k_inv), 0.0)
    o_intra = jnp.einsum('bhnij,bhnje->bhnie', att, vf)

    def step(state, xs):
        qc, kc, vc, dl = xs
        o = jnp.einsum('bhcd,bhde->bhce', qc, state)
        state = state * dl[..., None] + jnp.einsum('bhcd,bhce->bhde', kc, vc)
        return state, o

    xs = (jnp.moveaxis(q_dec, 2, 0), jnp.moveaxis(k_dec, 2, 0), jnp.moveaxis(vf, 2, 0),
          jnp.moveaxis(jnp.exp(blast[:, :, :, 0]), 2, 0))
    state0 = jnp.zeros((b, GLA_HEADS, GLA_DK, GLA_DV), jnp.float32)
    _, o_inter = lax.scan(step, state0, xs)
    o = o_intra + jnp.moveaxis(o_inter, 0, 2)
    o = o * lax.rsqrt(jnp.mean(o * o, -1, keepdims=True) + LN_EPS) * norm_g
    o = o.transpose(0, 2, 3, 1, 4).reshape(b, s, dv).astype(h.dtype)
    return (o * jax.nn.silu(r)) @ w_o


def compress(x, pe, w1, b1, w2):
    b, s, g, dh = x.shape
    ratio = CMP_LEN // CMP_STRIDE
    pieces = x.reshape(b, s // CMP_STRIDE, CMP_STRIDE, g, dh)
    nc = s // CMP_STRIDE - ratio + 1
    blocks = jnp.concatenate([pieces[:, j:j + nc] for j in range(ratio)], axis=2)
    blocks = (blocks + pe[:, None, :]).transpose(0, 1, 3, 2, 4).reshape(b, nc, g, CMP_LEN * dh)
    return jax.nn.silu(blocks @ w1 + b1) @ w2


def nsa_cmp_sel(q, kcmp, vcmp, ks, vs, rel_table):
    b, s, G, R, dh = q.shape
    nc = kcmp.shape[1]
    nsel = s // SEL_LEN
    top = min(SEL_TOP, nsel)
    ratio_sel = SEL_LEN // CMP_STRIDE
    ratio_cmp = CMP_LEN // CMP_STRIDE
    scale = dh ** -0.5
    cmp_end = jnp.arange(nc) * CMP_STRIDE + CMP_LEN - 1
    ks_blk = ks.reshape(b, nsel, SEL_LEN, G, dh).transpose(0, 3, 1, 2, 4)
    vs_blk = vs.reshape(b, nsel, SEL_LEN, G, dh).transpose(0, 3, 1, 2, 4)
    tab_g = rel_table.reshape(REL_BUCKETS, G, R).transpose(1, 0, 2).reshape(G * REL_BUCKETS, R)
    b_ix = jnp.arange(b)[:, None, None, None]
    g_ix = jnp.arange(G)[None, :, None, None]
    blk_ids = jnp.arange(nsel)[None, :]

    def block(i):
        t = i * Q_BLOCK + jnp.arange(Q_BLOCK)
        qi = lax.dynamic_slice_in_dim(q, i * Q_BLOCK, Q_BLOCK, axis=1)
        dist_c = t[:, None] - cmp_end[None, :]
        bias_c = rel_table[rel_bucket(dist_c)].reshape(Q_BLOCK, nc, G, R).transpose(2, 3, 0, 1)
        s_c = jnp.einsum('bqgrd,bjgd->bgrqj', qi, kcmp).astype(jnp.float32) * scale + bias_c
        p_c, _ = masked_softmax(s_c, dist_c >= 0)
        o_c = jnp.einsum('bgrqj,bjgd->bqgrd', p_c.astype(vcmp.dtype), vcmp)
        p_g = jnp.pad(p_c.sum(axis=2), ((0, 0), (0, 0), (0, 0), (ratio_cmp - 1, ratio_sel * nsel - nc)))
        imp = jnp.stack([p_g[..., m - n + ratio_cmp - 1: m - n + ratio_cmp - 1 + ratio_sel * nsel: ratio_sel]
                         for m in range(ratio_sel) for n in range(ratio_cmp)]).sum(0)
        cur = (t // SEL_LEN)[:, None]
        valid = blk_ids * SEL_LEN <= t[:, None]
        forced = (blk_ids == 0) | (blk_ids == cur) | (blk_ids == cur - 1)
        score = jnp.where(valid, jnp.where(forced, jnp.inf, imp), -jnp.inf)
        top_s, idx = lax.top_k(score, top)
        k_sel = ks_blk[b_ix, g_ix, idx].reshape(b, G, Q_BLOCK, top * SEL_LEN, dh)
        v_sel = vs_blk[b_ix, g_ix, idx].reshape(b, G, Q_BLOCK, top * SEL_LEN, dh)
        pos = (idx[..., None] * SEL_LEN + jnp.arange(SEL_LEN)).reshape(b, G, Q_BLOCK, top * SEL_LEN)
        dist_s = t[None, None, :, None] - pos
        mask_s = jnp.repeat(top_s > -jnp.inf, SEL_LEN, axis=-1) & (dist_s >= 0)
        bias_s = tab_g[g_ix * REL_BUCKETS + rel_bucket(dist_s)].transpose(0, 1, 4, 2, 3)
        s_s = jnp.einsum('bqgrd,bgqkd->bgrqk', qi, k_sel).astype(jnp.float32) * scale + bias_s
        p_s, _ = masked_softmax(s_s, mask_s[:, :, None])
        o_s = jnp.einsum('bgrqk,bgqkd->bqgrd', p_s.astype(v_sel.dtype), v_sel)
        return o_c, o_s

    o_c, o_s = lax.map(block, jnp.arange(s // Q_BLOCK))
    o_c = jnp.moveaxis(o_c, 0, 1).reshape(b, s, G, R, dh)
    o_s = jnp.moveaxis(o_s, 0, 1).reshape(b, s, G, R, dh)
    return o_c, o_s


def nsa_mixer(h, w_in, cmp_pe, cmp_w1, cmp_b1, cmp_w2, rel_table, w_o):
    b, s, _ = h.shape
    G, R, dh = NSA_KV_GROUPS, NSA_Q_PER_KV, HEAD_DIM
    splits = np.cumsum([N_HEADS * dh] + [G * dh] * 6).tolist()
    q, kc, vc, ks, vs, kw, vw, gates = jnp.split(h @ w_in, splits, axis=-1)
    q = q.reshape(b, s, G, R, dh)
    kc, vc, ks, vs, kw, vw = (t.reshape(b, s, G, dh) for t in (kc, vc, ks, vs, kw, vw))
    kcmp = compress(kc, cmp_pe[0], cmp_w1[0], cmp_b1[0], cmp_w2[0])
    vcmp = compress(vc, cmp_pe[1], cmp_w1[1], cmp_b1[1], cmp_w2[1])
    o_cmp, o_sel = nsa_cmp_sel(q, kcmp, vcmp, ks, vs, rel_table)
    o_win, _ = banded_attention(q, kw, vw, WIN_LEN - 1, 1, rel_table)
    gt = jax.nn.sigmoid(gates.astype(jnp.float32)).reshape(b, s, 3, G, R, 1).astype(h.dtype)
    o = gt[:, :, 0] * o_cmp + gt[:, :, 1] * o_sel + gt[:, :, 2] * o_win
    return o.reshape(b, s, N_HEADS * dh) @ w_o


def dilated_mixer(h, w_in, rel_table, w_o):
    b, s, d_model = h.shape
    hd = N_HEADS * HEAD_DIM
    w_g = w_in.reshape(d_model, len(DIL_GROUPS), 3, hd)
    outs, lses = [], []
    for gi, (window, dil) in enumerate(DIL_GROUPS):
        qkv = jnp.einsum('bsd,dce->bsce', h, w_g[:, gi])
        lsub = s // dil

        def by_residue(t):
            return t.reshape(b, lsub, dil, N_HEADS, HEAD_DIM).transpose(0, 2, 1, 3, 4).reshape(b * dil, lsub, N_HEADS, HEAD_DIM)

        o, lse = banded_attention(by_residue(qkv[:, :, 0])[:, :, :, None], by_residue(qkv[:, :, 1]),
                                  by_residue(qkv[:, :, 2]), window // dil, dil, rel_table)
        o = o[:, :, :, 0].reshape(b, dil, lsub, N_HEADS, HEAD_DIM).transpose(0, 2, 1, 3, 4).reshape(b, s, N_HEADS, HEAD_DIM)
        lse = lse[..., 0].reshape(b, dil, lsub, N_HEADS).transpose(0, 2, 1, 3).reshape(b, s, N_HEADS)
        outs.append(o)
        lses.append(lse)
    wts = jax.nn.softmax(jnp.stack(lses, axis=-1), axis=-1).astype(h.dtype)
    o = jnp.einsum('bshg,bshgd->bshd', wts, jnp.stack(outs, axis=3))
    return o.reshape(b, s, hd) @ w_o


def conv_ffn(h, w_up, conv_w, conv_b, w_down):
    u, g = jnp.split(h @ w_up, 2, axis=-1)
    u = lax.conv_general_dilated(u, conv_w[:, None, :], window_strides=(1,), padding=((CONV_W - 1, 0),),
                                 dimension_numbers=('NWC', 'WIO', 'NWC'), feature_group_count=D_FF) + conv_b
    return (jax.nn.silu(u) * g) @ w_down


def post_norm_residual(x, cs, w_mod, b_mod, ln_gain, ln_bias, sublayer):
    shift, scale, gate = jnp.split((cs @ w_mod + b_mod)[:, None, :], 3, axis=-1)
    y = sublayer(x * (1 + scale) + shift)
    return layer_norm(DN_ALPHA * x + (1 + gate) * y, ln_gain, ln_bias)


def setup_inputs(seed: int = 0) -> dict:
    key = jax.random.key(seed)
    keys = iter(jax.random.split(key, 32))

    def nrm(shape, std):
        return std * jax.random.normal(next(keys), shape, jnp.float32)

    n_a, n_b, n_c = (len(range(m, DEPTH, N_MIXERS)) for m in range(N_MIXERS))
    D = D_MODEL
    hd = N_HEADS * HEAD_DIM
    return {
        'x': nrm((BATCH, SEQ, D), 1.0),
        'c': nrm((BATCH, D), 1.0),
        'rel_table': nrm((REL_BUCKETS, N_HEADS), 0.5),
        'mod_w': nrm((DEPTH, 2, D, 3 * D), 0.5 * D ** -0.5),
        'mod_b': nrm((DEPTH, 2, 3 * D), 0.02),
        'ln_g': 1.0 + nrm((DEPTH, 2, D), 0.02),
        'ln_b': nrm((DEPTH, 2, D), 0.02),
        'gla_w_in': nrm((n_a, D, GLA_IN), D ** -0.5),
        'gla_w_a2': nrm((n_a, GLA_GATE_RANK, GLA_HEADS * GLA_DK), GLA_GATE_RANK ** -0.5),
        'gla_b_a': nrm((n_a, GLA_HEADS * GLA_DK), 0.1),
        'gla_norm_g': 1.0 + nrm((n_a, GLA_DV), 0.02),
        'gla_w_o': nrm((n_a, GLA_HEADS * GLA_DV, D), DN_BETA * (GLA_HEADS * GLA_DV) ** -0.5),
        'nsa_w_in': nrm((n_b, D, NSA_IN), D ** -0.5),
        'nsa_cmp_pe': nrm((n_b, 2, CMP_LEN, HEAD_DIM), 0.1),
        'nsa_cmp_w1': nrm((n_b, 2, CMP_LEN * HEAD_DIM, CMP_HIDDEN), (CMP_LEN * HEAD_DIM) ** -0.5),
        'nsa_cmp_b1': nrm((n_b, 2, CMP_HIDDEN), 0.02),
        'nsa_cmp_w2': nrm((n_b, 2, CMP_HIDDEN, HEAD_DIM), CMP_HIDDEN ** -0.5),
        'nsa_w_o': nrm((n_b, hd, D), DN_BETA * hd ** -0.5),
        'dil_w_in': nrm((n_c, D, DIL_IN), D ** -0.5),
        'dil_w_o': nrm((n_c, hd, D), DN_BETA * hd ** -0.5),
        'ffn_w_up': nrm((DEPTH, D, 2 * D_FF), D ** -0.5),
        'ffn_conv_w': nrm((DEPTH, CONV_W, D_FF), CONV_W ** -0.5),
        'ffn_conv_b': nrm((DEPTH, D_FF), 0.02),
        'ffn_w_down': nrm((DEPTH, D_FF, D), DN_BETA * D_FF ** -0.5),
    }


def reference(x, c, rel_table, mod_w, mod_b, ln_g, ln_b,
              gla_w_in, gla_w_a2, gla_b_a, gla_norm_g, gla_w_o,
              nsa_w_in, nsa_cmp_pe, nsa_cmp_w1, nsa_cmp_b1, nsa_cmp_w2, nsa_w_o,
              dil_w_in, dil_w_o,
              ffn_w_up, ffn_conv_w, ffn_conv_b, ffn_w_down):
    cs = jax.nn.silu(c)
    for i in range(DEPTH):
        kind, j = i % N_MIXERS, i // N_MIXERS
        if kind == 0:
            mixer = functools.partial(gla_mixer, w_in=gla_w_in[j], w_a2=gla_w_a2[j], b_a=gla_b_a[j],
                                      norm_g=gla_norm_g[j], w_o=gla_w_o[j])
        elif kind == 1:
            mixer = functools.partial(nsa_mixer, w_in=nsa_w_in[j], cmp_pe=nsa_cmp_pe[j], cmp_w1=nsa_cmp_w1[j],
                                      cmp_b1=nsa_cmp_b1[j], cmp_w2=nsa_cmp_w2[j], rel_table=rel_table,
                                      w_o=nsa_w_o[j])
        else:
            mixer = functools.partial(dilated_mixer, w_in=dil_w_in[j], rel_table=rel_table, w_o=dil_w_o[j])
        x = post_norm_residual(x, cs, mod_w[i, 0], mod_b[i, 0], ln_g[i, 0], ln_b[i, 0], mixer)
        ffn = functools.partial(conv_ffn, w_up=ffn_w_up[i], conv_w=ffn_conv_w[i], conv_b=ffn_conv_b[i],
                                w_down=ffn_w_down[i])
        x = post_norm_residual(x, cs, mod_w[i, 1], mod_b[i, 1], ln_g[i, 1], ln_b[i, 1], ffn)
    return x
```

```python
import functools
import math

import jax
import jax.numpy as jnp
import numpy as np
from jax import lax
from jax.experimental import pallas as pl
from jax.experimental.pallas import tpu as pltpu

F32 = jnp.float32
BF16 = jnp.bfloat16

D_MODEL = 1024
DEPTH = 4
N_MIXERS = 3
HEAD_DIM = 64
N_HEADS = D_MODEL // HEAD_DIM
REL_BUCKETS = 32
REL_MAX_DIST = 2048
LN_EPS = 1e-5
GLA_HEADS = 4
GLA_DK = D_MODEL // 2 // GLA_HEADS
GLA_DV = D_MODEL // GLA_HEADS
GLA_GATE_RANK = 16
GLA_TAU = 16.0
GLA_CHUNK = 64
NSA_G = 4
NSA_R = N_HEADS // NSA_G
CMP_LEN = 32
CMP_STRIDE = 16
CMP_HIDDEN = 256
SEL_LEN = 64
SEL_TOP = 16
WIN_LEN = 512
DIL_GROUPS = ((128, 1), (512, 4), (2048, 16))
D_FF = 2816
DN_ALPHA = (2 * DEPTH) ** 0.25

NEG = -1e30
VMEM_LIMIT = 56 * 1024 * 1024
LANES = 128
SUB_Q = 128


def _params(*sem):
    return pltpu.CompilerParams(dimension_semantics=sem, vmem_limit_bytes=VMEM_LIMIT)


def _dot(a, b):
    return jnp.dot(a, b, preferred_element_type=F32)


def _dot_nt(a, b):
    return lax.dot_general(a, b, (((1,), (1,)), ((), ())), preferred_element_type=F32)


def _dot_tn(a, b):
    return lax.dot_general(a, b, (((0,), (0,)), ((), ())), preferred_element_type=F32)


def _split(a):
    hi = a.astype(BF16)
    lo = (a - hi.astype(F32)).astype(BF16)
    return hi, lo


def _dot_hi_exact_rhs(a, b_bf16):
    hi, lo = _split(a)
    return _dot(hi, b_bf16) + _dot(lo, b_bf16)


def _silu(x):
    return x * jax.nn.sigmoid(x)


def _rel_bucket(dist):
    n = jnp.maximum(dist, 0)
    exact = REL_BUCKETS // 2
    logn = jnp.log(jnp.maximum(n, 1).astype(jnp.float32) / exact)
    large = exact + (logn / math.log(REL_MAX_DIST / exact) * (REL_BUCKETS - exact)).astype(jnp.int32)
    large = jnp.minimum(large, REL_BUCKETS - 1)
    return jnp.where(n < exact, n, large)


def _mod_kernel(c_ref, w_ref, b_ref, o_ref):
    cs = _silu(c_ref[...])
    ch, cl = _split(cs)
    wh, wl = _split(w_ref[0])
    o_ref[0] = _dot(ch, wh) + _dot(ch, wl) + _dot(cl, wh) + b_ref[0]


def _modulation(c, mod_w, mod_b):
    b, d = c.shape
    n_sub = mod_w.shape[0] * mod_w.shape[1]
    rows = 8
    cp = jnp.zeros((rows, d), F32).at[:b].set(c)
    w = mod_w.reshape(n_sub, d, 3 * d)
    bias = mod_b.reshape(n_sub, 1, 3 * d)
    tn = d
    out = pl.pallas_call(
        _mod_kernel,
        grid=(n_sub, 3 * d // tn),
        in_specs=[pl.BlockSpec((rows, d), lambda i, j: (0, 0)),
                  pl.BlockSpec((1, d, tn), lambda i, j: (i, 0, j)),
                  pl.BlockSpec((1, 1, tn), lambda i, j: (i, 0, j))],
        out_specs=pl.BlockSpec((1, rows, tn), lambda i, j: (i, 0, j)),
        out_shape=jax.ShapeDtypeStruct((n_sub, rows, 3 * d), F32),
        compiler_params=_params("parallel", "parallel"),
        name="modulation",
    )(cp, w, bias)
    out = out[:, :b]
    shift, scale, gate = jnp.split(out, 3, axis=-1)
    return shift[:, :, None, :], scale[:, :, None, :], gate[:, :, None, :]


def _linear_in_kernel(x_ref, sc_ref, sh_ref, w_ref, o_ref):
    h = (x_ref[0] * (1.0 + sc_ref[0]) + sh_ref[0]).astype(BF16)
    o_ref[0] = _dot(h, w_ref[...]).astype(o_ref.dtype)


def _linear_in(x, scale, shift, w, tn, tm=512):
    b, s, d = x.shape
    n = w.shape[1]
    assert n % tn == 0 and s % tm == 0
    return pl.pallas_call(
        _linear_in_kernel,
        grid=(n // tn, b, s // tm),
        in_specs=[pl.BlockSpec((1, tm, d), lambda j, bi, i: (bi, i, 0)),
                  pl.BlockSpec((1, 1, d), lambda j, bi, i: (bi, 0, 0)),
                  pl.BlockSpec((1, 1, d), lambda j, bi, i: (bi, 0, 0)),
                  pl.BlockSpec((d, tn), lambda j, bi, i: (0, j))],
        out_specs=pl.BlockSpec((1, tm, tn), lambda j, bi, i: (bi, i, j)),
        out_shape=jax.ShapeDtypeStruct((b, s, n), BF16),
        compiler_params=_params("parallel", "parallel", "parallel"),
        name="linear_in",
    )(x, scale, shift, w)


def _out_norm_kernel(a_ref, w_ref, x_ref, gate_ref, g_ref, b_ref, o_ref):
    y = _dot(a_ref[0], w_ref[...])
    z = DN_ALPHA * x_ref[0] + (1.0 + gate_ref[0]) * y
    mu = jnp.mean(z, axis=-1, keepdims=True)
    zc = z - mu
    var = jnp.mean(zc * zc, axis=-1, keepdims=True)
    o_ref[0] = zc * lax.rsqrt(var + LN_EPS) * g_ref[...] + b_ref[...]


def _out_norm(a, w, x, gate, ln_g, ln_b, tm=512):
    b, s, d = x.shape
    k = a.shape[-1]
    return pl.pallas_call(
        _out_norm_kernel,
        grid=(b, s // tm),
        in_specs=[pl.BlockSpec((1, tm, k), lambda bi, i: (bi, i, 0)),
                  pl.BlockSpec((k, d), lambda bi, i: (0, 0)),
                  pl.BlockSpec((1, tm, d), lambda bi, i: (bi, i, 0)),
                  pl.BlockSpec((1, 1, d), lambda bi, i: (bi, 0, 0)),
                  pl.BlockSpec((1, d), lambda bi, i: (0, 0)),
                  pl.BlockSpec((1, d), lambda bi, i: (0, 0))],
        out_specs=pl.BlockSpec((1, tm, d), lambda bi, i: (bi, i, 0)),
        out_shape=jax.ShapeDtypeStruct((b, s, d), F32),
        compiler_params=_params("parallel", "parallel"),
        name="out_norm",
    )(a, w, x, gate, ln_g.reshape(1, d), ln_b.reshape(1, d))


FFN_HALO = 16


def _ffn_up_kernel(x_ref, xh_ref, sc_ref, sh_ref, wu_ref, wg_ref, cw_ref, cb_ref, o_ref, h_ref, u_ref):
    i = pl.program_id(1)
    tm = x_ref.shape[1]

    @pl.when(pl.program_id(2) == 0)
    def _():
        sc = 1.0 + sc_ref[0]
        sh = sh_ref[0]
        h_ref[0:FFN_HALO, :] = (xh_ref[0] * sc + sh).astype(BF16)
        h_ref[FFN_HALO:, :] = (x_ref[0] * sc + sh).astype(BF16)

    u = _dot(h_ref[...], wu_ref[...])
    keep = (i > 0).astype(F32)
    u_ref[0:FFN_HALO, :] = u[0:FFN_HALO] * keep
    u_ref[FFN_HALO:, :] = u[FFN_HALO:]
    g = _dot(h_ref[FFN_HALO:, :], wg_ref[...])
    cw = cw_ref[...]
    conv = (u_ref[pl.ds(FFN_HALO - 2, tm), :] * cw[0:1] + u_ref[pl.ds(FFN_HALO - 1, tm), :] * cw[1:2]
            + u_ref[pl.ds(FFN_HALO, tm), :] * cw[2:3] + cb_ref[...])
    o_ref[0] = (_silu(conv) * g).astype(o_ref.dtype)


def _ffn_up(x, scale, shift, w_up, conv_w, conv_b, tm=512, tf=1408):
    b, s, d = x.shape
    f = conv_w.shape[1]
    assert f % tf == 0 and s % tm == 0 and tm % FFN_HALO == 0
    nf = f // tf
    hb = tm // FFN_HALO
    return pl.pallas_call(
        _ffn_up_kernel,
        grid=(b, s // tm, nf),
        in_specs=[pl.BlockSpec((1, tm, d), lambda bi, i, j: (bi, i, 0)),
                  pl.BlockSpec((1, FFN_HALO, d), lambda bi, i, j: (bi, jnp.maximum(i * hb - 1, 0), 0)),
                  pl.BlockSpec((1, 1, d), lambda bi, i, j: (bi, 0, 0)),
                  pl.BlockSpec((1, 1, d), lambda bi, i, j: (bi, 0, 0)),
                  pl.BlockSpec((d, tf), lambda bi, i, j: (0, j)),
                  pl.BlockSpec((d, tf), lambda bi, i, j: (0, nf + j)),
                  pl.BlockSpec((3, tf), lambda bi, i, j: (0, j)),
                  pl.BlockSpec((1, tf), lambda bi, i, j: (0, j))],
        out_specs=pl.BlockSpec((1, tm, tf), lambda bi, i, j: (bi, i, j)),
        out_shape=jax.ShapeDtypeStruct((b, s, f), BF16),
        scratch_shapes=[pltpu.VMEM((tm + FFN_HALO, d), BF16), pltpu.VMEM((tm + FFN_HALO, tf), F32)],
        compiler_params=_params("parallel", "parallel", "arbitrary"),
        name="ffn_up",
    )(x, x, scale, shift, w_up, w_up, conv_w, conv_b.reshape(1, f))


def _gla_kernel(q_ref, k_ref, v_ref, r_ref, a_ref, wa_ref, ba_ref, ng_ref, o_ref, st_ref):
    tc = q_ref.shape[1]
    nchunk = tc // GLA_CHUNK

    @pl.when(pl.program_id(2) == 0)
    def _():
        st_ref[...] = jnp.zeros_like(st_ref)

    z = _dot(a_ref[0], wa_ref[...]) + ba_ref[...]
    log_a = (jnp.minimum(z, 0.0) - jnp.log1p(jnp.exp(-jnp.abs(z)))) / GLA_TAU
    row = lax.broadcasted_iota(jnp.int32, (tc, tc), 0)
    col = lax.broadcasted_iota(jnp.int32, (tc, tc), 1)
    same = (row // GLA_CHUNK) == (col // GLA_CHUNK)
    cum_m = jnp.where(same & (col <= row), 1.0, 0.0).astype(BF16)
    tot_m = jnp.where(same, 1.0, 0.0).astype(BF16)
    la_hi, la_lo = _split(log_a)
    bcum = _dot(cum_m, la_hi) + _dot(cum_m, la_lo)
    btot = _dot(tot_m, la_hi) + _dot(tot_m, la_lo)
    qf = q_ref[0].astype(F32) * (GLA_DK ** -0.5)
    kf = k_ref[0].astype(F32)
    q_dec = (qf * jnp.exp(bcum)).astype(BF16)
    k_inv = (kf * jnp.exp(-bcum)).astype(BF16)
    k_dec = (kf * jnp.exp(btot - bcum)).astype(BF16)
    dl = jnp.exp(btot)
    ci = lax.broadcasted_iota(jnp.int32, (GLA_CHUNK, GLA_CHUNK), 0)
    cj = lax.broadcasted_iota(jnp.int32, (GLA_CHUNK, GLA_CHUNK), 1)
    causal = cj <= ci
    ng = ng_ref[...]
    for c in range(nchunk):
        sl = slice(c * GLA_CHUNK, (c + 1) * GLA_CHUNK)
        qd, ki, kd = q_dec[sl], k_inv[sl], k_dec[sl]
        vc = v_ref[0, sl, :]
        att = jnp.where(causal, _dot_nt(qd, ki), 0.0)
        st = st_ref[...]
        o = _dot(att.astype(BF16), vc) + _dot_nt(qd, st.astype(BF16))
        st_ref[...] = st * dl[c * GLA_CHUNK:c * GLA_CHUNK + 1] + _dot_tn(vc, kd)
        o = o * lax.rsqrt(jnp.mean(o * o, axis=-1, keepdims=True) + LN_EPS) * ng
        o_ref[0, sl, :] = (o * _silu(r_ref[0, sl, :].astype(F32))).astype(o_ref.dtype)


def _gla_core(proj, w_a2, b_a, norm_g, tc=512):
    b, s, _ = proj.shape
    h, dk, dv = GLA_HEADS, GLA_DK, GLA_DV
    assert s % tc == 0
    wa = jnp.zeros((LANES, h * dk), F32).at[:GLA_GATE_RANK].set(w_a2).astype(BF16)
    kb, vb, rb, ab = (h * dk) // dk, (2 * h * dk) // dv, (2 * h * dk + h * dv) // dv, (2 * h * dk + 2 * h * dv) // LANES
    return pl.pallas_call(
        _gla_kernel,
        grid=(b, h, s // tc),
        in_specs=[pl.BlockSpec((1, tc, dk), lambda bi, hi, t: (bi, t, hi)),
                  pl.BlockSpec((1, tc, dk), lambda bi, hi, t: (bi, t, kb + hi)),
                  pl.BlockSpec((1, tc, dv), lambda bi, hi, t: (bi, t, vb + hi)),
                  pl.BlockSpec((1, tc, dv), lambda bi, hi, t: (bi, t, rb + hi)),
                  pl.BlockSpec((1, tc, LANES), lambda bi, hi, t: (bi, t, ab)),
                  pl.BlockSpec((LANES, dk), lambda bi, hi, t: (0, hi)),
                  pl.BlockSpec((1, dk), lambda bi, hi, t: (0, hi)),
                  pl.BlockSpec((1, dv), lambda bi, hi, t: (0, 0))],
        out_specs=pl.BlockSpec((1, tc, dv), lambda bi, hi, t: (bi, t, hi)),
        out_shape=jax.ShapeDtypeStruct((b, s, h * dv), BF16),
        scratch_shapes=[pltpu.VMEM((dv, dk), F32)],
        compiler_params=_params("parallel", "parallel", "arbitrary"),
        name="gla_core",
    )(proj, proj, proj, proj, proj, wa, b_a.reshape(1, h * dk), norm_g.reshape(1, dv))


def _pad_cols(w, n):
    return jnp.pad(w, ((0, 0), (0, n - w.shape[1])))


def _gla_mixer(x, scale, shift, w_in, w_a2, b_a, norm_g):
    n_in = 2 * GLA_HEADS * GLA_DK + 2 * GLA_HEADS * GLA_DV + LANES
    proj = _linear_in(x, scale, shift, _pad_cols(w_in, n_in).astype(BF16), tn=n_in)
    return _gla_core(proj, w_a2, b_a, norm_g)


def _compress_kernel(lo_ref, hi_ref, pe_ref, w1_ref, b1_ref, w2_ref, o_ref):
    half = lo_ref.shape[-1]
    pe = pe_ref[0]
    a_lo = (lo_ref[0, 0, 0].astype(F32) + pe[:, :half]).astype(BF16)
    a_hi = (hi_ref[0, 0, 0].astype(F32) + pe[:, half:]).astype(BF16)
    h1 = _dot(a_lo, w1_ref[0, :half, :]) + _dot(a_hi, w1_ref[0, half:, :]) + b1_ref[0]
    o_ref[0, 0, 0] = _dot(_silu(h1).astype(BF16), w2_ref[0])


def _compress(kv, pe, w1, b1, w2):
    two, b, s, _ = kv.shape
    g, dh = NSA_G, HEAD_DIM
    npc = s // CMP_STRIDE
    half = CMP_STRIDE * dh
    a = kv.reshape(two, b, npc, CMP_STRIDE, g, dh).transpose(0, 1, 4, 2, 3, 5).reshape(two, b, g, npc, half)
    a_hi = jnp.concatenate([a[:, :, :, 1:], jnp.zeros_like(a[:, :, :, :1])], axis=3)
    spec_a = pl.BlockSpec((1, 1, 1, npc, half), lambda c, bi, gi: (c, bi, gi, 0, 0))
    return pl.pallas_call(
        _compress_kernel,
        grid=(two, b, g),
        in_specs=[spec_a, spec_a,
                  pl.BlockSpec((1, 1, 2 * half), lambda c, bi, gi: (c, 0, 0)),
                  pl.BlockSpec((1, 2 * half, CMP_HIDDEN), lambda c, bi, gi: (c, 0, 0)),
                  pl.BlockSpec((1, 1, CMP_HIDDEN), lambda c, bi, gi: (c, 0, 0)),
                  pl.BlockSpec((1, CMP_HIDDEN, dh), lambda c, bi, gi: (c, 0, 0))],
        out_specs=pl.BlockSpec((1, 1, 1, npc, dh), lambda c, bi, gi: (c, bi, gi, 0, 0)),
        out_shape=jax.ShapeDtypeStruct((two, b, g, npc, dh), F32),
        compiler_params=_params("parallel", "parallel", "parallel"),
        name="nsa_compress",
    )(a, a_hi, pe.reshape(two, 1, 2 * half), w1.astype(BF16), b1.reshape(two, 1, CMP_HIDDEN), w2.astype(BF16))


def _stack_heads(q):
    return jnp.concatenate([q[:, r * HEAD_DIM:(r + 1) * HEAD_DIM] for r in range(NSA_R)], axis=0)


def _unstack_heads(o, rows):
    return jnp.concatenate([o[r * rows:(r + 1) * rows] for r in range(NSA_R)], axis=1)


def _nsa_cmp_kernel(q_ref, kc_ref, vc_ref, nb_ref, o_ref, mb_ref):
    i = pl.program_id(2)
    npc = nb_ref.shape[2]
    nblk = mb_ref.shape[3]
    front = npc - SUB_Q // CMP_STRIDE
    start = pl.multiple_of(i * (SUB_Q // CMP_STRIDE), 8)
    kwin = kc_ref[0, 0, pl.ds(start, npc), :].astype(BF16)
    vwin = vc_ref[0, 0, pl.ds(start, npc), :].astype(BF16)
    qs = _stack_heads(q_ref[0])
    s = _dot_nt(qs, kwin) * (HEAD_DIM ** -0.5) + nb_ref[0]
    col = lax.broadcasted_iota(jnp.int32, s.shape, 1)
    s = jnp.where(col >= front - start, s, NEG)
    valid = s > 0.5 * NEG
    m = jnp.max(s, axis=1, keepdims=True)
    e = jnp.where(valid, jnp.exp(s - m), 0.0)
    den = jnp.sum(e, axis=1, keepdims=True)
    p = e / jnp.where(den > 0.0, den, 1.0)
    o = _dot(p.astype(BF16), vwin)
    o_ref[0] = _unstack_heads(o, SUB_Q).astype(o_ref.dtype)

    pg = p[0:SUB_Q]
    for r in range(1, NSA_R):
        pg = pg + p[r * SUB_Q:(r + 1) * SUB_Q]
    jrow = lax.broadcasted_iota(jnp.int32, (npc, nblk), 0) + (start - front)
    bcol = lax.broadcasted_iota(jnp.int32, (npc, nblk), 1)
    diff = jrow - (SEL_LEN // CMP_STRIDE) * bcol
    pool = jnp.where((diff == -1) | (diff == 3), 1.0, jnp.where((diff >= 0) & (diff <= 2), 2.0, 0.0)).astype(BF16)
    imp = _dot_hi_exact_rhs(pg, pool)

    t = i * SUB_Q + lax.broadcasted_iota(jnp.int32, (SUB_Q, nblk), 0)
    blk = lax.broadcasted_iota(jnp.int32, (SUB_Q, nblk), 1)
    cur = t // SEL_LEN
    forced = (blk == 0) | (blk == cur) | (blk == cur - 1)
    work = jnp.where(blk * SEL_LEN <= t, jnp.where(forced, -NEG, imp), NEG)
    blk_f = blk.astype(F32)
    sel = jnp.zeros((SUB_Q, nblk), F32)
    for _ in range(min(SEL_TOP, nblk)):
        mx = jnp.max(work, axis=1, keepdims=True)
        first = jnp.min(jnp.where(work == mx, blk_f, 1e9), axis=1, keepdims=True)
        pick = blk_f == first
        sel = jnp.where(pick & (mx > 0.5 * NEG), 1.0, sel)
        work = jnp.where(pick, 2.0 * NEG, work)
    mb_ref[0, 0] = jnp.where(sel > 0.0, 0.0, NEG).astype(mb_ref.dtype)


def _nsa_cmp(proj, kcmp, vcmp, rel_table):
    b, s, _ = proj.shape
    g = NSA_G
    npc = s // CMP_STRIDE
    nblk = s // SEL_LEN
    front = npc - SUB_Q // CMP_STRIDE

    def padded(c):
        return jnp.pad(c, ((0, 0), (0, 0), (front, npc - front), (0, 0)))

    qi = jnp.arange(SUB_Q)[:, None]
    cc = jnp.arange(npc)[None, :]
    dd = qi - (CMP_LEN - 1) + CMP_STRIDE * (front - cc)
    bias = rel_table[_rel_bucket(dd)].transpose(2, 0, 1)
    nb = jnp.where(dd[None] >= 0, bias, NEG).reshape(g, NSA_R * SUB_Q, npc)
    return pl.pallas_call(
        _nsa_cmp_kernel,
        grid=(g, b, s // SUB_Q),
        in_specs=[pl.BlockSpec((1, SUB_Q, NSA_R * HEAD_DIM), lambda gi, bi, i: (bi, i, gi)),
                  pl.BlockSpec((1, 1, 2 * npc, HEAD_DIM), lambda gi, bi, i: (bi, gi, 0, 0)),
                  pl.BlockSpec((1, 1, 2 * npc, HEAD_DIM), lambda gi, bi, i: (bi, gi, 0, 0)),
                  pl.BlockSpec((1, NSA_R * SUB_Q, npc), lambda gi, bi, i: (gi, 0, 0))],
        out_specs=[pl.BlockSpec((1, SUB_Q, NSA_R * HEAD_DIM), lambda gi, bi, i: (bi, i, gi)),
                   pl.BlockSpec((1, 1, SUB_Q, nblk), lambda gi, bi, i: (bi, gi, i, 0))],
        out_shape=[jax.ShapeDtypeStruct((b, s, N_HEADS * HEAD_DIM), BF16),
                   jax.ShapeDtypeStruct((b, g, s, nblk), BF16)],
        compiler_params=_params("parallel", "parallel", "parallel"),
        name="nsa_cmp_select",
    )(proj, padded(kcmp), padded(vcmp), nb)


SEL_T = 256
SEL_FAR = 16 * 128 ** (15 / 16)
SEL_NEAR = int(math.ceil((SEL_FAR + SEL_T - 1) / SEL_T))


def _nsa_sel_kernel(q_ref, fq_ref, mb_ref, k_ref, v_ref, bt_ref, o_ref, qp_ref, m_ref, acc_ref):
    i = pl.program_id(2)
    rows = NSA_R * SEL_T
    fw = (k_ref.shape[3] - LANES)
    kh = fw // (SEL_T // SEL_LEN)
    qs = _stack_heads(q_ref[0]) * (HEAD_DIM ** -0.5)
    mb = mb_ref[0, 0]
    for half in range(2):
        mbh = jnp.concatenate([mb[:, half * fw:(half + 1) * fw]] * NSA_R, axis=0)
        qp_ref[half] = jnp.concatenate([qs, fq_ref[0], mbh], axis=1)
    m_ref[...] = jnp.full_like(m_ref, NEG)
    acc_ref[...] = jnp.zeros_like(acc_ref)

    def update(kt, half, bias):
        k0 = pl.multiple_of(kt * SEL_T, SEL_T)
        kk = k_ref[0, 0, pl.ds(k0, SEL_T), :]
        vv = v_ref[0, 0, pl.ds(k0, SEL_T), :]
        s = _dot_nt(qp_ref[half], kk)
        if bias is not None:
            s = s + bias
        m_prev = m_ref[...]
        m_new = jnp.maximum(m_prev, jnp.max(s, axis=1, keepdims=True))
        alpha = jnp.exp(m_prev - m_new)
        p = jnp.exp(s - jnp.concatenate([m_new] * (SEL_T // LANES), axis=1))
        acc_ref[...] = acc_ref[...] * alpha + _dot(p.astype(BF16), vv)
        m_ref[...] = m_new

    n_far = jnp.maximum(i - SEL_NEAR + 1, 0)

    def far_lo(kt, carry):
        update(kt, 0, None)
        return carry

    def far_hi(kt, carry):
        update(kt, 1, None)
        return carry

    lax.fori_loop(0, jnp.minimum(n_far, kh), far_lo, 0)
    lax.fori_loop(kh, jnp.maximum(n_far, kh), far_hi, 0)
    for e in range(SEL_NEAR - 1, -1, -1):
        @pl.when(i - e >= 0)
        def _(e=e):
            kt = i - e
            update(kt, jnp.where(kt < kh, 0, 1), bt_ref[0, :, e * SEL_T:(e + 1) * SEL_T])

    acc = acc_ref[...]
    o = acc[:, :HEAD_DIM] / acc[:, HEAD_DIM:HEAD_DIM + 1]
    o_ref[0] = _unstack_heads(o, SEL_T).astype(o_ref.dtype)


def _nsa_sel(proj, ks, vs, mb, rel_table):
    b, s, _ = proj.shape
    g, dh = NSA_G, HEAD_DIM
    nblk = s // SEL_LEN
    fw = nblk // 2
    rows = NSA_R * SEL_T
    pos = jnp.arange(s)
    onehot = (((pos // SEL_LEN) % fw)[:, None] == jnp.arange(fw)[None, :]).astype(BF16)
    ones2 = jnp.zeros((s, LANES - dh), BF16).at[:, :2].set(1.0)
    kp = jnp.concatenate([ks, jnp.broadcast_to(ones2, (b, g, s, LANES - dh)),
                          jnp.broadcast_to(onehot, (b, g, s, fw))], axis=-1)
    ones1 = jnp.zeros((s, LANES - dh), BF16).at[:, :1].set(1.0)
    vp = jnp.concatenate([vs, jnp.broadcast_to(ones1, (b, g, s, LANES - dh))], axis=-1)
    b_far = rel_table[REL_BUCKETS - 1].reshape(g, NSA_R)
    hi = b_far.astype(BF16)
    lo = (b_far - hi.astype(F32)).astype(BF16)
    fq = jnp.zeros((g, NSA_R, SEL_T, LANES - dh), BF16)
    fq = fq.at[:, :, :, 0].set(hi[:, :, None]).at[:, :, :, 1].set(lo[:, :, None]).reshape(g, rows, LANES - dh)
    qi = jnp.arange(SEL_T)[:, None]
    cc = jnp.arange(SEL_NEAR * SEL_T)[None, :]
    dd = (cc // SEL_T) * SEL_T + qi - (cc % SEL_T)
    bias = rel_table[_rel_bucket(dd)].transpose(2, 0, 1).reshape(g, NSA_R, SEL_T, -1) - b_far[:, :, None, None]
    bt = jnp.where(dd[None, None] >= 0, bias, NEG).reshape(g, rows, SEL_NEAR * SEL_T)
    return pl.pallas_call(
        _nsa_sel_kernel,
        grid=(g, b, s // SEL_T),
        in_specs=[pl.BlockSpec((1, SEL_T, NSA_R * dh), lambda gi, bi, i: (bi, i, gi)),
                  pl.BlockSpec((1, rows, LANES - dh), lambda gi, bi, i: (gi, 0, 0)),
                  pl.BlockSpec((1, 1, SEL_T, nblk), lambda gi, bi, i: (bi, gi, i, 0)),
                  pl.BlockSpec((1, 1, s, LANES + fw), lambda gi, bi, i: (bi, gi, 0, 0)),
                  pl.BlockSpec((1, 1, s, LANES), lambda gi, bi, i: (bi, gi, 0, 0)),
                  pl.BlockSpec((1, rows, SEL_NEAR * SEL_T), lambda gi, bi, i: (gi, 0, 0))],
        out_specs=pl.BlockSpec((1, SEL_T, NSA_R * dh), lambda gi, bi, i: (bi, i, gi)),
        out_shape=jax.ShapeDtypeStruct((b, s, N_HEADS * dh), BF16),
        scratch_shapes=[pltpu.VMEM((2, rows, LANES + fw), BF16), pltpu.VMEM((rows, LANES), F32),
                        pltpu.VMEM((rows, LANES), F32)],
        compiler_params=_params("parallel", "parallel", "parallel"),
        name="nsa_selected",
    )(proj, fq, mb, kp, vp, bt)


WIN_T = 512


def _softmax_rows(s):
    m = jnp.max(s, axis=1, keepdims=True)
    e = jnp.exp(s - m)
    den = jnp.sum(e, axis=1, keepdims=True)
    return e / den, m + jnp.log(den)


def _nsa_win_kernel(q_ref, kp_ref, kc_ref, vp_ref, vc_ref, bt_ref, o_ref):
    i = pl.program_id(2)
    back = WIN_T
    span = back + SUB_Q
    kcat = jnp.concatenate([kp_ref[0, 0], kc_ref[0, 0]], axis=0)
    vcat = jnp.concatenate([vp_ref[0, 0], vc_ref[0, 0]], axis=0)
    col = lax.broadcasted_iota(jnp.int32, (NSA_R * SUB_Q, span), 1)
    for j in range(WIN_T // SUB_Q):
        qs = _stack_heads(q_ref[0, j * SUB_Q:(j + 1) * SUB_Q, :]) * (HEAD_DIM ** -0.5)
        s = _dot_nt(qs, kcat[j * SUB_Q:j * SUB_Q + span]) + bt_ref[0]
        s = jnp.where((col >= back - j * SUB_Q) | (i > 0), s, NEG)
        p, _ = _softmax_rows(s)
        o = _dot(p.astype(BF16), vcat[j * SUB_Q:j * SUB_Q + span])
        o_ref[0, j * SUB_Q:(j + 1) * SUB_Q, :] = _unstack_heads(o, SUB_Q).astype(o_ref.dtype)


def _nsa_win(proj, kw, vw, rel_table):
    b, s, _ = proj.shape
    g, dh = NSA_G, HEAD_DIM
    span = WIN_T + SUB_Q
    qi = jnp.arange(SUB_Q)[:, None]
    cc = jnp.arange(span)[None, :]
    dd = qi + WIN_T - cc
    bias = rel_table[_rel_bucket(dd)].transpose(2, 0, 1)
    bt = jnp.where((dd[None] >= 0) & (dd[None] <= WIN_LEN - 1), bias, NEG).reshape(g, NSA_R * SUB_Q, span)
    prev = lambda gi, bi, i: (bi, gi, jnp.maximum(i - 1, 0), 0)
    cur = lambda gi, bi, i: (bi, gi, i, 0)
    return pl.pallas_call(
        _nsa_win_kernel,
        grid=(g, b, s // WIN_T),
        in_specs=[pl.BlockSpec((1, WIN_T, NSA_R * dh), lambda gi, bi, i: (bi, i, gi)),
                  pl.BlockSpec((1, 1, WIN_T, dh), prev), pl.BlockSpec((1, 1, WIN_T, dh), cur),
                  pl.BlockSpec((1, 1, WIN_T, dh), prev), pl.BlockSpec((1, 1, WIN_T, dh), cur),
                  pl.BlockSpec((1, NSA_R * SUB_Q, span), lambda gi, bi, i: (gi, 0, 0))],
        out_specs=pl.BlockSpec((1, WIN_T, NSA_R * dh), lambda gi, bi, i: (bi, i, gi)),
        out_shape=jax.ShapeDtypeStruct((b, s, N_HEADS * dh), BF16),
        compiler_params=_params("parallel", "parallel", "parallel"),
        name="nsa_window",
    )(proj, kw, kw, vw, vw, bt)


def _nsa_combine_kernel(oc_ref, os_ref, ow_ref, g_ref, e_ref, o_ref):
    d = o_ref.shape[2]
    sg = jax.nn.sigmoid(g_ref[0].astype(F32))
    ge = _dot_hi_exact_rhs(sg, e_ref[...])
    o = (ge[:, :d] * oc_ref[0].astype(F32) + ge[:, d:2 * d] * os_ref[0].astype(F32)
         + ge[:, 2 * d:] * ow_ref[0].astype(F32))
    o_ref[0] = o.astype(o_ref.dtype)


def _nsa_combine(o_cmp, o_sel, o_win, proj, gate_col, tm=512):
    b, s, d = o_cmp.shape
    c = jnp.arange(LANES)[:, None]
    n = jnp.arange(3 * d)[None, :]
    expand = (c == (n // d) * N_HEADS + (n % d) // HEAD_DIM).astype(BF16)
    spec = pl.BlockSpec((1, tm, d), lambda bi, i: (bi, i, 0))
    return pl.pallas_call(
        _nsa_combine_kernel,
        grid=(b, s // tm),
        in_specs=[spec, spec, spec,
                  pl.BlockSpec((1, tm, LANES), lambda bi, i: (bi, i, gate_col // LANES)),
                  pl.BlockSpec((LANES, 3 * d), lambda bi, i: (0, 0))],
        out_specs=spec,
        out_shape=jax.ShapeDtypeStruct((b, s, d), BF16),
        compiler_params=_params("parallel", "parallel"),
        name="nsa_combine",
    )(o_cmp, o_sel, o_win, proj, expand)


def _nsa_mixer(x, scale, shift, w_in, pe, w1, b1, w2, rel_table):
    b, s, _ = x.shape
    hd, gd = N_HEADS * HEAD_DIM, NSA_G * HEAD_DIM
    gate_col = hd + 6 * gd
    n_in = gate_col + LANES
    proj = _linear_in(x, scale, shift, _pad_cols(w_in, n_in).astype(BF16), tn=n_in)

    def group_major(t):
        return t.reshape(b, s, NSA_G, HEAD_DIM).transpose(0, 2, 1, 3)

    kc, vc, ks, vs, kw, vw = (proj[:, :, hd + n * gd: hd + (n + 1) * gd] for n in range(6))
    cmp = _compress(jnp.stack([kc, vc]), pe.reshape(2, -1), w1, b1, w2)
    o_cmp, mb = _nsa_cmp(proj, cmp[0], cmp[1], rel_table)
    o_sel = _nsa_sel(proj, group_major(ks), group_major(vs), mb, rel_table)
    o_win = _nsa_win(proj, group_major(kw), group_major(vw), rel_table)
    return _nsa_combine(o_cmp, o_sel, o_win, proj, gate_col)


DIL_BACK = 128


def _dil_kernel(q_ref, kp_ref, kc_ref, vp_ref, vc_ref, bt_ref, o_ref, l_ref):
    t = pl.program_id(3)
    tq = q_ref.shape[1]
    span = DIL_BACK + SUB_Q
    kcat = jnp.concatenate([kp_ref[0, tq - DIL_BACK:, :], kc_ref[0]], axis=0)
    vcat = jnp.concatenate([vp_ref[0, tq - DIL_BACK:, :], vc_ref[0]], axis=0)
    lane_q = lax.broadcasted_iota(jnp.int32, (SUB_Q, LANES), 1) // HEAD_DIM
    lane_k = lax.broadcasted_iota(jnp.int32, (span, LANES), 1) // HEAD_DIM
    col = lax.broadcasted_iota(jnp.int32, (SUB_Q, span), 1)
    for j in range(tq // SUB_Q):
        qj = q_ref[0, j * SUB_Q:(j + 1) * SUB_Q, :] * (HEAD_DIM ** -0.5)
        kk = kcat[j * SUB_Q:j * SUB_Q + span]
        vv = vcat[j * SUB_Q:j * SUB_Q + span]
        o = jnp.zeros((SUB_Q, LANES), F32)
        lse = jnp.zeros((SUB_Q, LANES), F32)
        for hh in range(LANES // HEAD_DIM):
            s = _dot_nt(jnp.where(lane_q == hh, qj, 0.0).astype(BF16), kk) + bt_ref[hh]
            if j == 0:
                s = jnp.where((col >= DIL_BACK) | (t > 0), s, NEG)
            p, lse_h = _softmax_rows(s)
            o = o + _dot(p.astype(BF16), jnp.where(lane_k == hh, vv, 0.0).astype(BF16))
            lse = jnp.where(lane_q == hh, lse_h, lse)
        o_ref[0, j * SUB_Q:(j + 1) * SUB_Q, :] = o.astype(o_ref.dtype)
        l_ref[0, j * SUB_Q:(j + 1) * SUB_Q, :] = lse


def _dil_group(proj, gi, window, dil, rel_table):
    b, s, n_all = proj.shape
    hd = N_HEADS * HEAD_DIM
    assert window // dil == DIL_BACK
    l = s // dil
    tq = min(512, l)
    assert l % tq == 0 and tq >= DIL_BACK
    view = proj.reshape(b, l, dil * n_all)
    bpr = n_all // LANES
    hpb = hd // LANES
    base = gi * 3 * hpb
    qi = jnp.arange(SUB_Q)[:, None]
    cc = jnp.arange(DIL_BACK + SUB_Q)[None, :]
    dd = qi + DIL_BACK - cc
    bias = rel_table[_rel_bucket(dd * dil)].transpose(2, 0, 1)
    bt = jnp.where((dd[None] >= 0) & (dd[None] <= DIL_BACK), bias, NEG)

    def spec(comp, prev):
        def index(bi, r, hp, t):
            return (bi, jnp.maximum(t - 1, 0) if prev else t, r * bpr + base + comp * hpb + hp)
        return pl.BlockSpec((1, tq, LANES), index)

    out_spec = pl.BlockSpec((1, tq, LANES), lambda bi, r, hp, t: (bi, t, r * hpb + hp))
    o, lse = pl.pallas_call(
        _dil_kernel,
        grid=(b, dil, hpb, l // tq),
        in_specs=[spec(0, False), spec(1, True), spec(1, False), spec(2, True), spec(2, False),
                  pl.BlockSpec((LANES // HEAD_DIM, SUB_Q, DIL_BACK + SUB_Q), lambda bi, r, hp, t: (hp, 0, 0))],
        out_specs=[out_spec, out_spec],
        out_shape=[jax.ShapeDtypeStruct((b, l, dil * hd), BF16), jax.ShapeDtypeStruct((b, l, dil * hd), F32)],
        compiler_params=_params("parallel", "parallel", "parallel", "parallel"),
        name=f"dilated_attention_{dil}",
    )(view, view, view, view, view, bt)
    return o.reshape(b, s, hd), lse.reshape(b, s, hd)


def _dil_merge_kernel(o0_ref, o1_ref, o2_ref, l0_ref, l1_ref, l2_ref, o_ref):
    l0, l1, l2 = l0_ref[0], l1_ref[0], l2_ref[0]
    m = jnp.maximum(jnp.maximum(l0, l1), l2)
    e0, e1, e2 = jnp.exp(l0 - m), jnp.exp(l1 - m), jnp.exp(l2 - m)
    den = e0 + e1 + e2
    o = (e0 / den) * o0_ref[0].astype(F32) + (e1 / den) * o1_ref[0].astype(F32) + (e2 / den) * o2_ref[0].astype(F32)
    o_ref[0] = o.astype(o_ref.dtype)


def _dil_merge(outs, lses, tm=512):
    b, s, d = outs[0].shape
    spec = pl.BlockSpec((1, tm, d), lambda bi, i: (bi, i, 0))
    return pl.pallas_call(
        _dil_merge_kernel,
        grid=(b, s // tm),
        in_specs=[spec] * 6,
        out_specs=spec,
        out_shape=jax.ShapeDtypeStruct((b, s, d), BF16),
        compiler_params=_params("parallel", "parallel"),
        name="dilated_merge",
    )(*outs, *lses)


def _dil_mixer(x, scale, shift, w_in, rel_table):
    hd = N_HEADS * HEAD_DIM
    proj = _linear_in(x, scale, shift, w_in.astype(BF16), tn=3 * hd)
    outs, lses = zip(*[_dil_group(proj, gi, window, dil, rel_table) for gi, (window, dil) in enumerate(DIL_GROUPS)])
    return _dil_merge(outs, lses)


def kernel(x, c, rel_table, mod_w, mod_b, ln_g, ln_b, gla_w_in, gla_w_a2, gla_b_a, gla_norm_g, gla_w_o,
           nsa_w_in, nsa_cmp_pe, nsa_cmp_w1, nsa_cmp_b1, nsa_cmp_w2, nsa_w_o, dil_w_in, dil_w_o,
           ffn_w_up, ffn_conv_w, ffn_conv_b, ffn_w_down):
    shift, scale, gate = _modulation(c, mod_w, mod_b)
    for i in range(DEPTH):
        kind, j = i % N_MIXERS, i // N_MIXERS
        m = 2 * i
        if kind == 0:
            a = _gla_mixer(x, scale[m], shift[m], gla_w_in[j], gla_w_a2[j], gla_b_a[j], gla_norm_g[j])
            w_o = gla_w_o[j]
        elif kind == 1:
            a = _nsa_mixer(x, scale[m], shift[m], nsa_w_in[j], nsa_cmp_pe[j], nsa_cmp_w1[j], nsa_cmp_b1[j],
                           nsa_cmp_w2[j], rel_table)
            w_o = nsa_w_o[j]
        else:
            a = _dil_mixer(x, scale[m], shift[m], dil_w_in[j], rel_table)
            w_o = dil_w_o[j]
        x = _out_norm(a, w_o.astype(BF16), x, gate[m], ln_g[i, 0], ln_b[i, 0])
        a = _ffn_up(x, scale[m + 1], shift[m + 1], ffn_w_up[i].astype(BF16), ffn_conv_w[i], ffn_conv_b[i])
        x = _out_norm(a, ffn_w_down[i].astype(BF16), x, gate[m + 1], ln_g[i, 1], ln_b[i, 1])
    return x
```

```python
import functools
import math

import jax
import jax.numpy as jnp
import numpy as np
from jax import lax
from jax.experimental import pallas as pl
from jax.experimental.pallas import tpu as pltpu

F32 = jnp.float32
BF16 = jnp.bfloat16

D_MODEL = 1024
DEPTH = 4
N_MIXERS = 3
HEAD_DIM = 64
N_HEADS = D_MODEL // HEAD_DIM
REL_BUCKETS = 32
REL_MAX_DIST = 2048
LN_EPS = 1e-5
GLA_HEADS = 4
GLA_DK = D_MODEL // 2 // GLA_HEADS
GLA_DV = D_MODEL // GLA_HEADS
GLA_GATE_RANK = 16
GLA_TAU = 16.0
GLA_CHUNK = 64
NSA_G = 4
NSA_R = N_HEADS // NSA_G
CMP_LEN = 32
CMP_STRIDE = 16
CMP_HIDDEN = 256
SEL_LEN = 64
SEL_TOP = 16
WIN_LEN = 512
DIL_GROUPS = ((128, 1), (512, 4), (2048, 16))
D_FF = 2816
DN_ALPHA = (2 * DEPTH) ** 0.25

NEG = -1e30
VMEM_LIMIT = 56 * 1024 * 1024
LANES = 128
SUB_Q = 128


def _params(*sem):
    return pltpu.CompilerParams(dimension_semantics=sem, vmem_limit_bytes=VMEM_LIMIT)


def _dot(a, b):
    return jnp.dot(a, b, preferred_element_type=F32)


def _dot_nt(a, b):
    return lax.dot_general(a, b, (((1,), (1,)), ((), ())), preferred_element_type=F32)


def _dot_tn(a, b):
    return lax.dot_general(a, b, (((0,), (0,)), ((), ())), preferred_element_type=F32)


def _split(a):
    hi = a.astype(BF16)
    lo = (a - hi.astype(F32)).astype(BF16)
    return hi, lo


def _dot_hi_exact_rhs(a, b_bf16):
    hi, lo = _split(a)
    return _dot(hi, b_bf16) + _dot(lo, b_bf16)


def _silu(x):
    return x * jax.nn.sigmoid(x)


def _bucket_starts():
    n = np.arange(2 * REL_MAX_DIST)
    exact = REL_BUCKETS // 2
    logn = np.log(np.maximum(n, 1).astype(np.float32) / np.float32(exact))
    large = exact + (logn / np.float32(math.log(REL_MAX_DIST / exact)) * np.float32(REL_BUCKETS - exact)).astype(np.int32)
    bucket = np.where(n < exact, n, np.minimum(large, REL_BUCKETS - 1))
    assert np.all(np.diff(bucket) >= 0) and bucket[-1] == REL_BUCKETS - 1
    return [int(np.argmax(bucket >= k)) for k in range(REL_BUCKETS)]


BUCKET_START = _bucket_starts()
FAR_DIST = BUCKET_START[-1]


def _rel_bias(rel_table, dist):
    shape = (rel_table.shape[1],) + (1,) * dist.ndim
    out = jnp.broadcast_to(rel_table[0].reshape(shape), (rel_table.shape[1],) + dist.shape)
    for k in range(1, REL_BUCKETS):
        out = jnp.where(dist[None] >= BUCKET_START[k], rel_table[k].reshape(shape), out)
    return out


def _mod_kernel(c_ref, w_ref, b_ref, o_ref):
    cs = _silu(c_ref[...])
    ch, cl = _split(cs)
    wh, wl = _split(w_ref[0])
    o_ref[0] = _dot(ch, wh) + _dot(ch, wl) + _dot(cl, wh) + b_ref[0]


def _modulation(c, mod_w, mod_b):
    b, d = c.shape
    n_sub = mod_w.shape[0] * mod_w.shape[1]
    rows = 8
    cp = jnp.zeros((rows, d), F32).at[:b].set(c)
    w = mod_w.reshape(n_sub, d, 3 * d)
    bias = mod_b.reshape(n_sub, 1, 3 * d)
    tn = d
    out = pl.pallas_call(
        _mod_kernel,
        grid=(n_sub, 3 * d // tn),
        in_specs=[pl.BlockSpec((rows, d), lambda i, j: (0, 0)),
                  pl.BlockSpec((1, d, tn), lambda i, j: (i, 0, j)),
                  pl.BlockSpec((1, 1, tn), lambda i, j: (i, 0, j))],
        out_specs=pl.BlockSpec((1, rows, tn), lambda i, j: (i, 0, j)),
        out_shape=jax.ShapeDtypeStruct((n_sub, rows, 3 * d), F32),
        compiler_params=_params("parallel", "parallel"),
        name="modulation",
    )(cp, w, bias)
    out = out[:, :b]
    shift, scale, gate = jnp.split(out, 3, axis=-1)
    return shift[:, :, None, :], scale[:, :, None, :], gate[:, :, None, :]


def _linear_in_kernel(x_ref, sc_ref, sh_ref, w_ref, o_ref, *acc, dil):
    h = (x_ref[0] * (1.0 + sc_ref[0]) + sh_ref[0]).astype(BF16)
    res = _dot(h, w_ref[...])
    if dil == 1:
        o_ref[0] = res.astype(o_ref.dtype)
    else:
        acc_ref, = acc
        rows, n = res.shape[0] // dil, res.shape[1]
        for c in range(n // LANES):
            acc_ref[c] = res[:, c * LANES:(c + 1) * LANES]
        for r in range(dil):
            for c in range(n // LANES):
                o_ref[0, :, r * n + c * LANES:r * n + (c + 1) * LANES] = (
                    acc_ref[c, pl.ds(r, rows, stride=dil), :].astype(o_ref.dtype))


def _linear_in(x, scale, shift, w, tn, tm=512, dil=1):
    b, s, d = x.shape
    n = w.shape[1]
    assert n % tn == 0 and s % tm == 0 and tm % (16 * dil) == 0 and (dil == 1 or n == tn)
    return pl.pallas_call(
        functools.partial(_linear_in_kernel, dil=dil),
        grid=(n // tn, b, s // tm),
        in_specs=[pl.BlockSpec((1, tm, d), lambda j, bi, i: (bi, i, 0)),
                  pl.BlockSpec((1, 1, d), lambda j, bi, i: (bi, 0, 0)),
                  pl.BlockSpec((1, 1, d), lambda j, bi, i: (bi, 0, 0)),
                  pl.BlockSpec((d, tn), lambda j, bi, i: (0, j))],
        out_specs=pl.BlockSpec((1, tm // dil, dil * tn), lambda j, bi, i: (bi, i, j)),
        out_shape=jax.ShapeDtypeStruct((b, s // dil, dil * n), BF16),
        scratch_shapes=[] if dil == 1 else [pltpu.VMEM((tn // LANES, tm, LANES), F32)],
        compiler_params=_params("parallel", "parallel", "parallel"),
        name="linear_in",
    )(x, scale, shift, w)


def _out_norm_kernel(a_ref, w_ref, x_ref, gate_ref, g_ref, b_ref, o_ref):
    y = _dot(a_ref[0], w_ref[...])
    z = DN_ALPHA * x_ref[0] + (1.0 + gate_ref[0]) * y
    mu = jnp.mean(z, axis=-1, keepdims=True)
    zc = z - mu
    var = jnp.mean(zc * zc, axis=-1, keepdims=True)
    o_ref[0] = zc * lax.rsqrt(var + LN_EPS) * g_ref[...] + b_ref[...]


def _out_norm(a, w, x, gate, ln_g, ln_b, tm=512):
    b, s, d = x.shape
    k = a.shape[-1]
    return pl.pallas_call(
        _out_norm_kernel,
        grid=(b, s // tm),
        in_specs=[pl.BlockSpec((1, tm, k), lambda bi, i: (bi, i, 0)),
                  pl.BlockSpec((k, d), lambda bi, i: (0, 0)),
                  pl.BlockSpec((1, tm, d), lambda bi, i: (bi, i, 0)),
                  pl.BlockSpec((1, 1, d), lambda bi, i: (bi, 0, 0)),
                  pl.BlockSpec((1, d), lambda bi, i: (0, 0)),
                  pl.BlockSpec((1, d), lambda bi, i: (0, 0))],
        out_specs=pl.BlockSpec((1, tm, d), lambda bi, i: (bi, i, 0)),
        out_shape=jax.ShapeDtypeStruct((b, s, d), F32),
        compiler_params=_params("parallel", "parallel"),
        name="out_norm",
    )(a, w, x, gate, ln_g.reshape(1, d), ln_b.reshape(1, d))


FFN_HALO = 16


def _ffn_up_kernel(x_ref, xh_ref, sc_ref, sh_ref, wu_ref, wg_ref, cw_ref, cb_ref, o_ref, h_ref, u_ref):
    i = pl.program_id(1)
    tm = x_ref.shape[1]

    @pl.when(pl.program_id(2) == 0)
    def _():
        sc = 1.0 + sc_ref[0]
        sh = sh_ref[0]
        h_ref[0:FFN_HALO, :] = (xh_ref[0] * sc + sh).astype(BF16)
        h_ref[FFN_HALO:, :] = (x_ref[0] * sc + sh).astype(BF16)

    u = _dot(h_ref[...], wu_ref[...])
    keep = (i > 0).astype(F32)
    u_ref[0:FFN_HALO, :] = u[0:FFN_HALO] * keep
    u_ref[FFN_HALO:, :] = u[FFN_HALO:]
    g = _dot(h_ref[FFN_HALO:, :], wg_ref[...])
    cw = cw_ref[...]
    conv = (u_ref[pl.ds(FFN_HALO - 2, tm), :] * cw[0:1] + u_ref[pl.ds(FFN_HALO - 1, tm), :] * cw[1:2]
            + u_ref[pl.ds(FFN_HALO, tm), :] * cw[2:3] + cb_ref[...])
    o_ref[0] = (_silu(conv) * g).astype(o_ref.dtype)


def _ffn_up(x, scale, shift, w_up, conv_w, conv_b, tm=512, tf=1408):
    b, s, d = x.shape
    f = conv_w.shape[1]
    assert f % tf == 0 and s % tm == 0 and tm % FFN_HALO == 0
    nf = f // tf
    hb = tm // FFN_HALO
    return pl.pallas_call(
        _ffn_up_kernel,
        grid=(b, s // tm, nf),
        in_specs=[pl.BlockSpec((1, tm, d), lambda bi, i, j: (bi, i, 0)),
                  pl.BlockSpec((1, FFN_HALO, d), lambda bi, i, j: (bi, jnp.maximum(i * hb - 1, 0), 0)),
                  pl.BlockSpec((1, 1, d), lambda bi, i, j: (bi, 0, 0)),
                  pl.BlockSpec((1, 1, d), lambda bi, i, j: (bi, 0, 0)),
                  pl.BlockSpec((d, tf), lambda bi, i, j: (0, j)),
                  pl.BlockSpec((d, tf), lambda bi, i, j: (0, nf + j)),
                  pl.BlockSpec((3, tf), lambda bi, i, j: (0, j)),
                  pl.BlockSpec((1, tf), lambda bi, i, j: (0, j))],
        out_specs=pl.BlockSpec((1, tm, tf), lambda bi, i, j: (bi, i, j)),
        out_shape=jax.ShapeDtypeStruct((b, s, f), BF16),
        scratch_shapes=[pltpu.VMEM((tm + FFN_HALO, d), BF16), pltpu.VMEM((tm + FFN_HALO, tf), F32)],
        compiler_params=_params("parallel", "parallel", "arbitrary"),
        name="ffn_up",
    )(x, x, scale, shift, w_up, w_up, conv_w, conv_b.reshape(1, f))


def _gla_kernel(q_ref, k_ref, v_ref, r_ref, a_ref, wa_ref, ba_ref, ng_ref, o_ref, st_ref):
    tc = q_ref.shape[1]
    nchunk = tc // GLA_CHUNK

    @pl.when(pl.program_id(2) == 0)
    def _():
        st_ref[...] = jnp.zeros_like(st_ref)

    z = _dot(a_ref[0], wa_ref[...]) + ba_ref[...]
    log_a = (jnp.minimum(z, 0.0) - jnp.log1p(jnp.exp(-jnp.abs(z)))) / GLA_TAU
    row = lax.broadcasted_iota(jnp.int32, (tc, tc), 0)
    col = lax.broadcasted_iota(jnp.int32, (tc, tc), 1)
    same = (row // GLA_CHUNK) == (col // GLA_CHUNK)
    cum_m = jnp.where(same & (col <= row), 1.0, 0.0).astype(BF16)
    tot_m = jnp.where(same, 1.0, 0.0).astype(BF16)
    la_hi, la_lo = _split(log_a)
    bcum = _dot(cum_m, la_hi) + _dot(cum_m, la_lo)
    btot = _dot(tot_m, la_hi) + _dot(tot_m, la_lo)
    qf = q_ref[0].astype(F32) * (GLA_DK ** -0.5)
    kf = k_ref[0].astype(F32)
    q_dec = (qf * jnp.exp(bcum)).astype(BF16)
    k_inv = (kf * jnp.exp(-bcum)).astype(BF16)
    k_dec = (kf * jnp.exp(btot - bcum)).astype(BF16)
    dl = jnp.exp(btot)
    ci = lax.broadcasted_iota(jnp.int32, (GLA_CHUNK, GLA_CHUNK), 0)
    cj = lax.broadcasted_iota(jnp.int32, (GLA_CHUNK, GLA_CHUNK), 1)
    causal = cj <= ci
    ng = ng_ref[...]
    for c in range(nchunk):
        sl = slice(c * GLA_CHUNK, (c + 1) * GLA_CHUNK)
        qd, ki, kd = q_dec[sl], k_inv[sl], k_dec[sl]
        vc = v_ref[0, sl, :]
        att = jnp.where(causal, _dot_nt(qd, ki), 0.0)
        st = st_ref[...]
        o = _dot(att.astype(BF16), vc) + _dot_nt(qd, st.astype(BF16))
        st_ref[...] = st * dl[c * GLA_CHUNK:c * GLA_CHUNK + 1] + _dot_tn(vc, kd)
        o = o * lax.rsqrt(jnp.mean(o * o, axis=-1, keepdims=True) + LN_EPS) * ng
        o_ref[0, sl, :] = (o * _silu(r_ref[0, sl, :].astype(F32))).astype(o_ref.dtype)


def _gla_core(proj, w_a2, b_a, norm_g, tc=512):
    b, s, _ = proj.shape
    h, dk, dv = GLA_HEADS, GLA_DK, GLA_DV
    assert s % tc == 0
    wa = jnp.zeros((LANES, h * dk), F32).at[:GLA_GATE_RANK].set(w_a2).astype(BF16)
    kb, vb, rb, ab = (h * dk) // dk, (2 * h * dk) // dv, (2 * h * dk + h * dv) // dv, (2 * h * dk + 2 * h * dv) // LANES
    return pl.pallas_call(
        _gla_kernel,
        grid=(b, h, s // tc),
        in_specs=[pl.BlockSpec((1, tc, dk), lambda bi, hi, t: (bi, t, hi)),
                  pl.BlockSpec((1, tc, dk), lambda bi, hi, t: (bi, t, kb + hi)),
                  pl.BlockSpec((1, tc, dv), lambda bi, hi, t: (bi, t, vb + hi)),
                  pl.BlockSpec((1, tc, dv), lambda bi, hi, t: (bi, t, rb + hi)),
                  pl.BlockSpec((1, tc, LANES), lambda bi, hi, t: (bi, t, ab)),
                  pl.BlockSpec((LANES, dk), lambda bi, hi, t: (0, hi)),
                  pl.BlockSpec((1, dk), lambda bi, hi, t: (0, hi)),
                  pl.BlockSpec((1, dv), lambda bi, hi, t: (0, 0))],
        out_specs=pl.BlockSpec((1, tc, dv), lambda bi, hi, t: (bi, t, hi)),
        out_shape=jax.ShapeDtypeStruct((b, s, h * dv), BF16),
        scratch_shapes=[pltpu.VMEM((dv, dk), F32)],
        compiler_params=_params("parallel", "parallel", "arbitrary"),
        name="gla_core",
    )(proj, proj, proj, proj, proj, wa, b_a.reshape(1, h * dk), norm_g.reshape(1, dv))


def _pad_cols(w, n):
    return jnp.pad(w, ((0, 0), (0, n - w.shape[1])))


def _gla_mixer(x, scale, shift, w_in, w_a2, b_a, norm_g):
    n_in = 2 * GLA_HEADS * GLA_DK + 2 * GLA_HEADS * GLA_DV + LANES
    proj = _linear_in(x, scale, shift, _pad_cols(w_in, n_in).astype(BF16), tn=n_in)
    return _gla_core(proj, w_a2, b_a, norm_g)


def _compress_kernel(lo_ref, hi_ref, pe_ref, w1_ref, b1_ref, w2_ref, o_ref):
    half = lo_ref.shape[-1]
    pe = pe_ref[0]
    a_lo = (lo_ref[0, 0, 0].astype(F32) + pe[:, :half]).astype(BF16)
    a_hi = (hi_ref[0, 0, 0].astype(F32) + pe[:, half:]).astype(BF16)
    h1 = _dot(a_lo, w1_ref[0, :half, :]) + _dot(a_hi, w1_ref[0, half:, :]) + b1_ref[0]
    o_ref[0, 0, 0] = _dot(_silu(h1).astype(BF16), w2_ref[0])


def _compress(kv, pe, w1, b1, w2):
    two, b, s, _ = kv.shape
    g, dh = NSA_G, HEAD_DIM
    npc = s // CMP_STRIDE
    half = CMP_STRIDE * dh
    a = kv.reshape(two, b, npc, CMP_STRIDE, g, dh).transpose(0, 1, 4, 2, 3, 5).reshape(two, b, g, npc, half)
    a_hi = jnp.concatenate([a[:, :, :, 1:], jnp.zeros_like(a[:, :, :, :1])], axis=3)
    spec_a = pl.BlockSpec((1, 1, 1, npc, half), lambda c, bi, gi: (c, bi, gi, 0, 0))
    return pl.pallas_call(
        _compress_kernel,
        grid=(two, b, g),
        in_specs=[spec_a, spec_a,
                  pl.BlockSpec((1, 1, 2 * half), lambda c, bi, gi: (c, 0, 0)),
                  pl.BlockSpec((1, 2 * half, CMP_HIDDEN), lambda c, bi, gi: (c, 0, 0)),
                  pl.BlockSpec((1, 1, CMP_HIDDEN), lambda c, bi, gi: (c, 0, 0)),
                  pl.BlockSpec((1, CMP_HIDDEN, dh), lambda c, bi, gi: (c, 0, 0))],
        out_specs=pl.BlockSpec((1, 1, 1, npc, dh), lambda c, bi, gi: (c, bi, gi, 0, 0)),
        out_shape=jax.ShapeDtypeStruct((two, b, g, npc, dh), F32),
        compiler_params=_params("parallel", "parallel", "parallel"),
        name="nsa_compress",
    )(a, a_hi, pe.reshape(two, 1, 2 * half), w1.astype(BF16), b1.reshape(two, 1, CMP_HIDDEN), w2.astype(BF16))


def _stack_heads(q):
    return jnp.concatenate([q[:, r * HEAD_DIM:(r + 1) * HEAD_DIM] for r in range(NSA_R)], axis=0)


def _unstack_heads(o, rows):
    return jnp.concatenate([o[r * rows:(r + 1) * rows] for r in range(NSA_R)], axis=1)


CMP_TQ = 512
LOG2E = 1.4426950408889634


def _nsa_cmp_kernel(q_ref, fq_ref, kc_ref, vc_ref, nb_ref, pb_ref, o_ref, mb_ref):
    i = pl.program_id(2)
    npc = nb_ref.shape[2]
    nblk = mb_ref.shape[3]
    n_sub = CMP_TQ // SUB_Q
    blk = lax.broadcasted_iota(jnp.int32, (SUB_Q, nblk), 1)
    blk_f = blk.astype(F32)
    qrow = lax.broadcasted_iota(jnp.int32, (SUB_Q, nblk), 0)
    for j in range(n_sub):
        sub = i * n_sub + j
        start = pl.multiple_of(sub * (SUB_Q // CMP_STRIDE), 8)
        kwin = kc_ref[0, 0, pl.ds(start, npc), :].astype(BF16)
        vwin = vc_ref[0, 0, pl.ds(start, npc), :].astype(BF16)
        qs = _stack_heads(q_ref[0, j * SUB_Q:(j + 1) * SUB_Q, :]).astype(F32) * (HEAD_DIM ** -0.5 * LOG2E)
        qp = jnp.concatenate([qs, fq_ref[...]], axis=1).astype(BF16)
        s = _dot_nt(qp, kwin) + nb_ref[0]
        m = jnp.max(s, axis=1, keepdims=True)
        e = jnp.exp2(s - m)
        den = jnp.sum(e, axis=1, keepdims=True)
        p = e * jnp.where(m > 0.5 * NEG, 1.0 / den, 0.0)
        o = _dot(p.astype(BF16), vwin)
        o_ref[0, j * SUB_Q:(j + 1) * SUB_Q, :] = _unstack_heads(o, SUB_Q).astype(o_ref.dtype)

        pg = p[0:SUB_Q]
        for r in range(1, NSA_R):
            pg = pg + p[r * SUB_Q:(r + 1) * SUB_Q]
        imp = _dot_hi_exact_rhs(pg, pb_ref[pl.ds(start, npc), :].astype(BF16))

        t = sub * SUB_Q + qrow
        cur = t // SEL_LEN
        forced = (blk == 0) | (blk == cur) | (blk == cur - 1)
        work = jnp.where((blk * SEL_LEN <= t) & jnp.logical_not(forced), imp, NEG)
        free = work > 0.5 * NEG
        for _ in range(SEL_TOP - 3):
            mx = jnp.max(work, axis=1, keepdims=True)
            first = jnp.min(jnp.where(work == mx, blk_f, 1e9), axis=1, keepdims=True)
            work = jnp.where(blk_f == first, 2.0 * NEG, work)
        sel = forced | (free & (work < 1.5 * NEG))
        mb_ref[0, 0, j * SUB_Q:(j + 1) * SUB_Q, :] = jnp.where(sel, 0.0, NEG).astype(mb_ref.dtype)


def _nsa_cmp(proj, kcmp, vcmp, rel_table):
    b, s, _ = proj.shape
    g, dh = NSA_G, HEAD_DIM
    npc = s // CMP_STRIDE
    nblk = s // SEL_LEN
    front = npc - SUB_Q // CMP_STRIDE
    assert s % CMP_TQ == 0 and nblk >= SEL_TOP
    kpad = jnp.pad(kcmp, ((0, 0), (0, 0), (front, npc - front), (0, LANES - dh)))
    kpad = kpad.at[:, :, :front, dh].set(1.0)
    vpad = jnp.pad(vcmp, ((0, 0), (0, 0), (front, npc - front), (0, 0)))
    fq = jnp.zeros((NSA_R * SUB_Q, LANES - dh), F32).at[:, 0].set(NEG)
    qi = jnp.arange(SUB_Q)[:, None]
    cc = jnp.arange(npc)[None, :]
    dd = qi - (CMP_LEN - 1) + CMP_STRIDE * (front - cc)
    nb = jnp.where(dd[None] >= 0, _rel_bias(rel_table, dd) * LOG2E, NEG).reshape(g, NSA_R * SUB_Q, npc)
    diff = jnp.arange(2 * npc)[:, None] - front - (SEL_LEN // CMP_STRIDE) * jnp.arange(nblk)[None, :]
    pool = jnp.where((diff == -1) | (diff == 3), 1.0, jnp.where((diff >= 0) & (diff <= 2), 2.0, 0.0))
    return pl.pallas_call(
        _nsa_cmp_kernel,
        grid=(g, b, s // CMP_TQ),
        in_specs=[pl.BlockSpec((1, CMP_TQ, NSA_R * dh), lambda gi, bi, i: (bi, i, gi)),
                  pl.BlockSpec((NSA_R * SUB_Q, LANES - dh), lambda gi, bi, i: (0, 0)),
                  pl.BlockSpec((1, 1, 2 * npc, LANES), lambda gi, bi, i: (bi, gi, 0, 0)),
                  pl.BlockSpec((1, 1, 2 * npc, dh), lambda gi, bi, i: (bi, gi, 0, 0)),
                  pl.BlockSpec((1, NSA_R * SUB_Q, npc), lambda gi, bi, i: (gi, 0, 0)),
                  pl.BlockSpec((2 * npc, nblk), lambda gi, bi, i: (0, 0))],
        out_specs=[pl.BlockSpec((1, CMP_TQ, NSA_R * dh), lambda gi, bi, i: (bi, i, gi)),
                   pl.BlockSpec((1, 1, CMP_TQ, nblk), lambda gi, bi, i: (bi, gi, i, 0))],
        out_shape=[jax.ShapeDtypeStruct((b, s, N_HEADS * dh), BF16),
                   jax.ShapeDtypeStruct((b, g, s, nblk), BF16)],
        compiler_params=_params("parallel", "parallel", "parallel"),
        name="nsa_cmp_select",
    )(proj, fq, kpad, vpad, nb, pool.astype(F32))


SEL_T = 256
SEL_NEAR = -(-(FAR_DIST + SEL_T - 1) // SEL_T)


def _nsa_sel_kernel(q_ref, fq_ref, mb_ref, k_ref, v_ref, bt_ref, o_ref, qp_ref, m_ref, acc_ref):
    i = pl.program_id(2)
    rows = NSA_R * SEL_T
    fw = (k_ref.shape[3] - LANES)
    kh = fw // (SEL_T // SEL_LEN)
    qs = _stack_heads(q_ref[0]) * (HEAD_DIM ** -0.5)
    mb = mb_ref[0, 0]
    for half in range(2):
        mbh = jnp.concatenate([mb[:, half * fw:(half + 1) * fw]] * NSA_R, axis=0)
        qp_ref[half] = jnp.concatenate([qs, fq_ref[0], mbh], axis=1)
    m_ref[...] = jnp.full_like(m_ref, NEG)
    acc_ref[...] = jnp.zeros_like(acc_ref)

    def update(kt, half, bias):
        k0 = pl.multiple_of(kt * SEL_T, SEL_T)
        kk = k_ref[0, 0, pl.ds(k0, SEL_T), :]
        vv = v_ref[0, 0, pl.ds(k0, SEL_T), :]
        s = _dot_nt(qp_ref[half], kk)
        if bias is not None:
            s = s + bias
        m_prev = m_ref[...]
        m_new = jnp.maximum(m_prev, jnp.max(s, axis=1, keepdims=True))
        alpha = jnp.exp(m_prev - m_new)
        p = jnp.exp(s - jnp.concatenate([m_new] * (SEL_T // LANES), axis=1))
        acc_ref[...] = acc_ref[...] * alpha + _dot(p.astype(BF16), vv)
        m_ref[...] = m_new

    n_far = jnp.maximum(i - SEL_NEAR + 1, 0)

    def far_lo(kt, carry):
        update(kt, 0, None)
        return carry

    def far_hi(kt, carry):
        update(kt, 1, None)
        return carry

    lax.fori_loop(0, jnp.minimum(n_far, kh), far_lo, 0)
    lax.fori_loop(kh, jnp.maximum(n_far, kh), far_hi, 0)
    for e in range(SEL_NEAR - 1, -1, -1):
        @pl.when(i - e >= 0)
        def _(e=e):
            kt = i - e
            update(kt, jnp.where(kt < kh, 0, 1), bt_ref[0, :, e * SEL_T:(e + 1) * SEL_T])

    acc = acc_ref[...]
    o = acc[:, :HEAD_DIM] / acc[:, HEAD_DIM:HEAD_DIM + 1]
    o_ref[0] = _unstack_heads(o, SEL_T).astype(o_ref.dtype)


def _nsa_sel(proj, ks, vs, mb, rel_table):
    b, s, _ = proj.shape
    g, dh = NSA_G, HEAD_DIM
    nblk = s // SEL_LEN
    fw = nblk // 2
    rows = NSA_R * SEL_T
    pos = jnp.arange(s)
    onehot = (((pos // SEL_LEN) % fw)[:, None] == jnp.arange(fw)[None, :]).astype(BF16)
    ones2 = jnp.zeros((s, LANES - dh), BF16).at[:, :2].set(1.0)
    kp = jnp.concatenate([ks, jnp.broadcast_to(ones2, (b, g, s, LANES - dh)),
                          jnp.broadcast_to(onehot, (b, g, s, fw))], axis=-1)
    ones1 = jnp.zeros((s, LANES - dh), BF16).at[:, :1].set(1.0)
    vp = jnp.concatenate([vs, jnp.broadcast_to(ones1, (b, g, s, LANES - dh))], axis=-1)
    b_far = rel_table[REL_BUCKETS - 1].reshape(g, NSA_R)
    hi = b_far.astype(BF16)
    lo = (b_far - hi.astype(F32)).astype(BF16)
    fq = jnp.zeros((g, NSA_R, SEL_T, LANES - dh), BF16)
    fq = fq.at[:, :, :, 0].set(hi[:, :, None]).at[:, :, :, 1].set(lo[:, :, None]).reshape(g, rows, LANES - dh)
    qi = jnp.arange(SEL_T)[:, None]
    cc = jnp.arange(SEL_NEAR * SEL_T)[None, :]
    dd = (cc // SEL_T) * SEL_T + qi - (cc % SEL_T)
    bias = _rel_bias(rel_table, dd).reshape(g, NSA_R, SEL_T, -1) - b_far[:, :, None, None]
    bt = jnp.where(dd[None, None] >= 0, bias, NEG).reshape(g, rows, SEL_NEAR * SEL_T)
    return pl.pallas_call(
        _nsa_sel_kernel,
        grid=(g, b, s // SEL_T),
        in_specs=[pl.BlockSpec((1, SEL_T, NSA_R * dh), lambda gi, bi, i: (bi, i, gi)),
                  pl.BlockSpec((1, rows, LANES - dh), lambda gi, bi, i: (gi, 0, 0)),
                  pl.BlockSpec((1, 1, SEL_T, nblk), lambda gi, bi, i: (bi, gi, i, 0)),
                  pl.BlockSpec((1, 1, s, LANES + fw), lambda gi, bi, i: (bi, gi, 0, 0)),
                  pl.BlockSpec((1, 1, s, LANES), lambda gi, bi, i: (bi, gi, 0, 0)),
                  pl.BlockSpec((1, rows, SEL_NEAR * SEL_T), lambda gi, bi, i: (gi, 0, 0))],
        out_specs=pl.BlockSpec((1, SEL_T, NSA_R * dh), lambda gi, bi, i: (bi, i, gi)),
        out_shape=jax.ShapeDtypeStruct((b, s, N_HEADS * dh), BF16),
        scratch_shapes=[pltpu.VMEM((2, rows, LANES + fw), BF16), pltpu.VMEM((rows, LANES), F32),
                        pltpu.VMEM((rows, LANES), F32)],
        compiler_params=_params("parallel", "parallel", "parallel"),
        name="nsa_selected",
    )(proj, fq, mb, kp, vp, bt)


WIN_T = 512


def _softmax_rows(s):
    m = jnp.max(s, axis=1, keepdims=True)
    e = jnp.exp(s - m)
    den = jnp.sum(e, axis=1, keepdims=True)
    return e / den, m + jnp.log(den)


def _nsa_win_kernel(q_ref, kp_ref, kc_ref, vp_ref, vc_ref, bt_ref, o_ref):
    i = pl.program_id(2)
    back = WIN_T
    span = back + SUB_Q
    kcat = jnp.concatenate([kp_ref[0, 0], kc_ref[0, 0]], axis=0)
    vcat = jnp.concatenate([vp_ref[0, 0], vc_ref[0, 0]], axis=0)
    col = lax.broadcasted_iota(jnp.int32, (NSA_R * SUB_Q, span), 1)
    for j in range(WIN_T // SUB_Q):
        qs = _stack_heads(q_ref[0, j * SUB_Q:(j + 1) * SUB_Q, :]) * (HEAD_DIM ** -0.5)
        s = _dot_nt(qs, kcat[j * SUB_Q:j * SUB_Q + span]) + bt_ref[0]
        s = jnp.where((col >= back - j * SUB_Q) | (i > 0), s, NEG)
        p, _ = _softmax_rows(s)
        o = _dot(p.astype(BF16), vcat[j * SUB_Q:j * SUB_Q + span])
        o_ref[0, j * SUB_Q:(j + 1) * SUB_Q, :] = _unstack_heads(o, SUB_Q).astype(o_ref.dtype)


def _nsa_win(proj, kw, vw, rel_table):
    b, s, _ = proj.shape
    g, dh = NSA_G, HEAD_DIM
    span = WIN_T + SUB_Q
    qi = jnp.arange(SUB_Q)[:, None]
    cc = jnp.arange(span)[None, :]
    dd = qi + WIN_T - cc
    bias = _rel_bias(rel_table, dd)
    bt = jnp.where((dd[None] >= 0) & (dd[None] <= WIN_LEN - 1), bias, NEG).reshape(g, NSA_R * SUB_Q, span)
    prev = lambda gi, bi, i: (bi, gi, jnp.maximum(i - 1, 0), 0)
    cur = lambda gi, bi, i: (bi, gi, i, 0)
    return pl.pallas_call(
        _nsa_win_kernel,
        grid=(g, b, s // WIN_T),
        in_specs=[pl.BlockSpec((1, WIN_T, NSA_R * dh), lambda gi, bi, i: (bi, i, gi)),
                  pl.BlockSpec((1, 1, WIN_T, dh), prev), pl.BlockSpec((1, 1, WIN_T, dh), cur),
                  pl.BlockSpec((1, 1, WIN_T, dh), prev), pl.BlockSpec((1, 1, WIN_T, dh), cur),
                  pl.BlockSpec((1, NSA_R * SUB_Q, span), lambda gi, bi, i: (gi, 0, 0))],
        out_specs=pl.BlockSpec((1, WIN_T, NSA_R * dh), lambda gi, bi, i: (bi, i, gi)),
        out_shape=jax.ShapeDtypeStruct((b, s, N_HEADS * dh), BF16),
        compiler_params=_params("parallel", "parallel", "parallel"),
        name="nsa_window",
    )(proj, kw, kw, vw, vw, bt)


def _nsa_combine_kernel(oc_ref, os_ref, ow_ref, g_ref, e_ref, o_ref):
    d = o_ref.shape[2]
    sg = jax.nn.sigmoid(g_ref[0].astype(F32))
    ge = _dot_hi_exact_rhs(sg, e_ref[...])
    o = (ge[:, :d] * oc_ref[0].astype(F32) + ge[:, d:2 * d] * os_ref[0].astype(F32)
         + ge[:, 2 * d:] * ow_ref[0].astype(F32))
    o_ref[0] = o.astype(o_ref.dtype)


def _nsa_combine(o_cmp, o_sel, o_win, proj, gate_col, tm=512):
    b, s, d = o_cmp.shape
    c = jnp.arange(LANES)[:, None]
    n = jnp.arange(3 * d)[None, :]
    expand = (c == (n // d) * N_HEADS + (n % d) // HEAD_DIM).astype(BF16)
    spec = pl.BlockSpec((1, tm, d), lambda bi, i: (bi, i, 0))
    return pl.pallas_call(
        _nsa_combine_kernel,
        grid=(b, s // tm),
        in_specs=[spec, spec, spec,
                  pl.BlockSpec((1, tm, LANES), lambda bi, i: (bi, i, gate_col // LANES)),
                  pl.BlockSpec((LANES, 3 * d), lambda bi, i: (0, 0))],
        out_specs=spec,
        out_shape=jax.ShapeDtypeStruct((b, s, d), BF16),
        compiler_params=_params("parallel", "parallel"),
        name="nsa_combine",
    )(o_cmp, o_sel, o_win, proj, expand)


def _nsa_mixer(x, scale, shift, w_in, pe, w1, b1, w2, rel_table):
    b, s, _ = x.shape
    hd, gd = N_HEADS * HEAD_DIM, NSA_G * HEAD_DIM
    gate_col = hd + 6 * gd
    n_in = gate_col + LANES
    proj = _linear_in(x, scale, shift, _pad_cols(w_in, n_in).astype(BF16), tn=n_in)

    def group_major(t):
        return t.reshape(b, s, NSA_G, HEAD_DIM).transpose(0, 2, 1, 3)

    kc, vc, ks, vs, kw, vw = (proj[:, :, hd + n * gd: hd + (n + 1) * gd] for n in range(6))
    cmp = _compress(jnp.stack([kc, vc]), pe.reshape(2, -1), w1, b1, w2)
    o_cmp, mb = _nsa_cmp(proj, cmp[0], cmp[1], rel_table)
    o_sel = _nsa_sel(proj, group_major(ks), group_major(vs), mb, rel_table)
    o_win = _nsa_win(proj, group_major(kw), group_major(vw), rel_table)
    return _nsa_combine(o_cmp, o_sel, o_win, proj, gate_col)


DIL_BACK = 128


def _dil_kernel(q_ref, kp_ref, kc_ref, vp_ref, vc_ref, bt_ref, o_ref, l_ref):
    t = pl.program_id(3)
    tq = q_ref.shape[1]
    span = DIL_BACK + SUB_Q
    kcat = jnp.concatenate([kp_ref[0, tq - DIL_BACK:, :], kc_ref[0]], axis=0)
    vcat = jnp.concatenate([vp_ref[0, tq - DIL_BACK:, :], vc_ref[0]], axis=0)
    lane_q = lax.broadcasted_iota(jnp.int32, (SUB_Q, LANES), 1) // HEAD_DIM
    lane_k = lax.broadcasted_iota(jnp.int32, (span, LANES), 1) // HEAD_DIM
    col = lax.broadcasted_iota(jnp.int32, (SUB_Q, span), 1)
    for j in range(tq // SUB_Q):
        qj = q_ref[0, j * SUB_Q:(j + 1) * SUB_Q, :] * (HEAD_DIM ** -0.5)
        kk = kcat[j * SUB_Q:j * SUB_Q + span]
        vv = vcat[j * SUB_Q:j * SUB_Q + span]
        o = jnp.zeros((SUB_Q, LANES), F32)
        lse = jnp.zeros((SUB_Q, LANES), F32)
        for hh in range(LANES // HEAD_DIM):
            s = _dot_nt(jnp.where(lane_q == hh, qj, 0.0).astype(BF16), kk) + bt_ref[hh]
            if j == 0:
                s = jnp.where((col >= DIL_BACK) | (t > 0), s, NEG)
            p, lse_h = _softmax_rows(s)
            o = o + _dot(p.astype(BF16), jnp.where(lane_k == hh, vv, 0.0).astype(BF16))
            lse = jnp.where(lane_q == hh, lse_h, lse)
        o_ref[0, j * SUB_Q:(j + 1) * SUB_Q, :] = o.astype(o_ref.dtype)
        l_ref[0, j * SUB_Q:(j + 1) * SUB_Q, :] = lse


def _dil_group(view, window, dil, rel_table):
    b, l, n_all = view.shape
    hd = N_HEADS * HEAD_DIM
    assert window // dil == DIL_BACK and n_all == dil * 3 * hd
    tq = min(512, l)
    assert l % tq == 0 and tq >= DIL_BACK
    hpb = hd // LANES
    qi = jnp.arange(SUB_Q)[:, None]
    cc = jnp.arange(DIL_BACK + SUB_Q)[None, :]
    dd = qi + DIL_BACK - cc
    bt = jnp.where((dd[None] >= 0) & (dd[None] <= DIL_BACK), _rel_bias(rel_table, dd * dil), NEG)

    def spec(comp, prev):
        def index(bi, r, hp, t):
            return (bi, jnp.maximum(t - 1, 0) if prev else t, (r * 3 + comp) * hpb + hp)
        return pl.BlockSpec((1, tq, LANES), index)

    out_spec = pl.BlockSpec((1, tq, LANES), lambda bi, r, hp, t: (bi, t, r * hpb + hp))
    return pl.pallas_call(
        _dil_kernel,
        grid=(b, dil, hpb, l // tq),
        in_specs=[spec(0, False), spec(1, True), spec(1, False), spec(2, True), spec(2, False),
                  pl.BlockSpec((LANES // HEAD_DIM, SUB_Q, DIL_BACK + SUB_Q), lambda bi, r, hp, t: (hp, 0, 0))],
        out_specs=[out_spec, out_spec],
        out_shape=[jax.ShapeDtypeStruct((b, l, dil * hd), BF16), jax.ShapeDtypeStruct((b, l, dil * hd), F32)],
        compiler_params=_params("parallel", "parallel", "parallel", "parallel"),
        name=f"dilated_attention_{dil}",
    )(view, view, view, view, view, bt)


def _dil_merge_kernel(*refs, dils):
    n = len(dils)
    o_refs, l_refs, out_ref, scratch = refs[:n], refs[n:2 * n], refs[2 * n], refs[2 * n + 1:]
    tm, d = out_ref.shape[1], out_ref.shape[2]

    def token_major(ref, dil, buf):
        if dil == 1:
            return ref[0].astype(F32)
        for r in range(dil):
            for c in range(d // LANES):
                buf[c, pl.ds(r, tm // dil, stride=dil), :] = (
                    ref[0, :, r * d + c * LANES:r * d + (c + 1) * LANES].astype(F32))
        return jnp.concatenate([buf[c] for c in range(d // LANES)], axis=1)

    outs = [token_major(o_refs[g], dils[g], scratch[2 * g]) for g in range(n)]
    lses = [token_major(l_refs[g], dils[g], scratch[2 * g + 1]) for g in range(n)]
    m = functools.reduce(jnp.maximum, lses)
    es = [jnp.exp(l - m) for l in lses]
    inv = 1.0 / functools.reduce(jnp.add, es)
    out_ref[0] = functools.reduce(jnp.add, [e * inv * o for e, o in zip(es, outs)]).astype(out_ref.dtype)


def _dil_merge(outs, lses, dils, tm=512):
    b = outs[0].shape[0]
    d = N_HEADS * HEAD_DIM
    s = outs[0].shape[1] * dils[0]
    specs = [pl.BlockSpec((1, tm // dil, dil * d), lambda bi, i: (bi, i, 0)) for dil in dils]
    return pl.pallas_call(
        functools.partial(_dil_merge_kernel, dils=tuple(dils)),
        grid=(b, s // tm),
        in_specs=specs + specs,
        out_specs=pl.BlockSpec((1, tm, d), lambda bi, i: (bi, i, 0)),
        out_shape=jax.ShapeDtypeStruct((b, s, d), BF16),
        scratch_shapes=[pltpu.VMEM((d // LANES, tm, LANES), F32) for _ in range(2 * len(dils))],
        compiler_params=_params("parallel", "parallel"),
        name="dilated_merge",
    )(*outs, *lses)


def _dil_mixer(x, scale, shift, w_in, rel_table):
    hd = N_HEADS * HEAD_DIM
    w = w_in.astype(BF16)
    outs, lses, dils = [], [], []
    for gi, (window, dil) in enumerate(DIL_GROUPS):
        view = _linear_in(x, scale, shift, w[:, gi * 3 * hd:(gi + 1) * 3 * hd], tn=3 * hd, dil=dil)
        o, lse = _dil_group(view, window, dil, rel_table)
        outs.append(o)
        lses.append(lse)
        dils.append(dil)
    return _dil_merge(outs, lses, dils)


def kernel(x, c, rel_table, mod_w, mod_b, ln_g, ln_b, gla_w_in, gla_w_a2, gla_b_a, gla_norm_g, gla_w_o,
           nsa_w_in, nsa_cmp_pe, nsa_cmp_w1, nsa_cmp_b1, nsa_cmp_w2, nsa_w_o, dil_w_in, dil_w_o,
           ffn_w_up, ffn_conv_w, ffn_conv_b, ffn_w_down):
    shift, scale, gate = _modulation(c, mod_w, mod_b)
    for i in range(DEPTH):
        kind, j = i % N_MIXERS, i // N_MIXERS
        m = 2 * i
        if kind == 0:
            a = _gla_mixer(x, scale[m], shift[m], gla_w_in[j], gla_w_a2[j], gla_b_a[j], gla_norm_g[j])
            w_o = gla_w_o[j]
        elif kind == 1:
            a = _nsa_mixer(x, scale[m], shift[m], nsa_w_in[j], nsa_cmp_pe[j], nsa_cmp_w1[j], nsa_cmp_b1[j],
                           nsa_cmp_w2[j], rel_table)
            w_o = nsa_w_o[j]
        else:
            a = _dil_mixer(x, scale[m], shift[m], dil_w_in[j], rel_table)
            w_o = dil_w_o[j]
        x = _out_norm(a, w_o.astype(BF16), x, gate[m], ln_g[i, 0], ln_b[i, 0])
        a = _ffn_up(x, scale[m + 1], shift[m + 1], ffn_w_up[i].astype(BF16), ffn_conv_w[i], ffn_conv_b[i])
        x = _out_norm(a, ffn_w_down[i].astype(BF16), x, gate[m + 1], ln_g[i, 1], ln_b[i, 1])
    return x
```

```python
import functools
import math

import jax
import jax.numpy as jnp
import numpy as np
from jax import lax
from jax.experimental import pallas as pl
from jax.experimental.pallas import tpu as pltpu

F32 = jnp.float32
BF16 = jnp.bfloat16

D_MODEL = 1024
DEPTH = 4
N_MIXERS = 3
HEAD_DIM = 64
N_HEADS = D_MODEL // HEAD_DIM
REL_BUCKETS = 32
REL_MAX_DIST = 2048
LN_EPS = 1e-5
GLA_HEADS = 4
GLA_DK = D_MODEL // 2 // GLA_HEADS
GLA_DV = D_MODEL // GLA_HEADS
GLA_GATE_RANK = 16
GLA_TAU = 16.0
GLA_CHUNK = 64
NSA_G = 4
NSA_R = N_HEADS // NSA_G
CMP_LEN = 32
CMP_STRIDE = 16
CMP_HIDDEN = 256
SEL_LEN = 64
SEL_TOP = 16
WIN_LEN = 512
DIL_GROUPS = ((128, 1), (512, 4), (2048, 16))
D_FF = 2816
DN_ALPHA = (2 * DEPTH) ** 0.25

NEG = -1e30
VMEM_LIMIT = 56 * 1024 * 1024
LANES = 128
SUB_Q = 128


def _params(*sem):
    return pltpu.CompilerParams(dimension_semantics=sem, vmem_limit_bytes=VMEM_LIMIT)


def _dot(a, b):
    return jnp.dot(a, b, preferred_element_type=F32)


def _dot_nt(a, b):
    return lax.dot_general(a, b, (((1,), (1,)), ((), ())), preferred_element_type=F32)


def _dot_tn(a, b):
    return lax.dot_general(a, b, (((0,), (0,)), ((), ())), preferred_element_type=F32)


def _split(a):
    hi = a.astype(BF16)
    lo = (a - hi.astype(F32)).astype(BF16)
    return hi, lo


def _dot_hi_exact_rhs(a, b_bf16):
    hi, lo = _split(a)
    return _dot(hi, b_bf16) + _dot(lo, b_bf16)


def _silu(x):
    return x * jax.nn.sigmoid(x)


def _bucket_starts():
    n = np.arange(2 * REL_MAX_DIST)
    exact = REL_BUCKETS // 2
    logn = np.log(np.maximum(n, 1).astype(np.float32) / np.float32(exact))
    large = exact + (logn / np.float32(math.log(REL_MAX_DIST / exact)) * np.float32(REL_BUCKETS - exact)).astype(np.int32)
    bucket = np.where(n < exact, n, np.minimum(large, REL_BUCKETS - 1))
    assert np.all(np.diff(bucket) >= 0) and bucket[-1] == REL_BUCKETS - 1
    return [int(np.argmax(bucket >= k)) for k in range(REL_BUCKETS)]


BUCKET_START = _bucket_starts()
FAR_DIST = BUCKET_START[-1]


def _rel_bias(rel_table, dist):
    shape = (rel_table.shape[1],) + (1,) * dist.ndim
    out = jnp.broadcast_to(rel_table[0].reshape(shape), (rel_table.shape[1],) + dist.shape)
    for k in range(1, REL_BUCKETS):
        out = jnp.where(dist[None] >= BUCKET_START[k], rel_table[k].reshape(shape), out)
    return out


def _mod_kernel(c_ref, w_ref, b_ref, o_ref):
    cs = _silu(c_ref[...])
    ch, cl = _split(cs)
    wh, wl = _split(w_ref[0])
    o_ref[0] = _dot(ch, wh) + _dot(ch, wl) + _dot(cl, wh) + b_ref[0]


def _modulation(c, mod_w, mod_b):
    b, d = c.shape
    n_sub = mod_w.shape[0] * mod_w.shape[1]
    rows = 8
    cp = jnp.zeros((rows, d), F32).at[:b].set(c)
    w = mod_w.reshape(n_sub, d, 3 * d)
    bias = mod_b.reshape(n_sub, 1, 3 * d)
    tn = d
    out = pl.pallas_call(
        _mod_kernel,
        grid=(n_sub, 3 * d // tn),
        in_specs=[pl.BlockSpec((rows, d), lambda i, j: (0, 0)),
                  pl.BlockSpec((1, d, tn), lambda i, j: (i, 0, j)),
                  pl.BlockSpec((1, 1, tn), lambda i, j: (i, 0, j))],
        out_specs=pl.BlockSpec((1, rows, tn), lambda i, j: (i, 0, j)),
        out_shape=jax.ShapeDtypeStruct((n_sub, rows, 3 * d), F32),
        compiler_params=_params("parallel", "parallel"),
        name="modulation",
    )(cp, w, bias)
    out = out[:, :b]
    shift, scale, gate = jnp.split(out, 3, axis=-1)
    return shift[:, :, None, :], scale[:, :, None, :], gate[:, :, None, :]


def _linear_in_kernel(x_ref, sc_ref, sh_ref, w_ref, o_ref, *acc, dil):
    h = (x_ref[0] * (1.0 + sc_ref[0]) + sh_ref[0]).astype(BF16)
    res = _dot(h, w_ref[...])
    if dil == 1:
        o_ref[0] = res.astype(o_ref.dtype)
    else:
        acc_ref, = acc
        rows, n = res.shape[0] // dil, res.shape[1]
        for c in range(n // LANES):
            acc_ref[c] = res[:, c * LANES:(c + 1) * LANES]
        for r in range(dil):
            for c in range(n // LANES):
                o_ref[0, :, r * n + c * LANES:r * n + (c + 1) * LANES] = (
                    acc_ref[c, pl.ds(r, rows, stride=dil), :].astype(o_ref.dtype))


def _linear_in(x, scale, shift, w, tn, tm=512, dil=1):
    b, s, d = x.shape
    n = w.shape[1]
    assert n % tn == 0 and s % tm == 0 and tm % (16 * dil) == 0 and (dil == 1 or n == tn)
    return pl.pallas_call(
        functools.partial(_linear_in_kernel, dil=dil),
        grid=(n // tn, b, s // tm),
        in_specs=[pl.BlockSpec((1, tm, d), lambda j, bi, i: (bi, i, 0)),
                  pl.BlockSpec((1, 1, d), lambda j, bi, i: (bi, 0, 0)),
                  pl.BlockSpec((1, 1, d), lambda j, bi, i: (bi, 0, 0)),
                  pl.BlockSpec((d, tn), lambda j, bi, i: (0, j))],
        out_specs=pl.BlockSpec((1, tm // dil, dil * tn), lambda j, bi, i: (bi, i, j)),
        out_shape=jax.ShapeDtypeStruct((b, s // dil, dil * n), BF16),
        scratch_shapes=[] if dil == 1 else [pltpu.VMEM((tn // LANES, tm, LANES), F32)],
        compiler_params=_params("parallel", "parallel", "parallel"),
        name="linear_in",
    )(x, scale, shift, w)


def _out_norm_kernel(a_ref, w_ref, x_ref, gate_ref, g_ref, b_ref, o_ref):
    y = _dot(a_ref[0], w_ref[...])
    z = DN_ALPHA * x_ref[0] + (1.0 + gate_ref[0]) * y
    mu = jnp.mean(z, axis=-1, keepdims=True)
    zc = z - mu
    var = jnp.mean(zc * zc, axis=-1, keepdims=True)
    o_ref[0] = zc * lax.rsqrt(var + LN_EPS) * g_ref[...] + b_ref[...]


def _out_norm(a, w, x, gate, ln_g, ln_b, tm=512):
    b, s, d = x.shape
    k = a.shape[-1]
    return pl.pallas_call(
        _out_norm_kernel,
        grid=(b, s // tm),
        in_specs=[pl.BlockSpec((1, tm, k), lambda bi, i: (bi, i, 0)),
                  pl.BlockSpec((k, d), lambda bi, i: (0, 0)),
                  pl.BlockSpec((1, tm, d), lambda bi, i: (bi, i, 0)),
                  pl.BlockSpec((1, 1, d), lambda bi, i: (bi, 0, 0)),
                  pl.BlockSpec((1, d), lambda bi, i: (0, 0)),
                  pl.BlockSpec((1, d), lambda bi, i: (0, 0))],
        out_specs=pl.BlockSpec((1, tm, d), lambda bi, i: (bi, i, 0)),
        out_shape=jax.ShapeDtypeStruct((b, s, d), F32),
        compiler_params=_params("parallel", "parallel"),
        name="out_norm",
    )(a, w, x, gate, ln_g.reshape(1, d), ln_b.reshape(1, d))


FFN_HALO = 16


def _ffn_up_kernel(x_ref, xh_ref, sc_ref, sh_ref, wu_ref, wg_ref, cw_ref, cb_ref, o_ref, h_ref, u_ref):
    i = pl.program_id(1)
    tm = x_ref.shape[1]

    @pl.when(pl.program_id(2) == 0)
    def _():
        sc = 1.0 + sc_ref[0]
        sh = sh_ref[0]
        h_ref[0:FFN_HALO, :] = (xh_ref[0] * sc + sh).astype(BF16)
        h_ref[FFN_HALO:, :] = (x_ref[0] * sc + sh).astype(BF16)

    u = _dot(h_ref[...], wu_ref[...])
    keep = (i > 0).astype(F32)
    u_ref[0:FFN_HALO, :] = u[0:FFN_HALO] * keep
    u_ref[FFN_HALO:, :] = u[FFN_HALO:]
    g = _dot(h_ref[FFN_HALO:, :], wg_ref[...])
    cw = cw_ref[...]
    conv = (u_ref[pl.ds(FFN_HALO - 2, tm), :] * cw[0:1] + u_ref[pl.ds(FFN_HALO - 1, tm), :] * cw[1:2]
            + u_ref[pl.ds(FFN_HALO, tm), :] * cw[2:3] + cb_ref[...])
    o_ref[0] = (_silu(conv) * g).astype(o_ref.dtype)


def _ffn_up(x, scale, shift, w_up, conv_w, conv_b, tm=512, tf=1408):
    b, s, d = x.shape
    f = conv_w.shape[1]
    assert f % tf == 0 and s % tm == 0 and tm % FFN_HALO == 0
    nf = f // tf
    hb = tm // FFN_HALO
    return pl.pallas_call(
        _ffn_up_kernel,
        grid=(b, s // tm, nf),
        in_specs=[pl.BlockSpec((1, tm, d), lambda bi, i, j: (bi, i, 0)),
                  pl.BlockSpec((1, FFN_HALO, d), lambda bi, i, j: (bi, jnp.maximum(i * hb - 1, 0), 0)),
                  pl.BlockSpec((1, 1, d), lambda bi, i, j: (bi, 0, 0)),
                  pl.BlockSpec((1, 1, d), lambda bi, i, j: (bi, 0, 0)),
                  pl.BlockSpec((d, tf), lambda bi, i, j: (0, j)),
                  pl.BlockSpec((d, tf), lambda bi, i, j: (0, nf + j)),
                  pl.BlockSpec((3, tf), lambda bi, i, j: (0, j)),
                  pl.BlockSpec((1, tf), lambda bi, i, j: (0, j))],
        out_specs=pl.BlockSpec((1, tm, tf), lambda bi, i, j: (bi, i, j)),
        out_shape=jax.ShapeDtypeStruct((b, s, f), BF16),
        scratch_shapes=[pltpu.VMEM((tm + FFN_HALO, d), BF16), pltpu.VMEM((tm + FFN_HALO, tf), F32)],
        compiler_params=_params("parallel", "parallel", "arbitrary"),
        name="ffn_up",
    )(x, x, scale, shift, w_up, w_up, conv_w, conv_b.reshape(1, f))


def _gla_kernel(q_ref, k_ref, v_ref, r_ref, a_ref, wa_ref, ba_ref, ng_ref, o_ref, st_ref):
    tc = q_ref.shape[1]
    nchunk = tc // GLA_CHUNK

    @pl.when(pl.program_id(2) == 0)
    def _():
        st_ref[...] = jnp.zeros_like(st_ref)

    z = _dot(a_ref[0], wa_ref[...]) + ba_ref[...]
    log_a = (jnp.minimum(z, 0.0) - jnp.log1p(jnp.exp(-jnp.abs(z)))) / GLA_TAU
    row = lax.broadcasted_iota(jnp.int32, (tc, tc), 0)
    col = lax.broadcasted_iota(jnp.int32, (tc, tc), 1)
    same = (row // GLA_CHUNK) == (col // GLA_CHUNK)
    cum_m = jnp.where(same & (col <= row), 1.0, 0.0).astype(BF16)
    tot_m = jnp.where(same, 1.0, 0.0).astype(BF16)
    la_hi, la_lo = _split(log_a)
    bcum = _dot(cum_m, la_hi) + _dot(cum_m, la_lo)
    btot = _dot(tot_m, la_hi) + _dot(tot_m, la_lo)
    qf = q_ref[0].astype(F32) * (GLA_DK ** -0.5)
    kf = k_ref[0].astype(F32)
    q_dec = (qf * jnp.exp(bcum)).astype(BF16)
    k_inv = (kf * jnp.exp(-bcum)).astype(BF16)
    k_dec = (kf * jnp.exp(btot - bcum)).astype(BF16)
    dl = jnp.exp(btot)
    ci = lax.broadcasted_iota(jnp.int32, (GLA_CHUNK, GLA_CHUNK), 0)
    cj = lax.broadcasted_iota(jnp.int32, (GLA_CHUNK, GLA_CHUNK), 1)
    causal = cj <= ci
    ng = ng_ref[...]
    for c in range(nchunk):
        sl = slice(c * GLA_CHUNK, (c + 1) * GLA_CHUNK)
        qd, ki, kd = q_dec[sl], k_inv[sl], k_dec[sl]
        vc = v_ref[0, sl, :]
        att = jnp.where(causal, _dot_nt(qd, ki), 0.0)
        st = st_ref[...]
        o = _dot(att.astype(BF16), vc) + _dot_nt(qd, st.astype(BF16))
        st_ref[...] = st * dl[c * GLA_CHUNK:c * GLA_CHUNK + 1] + _dot_tn(vc, kd)
        o = o * lax.rsqrt(jnp.mean(o * o, axis=-1, keepdims=True) + LN_EPS) * ng
        o_ref[0, sl, :] = (o * _silu(r_ref[0, sl, :].astype(F32))).astype(o_ref.dtype)


def _gla_core(proj, w_a2, b_a, norm_g, tc=512):
    b, s, _ = proj.shape
    h, dk, dv = GLA_HEADS, GLA_DK, GLA_DV
    assert s % tc == 0
    wa = jnp.zeros((LANES, h * dk), F32).at[:GLA_GATE_RANK].set(w_a2).astype(BF16)
    kb, vb, rb, ab = (h * dk) // dk, (2 * h * dk) // dv, (2 * h * dk + h * dv) // dv, (2 * h * dk + 2 * h * dv) // LANES
    return pl.pallas_call(
        _gla_kernel,
        grid=(b, h, s // tc),
        in_specs=[pl.BlockSpec((1, tc, dk), lambda bi, hi, t: (bi, t, hi)),
                  pl.BlockSpec((1, tc, dk), lambda bi, hi, t: (bi, t, kb + hi)),
                  pl.BlockSpec((1, tc, dv), lambda bi, hi, t: (bi, t, vb + hi)),
                  pl.BlockSpec((1, tc, dv), lambda bi, hi, t: (bi, t, rb + hi)),
                  pl.BlockSpec((1, tc, LANES), lambda bi, hi, t: (bi, t, ab)),
                  pl.BlockSpec((LANES, dk), lambda bi, hi, t: (0, hi)),
                  pl.BlockSpec((1, dk), lambda bi, hi, t: (0, hi)),
                  pl.BlockSpec((1, dv), lambda bi, hi, t: (0, 0))],
        out_specs=pl.BlockSpec((1, tc, dv), lambda bi, hi, t: (bi, t, hi)),
        out_shape=jax.ShapeDtypeStruct((b, s, h * dv), BF16),
        scratch_shapes=[pltpu.VMEM((dv, dk), F32)],
        compiler_params=_params("parallel", "parallel", "arbitrary"),
        name="gla_core",
    )(proj, proj, proj, proj, proj, wa, b_a.reshape(1, h * dk), norm_g.reshape(1, dv))


def _pad_cols(w, n):
    return jnp.pad(w, ((0, 0), (0, n - w.shape[1])))


def _gla_mixer(x, scale, shift, w_in, w_a2, b_a, norm_g):
    n_in = 2 * GLA_HEADS * GLA_DK + 2 * GLA_HEADS * GLA_DV + LANES
    proj = _linear_in(x, scale, shift, _pad_cols(w_in, n_in).astype(BF16), tn=n_in)
    return _gla_core(proj, w_a2, b_a, norm_g)


def _compress_kernel(lo_ref, hi_ref, pe_ref, w1_ref, b1_ref, w2_ref, o_ref):
    half = lo_ref.shape[-1]
    pe = pe_ref[0]
    a_lo = (lo_ref[0, 0, 0].astype(F32) + pe[:, :half]).astype(BF16)
    a_hi = (hi_ref[0, 0, 0].astype(F32) + pe[:, half:]).astype(BF16)
    h1 = _dot(a_lo, w1_ref[0, :half, :]) + _dot(a_hi, w1_ref[0, half:, :]) + b1_ref[0]
    o_ref[0, 0, 0] = _dot(_silu(h1).astype(BF16), w2_ref[0])


def _compress(kv, pe, w1, b1, w2):
    two, b, s, _ = kv.shape
    g, dh = NSA_G, HEAD_DIM
    npc = s // CMP_STRIDE
    half = CMP_STRIDE * dh
    a = kv.reshape(two, b, npc, CMP_STRIDE, g, dh).transpose(0, 1, 4, 2, 3, 5).reshape(two, b, g, npc, half)
    a_hi = jnp.concatenate([a[:, :, :, 1:], jnp.zeros_like(a[:, :, :, :1])], axis=3)
    spec_a = pl.BlockSpec((1, 1, 1, npc, half), lambda c, bi, gi: (c, bi, gi, 0, 0))
    return pl.pallas_call(
        _compress_kernel,
        grid=(two, b, g),
        in_specs=[spec_a, spec_a,
                  pl.BlockSpec((1, 1, 2 * half), lambda c, bi, gi: (c, 0, 0)),
                  pl.BlockSpec((1, 2 * half, CMP_HIDDEN), lambda c, bi, gi: (c, 0, 0)),
                  pl.BlockSpec((1, 1, CMP_HIDDEN), lambda c, bi, gi: (c, 0, 0)),
                  pl.BlockSpec((1, CMP_HIDDEN, dh), lambda c, bi, gi: (c, 0, 0))],
        out_specs=pl.BlockSpec((1, 1, 1, npc, dh), lambda c, bi, gi: (c, bi, gi, 0, 0)),
        out_shape=jax.ShapeDtypeStruct((two, b, g, npc, dh), F32),
        compiler_params=_params("parallel", "parallel", "parallel"),
        name="nsa_compress",
    )(a, a_hi, pe.reshape(two, 1, 2 * half), w1.astype(BF16), b1.reshape(two, 1, CMP_HIDDEN), w2.astype(BF16))


def _stack_heads(q):
    return jnp.concatenate([q[:, r * HEAD_DIM:(r + 1) * HEAD_DIM] for r in range(NSA_R)], axis=0)


def _unstack_heads(o, rows):
    return jnp.concatenate([o[r * rows:(r + 1) * rows] for r in range(NSA_R)], axis=1)


CMP_TQ = 512
LOG2E = 1.4426950408889634


def _nsa_cmp_kernel(q_ref, fq_ref, kc_ref, vc_ref, nb_ref, pb_ref, o_ref, mb_ref):
    i = pl.program_id(2)
    npc = nb_ref.shape[2]
    nblk = mb_ref.shape[3]
    n_sub = CMP_TQ // SUB_Q
    blk = lax.broadcasted_iota(jnp.int32, (SUB_Q, nblk), 1)
    blk_f = blk.astype(F32)
    qrow = lax.broadcasted_iota(jnp.int32, (SUB_Q, nblk), 0)
    for j in range(n_sub):
        sub = i * n_sub + j
        start = pl.multiple_of(sub * (SUB_Q // CMP_STRIDE), 8)
        kwin = kc_ref[0, 0, pl.ds(start, npc), :].astype(BF16)
        vwin = vc_ref[0, 0, pl.ds(start, npc), :].astype(BF16)
        qs = _stack_heads(q_ref[0, j * SUB_Q:(j + 1) * SUB_Q, :]).astype(F32) * (HEAD_DIM ** -0.5 * LOG2E)
        qp = jnp.concatenate([qs, fq_ref[...]], axis=1).astype(BF16)
        s = _dot_nt(qp, kwin) + nb_ref[0]
        m = jnp.max(s, axis=1, keepdims=True)
        e = jnp.exp2(s - m)
        den = jnp.sum(e, axis=1, keepdims=True)
        p = e * jnp.where(m > 0.5 * NEG, 1.0 / den, 0.0)
        o = _dot(p.astype(BF16), vwin)
        o_ref[0, j * SUB_Q:(j + 1) * SUB_Q, :] = _unstack_heads(o, SUB_Q).astype(o_ref.dtype)

        pg = p[0:SUB_Q]
        for r in range(1, NSA_R):
            pg = pg + p[r * SUB_Q:(r + 1) * SUB_Q]
        imp = _dot_hi_exact_rhs(pg, pb_ref[pl.ds(start, npc), :].astype(BF16))

        t = sub * SUB_Q + qrow
        cur = t // SEL_LEN
        forced = (blk == 0) | (blk == cur) | (blk == cur - 1)
        work = jnp.where((blk * SEL_LEN <= t) & jnp.logical_not(forced), imp, NEG)
        free = work > 0.5 * NEG
        for _ in range(SEL_TOP - 3):
            mx = jnp.max(work, axis=1, keepdims=True)
            first = jnp.min(jnp.where(work == mx, blk_f, 1e9), axis=1, keepdims=True)
            work = jnp.where(blk_f == first, 2.0 * NEG, work)
        sel = forced | (free & (work < 1.5 * NEG))
        mb_ref[0, 0, j * SUB_Q:(j + 1) * SUB_Q, :] = jnp.where(sel, 0.0, NEG).astype(mb_ref.dtype)


def _nsa_cmp(proj, kcmp, vcmp, rel_table):
    b, s, _ = proj.shape
    g, dh = NSA_G, HEAD_DIM
    npc = s // CMP_STRIDE
    nblk = s // SEL_LEN
    front = npc - SUB_Q // CMP_STRIDE
    assert s % CMP_TQ == 0 and nblk >= SEL_TOP
    kpad = jnp.pad(kcmp, ((0, 0), (0, 0), (front, npc - front), (0, LANES - dh)))
    kpad = kpad.at[:, :, :front, dh].set(1.0)
    vpad = jnp.pad(vcmp, ((0, 0), (0, 0), (front, npc - front), (0, 0)))
    fq = jnp.zeros((NSA_R * SUB_Q, LANES - dh), F32).at[:, 0].set(NEG)
    qi = jnp.arange(SUB_Q)[:, None]
    cc = jnp.arange(npc)[None, :]
    dd = qi - (CMP_LEN - 1) + CMP_STRIDE * (front - cc)
    nb = jnp.where(dd[None] >= 0, _rel_bias(rel_table, dd) * LOG2E, NEG).reshape(g, NSA_R * SUB_Q, npc)
    diff = jnp.arange(2 * npc)[:, None] - front - (SEL_LEN // CMP_STRIDE) * jnp.arange(nblk)[None, :]
    pool = jnp.where((diff == -1) | (diff == 3), 1.0, jnp.where((diff >= 0) & (diff <= 2), 2.0, 0.0))
    return pl.pallas_call(
        _nsa_cmp_kernel,
        grid=(g, b, s // CMP_TQ),
        in_specs=[pl.BlockSpec((1, CMP_TQ, NSA_R * dh), lambda gi, bi, i: (bi, i, gi)),
                  pl.BlockSpec((NSA_R * SUB_Q, LANES - dh), lambda gi, bi, i: (0, 0)),
                  pl.BlockSpec((1, 1, 2 * npc, LANES), lambda gi, bi, i: (bi, gi, 0, 0)),
                  pl.BlockSpec((1, 1, 2 * npc, dh), lambda gi, bi, i: (bi, gi, 0, 0)),
                  pl.BlockSpec((1, NSA_R * SUB_Q, npc), lambda gi, bi, i: (gi, 0, 0)),
                  pl.BlockSpec((2 * npc, nblk), lambda gi, bi, i: (0, 0))],
        out_specs=[pl.BlockSpec((1, CMP_TQ, NSA_R * dh), lambda gi, bi, i: (bi, i, gi)),
                   pl.BlockSpec((1, 1, CMP_TQ, nblk), lambda gi, bi, i: (bi, gi, i, 0))],
        out_shape=[jax.ShapeDtypeStruct((b, s, N_HEADS * dh), BF16),
                   jax.ShapeDtypeStruct((b, g, s, nblk), BF16)],
        compiler_params=_params("parallel", "parallel", "parallel"),
        name="nsa_cmp_select",
    )(proj, fq, kpad, vpad, nb, pool.astype(F32))


SEL_T = 256
SEL_NEAR = -(-(FAR_DIST + SEL_T - 1) // SEL_T)


def _nsa_sel_kernel(q_ref, fq_ref, mb_ref, k_ref, v_ref, bt_ref, o_ref, qp_ref, m_ref, acc_ref):
    i = pl.program_id(2)
    rows = NSA_R * SEL_T
    fw = (k_ref.shape[3] - LANES)
    kh = fw // (SEL_T // SEL_LEN)
    qs = (_stack_heads(q_ref[0]).astype(F32) * (HEAD_DIM ** -0.5 * LOG2E)).astype(BF16)
    mb = mb_ref[0, 0]
    for half in range(2):
        mbh = jnp.concatenate([mb[:, half * fw:(half + 1) * fw]] * NSA_R, axis=0)
        qp_ref[half] = jnp.concatenate([qs, fq_ref[0], mbh], axis=1)
    m_ref[...] = jnp.full_like(m_ref, NEG)
    acc_ref[...] = jnp.zeros_like(acc_ref)

    def update(kt, n_tile, half, bias):
        k0 = pl.multiple_of(kt * SEL_T, SEL_T)
        kk = k_ref[0, 0, pl.ds(k0, n_tile * SEL_T), :]
        vv = v_ref[0, 0, pl.ds(k0, n_tile * SEL_T), :]
        s = _dot_nt(qp_ref[half], kk)
        if bias is not None:
            s = s + bias
        m_prev = m_ref[...]
        m_new = jnp.maximum(m_prev, jnp.max(s, axis=1, keepdims=True))
        alpha = jnp.exp2(m_prev - m_new)
        p = jnp.exp2(s - jnp.concatenate([m_new] * (n_tile * SEL_T // LANES), axis=1))
        acc_ref[...] = acc_ref[...] * alpha + _dot(p.astype(BF16), vv)
        m_ref[...] = m_new

    def far_region(lo, hi, half):
        n = jnp.maximum(hi - lo, 0)

        def quad(j, carry):
            update(lo + 4 * j, 4, half, None)
            return carry

        lax.fori_loop(0, lax.shift_right_logical(n, 2), quad, 0)

        @pl.when((n & 2) == 2)
        def _():
            update(lo + (n & ~3), 2, half, None)

        @pl.when((n & 1) == 1)
        def _():
            update(hi - 1, 1, half, None)

    n_far = jnp.maximum(i - SEL_NEAR + 1, 0)
    far_region(0, jnp.minimum(n_far, kh), 0)
    far_region(kh, n_far, 1)

    def near(e):
        kt = i - e
        update(kt, 1, jnp.where(kt < kh, 0, 1), bt_ref[0, :, e * SEL_T:(e + 1) * SEL_T])

    @pl.when(i >= SEL_NEAR - 1)
    def _():
        for e in range(SEL_NEAR - 1, -1, -1):
            near(e)

    @pl.when(i < SEL_NEAR - 1)
    def _():
        for e in range(SEL_NEAR - 2, -1, -1):
            @pl.when(i - e >= 0)
            def _(e=e):
                near(e)

    acc = acc_ref[...]
    o = acc[:, :HEAD_DIM] / acc[:, HEAD_DIM:HEAD_DIM + 1]
    o_ref[0] = _unstack_heads(o, SEL_T).astype(o_ref.dtype)


def _nsa_sel(proj, ks, vs, mb, rel_table):
    b, s, _ = proj.shape
    g, dh = NSA_G, HEAD_DIM
    nblk = s // SEL_LEN
    fw = nblk // 2
    rows = NSA_R * SEL_T
    pos = jnp.arange(s)
    onehot = (((pos // SEL_LEN) % fw)[:, None] == jnp.arange(fw)[None, :]).astype(BF16)
    ones2 = jnp.zeros((s, LANES - dh), BF16).at[:, :2].set(1.0)
    kp = jnp.concatenate([ks, jnp.broadcast_to(ones2, (b, g, s, LANES - dh)),
                          jnp.broadcast_to(onehot, (b, g, s, fw))], axis=-1)
    ones1 = jnp.zeros((s, LANES - dh), BF16).at[:, :1].set(1.0)
    vp = jnp.concatenate([vs, jnp.broadcast_to(ones1, (b, g, s, LANES - dh))], axis=-1)
    b_far = rel_table[REL_BUCKETS - 1].reshape(g, NSA_R) * LOG2E
    hi = b_far.astype(BF16)
    lo = (b_far - hi.astype(F32)).astype(BF16)
    fq = jnp.zeros((g, NSA_R, SEL_T, LANES - dh), BF16)
    fq = fq.at[:, :, :, 0].set(hi[:, :, None]).at[:, :, :, 1].set(lo[:, :, None]).reshape(g, rows, LANES - dh)
    qi = jnp.arange(SEL_T)[:, None]
    cc = jnp.arange(SEL_NEAR * SEL_T)[None, :]
    dd = (cc // SEL_T) * SEL_T + qi - (cc % SEL_T)
    bias = _rel_bias(rel_table, dd).reshape(g, NSA_R, SEL_T, -1) * LOG2E - b_far[:, :, None, None]
    bt = jnp.where(dd[None, None] >= 0, bias, NEG).reshape(g, rows, SEL_NEAR * SEL_T)
    return pl.pallas_call(
        _nsa_sel_kernel,
        grid=(g, b, s // SEL_T),
        in_specs=[pl.BlockSpec((1, SEL_T, NSA_R * dh), lambda gi, bi, i: (bi, i, gi)),
                  pl.BlockSpec((1, rows, LANES - dh), lambda gi, bi, i: (gi, 0, 0)),
                  pl.BlockSpec((1, 1, SEL_T, nblk), lambda gi, bi, i: (bi, gi, i, 0)),
                  pl.BlockSpec((1, 1, s, LANES + fw), lambda gi, bi, i: (bi, gi, 0, 0)),
                  pl.BlockSpec((1, 1, s, LANES), lambda gi, bi, i: (bi, gi, 0, 0)),
                  pl.BlockSpec((1, rows, SEL_NEAR * SEL_T), lambda gi, bi, i: (gi, 0, 0))],
        out_specs=pl.BlockSpec((1, SEL_T, NSA_R * dh), lambda gi, bi, i: (bi, i, gi)),
        out_shape=jax.ShapeDtypeStruct((b, s, N_HEADS * dh), BF16),
        scratch_shapes=[pltpu.VMEM((2, rows, LANES + fw), BF16), pltpu.VMEM((rows, LANES), F32),
                        pltpu.VMEM((rows, LANES), F32)],
        compiler_params=_params("parallel", "parallel", "parallel"),
        name="nsa_selected",
    )(proj, fq, mb, kp, vp, bt)


WIN_T = 512


def _softmax_rows(s):
    m = jnp.max(s, axis=1, keepdims=True)
    e = jnp.exp(s - m)
    den = jnp.sum(e, axis=1, keepdims=True)
    return e / den, m + jnp.log(den)


def _nsa_win_kernel(q_ref, kp_ref, kc_ref, vp_ref, vc_ref, bt_ref, o_ref):
    i = pl.program_id(2)
    back = WIN_T
    span = back + SUB_Q
    kcat = jnp.concatenate([kp_ref[0, 0], kc_ref[0, 0]], axis=0)
    vcat = jnp.concatenate([vp_ref[0, 0], vc_ref[0, 0]], axis=0)
    col = lax.broadcasted_iota(jnp.int32, (NSA_R * SUB_Q, span), 1)
    for j in range(WIN_T // SUB_Q):
        qs = _stack_heads(q_ref[0, j * SUB_Q:(j + 1) * SUB_Q, :]) * (HEAD_DIM ** -0.5)
        s = _dot_nt(qs, kcat[j * SUB_Q:j * SUB_Q + span]) + bt_ref[0]
        s = jnp.where((col >= back - j * SUB_Q) | (i > 0), s, NEG)
        p, _ = _softmax_rows(s)
        o = _dot(p.astype(BF16), vcat[j * SUB_Q:j * SUB_Q + span])
        o_ref[0, j * SUB_Q:(j + 1) * SUB_Q, :] = _unstack_heads(o, SUB_Q).astype(o_ref.dtype)


def _nsa_win(proj, kw, vw, rel_table):
    b, s, _ = proj.shape
    g, dh = NSA_G, HEAD_DIM
    span = WIN_T + SUB_Q
    qi = jnp.arange(SUB_Q)[:, None]
    cc = jnp.arange(span)[None, :]
    dd = qi + WIN_T - cc
    bias = _rel_bias(rel_table, dd)
    bt = jnp.where((dd[None] >= 0) & (dd[None] <= WIN_LEN - 1), bias, NEG).reshape(g, NSA_R * SUB_Q, span)
    prev = lambda gi, bi, i: (bi, gi, jnp.maximum(i - 1, 0), 0)
    cur = lambda gi, bi, i: (bi, gi, i, 0)
    return pl.pallas_call(
        _nsa_win_kernel,
        grid=(g, b, s // WIN_T),
        in_specs=[pl.BlockSpec((1, WIN_T, NSA_R * dh), lambda gi, bi, i: (bi, i, gi)),
                  pl.BlockSpec((1, 1, WIN_T, dh), prev), pl.BlockSpec((1, 1, WIN_T, dh), cur),
                  pl.BlockSpec((1, 1, WIN_T, dh), prev), pl.BlockSpec((1, 1, WIN_T, dh), cur),
                  pl.BlockSpec((1, NSA_R * SUB_Q, span), lambda gi, bi, i: (gi, 0, 0))],
        out_specs=pl.BlockSpec((1, WIN_T, NSA_R * dh), lambda gi, bi, i: (bi, i, gi)),
        out_shape=jax.ShapeDtypeStruct((b, s, N_HEADS * dh), BF16),
        compiler_params=_params("parallel", "parallel", "parallel"),
        name="nsa_window",
    )(proj, kw, kw, vw, vw, bt)


def _nsa_combine_kernel(oc_ref, os_ref, ow_ref, g_ref, e_ref, o_ref):
    d = o_ref.shape[2]
    sg = jax.nn.sigmoid(g_ref[0].astype(F32))
    ge = _dot_hi_exact_rhs(sg, e_ref[...])
    o = (ge[:, :d] * oc_ref[0].astype(F32) + ge[:, d:2 * d] * os_ref[0].astype(F32)
         + ge[:, 2 * d:] * ow_ref[0].astype(F32))
    o_ref[0] = o.astype(o_ref.dtype)


def _nsa_combine(o_cmp, o_sel, o_win, proj, gate_col, tm=512):
    b, s, d = o_cmp.shape
    c = jnp.arange(LANES)[:, None]
    n = jnp.arange(3 * d)[None, :]
    expand = (c == (n // d) * N_HEADS + (n % d) // HEAD_DIM).astype(BF16)
    spec = pl.BlockSpec((1, tm, d), lambda bi, i: (bi, i, 0))
    return pl.pallas_call(
        _nsa_combine_kernel,
        grid=(b, s // tm),
        in_specs=[spec, spec, spec,
                  pl.BlockSpec((1, tm, LANES), lambda bi, i: (bi, i, gate_col // LANES)),
                  pl.BlockSpec((LANES, 3 * d), lambda bi, i: (0, 0))],
        out_specs=spec,
        out_shape=jax.ShapeDtypeStruct((b, s, d), BF16),
        compiler_params=_params("parallel", "parallel"),
        name="nsa_combine",
    )(o_cmp, o_sel, o_win, proj, expand)


def _nsa_mixer(x, scale, shift, w_in, pe, w1, b1, w2, rel_table):
    b, s, _ = x.shape
    hd, gd = N_HEADS * HEAD_DIM, NSA_G * HEAD_DIM
    gate_col = hd + 6 * gd
    n_in = gate_col + LANES
    proj = _linear_in(x, scale, shift, _pad_cols(w_in, n_in).astype(BF16), tn=n_in)

    def group_major(t):
        return t.reshape(b, s, NSA_G, HEAD_DIM).transpose(0, 2, 1, 3)

    kc, vc, ks, vs, kw, vw = (proj[:, :, hd + n * gd: hd + (n + 1) * gd] for n in range(6))
    cmp = _compress(jnp.stack([kc, vc]), pe.reshape(2, -1), w1, b1, w2)
    o_cmp, mb = _nsa_cmp(proj, cmp[0], cmp[1], rel_table)
    o_sel = _nsa_sel(proj, group_major(ks), group_major(vs), mb, rel_table)
    o_win = _nsa_win(proj, group_major(kw), group_major(vw), rel_table)
    return _nsa_combine(o_cmp, o_sel, o_win, proj, gate_col)


DIL_BACK = 128


def _dil_kernel(q_ref, kp_ref, kc_ref, vp_ref, vc_ref, bt_ref, o_ref, l_ref):
    t = pl.program_id(3)
    tq = q_ref.shape[1]
    span = DIL_BACK + SUB_Q
    kcat = jnp.concatenate([kp_ref[0, tq - DIL_BACK:, :], kc_ref[0]], axis=0)
    vcat = jnp.concatenate([vp_ref[0, tq - DIL_BACK:, :], vc_ref[0]], axis=0)
    lane_q = lax.broadcasted_iota(jnp.int32, (SUB_Q, LANES), 1) // HEAD_DIM
    lane_k = lax.broadcasted_iota(jnp.int32, (span, LANES), 1) // HEAD_DIM
    col = lax.broadcasted_iota(jnp.int32, (SUB_Q, span), 1)
    for j in range(tq // SUB_Q):
        qj = q_ref[0, j * SUB_Q:(j + 1) * SUB_Q, :] * (HEAD_DIM ** -0.5)
        kk = kcat[j * SUB_Q:j * SUB_Q + span]
        vv = vcat[j * SUB_Q:j * SUB_Q + span]
        o = jnp.zeros((SUB_Q, LANES), F32)
        lse = jnp.zeros((SUB_Q, LANES), F32)
        for hh in range(LANES // HEAD_DIM):
            s = _dot_nt(jnp.where(lane_q == hh, qj, 0.0).astype(BF16), kk) + bt_ref[hh]
            if j == 0:
                s = jnp.where((col >= DIL_BACK) | (t > 0), s, NEG)
            p, lse_h = _softmax_rows(s)
            o = o + _dot(p.astype(BF16), jnp.where(lane_k == hh, vv, 0.0).astype(BF16))
            lse = jnp.where(lane_q == hh, lse_h, lse)
        o_ref[0, j * SUB_Q:(j + 1) * SUB_Q, :] = o.astype(o_ref.dtype)
        l_ref[0, j * SUB_Q:(j + 1) * SUB_Q, :] = lse


def _dil_group(view, window, dil, rel_table):
    b, l, n_all = view.shape
    hd = N_HEADS * HEAD_DIM
    assert window // dil == DIL_BACK and n_all == dil * 3 * hd
    tq = min(512, l)
    assert l % tq == 0 and tq >= DIL_BACK
    hpb = hd // LANES
    qi = jnp.arange(SUB_Q)[:, None]
    cc = jnp.arange(DIL_BACK + SUB_Q)[None, :]
    dd = qi + DIL_BACK - cc
    bt = jnp.where((dd[None] >= 0) & (dd[None] <= DIL_BACK), _rel_bias(rel_table, dd * dil), NEG)

    def spec(comp, prev):
        def index(bi, r, hp, t):
            return (bi, jnp.maximum(t - 1, 0) if prev else t, (r * 3 + comp) * hpb + hp)
        return pl.BlockSpec((1, tq, LANES), index)

    out_spec = pl.BlockSpec((1, tq, LANES), lambda bi, r, hp, t: (bi, t, r * hpb + hp))
    return pl.pallas_call(
        _dil_kernel,
        grid=(b, dil, hpb, l // tq),
        in_specs=[spec(0, False), spec(1, True), spec(1, False), spec(2, True), spec(2, False),
                  pl.BlockSpec((LANES // HEAD_DIM, SUB_Q, DIL_BACK + SUB_Q), lambda bi, r, hp, t: (hp, 0, 0))],
        out_specs=[out_spec, out_spec],
        out_shape=[jax.ShapeDtypeStruct((b, l, dil * hd), BF16), jax.ShapeDtypeStruct((b, l, dil * hd), F32)],
        compiler_params=_params("parallel", "parallel", "parallel", "parallel"),
        name=f"dilated_attention_{dil}",
    )(view, view, view, view, view, bt)


def _dil_merge_kernel(*refs, dils):
    n = len(dils)
    o_refs, l_refs, out_ref, scratch = refs[:n], refs[n:2 * n], refs[2 * n], refs[2 * n + 1:]
    tm, d = out_ref.shape[1], out_ref.shape[2]

    def token_major(ref, dil, buf):
        if dil == 1:
            return ref[0].astype(F32)
        for r in range(dil):
            for c in range(d // LANES):
                buf[c, pl.ds(r, tm // dil, stride=dil), :] = (
                    ref[0, :, r * d + c * LANES:r * d + (c + 1) * LANES].astype(F32))
        return jnp.concatenate([buf[c] for c in range(d // LANES)], axis=1)

    outs = [token_major(o_refs[g], dils[g], scratch[2 * g]) for g in range(n)]
    lses = [token_major(l_refs[g], dils[g], scratch[2 * g + 1]) for g in range(n)]
    m = functools.reduce(jnp.maximum, lses)
    es = [jnp.exp(l - m) for l in lses]
    inv = 1.0 / functools.reduce(jnp.add, es)
    out_ref[0] = functools.reduce(jnp.add, [e * inv * o for e, o in zip(es, outs)]).astype(out_ref.dtype)


def _dil_merge(outs, lses, dils, tm=512):
    b = outs[0].shape[0]
    d = N_HEADS * HEAD_DIM
    s = outs[0].shape[1] * dils[0]
    specs = [pl.BlockSpec((1, tm // dil, dil * d), lambda bi, i: (bi, i, 0)) for dil in dils]
    return pl.pallas_call(
        functools.partial(_dil_merge_kernel, dils=tuple(dils)),
        grid=(b, s // tm),
        in_specs=specs + specs,
        out_specs=pl.BlockSpec((1, tm, d), lambda bi, i: (bi, i, 0)),
        out_shape=jax.ShapeDtypeStruct((b, s, d), BF16),
        scratch_shapes=[pltpu.VMEM((d // LANES, tm, LANES), F32) for _ in range(2 * len(dils))],
        compiler_params=_params("parallel", "parallel"),
        name="dilated_merge",
    )(*outs, *lses)


def _dil_mixer(x, scale, shift, w_in, rel_table):
    hd = N_HEADS * HEAD_DIM
    w = w_in.astype(BF16)
    outs, lses, dils = [], [], []
    for gi, (window, dil) in enumerate(DIL_GROUPS):
        view = _linear_in(x, scale, shift, w[:, gi * 3 * hd:(gi + 1) * 3 * hd], tn=3 * hd, dil=dil)
        o, lse = _dil_group(view, window, dil, rel_table)
        outs.append(o)
        lses.append(lse)
        dils.append(dil)
    return _dil_merge(outs, lses, dils)


def kernel(x, c, rel_table, mod_w, mod_b, ln_g, ln_b, gla_w_in, gla_w_a2, gla_b_a, gla_norm_g, gla_w_o,
           nsa_w_in, nsa_cmp_pe, nsa_cmp_w1, nsa_cmp_b1, nsa_cmp_w2, nsa_w_o, dil_w_in, dil_w_o,
           ffn_w_up, ffn_conv_w, ffn_conv_b, ffn_w_down):
    shift, scale, gate = _modulation(c, mod_w, mod_b)
    for i in range(DEPTH):
        kind, j = i % N_MIXERS, i // N_MIXERS
        m = 2 * i
        if kind == 0:
            a = _gla_mixer(x, scale[m], shift[m], gla_w_in[j], gla_w_a2[j], gla_b_a[j], gla_norm_g[j])
            w_o = gla_w_o[j]
        elif kind == 1:
            a = _nsa_mixer(x, scale[m], shift[m], nsa_w_in[j], nsa_cmp_pe[j], nsa_cmp_w1[j], nsa_cmp_b1[j],
                           nsa_cmp_w2[j], rel_table)
            w_o = nsa_w_o[j]
        else:
            a = _dil_mixer(x, scale[m], shift[m], dil_w_in[j], rel_table)
            w_o = dil_w_o[j]
        x = _out_norm(a, w_o.astype(BF16), x, gate[m], ln_g[i, 0], ln_b[i, 0])
        a = _ffn_up(x, scale[m + 1], shift[m + 1], ffn_w_up[i].astype(BF16), ffn_conv_w[i], ffn_conv_b[i])
        x = _out_norm(a, ffn_w_down[i].astype(BF16), x, gate[m + 1], ln_g[i, 1], ln_b[i, 1])
    return x
```

```python
import functools
import math

import jax
import jax.numpy as jnp
import numpy as np
from jax import lax
from jax.experimental import pallas as pl
from jax.experimental.pallas import tpu as pltpu

F32 = jnp.float32
BF16 = jnp.bfloat16

D_MODEL = 1024
DEPTH = 4
N_MIXERS = 3
HEAD_DIM = 64
N_HEADS = D_MODEL // HEAD_DIM
REL_BUCKETS = 32
REL_MAX_DIST = 2048
LN_EPS = 1e-5
GLA_HEADS = 4
GLA_DK = D_MODEL // 2 // GLA_HEADS
GLA_DV = D_MODEL // GLA_HEADS
GLA_GATE_RANK = 16
GLA_TAU = 16.0
GLA_CHUNK = 64
NSA_G = 4
NSA_R = N_HEADS // NSA_G
CMP_LEN = 32
CMP_STRIDE = 16
CMP_HIDDEN = 256
SEL_LEN = 64
SEL_TOP = 16
WIN_LEN = 512
DIL_GROUPS = ((128, 1), (512, 4), (2048, 16))
D_FF = 2816
DN_ALPHA = (2 * DEPTH) ** 0.25

NEG = -1e30
VMEM_LIMIT = 56 * 1024 * 1024
LANES = 128
SUB_Q = 128


def _params(*sem):
    return pltpu.CompilerParams(dimension_semantics=sem, vmem_limit_bytes=VMEM_LIMIT)


def _dot(a, b):
    return jnp.dot(a, b, preferred_element_type=F32)


def _dot_nt(a, b):
    return lax.dot_general(a, b, (((1,), (1,)), ((), ())), preferred_element_type=F32)


def _dot_tn(a, b):
    return lax.dot_general(a, b, (((0,), (0,)), ((), ())), preferred_element_type=F32)


def _split(a):
    hi = a.astype(BF16)
    lo = (a - hi.astype(F32)).astype(BF16)
    return hi, lo


def _dot_hi_exact_rhs(a, b_bf16):
    hi, lo = _split(a)
    return _dot(hi, b_bf16) + _dot(lo, b_bf16)


def _silu(x):
    return x * jax.nn.sigmoid(x)


def _bucket_starts():
    n = np.arange(2 * REL_MAX_DIST)
    exact = REL_BUCKETS // 2
    logn = np.log(np.maximum(n, 1).astype(np.float32) / np.float32(exact))
    large = exact + (logn / np.float32(math.log(REL_MAX_DIST / exact)) * np.float32(REL_BUCKETS - exact)).astype(np.int32)
    bucket = np.where(n < exact, n, np.minimum(large, REL_BUCKETS - 1))
    assert np.all(np.diff(bucket) >= 0) and bucket[-1] == REL_BUCKETS - 1
    return [int(np.argmax(bucket >= k)) for k in range(REL_BUCKETS)]


BUCKET_START = _bucket_starts()
FAR_DIST = BUCKET_START[-1]


def _rel_bias(rel_table, dist):
    shape = (rel_table.shape[1],) + (1,) * dist.ndim
    out = jnp.broadcast_to(rel_table[0].reshape(shape), (rel_table.shape[1],) + dist.shape)
    for k in range(1, REL_BUCKETS):
        out = jnp.where(dist[None] >= BUCKET_START[k], rel_table[k].reshape(shape), out)
    return out


def _mod_kernel(c_ref, w_ref, b_ref, o_ref):
    cs = _silu(c_ref[...])
    ch, cl = _split(cs)
    wh, wl = _split(w_ref[0])
    o_ref[0] = _dot(ch, wh) + _dot(ch, wl) + _dot(cl, wh) + b_ref[0]


def _modulation(c, mod_w, mod_b):
    b, d = c.shape
    n_sub = mod_w.shape[0] * mod_w.shape[1]
    rows = 8
    cp = jnp.zeros((rows, d), F32).at[:b].set(c)
    w = mod_w.reshape(n_sub, d, 3 * d)
    bias = mod_b.reshape(n_sub, 1, 3 * d)
    tn = d
    out = pl.pallas_call(
        _mod_kernel,
        grid=(n_sub, 3 * d // tn),
        in_specs=[pl.BlockSpec((rows, d), lambda i, j: (0, 0)),
                  pl.BlockSpec((1, d, tn), lambda i, j: (i, 0, j)),
                  pl.BlockSpec((1, 1, tn), lambda i, j: (i, 0, j))],
        out_specs=pl.BlockSpec((1, rows, tn), lambda i, j: (i, 0, j)),
        out_shape=jax.ShapeDtypeStruct((n_sub, rows, 3 * d), F32),
        compiler_params=_params("parallel", "parallel"),
        name="modulation",
    )(cp, w, bias)
    out = out[:, :b]
    shift, scale, gate = jnp.split(out, 3, axis=-1)
    return shift[:, :, None, :], scale[:, :, None, :], gate[:, :, None, :]


def _linear_in_kernel(x_ref, sc_ref, sh_ref, w_ref, o_ref, *acc, dil):
    h = (x_ref[0] * (1.0 + sc_ref[0]) + sh_ref[0]).astype(BF16)
    res = _dot(h, w_ref[...])
    if dil == 1:
        o_ref[0] = res.astype(o_ref.dtype)
    else:
        acc_ref, = acc
        rows, n = res.shape[0] // dil, res.shape[1]
        for c in range(n // LANES):
            acc_ref[c] = res[:, c * LANES:(c + 1) * LANES]
        for r in range(dil):
            for c in range(n // LANES):
                o_ref[0, :, r * n + c * LANES:r * n + (c + 1) * LANES] = (
                    acc_ref[c, pl.ds(r, rows, stride=dil), :].astype(o_ref.dtype))


def _linear_in(x, scale, shift, w, tn, tm=512, dil=1):
    b, s, d = x.shape
    n = w.shape[1]
    assert n % tn == 0 and s % tm == 0 and tm % (16 * dil) == 0 and (dil == 1 or n == tn)
    return pl.pallas_call(
        functools.partial(_linear_in_kernel, dil=dil),
        grid=(n // tn, b, s // tm),
        in_specs=[pl.BlockSpec((1, tm, d), lambda j, bi, i: (bi, i, 0)),
                  pl.BlockSpec((1, 1, d), lambda j, bi, i: (bi, 0, 0)),
                  pl.BlockSpec((1, 1, d), lambda j, bi, i: (bi, 0, 0)),
                  pl.BlockSpec((d, tn), lambda j, bi, i: (0, j))],
        out_specs=pl.BlockSpec((1, tm // dil, dil * tn), lambda j, bi, i: (bi, i, j)),
        out_shape=jax.ShapeDtypeStruct((b, s // dil, dil * n), BF16),
        scratch_shapes=[] if dil == 1 else [pltpu.VMEM((tn // LANES, tm, LANES), F32)],
        compiler_params=_params("parallel", "parallel", "parallel"),
        name="linear_in",
    )(x, scale, shift, w)


def _out_norm_kernel(a_ref, w_ref, x_ref, gate_ref, g_ref, b_ref, o_ref):
    y = _dot(a_ref[0], w_ref[...])
    z = DN_ALPHA * x_ref[0] + (1.0 + gate_ref[0]) * y
    mu = jnp.mean(z, axis=-1, keepdims=True)
    zc = z - mu
    var = jnp.mean(zc * zc, axis=-1, keepdims=True)
    o_ref[0] = zc * lax.rsqrt(var + LN_EPS) * g_ref[...] + b_ref[...]


def _out_norm(a, w, x, gate, ln_g, ln_b, tm=512):
    b, s, d = x.shape
    k = a.shape[-1]
    return pl.pallas_call(
        _out_norm_kernel,
        grid=(b, s // tm),
        in_specs=[pl.BlockSpec((1, tm, k), lambda bi, i: (bi, i, 0)),
                  pl.BlockSpec((k, d), lambda bi, i: (0, 0)),
                  pl.BlockSpec((1, tm, d), lambda bi, i: (bi, i, 0)),
                  pl.BlockSpec((1, 1, d), lambda bi, i: (bi, 0, 0)),
                  pl.BlockSpec((1, d), lambda bi, i: (0, 0)),
                  pl.BlockSpec((1, d), lambda bi, i: (0, 0))],
        out_specs=pl.BlockSpec((1, tm, d), lambda bi, i: (bi, i, 0)),
        out_shape=jax.ShapeDtypeStruct((b, s, d), F32),
        compiler_params=_params("parallel", "parallel"),
        name="out_norm",
    )(a, w, x, gate, ln_g.reshape(1, d), ln_b.reshape(1, d))


FFN_HALO = 16


def _ffn_up_kernel(x_ref, xh_ref, sc_ref, sh_ref, wu_ref, wg_ref, cw_ref, cb_ref, o_ref, h_ref, u_ref):
    i = pl.program_id(1)
    tm = x_ref.shape[1]

    @pl.when(pl.program_id(2) == 0)
    def _():
        sc = 1.0 + sc_ref[0]
        sh = sh_ref[0]
        h_ref[0:FFN_HALO, :] = (xh_ref[0] * sc + sh).astype(BF16)
        h_ref[FFN_HALO:, :] = (x_ref[0] * sc + sh).astype(BF16)

    u = _dot(h_ref[...], wu_ref[...])
    keep = (i > 0).astype(F32)
    u_ref[0:FFN_HALO, :] = u[0:FFN_HALO] * keep
    u_ref[FFN_HALO:, :] = u[FFN_HALO:]
    g = _dot(h_ref[FFN_HALO:, :], wg_ref[...])
    cw = cw_ref[...]
    conv = (u_ref[pl.ds(FFN_HALO - 2, tm), :] * cw[0:1] + u_ref[pl.ds(FFN_HALO - 1, tm), :] * cw[1:2]
            + u_ref[pl.ds(FFN_HALO, tm), :] * cw[2:3] + cb_ref[...])
    o_ref[0] = (_silu(conv) * g).astype(o_ref.dtype)


def _ffn_up(x, scale, shift, w_up, conv_w, conv_b, tm=512, tf=1408):
    b, s, d = x.shape
    f = conv_w.shape[1]
    assert f % tf == 0 and s % tm == 0 and tm % FFN_HALO == 0
    nf = f // tf
    hb = tm // FFN_HALO
    return pl.pallas_call(
        _ffn_up_kernel,
        grid=(b, s // tm, nf),
        in_specs=[pl.BlockSpec((1, tm, d), lambda bi, i, j: (bi, i, 0)),
                  pl.BlockSpec((1, FFN_HALO, d), lambda bi, i, j: (bi, jnp.maximum(i * hb - 1, 0), 0)),
                  pl.BlockSpec((1, 1, d), lambda bi, i, j: (bi, 0, 0)),
                  pl.BlockSpec((1, 1, d), lambda bi, i, j: (bi, 0, 0)),
                  pl.BlockSpec((d, tf), lambda bi, i, j: (0, j)),
                  pl.BlockSpec((d, tf), lambda bi, i, j: (0, nf + j)),
                  pl.BlockSpec((3, tf), lambda bi, i, j: (0, j)),
                  pl.BlockSpec((1, tf), lambda bi, i, j: (0, j))],
        out_specs=pl.BlockSpec((1, tm, tf), lambda bi, i, j: (bi, i, j)),
        out_shape=jax.ShapeDtypeStruct((b, s, f), BF16),
        scratch_shapes=[pltpu.VMEM((tm + FFN_HALO, d), BF16), pltpu.VMEM((tm + FFN_HALO, tf), F32)],
        compiler_params=_params("parallel", "parallel", "arbitrary"),
        name="ffn_up",
    )(x, x, scale, shift, w_up, w_up, conv_w, conv_b.reshape(1, f))


def _gla_kernel(q_ref, k_ref, v_ref, r_ref, a_ref, wa_ref, ba_ref, ng_ref, o_ref, st_ref):
    tc = q_ref.shape[1]
    nchunk = tc // GLA_CHUNK

    @pl.when(pl.program_id(2) == 0)
    def _():
        st_ref[...] = jnp.zeros_like(st_ref)

    z = _dot(a_ref[0], wa_ref[...]) + ba_ref[...]
    log_a = (jnp.minimum(z, 0.0) - jnp.log1p(jnp.exp(-jnp.abs(z)))) / GLA_TAU
    row = lax.broadcasted_iota(jnp.int32, (tc, tc), 0)
    col = lax.broadcasted_iota(jnp.int32, (tc, tc), 1)
    same = (row // GLA_CHUNK) == (col // GLA_CHUNK)
    cum_m = jnp.where(same & (col <= row), 1.0, 0.0).astype(BF16)
    tot_m = jnp.where(same, 1.0, 0.0).astype(BF16)
    la_hi, la_lo = _split(log_a)
    bcum = _dot(cum_m, la_hi) + _dot(cum_m, la_lo)
    btot = _dot(tot_m, la_hi) + _dot(tot_m, la_lo)
    qf = q_ref[0].astype(F32) * (GLA_DK ** -0.5)
    kf = k_ref[0].astype(F32)
    q_dec = (qf * jnp.exp(bcum)).astype(BF16)
    k_inv = (kf * jnp.exp(-bcum)).astype(BF16)
    k_dec = (kf * jnp.exp(btot - bcum)).astype(BF16)
    dl = jnp.exp(btot)
    ci = lax.broadcasted_iota(jnp.int32, (GLA_CHUNK, GLA_CHUNK), 0)
    cj = lax.broadcasted_iota(jnp.int32, (GLA_CHUNK, GLA_CHUNK), 1)
    causal = cj <= ci
    v = v_ref[0]
    chunks = [slice(c * GLA_CHUNK, (c + 1) * GLA_CHUNK) for c in range(nchunk)]
    att = [jnp.where(causal, _dot_nt(q_dec[sl], k_inv[sl]), 0.0).astype(BF16) for sl in chunks]
    kv = [_dot_tn(v[sl], k_dec[sl]) for sl in chunks]
    o_intra = [_dot(att[c], v[chunks[c]]) for c in range(nchunk)]
    st = st_ref[...]
    o_inter = []
    for c, sl in enumerate(chunks):
        o_inter.append(_dot_nt(q_dec[sl], st.astype(BF16)))
        st = st * dl[c * GLA_CHUNK:c * GLA_CHUNK + 1] + kv[c]
    st_ref[...] = st
    o = jnp.concatenate(o_intra, axis=0) + jnp.concatenate(o_inter, axis=0)
    o = o * lax.rsqrt(jnp.mean(o * o, axis=-1, keepdims=True) + LN_EPS) * ng_ref[...]
    o_ref[0] = (o * _silu(r_ref[0].astype(F32))).astype(o_ref.dtype)


def _gla_core(proj, w_a2, b_a, norm_g, tc=512):
    b, s, _ = proj.shape
    h, dk, dv = GLA_HEADS, GLA_DK, GLA_DV
    assert s % tc == 0
    wa = jnp.zeros((LANES, h * dk), F32).at[:GLA_GATE_RANK].set(w_a2).astype(BF16)
    kb, vb, rb, ab = (h * dk) // dk, (2 * h * dk) // dv, (2 * h * dk + h * dv) // dv, (2 * h * dk + 2 * h * dv) // LANES
    return pl.pallas_call(
        _gla_kernel,
        grid=(b, h, s // tc),
        in_specs=[pl.BlockSpec((1, tc, dk), lambda bi, hi, t: (bi, t, hi)),
                  pl.BlockSpec((1, tc, dk), lambda bi, hi, t: (bi, t, kb + hi)),
                  pl.BlockSpec((1, tc, dv), lambda bi, hi, t: (bi, t, vb + hi)),
                  pl.BlockSpec((1, tc, dv), lambda bi, hi, t: (bi, t, rb + hi)),
                  pl.BlockSpec((1, tc, LANES), lambda bi, hi, t: (bi, t, ab)),
                  pl.BlockSpec((LANES, dk), lambda bi, hi, t: (0, hi)),
                  pl.BlockSpec((1, dk), lambda bi, hi, t: (0, hi)),
                  pl.BlockSpec((1, dv), lambda bi, hi, t: (0, 0))],
        out_specs=pl.BlockSpec((1, tc, dv), lambda bi, hi, t: (bi, t, hi)),
        out_shape=jax.ShapeDtypeStruct((b, s, h * dv), BF16),
        scratch_shapes=[pltpu.VMEM((dv, dk), F32)],
        compiler_params=_params("parallel", "parallel", "arbitrary"),
        name="gla_core",
    )(proj, proj, proj, proj, proj, wa, b_a.reshape(1, h * dk), norm_g.reshape(1, dv))


def _pad_cols(w, n):
    return jnp.pad(w, ((0, 0), (0, n - w.shape[1])))


def _gla_mixer(x, scale, shift, w_in, w_a2, b_a, norm_g):
    n_in = 2 * GLA_HEADS * GLA_DK + 2 * GLA_HEADS * GLA_DV + LANES
    proj = _linear_in(x, scale, shift, _pad_cols(w_in, n_in).astype(BF16), tn=n_in)
    return _gla_core(proj, w_a2, b_a, norm_g)


def _compress_kernel(lo_ref, hi_ref, pe_ref, w1_ref, b1_ref, w2_ref, o_ref):
    half = lo_ref.shape[-1]
    pe = pe_ref[0]
    a_lo = (lo_ref[0, 0, 0].astype(F32) + pe[:, :half]).astype(BF16)
    a_hi = (hi_ref[0, 0, 0].astype(F32) + pe[:, half:]).astype(BF16)
    h1 = _dot(a_lo, w1_ref[0, :half, :]) + _dot(a_hi, w1_ref[0, half:, :]) + b1_ref[0]
    o_ref[0, 0, 0] = _dot(_silu(h1).astype(BF16), w2_ref[0])


def _compress(kv, pe, w1, b1, w2):
    two, b, s, _ = kv.shape
    g, dh = NSA_G, HEAD_DIM
    npc = s // CMP_STRIDE
    half = CMP_STRIDE * dh
    a = kv.reshape(two, b, npc, CMP_STRIDE, g, dh).transpose(0, 1, 4, 2, 3, 5).reshape(two, b, g, npc, half)
    a_hi = jnp.concatenate([a[:, :, :, 1:], jnp.zeros_like(a[:, :, :, :1])], axis=3)
    spec_a = pl.BlockSpec((1, 1, 1, npc, half), lambda c, bi, gi: (c, bi, gi, 0, 0))
    return pl.pallas_call(
        _compress_kernel,
        grid=(two, b, g),
        in_specs=[spec_a, spec_a,
                  pl.BlockSpec((1, 1, 2 * half), lambda c, bi, gi: (c, 0, 0)),
                  pl.BlockSpec((1, 2 * half, CMP_HIDDEN), lambda c, bi, gi: (c, 0, 0)),
                  pl.BlockSpec((1, 1, CMP_HIDDEN), lambda c, bi, gi: (c, 0, 0)),
                  pl.BlockSpec((1, CMP_HIDDEN, dh), lambda c, bi, gi: (c, 0, 0))],
        out_specs=pl.BlockSpec((1, 1, 1, npc, dh), lambda c, bi, gi: (c, bi, gi, 0, 0)),
        out_shape=jax.ShapeDtypeStruct((two, b, g, npc, dh), F32),
        compiler_params=_params("parallel", "parallel", "parallel"),
        name="nsa_compress",
    )(a, a_hi, pe.reshape(two, 1, 2 * half), w1.astype(BF16), b1.reshape(two, 1, CMP_HIDDEN), w2.astype(BF16))


def _stack_heads(q):
    return jnp.concatenate([q[:, r * HEAD_DIM:(r + 1) * HEAD_DIM] for r in range(NSA_R)], axis=0)


def _unstack_heads(o, rows):
    return jnp.concatenate([o[r * rows:(r + 1) * rows] for r in range(NSA_R)], axis=1)


CMP_TQ = 512
LOG2E = 1.4426950408889634


def _nsa_cmp_kernel(q_ref, fq_ref, kc_ref, vc_ref, nb_ref, pb_ref, o_ref, mb_ref):
    i = pl.program_id(2)
    npc = nb_ref.shape[2]
    nblk = mb_ref.shape[3]
    n_sub, rows = CMP_TQ // SUB_Q, NSA_R * SUB_Q
    starts = [pl.multiple_of((i * n_sub + j) * (SUB_Q // CMP_STRIDE), 8) for j in range(n_sub)]

    scores = []
    for j in range(n_sub):
        kwin = kc_ref[0, 0, pl.ds(starts[j], npc), :].astype(BF16)
        qs = _stack_heads(q_ref[0, j * SUB_Q:(j + 1) * SUB_Q, :]).astype(F32) * (HEAD_DIM ** -0.5 * LOG2E)
        qp = jnp.concatenate([qs, fq_ref[...]], axis=1).astype(BF16)
        scores.append(_dot_nt(qp, kwin) + nb_ref[0])
    s = jnp.concatenate(scores, axis=0)
    m = jnp.max(s, axis=1, keepdims=True)
    e = jnp.exp2(s - m)
    den = jnp.sum(e, axis=1, keepdims=True)
    p = e * jnp.where(m > 0.5 * NEG, 1.0 / den, 0.0)
    p_bf = p.astype(BF16)
    for j in range(n_sub):
        vwin = vc_ref[0, 0, pl.ds(starts[j], npc), :].astype(BF16)
        o = _dot(p_bf[j * rows:(j + 1) * rows], vwin)
        o_ref[0, j * SUB_Q:(j + 1) * SUB_Q, :] = _unstack_heads(o, SUB_Q).astype(o_ref.dtype)

    imps = []
    for j in range(n_sub):
        pg = p[j * rows:j * rows + SUB_Q]
        for r in range(1, NSA_R):
            pg = pg + p[j * rows + r * SUB_Q:j * rows + (r + 1) * SUB_Q]
        imps.append(_dot_hi_exact_rhs(pg, pb_ref[pl.ds(starts[j], npc), :].astype(BF16)))
    imp = jnp.concatenate(imps, axis=0)

    blk = lax.broadcasted_iota(jnp.int32, (CMP_TQ, nblk), 1)
    blk_f = blk.astype(F32)
    t = i * CMP_TQ + lax.broadcasted_iota(jnp.int32, (CMP_TQ, nblk), 0)
    cur = t // SEL_LEN
    forced = (blk == 0) | (blk == cur) | (blk == cur - 1)
    work = jnp.where((blk * SEL_LEN <= t) & jnp.logical_not(forced), imp, NEG)
    free = work > 0.5 * NEG
    for _ in range(SEL_TOP - 3):
        mx = jnp.max(work, axis=1, keepdims=True)
        first = jnp.min(jnp.where(work == mx, blk_f, 1e9), axis=1, keepdims=True)
        work = jnp.where(blk_f == first, 2.0 * NEG, work)
    sel = forced | (free & (work < 1.5 * NEG))
    mb_ref[0, 0] = jnp.where(sel, 0.0, NEG).astype(mb_ref.dtype)


def _nsa_cmp(proj, kcmp, vcmp, rel_table):
    b, s, _ = proj.shape
    g, dh = NSA_G, HEAD_DIM
    npc = s // CMP_STRIDE
    nblk = s // SEL_LEN
    front = npc - SUB_Q // CMP_STRIDE
    assert s % CMP_TQ == 0 and nblk >= SEL_TOP
    kpad = jnp.pad(kcmp, ((0, 0), (0, 0), (front, npc - front), (0, LANES - dh)))
    kpad = kpad.at[:, :, :front, dh].set(1.0)
    vpad = jnp.pad(vcmp, ((0, 0), (0, 0), (front, npc - front), (0, 0)))
    fq = jnp.zeros((NSA_R * SUB_Q, LANES - dh), F32).at[:, 0].set(NEG)
    qi = jnp.arange(SUB_Q)[:, None]
    cc = jnp.arange(npc)[None, :]
    dd = qi - (CMP_LEN - 1) + CMP_STRIDE * (front - cc)
    nb = jnp.where(dd[None] >= 0, _rel_bias(rel_table, dd) * LOG2E, NEG).reshape(g, NSA_R * SUB_Q, npc)
    diff = jnp.arange(2 * npc)[:, None] - front - (SEL_LEN // CMP_STRIDE) * jnp.arange(nblk)[None, :]
    pool = jnp.where((diff == -1) | (diff == 3), 1.0, jnp.where((diff >= 0) & (diff <= 2), 2.0, 0.0))
    return pl.pallas_call(
        _nsa_cmp_kernel,
        grid=(g, b, s // CMP_TQ),
        in_specs=[pl.BlockSpec((1, CMP_TQ, NSA_R * dh), lambda gi, bi, i: (bi, i, gi)),
                  pl.BlockSpec((NSA_R * SUB_Q, LANES - dh), lambda gi, bi, i: (0, 0)),
                  pl.BlockSpec((1, 1, 2 * npc, LANES), lambda gi, bi, i: (bi, gi, 0, 0)),
                  pl.BlockSpec((1, 1, 2 * npc, dh), lambda gi, bi, i: (bi, gi, 0, 0)),
                  pl.BlockSpec((1, NSA_R * SUB_Q, npc), lambda gi, bi, i: (gi, 0, 0)),
                  pl.BlockSpec((2 * npc, nblk), lambda gi, bi, i: (0, 0))],
        out_specs=[pl.BlockSpec((1, CMP_TQ, NSA_R * dh), lambda gi, bi, i: (bi, i, gi)),
                   pl.BlockSpec((1, 1, CMP_TQ, nblk), lambda gi, bi, i: (bi, gi, i, 0))],
        out_shape=[jax.ShapeDtypeStruct((b, s, N_HEADS * dh), BF16),
                   jax.ShapeDtypeStruct((b, g, s, nblk), BF16)],
        compiler_params=_params("parallel", "parallel", "parallel"),
        name="nsa_cmp_select",
    )(proj, fq, kpad, vpad, nb, pool.astype(F32))


SEL_T = 256
SEL_NEAR = -(-(FAR_DIST + SEL_T - 1) // SEL_T)


def _nsa_sel_kernel(q_ref, fq_ref, mb_ref, k_ref, v_ref, bt_ref, o_ref, qp_ref, m_ref, acc_ref):
    i = pl.program_id(2)
    rows = NSA_R * SEL_T
    fw = (k_ref.shape[3] - LANES)
    kh = fw // (SEL_T // SEL_LEN)
    qs = (_stack_heads(q_ref[0]).astype(F32) * (HEAD_DIM ** -0.5 * LOG2E)).astype(BF16)
    mb = mb_ref[0, 0]
    for half in range(2):
        mbh = jnp.concatenate([mb[:, half * fw:(half + 1) * fw]] * NSA_R, axis=0)
        qp_ref[half] = jnp.concatenate([qs, fq_ref[0], mbh], axis=1)
    m_ref[...] = jnp.full_like(m_ref, NEG)
    acc_ref[...] = jnp.zeros_like(acc_ref)

    def update(kt, n_tile, half, bias):
        k0 = pl.multiple_of(kt * SEL_T, SEL_T)
        kk = k_ref[0, 0, pl.ds(k0, n_tile * SEL_T), :]
        vv = v_ref[0, 0, pl.ds(k0, n_tile * SEL_T), :]
        s = _dot_nt(qp_ref[half], kk)
        if bias is not None:
            s = s + bias
        m_prev = m_ref[...]
        m_new = jnp.maximum(m_prev, jnp.max(s, axis=1, keepdims=True))
        alpha = jnp.exp2(m_prev - m_new)
        p = jnp.exp2(s - jnp.concatenate([m_new] * (n_tile * SEL_T // LANES), axis=1))
        acc_ref[...] = acc_ref[...] * alpha + _dot(p.astype(BF16), vv)
        m_ref[...] = m_new

    def far_region(lo, hi, half):
        n = jnp.maximum(hi - lo, 0)

        def quad(j, carry):
            update(lo + 4 * j, 4, half, None)
            return carry

        lax.fori_loop(0, lax.shift_right_logical(n, 2), quad, 0)

        @pl.when((n & 2) == 2)
        def _():
            update(lo + (n & ~3), 2, half, None)

        @pl.when((n & 1) == 1)
        def _():
            update(hi - 1, 1, half, None)

    n_far = jnp.maximum(i - SEL_NEAR + 1, 0)
    far_region(0, jnp.minimum(n_far, kh), 0)
    far_region(kh, n_far, 1)

    def near(e):
        kt = i - e
        update(kt, 1, jnp.where(kt < kh, 0, 1), bt_ref[0, :, e * SEL_T:(e + 1) * SEL_T])

    @pl.when(i >= SEL_NEAR - 1)
    def _():
        for e in range(SEL_NEAR - 1, -1, -1):
            near(e)

    @pl.when(i < SEL_NEAR - 1)
    def _():
        for e in range(SEL_NEAR - 2, -1, -1):
            @pl.when(i - e >= 0)
            def _(e=e):
                near(e)

    acc = acc_ref[...]
    o = acc[:, :HEAD_DIM] / acc[:, HEAD_DIM:HEAD_DIM + 1]
    o_ref[0] = _unstack_heads(o, SEL_T).astype(o_ref.dtype)


def _nsa_sel(proj, ks, vs, mb, rel_table):
    b, s, _ = proj.shape
    g, dh = NSA_G, HEAD_DIM
    nblk = s // SEL_LEN
    fw = nblk // 2
    rows = NSA_R * SEL_T
    pos = jnp.arange(s)
    onehot = (((pos // SEL_LEN) % fw)[:, None] == jnp.arange(fw)[None, :]).astype(BF16)
    ones2 = jnp.zeros((s, LANES - dh), BF16).at[:, :2].set(1.0)
    kp = jnp.concatenate([ks, jnp.broadcast_to(ones2, (b, g, s, LANES - dh)),
                          jnp.broadcast_to(onehot, (b, g, s, fw))], axis=-1)
    ones1 = jnp.zeros((s, LANES - dh), BF16).at[:, :1].set(1.0)
    vp = jnp.concatenate([vs, jnp.broadcast_to(ones1, (b, g, s, LANES - dh))], axis=-1)
    b_far = rel_table[REL_BUCKETS - 1].reshape(g, NSA_R) * LOG2E
    hi = b_far.astype(BF16)
    lo = (b_far - hi.astype(F32)).astype(BF16)
    fq = jnp.zeros((g, NSA_R, SEL_T, LANES - dh), BF16)
    fq = fq.at[:, :, :, 0].set(hi[:, :, None]).at[:, :, :, 1].set(lo[:, :, None]).reshape(g, rows, LANES - dh)
    qi = jnp.arange(SEL_T)[:, None]
    cc = jnp.arange(SEL_NEAR * SEL_T)[None, :]
    dd = (cc // SEL_T) * SEL_T + qi - (cc % SEL_T)
    bias = _rel_bias(rel_table, dd).reshape(g, NSA_R, SEL_T, -1) * LOG2E - b_far[:, :, None, None]
    bt = jnp.where(dd[None, None] >= 0, bias, NEG).reshape(g, rows, SEL_NEAR * SEL_T)
    return pl.pallas_call(
        _nsa_sel_kernel,
        grid=(g, b, s // SEL_T),
        in_specs=[pl.BlockSpec((1, SEL_T, NSA_R * dh), lambda gi, bi, i: (bi, i, gi)),
                  pl.BlockSpec((1, rows, LANES - dh), lambda gi, bi, i: (gi, 0, 0)),
                  pl.BlockSpec((1, 1, SEL_T, nblk), lambda gi, bi, i: (bi, gi, i, 0)),
                  pl.BlockSpec((1, 1, s, LANES + fw), lambda gi, bi, i: (bi, gi, 0, 0)),
                  pl.BlockSpec((1, 1, s, LANES), lambda gi, bi, i: (bi, gi, 0, 0)),
                  pl.BlockSpec((1, rows, SEL_NEAR * SEL_T), lambda gi, bi, i: (gi, 0, 0))],
        out_specs=pl.BlockSpec((1, SEL_T, NSA_R * dh), lambda gi, bi, i: (bi, i, gi)),
        out_shape=jax.ShapeDtypeStruct((b, s, N_HEADS * dh), BF16),
        scratch_shapes=[pltpu.VMEM((2, rows, LANES + fw), BF16), pltpu.VMEM((rows, LANES), F32),
                        pltpu.VMEM((rows, LANES), F32)],
        compiler_params=_params("parallel", "parallel", "parallel"),
        name="nsa_selected",
    )(proj, fq, mb, kp, vp, bt)


WIN_T = 512


def _softmax_rows(s):
    m = jnp.max(s, axis=1, keepdims=True)
    e = jnp.exp(s - m)
    den = jnp.sum(e, axis=1, keepdims=True)
    return e / den, m + jnp.log(den)


def _nsa_win_kernel(q_ref, kp_ref, kc_ref, vp_ref, vc_ref, bt_ref, o_ref):
    i = pl.program_id(2)
    back = WIN_T
    span = back + SUB_Q
    kcat = jnp.concatenate([kp_ref[0, 0], kc_ref[0, 0]], axis=0)
    vcat = jnp.concatenate([vp_ref[0, 0], vc_ref[0, 0]], axis=0)
    col = lax.broadcasted_iota(jnp.int32, (NSA_R * SUB_Q, span), 1)
    n_sub, rows = WIN_T // SUB_Q, NSA_R * SUB_Q
    scores = []
    for j in range(n_sub):
        qs = _stack_heads(q_ref[0, j * SUB_Q:(j + 1) * SUB_Q, :]) * (HEAD_DIM ** -0.5)
        s = _dot_nt(qs, kcat[j * SUB_Q:j * SUB_Q + span]) + bt_ref[0]
        scores.append(jnp.where((col >= back - j * SUB_Q) | (i > 0), s, NEG))
    p_all, _ = _softmax_rows(jnp.concatenate(scores, axis=0))
    p_all = p_all.astype(BF16)
    for j in range(n_sub):
        o = _dot(p_all[j * rows:(j + 1) * rows], vcat[j * SUB_Q:j * SUB_Q + span])
        o_ref[0, j * SUB_Q:(j + 1) * SUB_Q, :] = _unstack_heads(o, SUB_Q).astype(o_ref.dtype)


def _nsa_win(proj, kw, vw, rel_table):
    b, s, _ = proj.shape
    g, dh = NSA_G, HEAD_DIM
    span = WIN_T + SUB_Q
    qi = jnp.arange(SUB_Q)[:, None]
    cc = jnp.arange(span)[None, :]
    dd = qi + WIN_T - cc
    bias = _rel_bias(rel_table, dd)
    bt = jnp.where((dd[None] >= 0) & (dd[None] <= WIN_LEN - 1), bias, NEG).reshape(g, NSA_R * SUB_Q, span)
    prev = lambda gi, bi, i: (bi, gi, jnp.maximum(i - 1, 0), 0)
    cur = lambda gi, bi, i: (bi, gi, i, 0)
    return pl.pallas_call(
        _nsa_win_kernel,
        grid=(g, b, s // WIN_T),
        in_specs=[pl.BlockSpec((1, WIN_T, NSA_R * dh), lambda gi, bi, i: (bi, i, gi)),
                  pl.BlockSpec((1, 1, WIN_T, dh), prev), pl.BlockSpec((1, 1, WIN_T, dh), cur),
                  pl.BlockSpec((1, 1, WIN_T, dh), prev), pl.BlockSpec((1, 1, WIN_T, dh), cur),
                  pl.BlockSpec((1, NSA_R * SUB_Q, span), lambda gi, bi, i: (gi, 0, 0))],
        out_specs=pl.BlockSpec((1, WIN_T, NSA_R * dh), lambda gi, bi, i: (bi, i, gi)),
        out_shape=jax.ShapeDtypeStruct((b, s, N_HEADS * dh), BF16),
        compiler_params=_params("parallel", "parallel", "parallel"),
        name="nsa_window",
    )(proj, kw, kw, vw, vw, bt)


def _nsa_combine_kernel(oc_ref, os_ref, ow_ref, g_ref, e_ref, o_ref):
    d = o_ref.shape[2]
    sg = jax.nn.sigmoid(g_ref[0].astype(F32))
    ge = _dot_hi_exact_rhs(sg, e_ref[...])
    o = (ge[:, :d] * oc_ref[0].astype(F32) + ge[:, d:2 * d] * os_ref[0].astype(F32)
         + ge[:, 2 * d:] * ow_ref[0].astype(F32))
    o_ref[0] = o.astype(o_ref.dtype)


def _nsa_combine(o_cmp, o_sel, o_win, proj, gate_col, tm=512):
    b, s, d = o_cmp.shape
    c = jnp.arange(LANES)[:, None]
    n = jnp.arange(3 * d)[None, :]
    expand = (c == (n // d) * N_HEADS + (n % d) // HEAD_DIM).astype(BF16)
    spec = pl.BlockSpec((1, tm, d), lambda bi, i: (bi, i, 0))
    return pl.pallas_call(
        _nsa_combine_kernel,
        grid=(b, s // tm),
        in_specs=[spec, spec, spec,
                  pl.BlockSpec((1, tm, LANES), lambda bi, i: (bi, i, gate_col // LANES)),
                  pl.BlockSpec((LANES, 3 * d), lambda bi, i: (0, 0))],
        out_specs=spec,
        out_shape=jax.ShapeDtypeStruct((b, s, d), BF16),
        compiler_params=_params("parallel", "parallel"),
        name="nsa_combine",
    )(o_cmp, o_sel, o_win, proj, expand)


def _nsa_mixer(x, scale, shift, w_in, pe, w1, b1, w2, rel_table):
    b, s, _ = x.shape
    hd, gd = N_HEADS * HEAD_DIM, NSA_G * HEAD_DIM
    gate_col = hd + 6 * gd
    n_in = gate_col + LANES
    proj = _linear_in(x, scale, shift, _pad_cols(w_in, n_in).astype(BF16), tn=n_in)

    def group_major(t):
        return t.reshape(b, s, NSA_G, HEAD_DIM).transpose(0, 2, 1, 3)

    kc, vc, ks, vs, kw, vw = (proj[:, :, hd + n * gd: hd + (n + 1) * gd] for n in range(6))
    cmp = _compress(jnp.stack([kc, vc]), pe.reshape(2, -1), w1, b1, w2)
    o_cmp, mb = _nsa_cmp(proj, cmp[0], cmp[1], rel_table)
    o_sel = _nsa_sel(proj, group_major(ks), group_major(vs), mb, rel_table)
    o_win = _nsa_win(proj, group_major(kw), group_major(vw), rel_table)
    return _nsa_combine(o_cmp, o_sel, o_win, proj, gate_col)


DIL_BACK = 128


def _dil_kernel(q_ref, kp_ref, kc_ref, vp_ref, vc_ref, bt_ref, o_ref, l_ref):
    t = pl.program_id(3)
    tq = q_ref.shape[1]
    span = DIL_BACK + SUB_Q
    kcat = jnp.concatenate([kp_ref[0, tq - DIL_BACK:, :], kc_ref[0]], axis=0)
    vcat = jnp.concatenate([vp_ref[0, tq - DIL_BACK:, :], vc_ref[0]], axis=0)
    lane_q = lax.broadcasted_iota(jnp.int32, (SUB_Q, LANES), 1) // HEAD_DIM
    lane_k = lax.broadcasted_iota(jnp.int32, (span, LANES), 1) // HEAD_DIM
    col = lax.broadcasted_iota(jnp.int32, (SUB_Q, span), 1)
    n_sub, n_head = tq // SUB_Q, LANES // HEAD_DIM
    scores = []
    for j in range(n_sub):
        qj = q_ref[0, j * SUB_Q:(j + 1) * SUB_Q, :] * (HEAD_DIM ** -0.5)
        kk = kcat[j * SUB_Q:j * SUB_Q + span]
        for hh in range(n_head):
            s = _dot_nt(jnp.where(lane_q == hh, qj, 0.0).astype(BF16), kk) + bt_ref[hh]
            if j == 0:
                s = jnp.where((col >= DIL_BACK) | (t > 0), s, NEG)
            scores.append(s)
    p_all, lse_all = _softmax_rows(jnp.concatenate(scores, axis=0))
    p_all = p_all.astype(BF16)
    for j in range(n_sub):
        vv = vcat[j * SUB_Q:j * SUB_Q + span]
        o = jnp.zeros((SUB_Q, LANES), F32)
        lse = jnp.zeros((SUB_Q, LANES), F32)
        for hh in range(n_head):
            r0 = (j * n_head + hh) * SUB_Q
            o = o + _dot(p_all[r0:r0 + SUB_Q], jnp.where(lane_k == hh, vv, 0.0).astype(BF16))
            lse = jnp.where(lane_q == hh, lse_all[r0:r0 + SUB_Q], lse)
        o_ref[0, j * SUB_Q:(j + 1) * SUB_Q, :] = o.astype(o_ref.dtype)
        l_ref[0, j * SUB_Q:(j + 1) * SUB_Q, :] = lse


def _dil_group(view, window, dil, rel_table):
    b, l, n_all = view.shape
    hd = N_HEADS * HEAD_DIM
    assert window // dil == DIL_BACK and n_all == dil * 3 * hd
    tq = min(512, l)
    assert l % tq == 0 and tq >= DIL_BACK
    hpb = hd // LANES
    qi = jnp.arange(SUB_Q)[:, None]
    cc = jnp.arange(DIL_BACK + SUB_Q)[None, :]
    dd = qi + DIL_BACK - cc
    bt = jnp.where((dd[None] >= 0) & (dd[None] <= DIL_BACK), _rel_bias(rel_table, dd * dil), NEG)

    def spec(comp, prev):
        def index(bi, r, hp, t):
            return (bi, jnp.maximum(t - 1, 0) if prev else t, (r * 3 + comp) * hpb + hp)
        return pl.BlockSpec((1, tq, LANES), index)

    out_spec = pl.BlockSpec((1, tq, LANES), lambda bi, r, hp, t: (bi, t, r * hpb + hp))
    return pl.pallas_call(
        _dil_kernel,
        grid=(b, dil, hpb, l // tq),
        in_specs=[spec(0, False), spec(1, True), spec(1, False), spec(2, True), spec(2, False),
                  pl.BlockSpec((LANES // HEAD_DIM, SUB_Q, DIL_BACK + SUB_Q), lambda bi, r, hp, t: (hp, 0, 0))],
        out_specs=[out_spec, out_spec],
        out_shape=[jax.ShapeDtypeStruct((b, l, dil * hd), BF16), jax.ShapeDtypeStruct((b, l, dil * hd), F32)],
        compiler_params=_params("parallel", "parallel", "parallel", "parallel"),
        name=f"dilated_attention_{dil}",
    )(view, view, view, view, view, bt)


def _dil_merge_kernel(*refs, dils):
    n = len(dils)
    o_refs, l_refs, out_ref, scratch = refs[:n], refs[n:2 * n], refs[2 * n], refs[2 * n + 1:]
    tm, d = out_ref.shape[1], out_ref.shape[2]

    def token_major(ref, dil, buf):
        if dil == 1:
            return ref[0].astype(F32)
        for r in range(dil):
            for c in range(d // LANES):
                buf[c, pl.ds(r, tm // dil, stride=dil), :] = (
                    ref[0, :, r * d + c * LANES:r * d + (c + 1) * LANES].astype(F32))
        return jnp.concatenate([buf[c] for c in range(d // LANES)], axis=1)

    outs = [token_major(o_refs[g], dils[g], scratch[2 * g]) for g in range(n)]
    lses = [token_major(l_refs[g], dils[g], scratch[2 * g + 1]) for g in range(n)]
    m = functools.reduce(jnp.maximum, lses)
    es = [jnp.exp(l - m) for l in lses]
    inv = 1.0 / functools.reduce(jnp.add, es)
    out_ref[0] = functools.reduce(jnp.add, [e * inv * o for e, o in zip(es, outs)]).astype(out_ref.dtype)


def _dil_merge(outs, lses, dils, tm=512):
    b = outs[0].shape[0]
    d = N_HEADS * HEAD_DIM
    s = outs[0].shape[1] * dils[0]
    specs = [pl.BlockSpec((1, tm // dil, dil * d), lambda bi, i: (bi, i, 0)) for dil in dils]
    return pl.pallas_call(
        functools.partial(_dil_merge_kernel, dils=tuple(dils)),
        grid=(b, s // tm),
        in_specs=specs + specs,
        out_specs=pl.BlockSpec((1, tm, d), lambda bi, i: (bi, i, 0)),
        out_shape=jax.ShapeDtypeStruct((b, s, d), BF16),
        scratch_shapes=[pltpu.VMEM((d // LANES, tm, LANES), F32) for _ in range(2 * len(dils))],
        compiler_params=_params("parallel", "parallel"),
        name="dilated_merge",
    )(*outs, *lses)


def _dil_mixer(x, scale, shift, w_in, rel_table):
    hd = N_HEADS * HEAD_DIM
    w = w_in.astype(BF16)
    outs, lses, dils = [], [], []
    for gi, (window, dil) in enumerate(DIL_GROUPS):
        view = _linear_in(x, scale, shift, w[:, gi * 3 * hd:(gi + 1) * 3 * hd], tn=3 * hd, dil=dil)
        o, lse = _dil_group(view, window, dil, rel_table)
        outs.append(o)
        lses.append(lse)
        dils.append(dil)
    return _dil_merge(outs, lses, dils)


def kernel(x, c, rel_table, mod_w, mod_b, ln_g, ln_b, gla_w_in, gla_w_a2, gla_b_a, gla_norm_g, gla_w_o,
           nsa_w_in, nsa_cmp_pe, nsa_cmp_w1, nsa_cmp_b1, nsa_cmp_w2, nsa_w_o, dil_w_in, dil_w_o,
           ffn_w_up, ffn_conv_w, ffn_conv_b, ffn_w_down):
    shift, scale, gate = _modulation(c, mod_w, mod_b)
    for i in range(DEPTH):
        kind, j = i % N_MIXERS, i // N_MIXERS
        m = 2 * i
        if kind == 0:
            a = _gla_mixer(x, scale[m], shift[m], gla_w_in[j], gla_w_a2[j], gla_b_a[j], gla_norm_g[j])
            w_o = gla_w_o[j]
        elif kind == 1:
            a = _nsa_mixer(x, scale[m], shift[m], nsa_w_in[j], nsa_cmp_pe[j], nsa_cmp_w1[j], nsa_cmp_b1[j],
                           nsa_cmp_w2[j], rel_table)
            w_o = nsa_w_o[j]
        else:
            a = _dil_mixer(x, scale[m], shift[m], dil_w_in[j], rel_table)
            w_o = dil_w_o[j]
        x = _out_norm(a, w_o.astype(BF16), x, gate[m], ln_g[i, 0], ln_b[i, 0])
        a = _ffn_up(x, scale[m + 1], shift[m + 1], ffn_w_up[i].astype(BF16), ffn_conv_w[i], ffn_conv_b[i])
        x = _out_norm(a, ffn_w_down[i].astype(BF16), x, gate[m + 1], ln_g[i, 1], ln_b[i, 1])
    return x
```

```python
import functools
import math

import jax
import jax.numpy as jnp
import numpy as np
from jax import lax
from jax.experimental import pallas as pl
from jax.experimental.pallas import tpu as pltpu

F32 = jnp.float32
BF16 = jnp.bfloat16

D_MODEL = 1024
DEPTH = 4
N_MIXERS = 3
HEAD_DIM = 64
N_HEADS = D_MODEL // HEAD_DIM
REL_BUCKETS = 32
REL_MAX_DIST = 2048
LN_EPS = 1e-5
GLA_HEADS = 4
GLA_DK = D_MODEL // 2 // GLA_HEADS
GLA_DV = D_MODEL // GLA_HEADS
GLA_GATE_RANK = 16
GLA_TAU = 16.0
GLA_CHUNK = 64
NSA_G = 4
NSA_R = N_HEADS // NSA_G
CMP_LEN = 32
CMP_STRIDE = 16
CMP_HIDDEN = 256
SEL_LEN = 64
SEL_TOP = 16
WIN_LEN = 512
DIL_GROUPS = ((128, 1), (512, 4), (2048, 16))
D_FF = 2816
DN_ALPHA = (2 * DEPTH) ** 0.25

NEG = -1e30
VMEM_LIMIT = 56 * 1024 * 1024
LANES = 128
SUB_Q = 128


def _params(*sem):
    return pltpu.CompilerParams(dimension_semantics=sem, vmem_limit_bytes=VMEM_LIMIT)


def _dot(a, b):
    return jnp.dot(a, b, preferred_element_type=F32)


def _dot_nt(a, b):
    return lax.dot_general(a, b, (((1,), (1,)), ((), ())), preferred_element_type=F32)


def _dot_tn(a, b):
    return lax.dot_general(a, b, (((0,), (0,)), ((), ())), preferred_element_type=F32)


def _split(a):
    hi = a.astype(BF16)
    lo = (a - hi.astype(F32)).astype(BF16)
    return hi, lo


def _dot_hi_exact_rhs(a, b_bf16):
    hi, lo = _split(a)
    return _dot(hi, b_bf16) + _dot(lo, b_bf16)


def _silu(x):
    return x * jax.nn.sigmoid(x)


def _bucket_starts():
    n = np.arange(2 * REL_MAX_DIST)
    exact = REL_BUCKETS // 2
    logn = np.log(np.maximum(n, 1).astype(np.float32) / np.float32(exact))
    large = exact + (logn / np.float32(math.log(REL_MAX_DIST / exact)) * np.float32(REL_BUCKETS - exact)).astype(np.int32)
    bucket = np.where(n < exact, n, np.minimum(large, REL_BUCKETS - 1))
    assert np.all(np.diff(bucket) >= 0) and bucket[-1] == REL_BUCKETS - 1
    return [int(np.argmax(bucket >= k)) for k in range(REL_BUCKETS)]


BUCKET_START = _bucket_starts()
FAR_DIST = BUCKET_START[-1]


def _rel_bias(rel_table, dist):
    shape = (rel_table.shape[1],) + (1,) * dist.ndim
    out = jnp.broadcast_to(rel_table[0].reshape(shape), (rel_table.shape[1],) + dist.shape)
    for k in range(1, REL_BUCKETS):
        out = jnp.where(dist[None] >= BUCKET_START[k], rel_table[k].reshape(shape), out)
    return out


def _mod_kernel(c_ref, w_ref, b_ref, o_ref):
    cs = _silu(c_ref[...])
    ch, cl = _split(cs)
    wh, wl = _split(w_ref[0])
    o_ref[0] = _dot(ch, wh) + _dot(ch, wl) + _dot(cl, wh) + b_ref[0]


def _modulation(c, mod_w, mod_b):
    b, d = c.shape
    n_sub = mod_w.shape[0] * mod_w.shape[1]
    rows = 8
    cp = jnp.zeros((rows, d), F32).at[:b].set(c)
    w = mod_w.reshape(n_sub, d, 3 * d)
    bias = mod_b.reshape(n_sub, 1, 3 * d)
    tn = d
    out = pl.pallas_call(
        _mod_kernel,
        grid=(n_sub, 3 * d // tn),
        in_specs=[pl.BlockSpec((rows, d), lambda i, j: (0, 0)),
                  pl.BlockSpec((1, d, tn), lambda i, j: (i, 0, j)),
                  pl.BlockSpec((1, 1, tn), lambda i, j: (i, 0, j))],
        out_specs=pl.BlockSpec((1, rows, tn), lambda i, j: (i, 0, j)),
        out_shape=jax.ShapeDtypeStruct((n_sub, rows, 3 * d), F32),
        compiler_params=_params("parallel", "parallel"),
        name="modulation",
    )(cp, w, bias)
    out = out[:, :b]
    shift, scale, gate = jnp.split(out, 3, axis=-1)
    return shift[:, :, None, :], scale[:, :, None, :], gate[:, :, None, :]


def _linear_in_kernel(x_ref, sc_ref, sh_ref, w_ref, *rest, dil):
    h = (x_ref[0] * (1.0 + sc_ref[0]) + sh_ref[0]).astype(BF16)
    if dil == 1:
        o_ref, = rest
        o_ref[0] = _dot(h, w_ref[...]).astype(o_ref.dtype)
    else:
        perm_ref, o_ref = rest
        res = _dot(_dot(perm_ref[...], h).astype(BF16), w_ref[...])
        rows, n = res.shape[0] // dil, res.shape[1]
        for r in range(dil):
            o_ref[0, :, r * n:(r + 1) * n] = res[r * rows:(r + 1) * rows].astype(o_ref.dtype)


def _linear_in(x, scale, shift, w, tn, tm=512, dil=1):
    b, s, d = x.shape
    n = w.shape[1]
    assert n % tn == 0 and s % tm == 0 and tm % (16 * dil) == 0 and (dil == 1 or n == tn)
    in_specs = [pl.BlockSpec((1, tm, d), lambda j, bi, i: (bi, i, 0)),
                pl.BlockSpec((1, 1, d), lambda j, bi, i: (bi, 0, 0)),
                pl.BlockSpec((1, 1, d), lambda j, bi, i: (bi, 0, 0)),
                pl.BlockSpec((d, tn), lambda j, bi, i: (0, j))]
    args = [x, scale, shift, w]
    if dil > 1:
        out_row = jnp.arange(tm)
        src = (out_row % (tm // dil)) * dil + out_row // (tm // dil)
        in_specs.append(pl.BlockSpec((tm, tm), lambda j, bi, i: (0, 0)))
        args.append((src[:, None] == jnp.arange(tm)[None, :]).astype(BF16))
    return pl.pallas_call(
        functools.partial(_linear_in_kernel, dil=dil),
        grid=(n // tn, b, s // tm),
        in_specs=in_specs,
        out_specs=pl.BlockSpec((1, tm // dil, dil * tn), lambda j, bi, i: (bi, i, j)),
        out_shape=jax.ShapeDtypeStruct((b, s // dil, dil * n), BF16),
        compiler_params=_params("parallel", "parallel", "parallel"),
        name="linear_in",
    )(*args)


OUT_ROWS = 512


def _out_norm_kernel(a_ref, w_ref, x_ref, gate_ref, g_ref, b_ref, o_ref):
    chunks = [slice(r, r + OUT_ROWS) for r in range(0, a_ref.shape[1], OUT_ROWS)]
    ys = [_dot(a_ref[0, sl, :], w_ref[...]) for sl in chunks]
    for sl, y in zip(chunks, ys):
        z = DN_ALPHA * x_ref[0, sl, :] + (1.0 + gate_ref[0]) * y
        mu = jnp.mean(z, axis=-1, keepdims=True)
        zc = z - mu
        var = jnp.mean(zc * zc, axis=-1, keepdims=True)
        o_ref[0, sl, :] = zc * lax.rsqrt(var + LN_EPS) * g_ref[...] + b_ref[...]


def _out_norm(a, w, x, gate, ln_g, ln_b, tm=1024):
    b, s, d = x.shape
    k = a.shape[-1]
    return pl.pallas_call(
        _out_norm_kernel,
        grid=(b, s // tm),
        in_specs=[pl.BlockSpec((1, tm, k), lambda bi, i: (bi, i, 0)),
                  pl.BlockSpec((k, d), lambda bi, i: (0, 0)),
                  pl.BlockSpec((1, tm, d), lambda bi, i: (bi, i, 0)),
                  pl.BlockSpec((1, 1, d), lambda bi, i: (bi, 0, 0)),
                  pl.BlockSpec((1, d), lambda bi, i: (0, 0)),
                  pl.BlockSpec((1, d), lambda bi, i: (0, 0))],
        out_specs=pl.BlockSpec((1, tm, d), lambda bi, i: (bi, i, 0)),
        out_shape=jax.ShapeDtypeStruct((b, s, d), F32),
        compiler_params=_params("parallel", "parallel"),
        name="out_norm",
    )(a, w, x, gate, ln_g.reshape(1, d), ln_b.reshape(1, d))


FFN_HALO = 16


def _ffn_up_kernel(x_ref, xh_ref, sc_ref, sh_ref, wu_ref, wg_ref, cw_ref, cb_ref, o_ref, h_ref, u_ref):
    i = pl.program_id(1)
    tm = x_ref.shape[1]

    @pl.when(pl.program_id(2) == 0)
    def _():
        sc = 1.0 + sc_ref[0]
        sh = sh_ref[0]
        h_ref[0:FFN_HALO, :] = (xh_ref[0] * sc + sh).astype(BF16)
        h_ref[FFN_HALO:, :] = (x_ref[0] * sc + sh).astype(BF16)

    keep = (i > 0).astype(F32)
    gs = []
    for r in range(0, tm, OUT_ROWS):
        lo, hi = (0 if r == 0 else FFN_HALO + r), FFN_HALO + r + OUT_ROWS
        u = _dot(h_ref[lo:hi, :], wu_ref[...])
        if r == 0:
            u_ref[0:FFN_HALO, :] = u[0:FFN_HALO] * keep
            u_ref[FFN_HALO:hi, :] = u[FFN_HALO:]
        else:
            u_ref[lo:hi, :] = u
        gs.append(_dot(h_ref[FFN_HALO + r:hi, :], wg_ref[...]))
    cw = cw_ref[...]
    for c, r in enumerate(range(0, tm, OUT_ROWS)):
        at = FFN_HALO + r
        conv = (u_ref[pl.ds(at - 2, OUT_ROWS), :] * cw[0:1] + u_ref[pl.ds(at - 1, OUT_ROWS), :] * cw[1:2]
                + u_ref[pl.ds(at, OUT_ROWS), :] * cw[2:3] + cb_ref[...])
        o_ref[0, r:r + OUT_ROWS, :] = (_silu(conv) * gs[c]).astype(o_ref.dtype)


def _ffn_up(x, scale, shift, w_up, conv_w, conv_b, tm=1024, tf=1408):
    b, s, d = x.shape
    f = conv_w.shape[1]
    assert f % tf == 0 and s % tm == 0 and tm % FFN_HALO == 0
    nf = f // tf
    hb = tm // FFN_HALO
    return pl.pallas_call(
        _ffn_up_kernel,
        grid=(b, s // tm, nf),
        in_specs=[pl.BlockSpec((1, tm, d), lambda bi, i, j: (bi, i, 0)),
                  pl.BlockSpec((1, FFN_HALO, d), lambda bi, i, j: (bi, jnp.maximum(i * hb - 1, 0), 0)),
                  pl.BlockSpec((1, 1, d), lambda bi, i, j: (bi, 0, 0)),
                  pl.BlockSpec((1, 1, d), lambda bi, i, j: (bi, 0, 0)),
                  pl.BlockSpec((d, tf), lambda bi, i, j: (0, j)),
                  pl.BlockSpec((d, tf), lambda bi, i, j: (0, nf + j)),
                  pl.BlockSpec((3, tf), lambda bi, i, j: (0, j)),
                  pl.BlockSpec((1, tf), lambda bi, i, j: (0, j))],
        out_specs=pl.BlockSpec((1, tm, tf), lambda bi, i, j: (bi, i, j)),
        out_shape=jax.ShapeDtypeStruct((b, s, f), BF16),
        scratch_shapes=[pltpu.VMEM((tm + FFN_HALO, d), BF16), pltpu.VMEM((tm + FFN_HALO, tf), F32)],
        compiler_params=_params("parallel", "parallel", "arbitrary"),
        name="ffn_up",
    )(x, x, scale, shift, w_up, w_up, conv_w, conv_b.reshape(1, f))


def _gla_kernel(q_ref, k_ref, v_ref, r_ref, a_ref, wa_ref, ba_ref, ng_ref, o_ref, st_ref):
    tc = q_ref.shape[1]
    nchunk = tc // GLA_CHUNK

    @pl.when(pl.program_id(2) == 0)
    def _():
        st_ref[...] = jnp.zeros_like(st_ref)

    z = _dot(a_ref[0], wa_ref[...]) + ba_ref[...]
    log_a = (jnp.minimum(z, 0.0) - jnp.log1p(jnp.exp(-jnp.abs(z)))) / GLA_TAU
    row = lax.broadcasted_iota(jnp.int32, (tc, tc), 0)
    col = lax.broadcasted_iota(jnp.int32, (tc, tc), 1)
    same = (row // GLA_CHUNK) == (col // GLA_CHUNK)
    cum_m = jnp.where(same & (col <= row), 1.0, 0.0).astype(BF16)
    tot_m = jnp.where(same, 1.0, 0.0).astype(BF16)
    la_hi, la_lo = _split(log_a)
    bcum = _dot(cum_m, la_hi) + _dot(cum_m, la_lo)
    btot = _dot(tot_m, la_hi) + _dot(tot_m, la_lo)
    qf = q_ref[0].astype(F32) * (GLA_DK ** -0.5)
    kf = k_ref[0].astype(F32)
    q_dec = (qf * jnp.exp(bcum)).astype(BF16)
    k_inv = (kf * jnp.exp(-bcum)).astype(BF16)
    k_dec = (kf * jnp.exp(btot - bcum)).astype(BF16)
    dl = jnp.exp(btot)
    ci = lax.broadcasted_iota(jnp.int32, (GLA_CHUNK, GLA_CHUNK), 0)
    cj = lax.broadcasted_iota(jnp.int32, (GLA_CHUNK, GLA_CHUNK), 1)
    causal = cj <= ci
    v = v_ref[0]
    chunks = [slice(c * GLA_CHUNK, (c + 1) * GLA_CHUNK) for c in range(nchunk)]
    att = [jnp.where(causal, _dot_nt(q_dec[sl], k_inv[sl]), 0.0).astype(BF16) for sl in chunks]
    kv = [_dot_tn(v[sl], k_dec[sl]) for sl in chunks]
    o_intra = [_dot(att[c], v[chunks[c]]) for c in range(nchunk)]
    st = st_ref[...]
    o_inter = []
    for c, sl in enumerate(chunks):
        o_inter.append(_dot_nt(q_dec[sl], st.astype(BF16)))
        st = st * dl[c * GLA_CHUNK:c * GLA_CHUNK + 1] + kv[c]
    st_ref[...] = st
    o = jnp.concatenate(o_intra, axis=0) + jnp.concatenate(o_inter, axis=0)
    o = o * lax.rsqrt(jnp.mean(o * o, axis=-1, keepdims=True) + LN_EPS) * ng_ref[...]
    o_ref[0] = (o * _silu(r_ref[0].astype(F32))).astype(o_ref.dtype)


def _gla_core(proj, w_a2, b_a, norm_g, tc=512):
    b, s, _ = proj.shape
    h, dk, dv = GLA_HEADS, GLA_DK, GLA_DV
    assert s % tc == 0
    wa = jnp.zeros((LANES, h * dk), F32).at[:GLA_GATE_RANK].set(w_a2).astype(BF16)
    kb, vb, rb, ab = (h * dk) // dk, (2 * h * dk) // dv, (2 * h * dk + h * dv) // dv, (2 * h * dk + 2 * h * dv) // LANES
    return pl.pallas_call(
        _gla_kernel,
        grid=(b, h, s // tc),
        in_specs=[pl.BlockSpec((1, tc, dk), lambda bi, hi, t: (bi, t, hi)),
                  pl.BlockSpec((1, tc, dk), lambda bi, hi, t: (bi, t, kb + hi)),
                  pl.BlockSpec((1, tc, dv), lambda bi, hi, t: (bi, t, vb + hi)),
                  pl.BlockSpec((1, tc, dv), lambda bi, hi, t: (bi, t, rb + hi)),
                  pl.BlockSpec((1, tc, LANES), lambda bi, hi, t: (bi, t, ab)),
                  pl.BlockSpec((LANES, dk), lambda bi, hi, t: (0, hi)),
                  pl.BlockSpec((1, dk), lambda bi, hi, t: (0, hi)),
                  pl.BlockSpec((1, dv), lambda bi, hi, t: (0, 0))],
        out_specs=pl.BlockSpec((1, tc, dv), lambda bi, hi, t: (bi, t, hi)),
        out_shape=jax.ShapeDtypeStruct((b, s, h * dv), BF16),
        scratch_shapes=[pltpu.VMEM((dv, dk), F32)],
        compiler_params=_params("parallel", "parallel", "arbitrary"),
        name="gla_core",
    )(proj, proj, proj, proj, proj, wa, b_a.reshape(1, h * dk), norm_g.reshape(1, dv))


def _pad_cols(w, n):
    return jnp.pad(w, ((0, 0), (0, n - w.shape[1])))


def _gla_mixer(x, scale, shift, w_in, w_a2, b_a, norm_g):
    n_in = 2 * GLA_HEADS * GLA_DK + 2 * GLA_HEADS * GLA_DV + LANES
    proj = _linear_in(x, scale, shift, _pad_cols(w_in, n_in).astype(BF16), tn=n_in)
    return _gla_core(proj, w_a2, b_a, norm_g)


def _compress_kernel(lo_ref, hi_ref, pe_ref, w1_ref, b1_ref, w2_ref, o_ref):
    half = lo_ref.shape[-1]
    pe = pe_ref[0]
    a_lo = (lo_ref[0, 0, 0].astype(F32) + pe[:, :half]).astype(BF16)
    a_hi = (hi_ref[0, 0, 0].astype(F32) + pe[:, half:]).astype(BF16)
    h1 = _dot(a_lo, w1_ref[0, :half, :]) + _dot(a_hi, w1_ref[0, half:, :]) + b1_ref[0]
    o_ref[0, 0, 0] = _dot(_silu(h1).astype(BF16), w2_ref[0])


def _compress(kv, pe, w1, b1, w2):
    two, b, s, _ = kv.shape
    g, dh = NSA_G, HEAD_DIM
    npc = s // CMP_STRIDE
    half = CMP_STRIDE * dh
    a = kv.reshape(two, b, npc, CMP_STRIDE, g, dh).transpose(0, 1, 4, 2, 3, 5).reshape(two, b, g, npc, half)
    a_hi = jnp.concatenate([a[:, :, :, 1:], jnp.zeros_like(a[:, :, :, :1])], axis=3)
    spec_a = pl.BlockSpec((1, 1, 1, npc, half), lambda c, bi, gi: (c, bi, gi, 0, 0))
    return pl.pallas_call(
        _compress_kernel,
        grid=(two, b, g),
        in_specs=[spec_a, spec_a,
                  pl.BlockSpec((1, 1, 2 * half), lambda c, bi, gi: (c, 0, 0)),
                  pl.BlockSpec((1, 2 * half, CMP_HIDDEN), lambda c, bi, gi: (c, 0, 0)),
                  pl.BlockSpec((1, 1, CMP_HIDDEN), lambda c, bi, gi: (c, 0, 0)),
                  pl.BlockSpec((1, CMP_HIDDEN, dh), lambda c, bi, gi: (c, 0, 0))],
        out_specs=pl.BlockSpec((1, 1, 1, npc, dh), lambda c, bi, gi: (c, bi, gi, 0, 0)),
        out_shape=jax.ShapeDtypeStruct((two, b, g, npc, dh), F32),
        compiler_params=_params("parallel", "parallel", "parallel"),
        name="nsa_compress",
    )(a, a_hi, pe.reshape(two, 1, 2 * half), w1.astype(BF16), b1.reshape(two, 1, CMP_HIDDEN), w2.astype(BF16))


def _stack_heads(q):
    return jnp.concatenate([q[:, r * HEAD_DIM:(r + 1) * HEAD_DIM] for r in range(NSA_R)], axis=0)


def _unstack_heads(o, rows):
    return jnp.concatenate([o[r * rows:(r + 1) * rows] for r in range(NSA_R)], axis=1)


CMP_TQ = 512
LOG2E = 1.4426950408889634


def _nsa_cmp_kernel(q_ref, fq_ref, kc_ref, vc_ref, nb_ref, pb_ref, o_ref, mb_ref, imp_ref):
    i = pl.program_id(2)
    n_tile = pl.num_programs(2) - 1

    @pl.when(i == 0)
    def _():
        _cmp_attend(i, q_ref, fq_ref, kc_ref, vc_ref, nb_ref, pb_ref, o_ref, imp_ref)

    @pl.when((i > 0) & (i < n_tile))
    def _():
        imp = imp_ref[...]
        _cmp_attend(i, q_ref, fq_ref, kc_ref, vc_ref, nb_ref, pb_ref, o_ref, imp_ref)
        _cmp_select(i - 1, imp, mb_ref)

    @pl.when(i == n_tile)
    def _():
        _cmp_select(i - 1, imp_ref[...], mb_ref)


def _cmp_attend(i, q_ref, fq_ref, kc_ref, vc_ref, nb_ref, pb_ref, o_ref, imp_ref):
    npc = nb_ref.shape[2]
    n_sub, rows = CMP_TQ // SUB_Q, NSA_R * SUB_Q
    starts = [pl.multiple_of((i * n_sub + j) * (SUB_Q // CMP_STRIDE), 8) for j in range(n_sub)]

    scores = []
    for j in range(n_sub):
        kwin = kc_ref[0, 0, pl.ds(starts[j], npc), :].astype(BF16)
        qs = _stack_heads(q_ref[0, j * SUB_Q:(j + 1) * SUB_Q, :]).astype(F32) * (HEAD_DIM ** -0.5 * LOG2E)
        qp = jnp.concatenate([qs, fq_ref[...]], axis=1).astype(BF16)
        scores.append(_dot_nt(qp, kwin) + nb_ref[0])
    s = jnp.concatenate(scores, axis=0)
    m = jnp.max(s, axis=1, keepdims=True)
    e = jnp.exp2(s - m)
    den = jnp.sum(e, axis=1, keepdims=True)
    p = e * jnp.where(m > 0.5 * NEG, 1.0 / den, 0.0)
    p_bf = p.astype(BF16)
    for j in range(n_sub):
        vwin = vc_ref[0, 0, pl.ds(starts[j], npc), :].astype(BF16)
        o = _dot(p_bf[j * rows:(j + 1) * rows], vwin)
        o_ref[0, j * SUB_Q:(j + 1) * SUB_Q, :] = _unstack_heads(o, SUB_Q).astype(o_ref.dtype)

    imps = []
    for j in range(n_sub):
        pg = p[j * rows:j * rows + SUB_Q]
        for r in range(1, NSA_R):
            pg = pg + p[j * rows + r * SUB_Q:j * rows + (r + 1) * SUB_Q]
        imps.append(_dot_hi_exact_rhs(pg, pb_ref[pl.ds(starts[j], npc), :].astype(BF16)))
    imp_ref[...] = jnp.concatenate(imps, axis=0)


def _cmp_select(i, imp, mb_ref):
    nblk = mb_ref.shape[3]
    blk = lax.broadcasted_iota(jnp.int32, (CMP_TQ, nblk), 1)
    blk_f = blk.astype(F32)
    t = i * CMP_TQ + lax.broadcasted_iota(jnp.int32, (CMP_TQ, nblk), 0)
    cur = t // SEL_LEN
    forced = (blk == 0) | (blk == cur) | (blk == cur - 1)
    work = jnp.where((blk * SEL_LEN <= t) & jnp.logical_not(forced), imp, NEG)
    free = work > 0.5 * NEG
    for _ in range(SEL_TOP - 3):
        mx = jnp.max(work, axis=1, keepdims=True)
        first = jnp.min(jnp.where(work == mx, blk_f, 1e9), axis=1, keepdims=True)
        work = jnp.where(blk_f == first, 2.0 * NEG, work)
    sel = forced | (free & (work < 1.5 * NEG))
    mb_ref[0, 0] = jnp.where(sel, 0.0, NEG).astype(mb_ref.dtype)


def _nsa_cmp(proj, kcmp, vcmp, rel_table):
    b, s, _ = proj.shape
    g, dh = NSA_G, HEAD_DIM
    npc = s // CMP_STRIDE
    nblk = s // SEL_LEN
    front = npc - SUB_Q // CMP_STRIDE
    assert s % CMP_TQ == 0 and nblk >= SEL_TOP
    n_tile = s // CMP_TQ
    kpad = jnp.pad(kcmp, ((0, 0), (0, 0), (front, npc - front), (0, LANES - dh)))
    kpad = kpad.at[:, :, :front, dh].set(1.0)
    vpad = jnp.pad(vcmp, ((0, 0), (0, 0), (front, npc - front), (0, 0)))
    fq = jnp.zeros((NSA_R * SUB_Q, LANES - dh), F32).at[:, 0].set(NEG)
    qi = jnp.arange(SUB_Q)[:, None]
    cc = jnp.arange(npc)[None, :]
    dd = qi - (CMP_LEN - 1) + CMP_STRIDE * (front - cc)
    nb = jnp.where(dd[None] >= 0, _rel_bias(rel_table, dd) * LOG2E, NEG).reshape(g, NSA_R * SUB_Q, npc)
    diff = jnp.arange(2 * npc)[:, None] - front - (SEL_LEN // CMP_STRIDE) * jnp.arange(nblk)[None, :]
    pool = jnp.where((diff == -1) | (diff == 3), 1.0, jnp.where((diff >= 0) & (diff <= 2), 2.0, 0.0))
    return pl.pallas_call(
        _nsa_cmp_kernel,
        grid=(g, b, n_tile + 1),
        in_specs=[pl.BlockSpec((1, CMP_TQ, NSA_R * dh), lambda gi, bi, i: (bi, jnp.minimum(i, n_tile - 1), gi)),
                  pl.BlockSpec((NSA_R * SUB_Q, LANES - dh), lambda gi, bi, i: (0, 0)),
                  pl.BlockSpec((1, 1, 2 * npc, LANES), lambda gi, bi, i: (bi, gi, 0, 0)),
                  pl.BlockSpec((1, 1, 2 * npc, dh), lambda gi, bi, i: (bi, gi, 0, 0)),
                  pl.BlockSpec((1, NSA_R * SUB_Q, npc), lambda gi, bi, i: (gi, 0, 0)),
                  pl.BlockSpec((2 * npc, nblk), lambda gi, bi, i: (0, 0))],
        out_specs=[pl.BlockSpec((1, CMP_TQ, NSA_R * dh), lambda gi, bi, i: (bi, jnp.minimum(i, n_tile - 1), gi)),
                   pl.BlockSpec((1, 1, CMP_TQ, nblk), lambda gi, bi, i: (bi, gi, jnp.maximum(i - 1, 0), 0))],
        out_shape=[jax.ShapeDtypeStruct((b, s, N_HEADS * dh), BF16),
                   jax.ShapeDtypeStruct((b, g, s, nblk), BF16)],
        scratch_shapes=[pltpu.VMEM((CMP_TQ, nblk), F32)],
        compiler_params=_params("parallel", "parallel", "arbitrary"),
        name="nsa_cmp_select",
    )(proj, fq, kpad, vpad, nb, pool.astype(F32))


SEL_T = 256
SEL_NEAR = -(-(FAR_DIST + SEL_T - 1) // SEL_T)


def _nsa_sel_kernel(q_ref, fq_ref, mb_ref, k_ref, v_ref, bt_ref, o_ref, qp_ref, m_ref, acc_ref):
    i = pl.program_id(2)
    rows = NSA_R * SEL_T
    fw = (k_ref.shape[3] - LANES)
    kh = fw // (SEL_T // SEL_LEN)
    qs = (_stack_heads(q_ref[0]).astype(F32) * (HEAD_DIM ** -0.5 * LOG2E)).astype(BF16)
    mb = mb_ref[0, 0]
    for half in range(2):
        mbh = jnp.concatenate([mb[:, half * fw:(half + 1) * fw]] * NSA_R, axis=0)
        qp_ref[half] = jnp.concatenate([qs, fq_ref[0], mbh], axis=1)
    m_ref[...] = jnp.full_like(m_ref, NEG)
    acc_ref[...] = jnp.zeros_like(acc_ref)

    def update(kt, n_tile, half, bias):
        k0 = pl.multiple_of(kt * SEL_T, SEL_T)
        kk = k_ref[0, 0, pl.ds(k0, n_tile * SEL_T), :]
        vv = v_ref[0, 0, pl.ds(k0, n_tile * SEL_T), :]
        s = _dot_nt(qp_ref[half], kk)
        if bias is not None:
            s = s + bias
        m_prev = m_ref[...]
        m_new = jnp.maximum(m_prev, jnp.max(s, axis=1, keepdims=True))
        alpha = jnp.exp2(m_prev - m_new)
        p = jnp.exp2(s - jnp.concatenate([m_new] * (n_tile * SEL_T // LANES), axis=1))
        acc_ref[...] = acc_ref[...] * alpha + _dot(p.astype(BF16), vv)
        m_ref[...] = m_new

    def far_region(lo, hi, half):
        n = jnp.maximum(hi - lo, 0)

        def quad(j, carry):
            update(lo + 4 * j, 4, half, None)
            return carry

        lax.fori_loop(0, lax.shift_right_logical(n, 2), quad, 0)

        @pl.when((n & 2) == 2)
        def _():
            update(lo + (n & ~3), 2, half, None)

        @pl.when((n & 1) == 1)
        def _():
            update(hi - 1, 1, half, None)

    n_far = jnp.maximum(i - SEL_NEAR + 1, 0)
    far_region(0, jnp.minimum(n_far, kh), 0)
    far_region(kh, n_far, 1)

    def near(e):
        kt = i - e
        update(kt, 1, jnp.where(kt < kh, 0, 1), bt_ref[0, :, e * SEL_T:(e + 1) * SEL_T])

    @pl.when(i >= SEL_NEAR - 1)
    def _():
        for e in range(SEL_NEAR - 1, -1, -1):
            near(e)

    @pl.when(i < SEL_NEAR - 1)
    def _():
        for e in range(SEL_NEAR - 2, -1, -1):
            @pl.when(i - e >= 0)
            def _(e=e):
                near(e)

    acc = acc_ref[...]
    o = acc[:, :HEAD_DIM] / acc[:, HEAD_DIM:HEAD_DIM + 1]
    o_ref[0] = _unstack_heads(o, SEL_T).astype(o_ref.dtype)


def _nsa_sel(proj, ks, vs, mb, rel_table):
    b, s, _ = proj.shape
    g, dh = NSA_G, HEAD_DIM
    nblk = s // SEL_LEN
    fw = nblk // 2
    rows = NSA_R * SEL_T
    pos = jnp.arange(s)
    onehot = (((pos // SEL_LEN) % fw)[:, None] == jnp.arange(fw)[None, :]).astype(BF16)
    ones2 = jnp.zeros((s, LANES - dh), BF16).at[:, :2].set(1.0)
    kp = jnp.concatenate([ks, jnp.broadcast_to(ones2, (b, g, s, LANES - dh)),
                          jnp.broadcast_to(onehot, (b, g, s, fw))], axis=-1)
    ones1 = jnp.zeros((s, LANES - dh), BF16).at[:, :1].set(1.0)
    vp = jnp.concatenate([vs, jnp.broadcast_to(ones1, (b, g, s, LANES - dh))], axis=-1)
    b_far = rel_table[REL_BUCKETS - 1].reshape(g, NSA_R) * LOG2E
    hi = b_far.astype(BF16)
    lo = (b_far - hi.astype(F32)).astype(BF16)
    fq = jnp.zeros((g, NSA_R, SEL_T, LANES - dh), BF16)
    fq = fq.at[:, :, :, 0].set(hi[:, :, None]).at[:, :, :, 1].set(lo[:, :, None]).reshape(g, rows, LANES - dh)
    qi = jnp.arange(SEL_T)[:, None]
    cc = jnp.arange(SEL_NEAR * SEL_T)[None, :]
    dd = (cc // SEL_T) * SEL_T + qi - (cc % SEL_T)
    bias = _rel_bias(rel_table, dd).reshape(g, NSA_R, SEL_T, -1) * LOG2E - b_far[:, :, None, None]
    bt = jnp.where(dd[None, None] >= 0, bias, NEG).reshape(g, rows, SEL_NEAR * SEL_T)
    return pl.pallas_call(
        _nsa_sel_kernel,
        grid=(g, b, s // SEL_T),
        in_specs=[pl.BlockSpec((1, SEL_T, NSA_R * dh), lambda gi, bi, i: (bi, i, gi)),
                  pl.BlockSpec((1, rows, LANES - dh), lambda gi, bi, i: (gi, 0, 0)),
                  pl.BlockSpec((1, 1, SEL_T, nblk), lambda gi, bi, i: (bi, gi, i, 0)),
                  pl.BlockSpec((1, 1, s, LANES + fw), lambda gi, bi, i: (bi, gi, 0, 0)),
                  pl.BlockSpec((1, 1, s, LANES), lambda gi, bi, i: (bi, gi, 0, 0)),
                  pl.BlockSpec((1, rows, SEL_NEAR * SEL_T), lambda gi, bi, i: (gi, 0, 0))],
        out_specs=pl.BlockSpec((1, SEL_T, NSA_R * dh), lambda gi, bi, i: (bi, i, gi)),
        out_shape=jax.ShapeDtypeStruct((b, s, N_HEADS * dh), BF16),
        scratch_shapes=[pltpu.VMEM((2, rows, LANES + fw), BF16), pltpu.VMEM((rows, LANES), F32),
                        pltpu.VMEM((rows, LANES), F32)],
        compiler_params=_params("parallel", "parallel", "parallel"),
        name="nsa_selected",
    )(proj, fq, mb, kp, vp, bt)


WIN_T = 512


def _softmax_rows(s):
    m = jnp.max(s, axis=1, keepdims=True)
    e = jnp.exp(s - m)
    den = jnp.sum(e, axis=1, keepdims=True)
    return e / den, m + jnp.log(den)


def _nsa_win_kernel(q_ref, kp_ref, kc_ref, vp_ref, vc_ref, bt_ref, o_ref):
    i = pl.program_id(2)
    back = WIN_T
    span = back + SUB_Q
    kcat = jnp.concatenate([kp_ref[0, 0], kc_ref[0, 0]], axis=0)
    vcat = jnp.concatenate([vp_ref[0, 0], vc_ref[0, 0]], axis=0)
    col = lax.broadcasted_iota(jnp.int32, (NSA_R * SUB_Q, span), 1)
    n_sub, rows = WIN_T // SUB_Q, NSA_R * SUB_Q
    scores = []
    for j in range(n_sub):
        qs = _stack_heads(q_ref[0, j * SUB_Q:(j + 1) * SUB_Q, :]) * (HEAD_DIM ** -0.5)
        s = _dot_nt(qs, kcat[j * SUB_Q:j * SUB_Q + span]) + bt_ref[0]
        scores.append(jnp.where((col >= back - j * SUB_Q) | (i > 0), s, NEG))
    p_all, _ = _softmax_rows(jnp.concatenate(scores, axis=0))
    p_all = p_all.astype(BF16)
    for j in range(n_sub):
        o = _dot(p_all[j * rows:(j + 1) * rows], vcat[j * SUB_Q:j * SUB_Q + span])
        o_ref[0, j * SUB_Q:(j + 1) * SUB_Q, :] = _unstack_heads(o, SUB_Q).astype(o_ref.dtype)


def _nsa_win(proj, kw, vw, rel_table):
    b, s, _ = proj.shape
    g, dh = NSA_G, HEAD_DIM
    span = WIN_T + SUB_Q
    qi = jnp.arange(SUB_Q)[:, None]
    cc = jnp.arange(span)[None, :]
    dd = qi + WIN_T - cc
    bias = _rel_bias(rel_table, dd)
    bt = jnp.where((dd[None] >= 0) & (dd[None] <= WIN_LEN - 1), bias, NEG).reshape(g, NSA_R * SUB_Q, span)
    prev = lambda gi, bi, i: (bi, gi, jnp.maximum(i - 1, 0), 0)
    cur = lambda gi, bi, i: (bi, gi, i, 0)
    return pl.pallas_call(
        _nsa_win_kernel,
        grid=(g, b, s // WIN_T),
        in_specs=[pl.BlockSpec((1, WIN_T, NSA_R * dh), lambda gi, bi, i: (bi, i, gi)),
                  pl.BlockSpec((1, 1, WIN_T, dh), prev), pl.BlockSpec((1, 1, WIN_T, dh), cur),
                  pl.BlockSpec((1, 1, WIN_T, dh), prev), pl.BlockSpec((1, 1, WIN_T, dh), cur),
                  pl.BlockSpec((1, NSA_R * SUB_Q, span), lambda gi, bi, i: (gi, 0, 0))],
        out_specs=pl.BlockSpec((1, WIN_T, NSA_R * dh), lambda gi, bi, i: (bi, i, gi)),
        out_shape=jax.ShapeDtypeStruct((b, s, N_HEADS * dh), BF16),
        compiler_params=_params("parallel", "parallel", "parallel"),
        name="nsa_window",
    )(proj, kw, kw, vw, vw, bt)


def _nsa_combine_kernel(oc_ref, os_ref, ow_ref, g_ref, e_ref, o_ref):
    d = o_ref.shape[2]
    sg = jax.nn.sigmoid(g_ref[0].astype(F32))
    ge = _dot_hi_exact_rhs(sg, e_ref[...])
    o = (ge[:, :d] * oc_ref[0].astype(F32) + ge[:, d:2 * d] * os_ref[0].astype(F32)
         + ge[:, 2 * d:] * ow_ref[0].astype(F32))
    o_ref[0] = o.astype(o_ref.dtype)


def _nsa_combine(o_cmp, o_sel, o_win, proj, gate_col, tm=512):
    b, s, d = o_cmp.shape
    c = jnp.arange(LANES)[:, None]
    n = jnp.arange(3 * d)[None, :]
    expand = (c == (n // d) * N_HEADS + (n % d) // HEAD_DIM).astype(BF16)
    spec = pl.BlockSpec((1, tm, d), lambda bi, i: (bi, i, 0))
    return pl.pallas_call(
        _nsa_combine_kernel,
        grid=(b, s // tm),
        in_specs=[spec, spec, spec,
                  pl.BlockSpec((1, tm, LANES), lambda bi, i: (bi, i, gate_col // LANES)),
                  pl.BlockSpec((LANES, 3 * d), lambda bi, i: (0, 0))],
        out_specs=spec,
        out_shape=jax.ShapeDtypeStruct((b, s, d), BF16),
        compiler_params=_params("parallel", "parallel"),
        name="nsa_combine",
    )(o_cmp, o_sel, o_win, proj, expand)


def _nsa_mixer(x, scale, shift, w_in, pe, w1, b1, w2, rel_table):
    b, s, _ = x.shape
    hd, gd = N_HEADS * HEAD_DIM, NSA_G * HEAD_DIM
    gate_col = hd + 6 * gd
    n_in = gate_col + LANES
    proj = _linear_in(x, scale, shift, _pad_cols(w_in, n_in).astype(BF16), tn=n_in)

    def group_major(t):
        return t.reshape(b, s, NSA_G, HEAD_DIM).transpose(0, 2, 1, 3)

    kc, vc, ks, vs, kw, vw = (proj[:, :, hd + n * gd: hd + (n + 1) * gd] for n in range(6))
    cmp = _compress(jnp.stack([kc, vc]), pe.reshape(2, -1), w1, b1, w2)
    o_cmp, mb = _nsa_cmp(proj, cmp[0], cmp[1], rel_table)
    o_sel = _nsa_sel(proj, group_major(ks), group_major(vs), mb, rel_table)
    o_win = _nsa_win(proj, group_major(kw), group_major(vw), rel_table)
    return _nsa_combine(o_cmp, o_sel, o_win, proj, gate_col)


DIL_BACK = 128


def _dil_kernel(q_ref, kp_ref, kc_ref, vp_ref, vc_ref, bt_ref, o_ref, l_ref):
    t = pl.program_id(3)
    tq = q_ref.shape[1]
    span = DIL_BACK + SUB_Q
    kcat = jnp.concatenate([kp_ref[0, tq - DIL_BACK:, :], kc_ref[0]], axis=0)
    vcat = jnp.concatenate([vp_ref[0, tq - DIL_BACK:, :], vc_ref[0]], axis=0)
    lane_q = lax.broadcasted_iota(jnp.int32, (SUB_Q, LANES), 1) // HEAD_DIM
    lane_k = lax.broadcasted_iota(jnp.int32, (span, LANES), 1) // HEAD_DIM
    col = lax.broadcasted_iota(jnp.int32, (SUB_Q, span), 1)
    n_sub, n_head = tq // SUB_Q, LANES // HEAD_DIM
    scores = []
    for j in range(n_sub):
        qj = q_ref[0, j * SUB_Q:(j + 1) * SUB_Q, :] * (HEAD_DIM ** -0.5)
        kk = kcat[j * SUB_Q:j * SUB_Q + span]
        for hh in range(n_head):
            s = _dot_nt(jnp.where(lane_q == hh, qj, 0.0).astype(BF16), kk) + bt_ref[hh]
            if j == 0:
                s = jnp.where((col >= DIL_BACK) | (t > 0), s, NEG)
            scores.append(s)
    p_all, lse_all = _softmax_rows(jnp.concatenate(scores, axis=0))
    p_all = p_all.astype(BF16)
    for j in range(n_sub):
        vv = vcat[j * SUB_Q:j * SUB_Q + span]
        o = jnp.zeros((SUB_Q, LANES), F32)
        lse = jnp.zeros((SUB_Q, LANES), F32)
        for hh in range(n_head):
            r0 = (j * n_head + hh) * SUB_Q
            o = o + _dot(p_all[r0:r0 + SUB_Q], jnp.where(lane_k == hh, vv, 0.0).astype(BF16))
            lse = jnp.where(lane_q == hh, lse_all[r0:r0 + SUB_Q], lse)
        o_ref[0, j * SUB_Q:(j + 1) * SUB_Q, :] = o.astype(o_ref.dtype)
        l_ref[0, j * SUB_Q:(j + 1) * SUB_Q, :] = lse


def _dil_group(view, window, dil, rel_table):
    b, l, n_all = view.shape
    hd = N_HEADS * HEAD_DIM
    assert window // dil == DIL_BACK and n_all == dil * 3 * hd
    tq = min(1024, l)
    assert l % tq == 0 and tq >= DIL_BACK
    hpb = hd // LANES
    qi = jnp.arange(SUB_Q)[:, None]
    cc = jnp.arange(DIL_BACK + SUB_Q)[None, :]
    dd = qi + DIL_BACK - cc
    bt = jnp.where((dd[None] >= 0) & (dd[None] <= DIL_BACK), _rel_bias(rel_table, dd * dil), NEG)

    def spec(comp, prev):
        def index(bi, r, hp, t):
            return (bi, jnp.maximum(t - 1, 0) if prev else t, (r * 3 + comp) * hpb + hp)
        return pl.BlockSpec((1, tq, LANES), index)

    out_spec = pl.BlockSpec((1, tq, LANES), lambda bi, r, hp, t: (bi, t, r * hpb + hp))
    return pl.pallas_call(
        _dil_kernel,
        grid=(b, dil, hpb, l // tq),
        in_specs=[spec(0, False), spec(1, True), spec(1, False), spec(2, True), spec(2, False),
                  pl.BlockSpec((LANES // HEAD_DIM, SUB_Q, DIL_BACK + SUB_Q), lambda bi, r, hp, t: (hp, 0, 0))],
        out_specs=[out_spec, out_spec],
        out_shape=[jax.ShapeDtypeStruct((b, l, dil * hd), BF16), jax.ShapeDtypeStruct((b, l, dil * hd), F32)],
        compiler_params=_params("parallel", "parallel", "parallel", "parallel"),
        name=f"dilated_attention_{dil}",
    )(view, view, view, view, view, bt)


def _dil_merge_kernel(*refs, dils):
    n = len(dils)
    o_refs, l_refs, out_ref, scratch = refs[:n], refs[n:2 * n], refs[2 * n], refs[2 * n + 1:]
    tm, d = out_ref.shape[1], out_ref.shape[2]

    def token_major(ref, dil, buf):
        if dil == 1:
            return ref[0].astype(F32)
        for r in range(dil):
            for c in range(d // LANES):
                buf[c, pl.ds(r, tm // dil, stride=dil), :] = (
                    ref[0, :, r * d + c * LANES:r * d + (c + 1) * LANES].astype(F32))
        return jnp.concatenate([buf[c] for c in range(d // LANES)], axis=1)

    outs = [token_major(o_refs[g], dils[g], scratch[2 * g]) for g in range(n)]
    lses = [token_major(l_refs[g], dils[g], scratch[2 * g + 1]) for g in range(n)]
    m = functools.reduce(jnp.maximum, lses)
    es = [jnp.exp(l - m) for l in lses]
    inv = 1.0 / functools.reduce(jnp.add, es)
    out_ref[0] = functools.reduce(jnp.add, [e * inv * o for e, o in zip(es, outs)]).astype(out_ref.dtype)


def _dil_merge(outs, lses, dils, tm=512):
    b = outs[0].shape[0]
    d = N_HEADS * HEAD_DIM
    s = outs[0].shape[1] * dils[0]
    specs = [pl.BlockSpec((1, tm // dil, dil * d), lambda bi, i: (bi, i, 0)) for dil in dils]
    return pl.pallas_call(
        functools.partial(_dil_merge_kernel, dils=tuple(dils)),
        grid=(b, s // tm),
        in_specs=specs + specs,
        out_specs=pl.BlockSpec((1, tm, d), lambda bi, i: (bi, i, 0)),
        out_shape=jax.ShapeDtypeStruct((b, s, d), BF16),
        scratch_shapes=[pltpu.VMEM((d // LANES, tm, LANES), F32) for _ in range(2 * len(dils))],
        compiler_params=_params("parallel", "parallel"),
        name="dilated_merge",
    )(*outs, *lses)


def _dil_mixer(x, scale, shift, w_in, rel_table):
    hd = N_HEADS * HEAD_DIM
    w = w_in.astype(BF16)
    outs, lses, dils = [], [], []
    for gi, (window, dil) in enumerate(DIL_GROUPS):
        view = _linear_in(x, scale, shift, w[:, gi * 3 * hd:(gi + 1) * 3 * hd], tn=3 * hd, dil=dil)
        o, lse = _dil_group(view, window, dil, rel_table)
        outs.append(o)
        lses.append(lse)
        dils.append(dil)
    return _dil_merge(outs, lses, dils)


def kernel(x, c, rel_table, mod_w, mod_b, ln_g, ln_b, gla_w_in, gla_w_a2, gla_b_a, gla_norm_g, gla_w_o,
           nsa_w_in, nsa_cmp_pe, nsa_cmp_w1, nsa_cmp_b1, nsa_cmp_w2, nsa_w_o, dil_w_in, dil_w_o,
           ffn_w_up, ffn_conv_w, ffn_conv_b, ffn_w_down):
    shift, scale, gate = _modulation(c, mod_w, mod_b)
    for i in range(DEPTH):
        kind, j = i % N_MIXERS, i // N_MIXERS
        m = 2 * i
        if kind == 0:
            a = _gla_mixer(x, scale[m], shift[m], gla_w_in[j], gla_w_a2[j], gla_b_a[j], gla_norm_g[j])
            w_o = gla_w_o[j]
        elif kind == 1:
            a = _nsa_mixer(x, scale[m], shift[m], nsa_w_in[j], nsa_cmp_pe[j], nsa_cmp_w1[j], nsa_cmp_b1[j],
                           nsa_cmp_w2[j], rel_table)
            w_o = nsa_w_o[j]
        else:
            a = _dil_mixer(x, scale[m], shift[m], dil_w_in[j], rel_table)
            w_o = dil_w_o[j]
        x = _out_norm(a, w_o.astype(BF16), x, gate[m], ln_g[i, 0], ln_b[i, 0])
        a = _ffn_up(x, scale[m + 1], shift[m + 1], ffn_w_up[i].astype(BF16), ffn_conv_w[i], ffn_conv_b[i])
        x = _out_norm(a, ffn_w_down[i].astype(BF16), x, gate[m + 1], ln_g[i, 1], ln_b[i, 1])
    return x
```

```python
import functools
import math

import jax
import jax.numpy as jnp
import numpy as np
from jax import lax
from jax.experimental import pallas as pl
from jax.experimental.pallas import tpu as pltpu

F32 = jnp.float32
BF16 = jnp.bfloat16

D_MODEL = 1024
DEPTH = 4
N_MIXERS = 3
HEAD_DIM = 64
N_HEADS = D_MODEL // HEAD_DIM
REL_BUCKETS = 32
REL_MAX_DIST = 2048
LN_EPS = 1e-5
GLA_HEADS = 4
GLA_DK = D_MODEL // 2 // GLA_HEADS
GLA_DV = D_MODEL // GLA_HEADS
GLA_GATE_RANK = 16
GLA_TAU = 16.0
GLA_CHUNK = 64
NSA_G = 4
NSA_R = N_HEADS // NSA_G
CMP_LEN = 32
CMP_STRIDE = 16
CMP_HIDDEN = 256
SEL_LEN = 64
SEL_TOP = 16
WIN_LEN = 512
DIL_GROUPS = ((128, 1), (512, 4), (2048, 16))
D_FF = 2816
DN_ALPHA = (2 * DEPTH) ** 0.25

NEG = -1e30
VMEM_LIMIT = 56 * 1024 * 1024
LANES = 128
SUB_Q = 128


def _params(*sem):
    return pltpu.CompilerParams(dimension_semantics=sem, vmem_limit_bytes=VMEM_LIMIT)


def _dot(a, b):
    return jnp.dot(a, b, preferred_element_type=F32)


def _dot_nt(a, b):
    return lax.dot_general(a, b, (((1,), (1,)), ((), ())), preferred_element_type=F32)


def _dot_tn(a, b):
    return lax.dot_general(a, b, (((0,), (0,)), ((), ())), preferred_element_type=F32)


def _split(a):
    hi = a.astype(BF16)
    lo = (a - hi.astype(F32)).astype(BF16)
    return hi, lo


def _dot_hi_exact_rhs(a, b_bf16):
    hi, lo = _split(a)
    return _dot(hi, b_bf16) + _dot(lo, b_bf16)


def _silu(x):
    return x * jax.nn.sigmoid(x)


def _bucket_starts():
    n = np.arange(2 * REL_MAX_DIST)
    exact = REL_BUCKETS // 2
    logn = np.log(np.maximum(n, 1).astype(np.float32) / np.float32(exact))
    large = exact + (logn / np.float32(math.log(REL_MAX_DIST / exact)) * np.float32(REL_BUCKETS - exact)).astype(np.int32)
    bucket = np.where(n < exact, n, np.minimum(large, REL_BUCKETS - 1))
    assert np.all(np.diff(bucket) >= 0) and bucket[-1] == REL_BUCKETS - 1
    return [int(np.argmax(bucket >= k)) for k in range(REL_BUCKETS)]


BUCKET_START = _bucket_starts()
FAR_DIST = BUCKET_START[-1]


def _rel_bias(rel_table, dist):
    shape = (rel_table.shape[1],) + (1,) * dist.ndim
    out = jnp.broadcast_to(rel_table[0].reshape(shape), (rel_table.shape[1],) + dist.shape)
    for k in range(1, REL_BUCKETS):
        out = jnp.where(dist[None] >= BUCKET_START[k], rel_table[k].reshape(shape), out)
    return out


def _mod_kernel(c_ref, w_ref, b_ref, o_ref):
    cs = _silu(c_ref[...])
    ch, cl = _split(cs)
    wh, wl = _split(w_ref[0])
    o_ref[0] = _dot(ch, wh) + _dot(ch, wl) + _dot(cl, wh) + b_ref[0]


def _modulation(c, mod_w, mod_b):
    b, d = c.shape
    n_sub = mod_w.shape[0] * mod_w.shape[1]
    rows = 8
    cp = jnp.zeros((rows, d), F32).at[:b].set(c)
    w = mod_w.reshape(n_sub, d, 3 * d)
    bias = mod_b.reshape(n_sub, 1, 3 * d)
    tn = d
    out = pl.pallas_call(
        _mod_kernel,
        grid=(n_sub, 3 * d // tn),
        in_specs=[pl.BlockSpec((rows, d), lambda i, j: (0, 0)),
                  pl.BlockSpec((1, d, tn), lambda i, j: (i, 0, j)),
                  pl.BlockSpec((1, 1, tn), lambda i, j: (i, 0, j))],
        out_specs=pl.BlockSpec((1, rows, tn), lambda i, j: (i, 0, j)),
        out_shape=jax.ShapeDtypeStruct((n_sub, rows, 3 * d), F32),
        compiler_params=_params("parallel", "parallel"),
        name="modulation",
    )(cp, w, bias)
    out = out[:, :b]
    shift, scale, gate = jnp.split(out, 3, axis=-1)
    return shift[:, :, None, :], scale[:, :, None, :], gate[:, :, None, :]


def _linear_in_kernel(x_ref, sc_ref, sh_ref, w_ref, *rest, dil):
    h = (x_ref[0] * (1.0 + sc_ref[0]) + sh_ref[0]).astype(BF16)
    if dil == 1:
        o_ref, = rest
        o_ref[0] = _dot(h, w_ref[...]).astype(o_ref.dtype)
    else:
        perm_ref, o_ref = rest
        res = _dot(_dot(perm_ref[...], h).astype(BF16), w_ref[...])
        rows, n = res.shape[0] // dil, res.shape[1]
        for r in range(dil):
            o_ref[0, :, r * n:(r + 1) * n] = res[r * rows:(r + 1) * rows].astype(o_ref.dtype)


def _linear_in(x, scale, shift, w, tn, tm=512, dil=1):
    b, s, d = x.shape
    n = w.shape[1]
    assert n % tn == 0 and s % tm == 0 and tm % (16 * dil) == 0 and (dil == 1 or n == tn)
    in_specs = [pl.BlockSpec((1, tm, d), lambda j, bi, i: (bi, i, 0)),
                pl.BlockSpec((1, 1, d), lambda j, bi, i: (bi, 0, 0)),
                pl.BlockSpec((1, 1, d), lambda j, bi, i: (bi, 0, 0)),
                pl.BlockSpec((d, tn), lambda j, bi, i: (0, j))]
    args = [x, scale, shift, w]
    if dil > 1:
        out_row = jnp.arange(tm)
        src = (out_row % (tm // dil)) * dil + out_row // (tm // dil)
        in_specs.append(pl.BlockSpec((tm, tm), lambda j, bi, i: (0, 0)))
        args.append((src[:, None] == jnp.arange(tm)[None, :]).astype(BF16))
    return pl.pallas_call(
        functools.partial(_linear_in_kernel, dil=dil),
        grid=(n // tn, b, s // tm),
        in_specs=in_specs,
        out_specs=pl.BlockSpec((1, tm // dil, dil * tn), lambda j, bi, i: (bi, i, j)),
        out_shape=jax.ShapeDtypeStruct((b, s // dil, dil * n), BF16),
        compiler_params=_params("parallel", "parallel", "parallel"),
        name="linear_in",
    )(*args)


OUT_ROWS = 512


def _out_norm_kernel(a_ref, w_ref, x_ref, gate_ref, g_ref, b_ref, o_ref):
    chunks = [slice(r, r + OUT_ROWS) for r in range(0, a_ref.shape[1], OUT_ROWS)]
    ys = [_dot(a_ref[0, sl, :], w_ref[...]) for sl in chunks]
    for sl, y in zip(chunks, ys):
        z = DN_ALPHA * x_ref[0, sl, :] + (1.0 + gate_ref[0]) * y
        mu = jnp.mean(z, axis=-1, keepdims=True)
        zc = z - mu
        var = jnp.mean(zc * zc, axis=-1, keepdims=True)
        o_ref[0, sl, :] = zc * lax.rsqrt(var + LN_EPS) * g_ref[...] + b_ref[...]


def _out_norm(a, w, x, gate, ln_g, ln_b, tm=1024):
    b, s, d = x.shape
    k = a.shape[-1]
    return pl.pallas_call(
        _out_norm_kernel,
        grid=(b, s // tm),
        in_specs=[pl.BlockSpec((1, tm, k), lambda bi, i: (bi, i, 0)),
                  pl.BlockSpec((k, d), lambda bi, i: (0, 0)),
                  pl.BlockSpec((1, tm, d), lambda bi, i: (bi, i, 0)),
                  pl.BlockSpec((1, 1, d), lambda bi, i: (bi, 0, 0)),
                  pl.BlockSpec((1, d), lambda bi, i: (0, 0)),
                  pl.BlockSpec((1, d), lambda bi, i: (0, 0))],
        out_specs=pl.BlockSpec((1, tm, d), lambda bi, i: (bi, i, 0)),
        out_shape=jax.ShapeDtypeStruct((b, s, d), F32),
        compiler_params=_params("parallel", "parallel"),
        name="out_norm",
    )(a, w, x, gate, ln_g.reshape(1, d), ln_b.reshape(1, d))


FFN_HALO = 16


def _ffn_up_kernel(x_ref, xh_ref, sc_ref, sh_ref, wu_ref, wg_ref, cw_ref, cb_ref, o_ref, h_ref, u_ref):
    i = pl.program_id(1)
    tm = x_ref.shape[1]

    @pl.when(pl.program_id(2) == 0)
    def _():
        sc = 1.0 + sc_ref[0]
        sh = sh_ref[0]
        h_ref[0:FFN_HALO, :] = (xh_ref[0] * sc + sh).astype(BF16)
        h_ref[FFN_HALO:, :] = (x_ref[0] * sc + sh).astype(BF16)

    keep = (i > 0).astype(F32)
    gs = []
    for r in range(0, tm, OUT_ROWS):
        lo, hi = (0 if r == 0 else FFN_HALO + r), FFN_HALO + r + OUT_ROWS
        u = _dot(h_ref[lo:hi, :], wu_ref[...])
        if r == 0:
            u_ref[0:FFN_HALO, :] = u[0:FFN_HALO] * keep
            u_ref[FFN_HALO:hi, :] = u[FFN_HALO:]
        else:
            u_ref[lo:hi, :] = u
        gs.append(_dot(h_ref[FFN_HALO + r:hi, :], wg_ref[...]))
    cw = cw_ref[...]
    for c, r in enumerate(range(0, tm, OUT_ROWS)):
        at = FFN_HALO + r
        conv = (u_ref[pl.ds(at - 2, OUT_ROWS), :] * cw[0:1] + u_ref[pl.ds(at - 1, OUT_ROWS), :] * cw[1:2]
                + u_ref[pl.ds(at, OUT_ROWS), :] * cw[2:3] + cb_ref[...])
        o_ref[0, r:r + OUT_ROWS, :] = (_silu(conv) * gs[c]).astype(o_ref.dtype)


def _ffn_up(x, scale, shift, w_up, conv_w, conv_b, tm=512, tf=1408):
    b, s, d = x.shape
    f = conv_w.shape[1]
    assert f % tf == 0 and s % tm == 0 and tm % FFN_HALO == 0
    nf = f // tf
    hb = tm // FFN_HALO
    return pl.pallas_call(
        _ffn_up_kernel,
        grid=(b, s // tm, nf),
        in_specs=[pl.BlockSpec((1, tm, d), lambda bi, i, j: (bi, i, 0)),
                  pl.BlockSpec((1, FFN_HALO, d), lambda bi, i, j: (bi, jnp.maximum(i * hb - 1, 0), 0)),
                  pl.BlockSpec((1, 1, d), lambda bi, i, j: (bi, 0, 0)),
                  pl.BlockSpec((1, 1, d), lambda bi, i, j: (bi, 0, 0)),
                  pl.BlockSpec((d, tf), lambda bi, i, j: (0, j)),
                  pl.BlockSpec((d, tf), lambda bi, i, j: (0, nf + j)),
                  pl.BlockSpec((3, tf), lambda bi, i, j: (0, j)),
                  pl.BlockSpec((1, tf), lambda bi, i, j: (0, j))],
        out_specs=pl.BlockSpec((1, tm, tf), lambda bi, i, j: (bi, i, j)),
        out_shape=jax.ShapeDtypeStruct((b, s, f), BF16),
        scratch_shapes=[pltpu.VMEM((tm + FFN_HALO, d), BF16), pltpu.VMEM((tm + FFN_HALO, tf), F32)],
        compiler_params=_params("parallel", "parallel", "arbitrary"),
        name="ffn_up",
    )(x, x, scale, shift, w_up, w_up, conv_w, conv_b.reshape(1, f))


def _gla_kernel(q_ref, k_ref, v_ref, r_ref, a_ref, wa_ref, ba_ref, ng_ref, o_ref, st_ref):
    tc = q_ref.shape[1]
    nchunk = tc // GLA_CHUNK

    @pl.when(pl.program_id(2) == 0)
    def _():
        st_ref[...] = jnp.zeros_like(st_ref)

    z = _dot(a_ref[0], wa_ref[...]) + ba_ref[...]
    log_a = (jnp.minimum(z, 0.0) - jnp.log1p(jnp.exp(-jnp.abs(z)))) / GLA_TAU
    ci = lax.broadcasted_iota(jnp.int32, (GLA_CHUNK, GLA_CHUNK), 0)
    cj = lax.broadcasted_iota(jnp.int32, (GLA_CHUNK, GLA_CHUNK), 1)
    causal = cj <= ci
    tri = jnp.where(causal, 1.0, 0.0).astype(BF16)
    chunks = [slice(c * GLA_CHUNK, (c + 1) * GLA_CHUNK) for c in range(nchunk)]
    la_hi, la_lo = _split(log_a)
    cums = [_dot(tri, la_hi[sl]) + _dot(tri, la_lo[sl]) for sl in chunks]
    bcum = jnp.concatenate(cums, axis=0)
    btot = jnp.concatenate([jnp.broadcast_to(cm[GLA_CHUNK - 1:], cm.shape) for cm in cums], axis=0)
    qf = q_ref[0].astype(F32) * (GLA_DK ** -0.5)
    kf = k_ref[0].astype(F32)
    q_dec = (qf * jnp.exp(bcum)).astype(BF16)
    k_inv = (kf * jnp.exp(-bcum)).astype(BF16)
    k_dec = (kf * jnp.exp(btot - bcum)).astype(BF16)
    dl = jnp.exp(btot)
    v = v_ref[0]
    att = [jnp.where(causal, _dot_nt(q_dec[sl], k_inv[sl]), 0.0).astype(BF16) for sl in chunks]
    kv = [_dot_tn(v[sl], k_dec[sl]) for sl in chunks]
    o_intra = [_dot(att[c], v[chunks[c]]) for c in range(nchunk)]
    st = st_ref[...]
    o_inter = []
    for c, sl in enumerate(chunks):
        o_inter.append(_dot_nt(q_dec[sl], st.astype(BF16)))
        st = st * dl[c * GLA_CHUNK:c * GLA_CHUNK + 1] + kv[c]
    st_ref[...] = st
    o = jnp.concatenate(o_intra, axis=0) + jnp.concatenate(o_inter, axis=0)
    o = o * lax.rsqrt(jnp.mean(o * o, axis=-1, keepdims=True) + LN_EPS) * ng_ref[...]
    o_ref[0] = (o * _silu(r_ref[0].astype(F32))).astype(o_ref.dtype)


def _gla_core(proj, w_a2, b_a, norm_g, tc=512):
    b, s, _ = proj.shape
    h, dk, dv = GLA_HEADS, GLA_DK, GLA_DV
    assert s % tc == 0
    wa = jnp.zeros((LANES, h * dk), F32).at[:GLA_GATE_RANK].set(w_a2).astype(BF16)
    kb, vb, rb, ab = (h * dk) // dk, (2 * h * dk) // dv, (2 * h * dk + h * dv) // dv, (2 * h * dk + 2 * h * dv) // LANES
    return pl.pallas_call(
        _gla_kernel,
        grid=(b, h, s // tc),
        in_specs=[pl.BlockSpec((1, tc, dk), lambda bi, hi, t: (bi, t, hi)),
                  pl.BlockSpec((1, tc, dk), lambda bi, hi, t: (bi, t, kb + hi)),
                  pl.BlockSpec((1, tc, dv), lambda bi, hi, t: (bi, t, vb + hi)),
                  pl.BlockSpec((1, tc, dv), lambda bi, hi, t: (bi, t, rb + hi)),
                  pl.BlockSpec((1, tc, LANES), lambda bi, hi, t: (bi, t, ab)),
                  pl.BlockSpec((LANES, dk), lambda bi, hi, t: (0, hi)),
                  pl.BlockSpec((1, dk), lambda bi, hi, t: (0, hi)),
                  pl.BlockSpec((1, dv), lambda bi, hi, t: (0, 0))],
        out_specs=pl.BlockSpec((1, tc, dv), lambda bi, hi, t: (bi, t, hi)),
        out_shape=jax.ShapeDtypeStruct((b, s, h * dv), BF16),
        scratch_shapes=[pltpu.VMEM((dv, dk), F32)],
        compiler_params=_params("parallel", "parallel", "arbitrary"),
        name="gla_core",
    )(proj, proj, proj, proj, proj, wa, b_a.reshape(1, h * dk), norm_g.reshape(1, dv))


def _pad_cols(w, n):
    return jnp.pad(w, ((0, 0), (0, n - w.shape[1])))


def _gla_mixer(x, scale, shift, w_in, w_a2, b_a, norm_g):
    n_in = 2 * GLA_HEADS * GLA_DK + 2 * GLA_HEADS * GLA_DV + LANES
    proj = _linear_in(x, scale, shift, _pad_cols(w_in, n_in).astype(BF16), tn=n_in)
    return _gla_core(proj, w_a2, b_a, norm_g)


def _compress_kernel(lo_ref, hi_ref, pe_ref, w1_ref, b1_ref, w2_ref, o_ref):
    half = lo_ref.shape[-1]
    pe = pe_ref[0]
    a_lo = (lo_ref[0, 0, 0].astype(F32) + pe[:, :half]).astype(BF16)
    a_hi = (hi_ref[0, 0, 0].astype(F32) + pe[:, half:]).astype(BF16)
    h1 = _dot(a_lo, w1_ref[0, :half, :]) + _dot(a_hi, w1_ref[0, half:, :]) + b1_ref[0]
    o_ref[0, 0, 0] = _dot(_silu(h1).astype(BF16), w2_ref[0])


def _compress(kv, pe, w1, b1, w2):
    two, b, s, _ = kv.shape
    g, dh = NSA_G, HEAD_DIM
    npc = s // CMP_STRIDE
    half = CMP_STRIDE * dh
    a = kv.reshape(two, b, npc, CMP_STRIDE, g, dh).transpose(0, 1, 4, 2, 3, 5).reshape(two, b, g, npc, half)
    a_hi = jnp.concatenate([a[:, :, :, 1:], jnp.zeros_like(a[:, :, :, :1])], axis=3)
    spec_a = pl.BlockSpec((1, 1, 1, npc, half), lambda c, bi, gi: (c, bi, gi, 0, 0))
    return pl.pallas_call(
        _compress_kernel,
        grid=(two, b, g),
        in_specs=[spec_a, spec_a,
                  pl.BlockSpec((1, 1, 2 * half), lambda c, bi, gi: (c, 0, 0)),
                  pl.BlockSpec((1, 2 * half, CMP_HIDDEN), lambda c, bi, gi: (c, 0, 0)),
                  pl.BlockSpec((1, 1, CMP_HIDDEN), lambda c, bi, gi: (c, 0, 0)),
                  pl.BlockSpec((1, CMP_HIDDEN, dh), lambda c, bi, gi: (c, 0, 0))],
        out_specs=pl.BlockSpec((1, 1, 1, npc, dh), lambda c, bi, gi: (c, bi, gi, 0, 0)),
        out_shape=jax.ShapeDtypeStruct((two, b, g, npc, dh), F32),
        compiler_params=_params("parallel", "parallel", "parallel"),
        name="nsa_compress",
    )(a, a_hi, pe.reshape(two, 1, 2 * half), w1.astype(BF16), b1.reshape(two, 1, CMP_HIDDEN), w2.astype(BF16))


def _stack_heads(q):
    return jnp.concatenate([q[:, r * HEAD_DIM:(r + 1) * HEAD_DIM] for r in range(NSA_R)], axis=0)


def _unstack_heads(o, rows):
    return jnp.concatenate([o[r * rows:(r + 1) * rows] for r in range(NSA_R)], axis=1)


CMP_TQ = 512
LOG2E = 1.4426950408889634


def _nsa_cmp_kernel(q_ref, fq_ref, kc_ref, vc_ref, nb_ref, pb_ref, o_ref, mb_ref, imp_ref):
    i = pl.program_id(2)
    n_tile = pl.num_programs(2) - 1

    @pl.when(i == 0)
    def _():
        _cmp_attend(i, q_ref, fq_ref, kc_ref, vc_ref, nb_ref, pb_ref, o_ref, imp_ref)

    @pl.when((i > 0) & (i < n_tile))
    def _():
        imp = imp_ref[...]
        _cmp_attend(i, q_ref, fq_ref, kc_ref, vc_ref, nb_ref, pb_ref, o_ref, imp_ref)
        _cmp_select(i - 1, imp, mb_ref)

    @pl.when(i == n_tile)
    def _():
        _cmp_select(i - 1, imp_ref[...], mb_ref)


def _cmp_attend(i, q_ref, fq_ref, kc_ref, vc_ref, nb_ref, pb_ref, o_ref, imp_ref):
    npc = nb_ref.shape[2]
    n_sub, rows = CMP_TQ // SUB_Q, NSA_R * SUB_Q
    starts = [pl.multiple_of((i * n_sub + j) * (SUB_Q // CMP_STRIDE), 8) for j in range(n_sub)]

    scores = []
    for j in range(n_sub):
        kwin = kc_ref[0, 0, pl.ds(starts[j], npc), :].astype(BF16)
        qs = _stack_heads(q_ref[0, j * SUB_Q:(j + 1) * SUB_Q, :]).astype(F32) * (HEAD_DIM ** -0.5 * LOG2E)
        qp = jnp.concatenate([qs, fq_ref[...]], axis=1).astype(BF16)
        scores.append(_dot_nt(qp, kwin) + nb_ref[0])
    s = jnp.concatenate(scores, axis=0)
    m = jnp.max(s, axis=1, keepdims=True)
    e = jnp.exp2(s - m)
    den = jnp.sum(e, axis=1, keepdims=True)
    p = e * jnp.where(m > 0.5 * NEG, 1.0 / den, 0.0)
    p_bf = p.astype(BF16)
    for j in range(n_sub):
        vwin = vc_ref[0, 0, pl.ds(starts[j], npc), :].astype(BF16)
        o = _dot(p_bf[j * rows:(j + 1) * rows], vwin)
        o_ref[0, j * SUB_Q:(j + 1) * SUB_Q, :] = _unstack_heads(o, SUB_Q).astype(o_ref.dtype)

    imps = []
    for j in range(n_sub):
        pg = p[j * rows:j * rows + SUB_Q]
        for r in range(1, NSA_R):
            pg = pg + p[j * rows + r * SUB_Q:j * rows + (r + 1) * SUB_Q]
        imps.append(_dot_hi_exact_rhs(pg, pb_ref[pl.ds(starts[j], npc), :].astype(BF16)))
    imp_ref[...] = jnp.concatenate(imps, axis=0)


def _cmp_select(i, imp, mb_ref):
    nblk = mb_ref.shape[3]
    blk = lax.broadcasted_iota(jnp.int32, (CMP_TQ, nblk), 1)
    blk_f = blk.astype(F32)
    t = i * CMP_TQ + lax.broadcasted_iota(jnp.int32, (CMP_TQ, nblk), 0)
    cur = t // SEL_LEN
    forced = (blk == 0) | (blk == cur) | (blk == cur - 1)
    work = jnp.where((blk * SEL_LEN <= t) & jnp.logical_not(forced), imp, NEG)
    free = work > 0.5 * NEG
    for _ in range(SEL_TOP - 3):
        mx = jnp.max(work, axis=1, keepdims=True)
        first = jnp.min(jnp.where(work == mx, blk_f, 1e9), axis=1, keepdims=True)
        work = jnp.where(blk_f == first, 2.0 * NEG, work)
    sel = forced | (free & (work < 1.5 * NEG))
    mb_ref[0, 0] = jnp.where(sel, 0.0, NEG).astype(mb_ref.dtype)


def _nsa_cmp(proj, kcmp, vcmp, rel_table):
    b, s, _ = proj.shape
    g, dh = NSA_G, HEAD_DIM
    npc = s // CMP_STRIDE
    nblk = s // SEL_LEN
    front = npc - SUB_Q // CMP_STRIDE
    assert s % CMP_TQ == 0 and nblk >= SEL_TOP
    n_tile = s // CMP_TQ
    kpad = jnp.pad(kcmp, ((0, 0), (0, 0), (front, npc - front), (0, LANES - dh)))
    kpad = kpad.at[:, :, :front, dh].set(1.0)
    vpad = jnp.pad(vcmp, ((0, 0), (0, 0), (front, npc - front), (0, 0)))
    fq = jnp.zeros((NSA_R * SUB_Q, LANES - dh), F32).at[:, 0].set(NEG)
    qi = jnp.arange(SUB_Q)[:, None]
    cc = jnp.arange(npc)[None, :]
    dd = qi - (CMP_LEN - 1) + CMP_STRIDE * (front - cc)
    nb = jnp.where(dd[None] >= 0, _rel_bias(rel_table, dd) * LOG2E, NEG).reshape(g, NSA_R * SUB_Q, npc)
    diff = jnp.arange(2 * npc)[:, None] - front - (SEL_LEN // CMP_STRIDE) * jnp.arange(nblk)[None, :]
    pool = jnp.where((diff == -1) | (diff == 3), 1.0, jnp.where((diff >= 0) & (diff <= 2), 2.0, 0.0))
    return pl.pallas_call(
        _nsa_cmp_kernel,
        grid=(g, b, n_tile + 1),
        in_specs=[pl.BlockSpec((1, CMP_TQ, NSA_R * dh), lambda gi, bi, i: (bi, jnp.minimum(i, n_tile - 1), gi)),
                  pl.BlockSpec((NSA_R * SUB_Q, LANES - dh), lambda gi, bi, i: (0, 0)),
                  pl.BlockSpec((1, 1, 2 * npc, LANES), lambda gi, bi, i: (bi, gi, 0, 0)),
                  pl.BlockSpec((1, 1, 2 * npc, dh), lambda gi, bi, i: (bi, gi, 0, 0)),
                  pl.BlockSpec((1, NSA_R * SUB_Q, npc), lambda gi, bi, i: (gi, 0, 0)),
                  pl.BlockSpec((2 * npc, nblk), lambda gi, bi, i: (0, 0))],
        out_specs=[pl.BlockSpec((1, CMP_TQ, NSA_R * dh), lambda gi, bi, i: (bi, jnp.minimum(i, n_tile - 1), gi)),
                   pl.BlockSpec((1, 1, CMP_TQ, nblk), lambda gi, bi, i: (bi, gi, jnp.maximum(i - 1, 0), 0))],
        out_shape=[jax.ShapeDtypeStruct((b, s, N_HEADS * dh), BF16),
                   jax.ShapeDtypeStruct((b, g, s, nblk), BF16)],
        scratch_shapes=[pltpu.VMEM((CMP_TQ, nblk), F32)],
        compiler_params=_params("parallel", "parallel", "arbitrary"),
        name="nsa_cmp_select",
    )(proj, fq, kpad, vpad, nb, pool.astype(F32))


SEL_T = 256
SEL_WIDE = 8
SEL_NEAR = -(-(FAR_DIST + SEL_T - 1) // SEL_T)


def _nsa_sel_kernel(q_ref, fq_ref, mb_ref, k_ref, v_ref, bt_ref, o_ref, qp_ref, m_ref, acc_ref):
    i = pl.program_id(2)
    rows = NSA_R * SEL_T
    fw = (k_ref.shape[3] - LANES)
    kh = fw // (SEL_T // SEL_LEN)
    qs = (_stack_heads(q_ref[0]).astype(F32) * (HEAD_DIM ** -0.5 * LOG2E)).astype(BF16)
    mb = mb_ref[0, 0]
    for half in range(2):
        mbh = jnp.concatenate([mb[:, half * fw:(half + 1) * fw]] * NSA_R, axis=0)
        qp_ref[half] = jnp.concatenate([qs, fq_ref[0], mbh], axis=1)
    m_ref[...] = jnp.full_like(m_ref, NEG)
    acc_ref[...] = jnp.zeros_like(acc_ref)

    def update(kt, n_tile, half, bias):
        k0 = pl.multiple_of(kt * SEL_T, SEL_T)
        kk = k_ref[0, 0, pl.ds(k0, n_tile * SEL_T), :]
        vv = v_ref[0, 0, pl.ds(k0, n_tile * SEL_T), :]
        s = _dot_nt(qp_ref[half], kk)
        if bias is not None:
            s = s + bias
        m_prev = m_ref[...]
        m_new = jnp.maximum(m_prev, jnp.max(s, axis=1, keepdims=True))
        alpha = jnp.exp2(m_prev - m_new)
        p = jnp.exp2(s - jnp.concatenate([m_new] * (n_tile * SEL_T // LANES), axis=1))
        acc_ref[...] = acc_ref[...] * alpha + _dot(p.astype(BF16), vv)
        m_ref[...] = m_new

    def far_region(lo, hi, half):
        n = jnp.maximum(hi - lo, 0)

        def wide(j, carry):
            update(lo + SEL_WIDE * j, SEL_WIDE, half, None)
            return carry

        lax.fori_loop(0, n // SEL_WIDE, wide, 0)
        piece = SEL_WIDE // 2
        while piece >= 1:
            @pl.when((n & piece) != 0)
            def _(piece=piece):
                update(lo + (n & ~(2 * piece - 1)), piece, half, None)
            piece //= 2

    n_far = jnp.maximum(i - SEL_NEAR + 1, 0)
    far_region(0, jnp.minimum(n_far, kh), 0)
    far_region(kh, n_far, 1)

    def near(e):
        kt = i - e
        update(kt, 1, jnp.where(kt < kh, 0, 1), bt_ref[0, :, e * SEL_T:(e + 1) * SEL_T])

    @pl.when(i >= SEL_NEAR - 1)
    def _():
        for e in range(SEL_NEAR - 1, -1, -1):
            near(e)

    @pl.when(i < SEL_NEAR - 1)
    def _():
        for e in range(SEL_NEAR - 2, -1, -1):
            @pl.when(i - e >= 0)
            def _(e=e):
                near(e)

    acc = acc_ref[...]
    o = acc[:, :HEAD_DIM] / acc[:, HEAD_DIM:HEAD_DIM + 1]
    o_ref[0] = _unstack_heads(o, SEL_T).astype(o_ref.dtype)


def _nsa_sel(proj, ks, vs, mb, rel_table):
    b, s, _ = proj.shape
    g, dh = NSA_G, HEAD_DIM
    nblk = s // SEL_LEN
    fw = nblk // 2
    rows = NSA_R * SEL_T
    pos = jnp.arange(s)
    onehot = (((pos // SEL_LEN) % fw)[:, None] == jnp.arange(fw)[None, :]).astype(BF16)
    ones2 = jnp.zeros((s, LANES - dh), BF16).at[:, :2].set(1.0)
    kp = jnp.concatenate([ks, jnp.broadcast_to(ones2, (b, g, s, LANES - dh)),
                          jnp.broadcast_to(onehot, (b, g, s, fw))], axis=-1)
    ones1 = jnp.zeros((s, LANES - dh), BF16).at[:, :1].set(1.0)
    vp = jnp.concatenate([vs, jnp.broadcast_to(ones1, (b, g, s, LANES - dh))], axis=-1)
    b_far = rel_table[REL_BUCKETS - 1].reshape(g, NSA_R) * LOG2E
    hi = b_far.astype(BF16)
    lo = (b_far - hi.astype(F32)).astype(BF16)
    fq = jnp.zeros((g, NSA_R, SEL_T, LANES - dh), BF16)
    fq = fq.at[:, :, :, 0].set(hi[:, :, None]).at[:, :, :, 1].set(lo[:, :, None]).reshape(g, rows, LANES - dh)
    qi = jnp.arange(SEL_T)[:, None]
    cc = jnp.arange(SEL_NEAR * SEL_T)[None, :]
    dd = (cc // SEL_T) * SEL_T + qi - (cc % SEL_T)
    bias = _rel_bias(rel_table, dd).reshape(g, NSA_R, SEL_T, -1) * LOG2E - b_far[:, :, None, None]
    bt = jnp.where(dd[None, None] >= 0, bias, NEG).reshape(g, rows, SEL_NEAR * SEL_T)
    return pl.pallas_call(
        _nsa_sel_kernel,
        grid=(g, b, s // SEL_T),
        in_specs=[pl.BlockSpec((1, SEL_T, NSA_R * dh), lambda gi, bi, i: (bi, i, gi)),
                  pl.BlockSpec((1, rows, LANES - dh), lambda gi, bi, i: (gi, 0, 0)),
                  pl.BlockSpec((1, 1, SEL_T, nblk), lambda gi, bi, i: (bi, gi, i, 0)),
                  pl.BlockSpec((1, 1, s, LANES + fw), lambda gi, bi, i: (bi, gi, 0, 0)),
                  pl.BlockSpec((1, 1, s, LANES), lambda gi, bi, i: (bi, gi, 0, 0)),
                  pl.BlockSpec((1, rows, SEL_NEAR * SEL_T), lambda gi, bi, i: (gi, 0, 0),
                               pipeline_mode=pl.Buffered(1))],
        out_specs=pl.BlockSpec((1, SEL_T, NSA_R * dh), lambda gi, bi, i: (bi, i, gi)),
        out_shape=jax.ShapeDtypeStruct((b, s, N_HEADS * dh), BF16),
        scratch_shapes=[pltpu.VMEM((2, rows, LANES + fw), BF16), pltpu.VMEM((rows, LANES), F32),
                        pltpu.VMEM((rows, LANES), F32)],
        compiler_params=_params("parallel", "parallel", "parallel"),
        name="nsa_selected",
    )(proj, fq, mb, kp, vp, bt)


WIN_T = 512


def _softmax_rows(s):
    m = jnp.max(s, axis=1, keepdims=True)
    e = jnp.exp(s - m)
    den = jnp.sum(e, axis=1, keepdims=True)
    return e / den, m + jnp.log(den)


def _nsa_win_kernel(q_ref, kp_ref, kc_ref, vp_ref, vc_ref, bt_ref, o_ref):
    i = pl.program_id(2)
    back = WIN_T
    span = back + SUB_Q
    kcat = jnp.concatenate([kp_ref[0, 0], kc_ref[0, 0]], axis=0)
    vcat = jnp.concatenate([vp_ref[0, 0], vc_ref[0, 0]], axis=0)
    col = lax.broadcasted_iota(jnp.int32, (NSA_R * SUB_Q, span), 1)
    n_sub, rows = WIN_T // SUB_Q, NSA_R * SUB_Q
    scores = []
    for j in range(n_sub):
        qs = _stack_heads(q_ref[0, j * SUB_Q:(j + 1) * SUB_Q, :]) * (HEAD_DIM ** -0.5)
        s = _dot_nt(qs, kcat[j * SUB_Q:j * SUB_Q + span]) + bt_ref[0]
        scores.append(jnp.where((col >= back - j * SUB_Q) | (i > 0), s, NEG))
    p_all, _ = _softmax_rows(jnp.concatenate(scores, axis=0))
    p_all = p_all.astype(BF16)
    for j in range(n_sub):
        o = _dot(p_all[j * rows:(j + 1) * rows], vcat[j * SUB_Q:j * SUB_Q + span])
        o_ref[0, j * SUB_Q:(j + 1) * SUB_Q, :] = _unstack_heads(o, SUB_Q).astype(o_ref.dtype)


def _nsa_win(proj, kw, vw, rel_table):
    b, s, _ = proj.shape
    g, dh = NSA_G, HEAD_DIM
    span = WIN_T + SUB_Q
    qi = jnp.arange(SUB_Q)[:, None]
    cc = jnp.arange(span)[None, :]
    dd = qi + WIN_T - cc
    bias = _rel_bias(rel_table, dd)
    bt = jnp.where((dd[None] >= 0) & (dd[None] <= WIN_LEN - 1), bias, NEG).reshape(g, NSA_R * SUB_Q, span)
    prev = lambda gi, bi, i: (bi, gi, jnp.maximum(i - 1, 0), 0)
    cur = lambda gi, bi, i: (bi, gi, i, 0)
    return pl.pallas_call(
        _nsa_win_kernel,
        grid=(g, b, s // WIN_T),
        in_specs=[pl.BlockSpec((1, WIN_T, NSA_R * dh), lambda gi, bi, i: (bi, i, gi)),
                  pl.BlockSpec((1, 1, WIN_T, dh), prev), pl.BlockSpec((1, 1, WIN_T, dh), cur),
                  pl.BlockSpec((1, 1, WIN_T, dh), prev), pl.BlockSpec((1, 1, WIN_T, dh), cur),
                  pl.BlockSpec((1, NSA_R * SUB_Q, span), lambda gi, bi, i: (gi, 0, 0))],
        out_specs=pl.BlockSpec((1, WIN_T, NSA_R * dh), lambda gi, bi, i: (bi, i, gi)),
        out_shape=jax.ShapeDtypeStruct((b, s, N_HEADS * dh), BF16),
        compiler_params=_params("parallel", "parallel", "parallel"),
        name="nsa_window",
    )(proj, kw, kw, vw, vw, bt)


def _nsa_combine_kernel(oc_ref, os_ref, ow_ref, g_ref, e_ref, o_ref):
    d = o_ref.shape[2]
    sg = jax.nn.sigmoid(g_ref[0].astype(F32))
    ge = _dot_hi_exact_rhs(sg, e_ref[...])
    o = (ge[:, :d] * oc_ref[0].astype(F32) + ge[:, d:2 * d] * os_ref[0].astype(F32)
         + ge[:, 2 * d:] * ow_ref[0].astype(F32))
    o_ref[0] = o.astype(o_ref.dtype)


def _nsa_combine(o_cmp, o_sel, o_win, proj, gate_col, tm=512):
    b, s, d = o_cmp.shape
    c = jnp.arange(LANES)[:, None]
    n = jnp.arange(3 * d)[None, :]
    expand = (c == (n // d) * N_HEADS + (n % d) // HEAD_DIM).astype(BF16)
    spec = pl.BlockSpec((1, tm, d), lambda bi, i: (bi, i, 0))
    return pl.pallas_call(
        _nsa_combine_kernel,
        grid=(b, s // tm),
        in_specs=[spec, spec, spec,
                  pl.BlockSpec((1, tm, LANES), lambda bi, i: (bi, i, gate_col // LANES)),
                  pl.BlockSpec((LANES, 3 * d), lambda bi, i: (0, 0))],
        out_specs=spec,
        out_shape=jax.ShapeDtypeStruct((b, s, d), BF16),
        compiler_params=_params("parallel", "parallel"),
        name="nsa_combine",
    )(o_cmp, o_sel, o_win, proj, expand)


def _nsa_mixer(x, scale, shift, w_in, pe, w1, b1, w2, rel_table):
    b, s, _ = x.shape
    hd, gd = N_HEADS * HEAD_DIM, NSA_G * HEAD_DIM
    gate_col = hd + 6 * gd
    n_in = gate_col + LANES
    proj = _linear_in(x, scale, shift, _pad_cols(w_in, n_in).astype(BF16), tn=n_in)

    def group_major(t):
        return t.reshape(b, s, NSA_G, HEAD_DIM).transpose(0, 2, 1, 3)

    kc, vc, ks, vs, kw, vw = (proj[:, :, hd + n * gd: hd + (n + 1) * gd] for n in range(6))
    cmp = _compress(jnp.stack([kc, vc]), pe.reshape(2, -1), w1, b1, w2)
    o_cmp, mb = _nsa_cmp(proj, cmp[0], cmp[1], rel_table)
    o_sel = _nsa_sel(proj, group_major(ks), group_major(vs), mb, rel_table)
    o_win = _nsa_win(proj, group_major(kw), group_major(vw), rel_table)
    return _nsa_combine(o_cmp, o_sel, o_win, proj, gate_col)


DIL_BACK = 128


def _dil_kernel(q_ref, kp_ref, kc_ref, vp_ref, vc_ref, bt_ref, o_ref, l_ref):
    t = pl.program_id(3)
    tq = q_ref.shape[1]
    span = DIL_BACK + SUB_Q
    kcat = jnp.concatenate([kp_ref[0, tq - DIL_BACK:, :], kc_ref[0]], axis=0)
    vcat = jnp.concatenate([vp_ref[0, tq - DIL_BACK:, :], vc_ref[0]], axis=0)
    lane_q = lax.broadcasted_iota(jnp.int32, (SUB_Q, LANES), 1) // HEAD_DIM
    lane_k = lax.broadcasted_iota(jnp.int32, (span, LANES), 1) // HEAD_DIM
    col = lax.broadcasted_iota(jnp.int32, (SUB_Q, span), 1)
    n_sub, n_head = tq // SUB_Q, LANES // HEAD_DIM
    scores = []
    for j in range(n_sub):
        qj = q_ref[0, j * SUB_Q:(j + 1) * SUB_Q, :] * (HEAD_DIM ** -0.5)
        kk = kcat[j * SUB_Q:j * SUB_Q + span]
        for hh in range(n_head):
            s = _dot_nt(jnp.where(lane_q == hh, qj, 0.0).astype(BF16), kk) + bt_ref[hh]
            if j == 0:
                s = jnp.where((col >= DIL_BACK) | (t > 0), s, NEG)
            scores.append(s)
    p_all, lse_all = _softmax_rows(jnp.concatenate(scores, axis=0))
    p_all = p_all.astype(BF16)
    for j in range(n_sub):
        vv = vcat[j * SUB_Q:j * SUB_Q + span]
        o = jnp.zeros((SUB_Q, LANES), F32)
        lse = jnp.zeros((SUB_Q, LANES), F32)
        for hh in range(n_head):
            r0 = (j * n_head + hh) * SUB_Q
            o = o + _dot(p_all[r0:r0 + SUB_Q], jnp.where(lane_k == hh, vv, 0.0).astype(BF16))
            lse = jnp.where(lane_q == hh, lse_all[r0:r0 + SUB_Q], lse)
        o_ref[0, j * SUB_Q:(j + 1) * SUB_Q, :] = o.astype(o_ref.dtype)
        l_ref[0, j * SUB_Q:(j + 1) * SUB_Q, :] = lse


def _dil_group(view, window, dil, rel_table):
    b, l, n_all = view.shape
    hd = N_HEADS * HEAD_DIM
    assert window // dil == DIL_BACK and n_all == dil * 3 * hd
    tq = min(1024, l)
    assert l % tq == 0 and tq >= DIL_BACK
    hpb = hd // LANES
    qi = jnp.arange(SUB_Q)[:, None]
    cc = jnp.arange(DIL_BACK + SUB_Q)[None, :]
    dd = qi + DIL_BACK - cc
    bt = jnp.where((dd[None] >= 0) & (dd[None] <= DIL_BACK), _rel_bias(rel_table, dd * dil), NEG)

    def spec(comp, prev):
        def index(bi, r, hp, t):
            return (bi, jnp.maximum(t - 1, 0) if prev else t, (r * 3 + comp) * hpb + hp)
        return pl.BlockSpec((1, tq, LANES), index)

    out_spec = pl.BlockSpec((1, tq, LANES), lambda bi, r, hp, t: (bi, t, r * hpb + hp))
    return pl.pallas_call(
        _dil_kernel,
        grid=(b, dil, hpb, l // tq),
        in_specs=[spec(0, False), spec(1, True), spec(1, False), spec(2, True), spec(2, False),
                  pl.BlockSpec((LANES // HEAD_DIM, SUB_Q, DIL_BACK + SUB_Q), lambda bi, r, hp, t: (hp, 0, 0))],
        out_specs=[out_spec, out_spec],
        out_shape=[jax.ShapeDtypeStruct((b, l, dil * hd), BF16), jax.ShapeDtypeStruct((b, l, dil * hd), F32)],
        compiler_params=_params("parallel", "parallel", "parallel", "parallel"),
        name=f"dilated_attention_{dil}",
    )(view, view, view, view, view, bt)


def _dil_merge_kernel(*refs, dils):
    n = len(dils)
    o_refs, l_refs, out_ref, scratch = refs[:n], refs[n:2 * n], refs[2 * n], refs[2 * n + 1:]
    tm, d = out_ref.shape[1], out_ref.shape[2]

    def token_major(ref, dil, buf):
        if dil == 1:
            return ref[0].astype(F32)
        for r in range(dil):
            for c in range(d // LANES):
                buf[c, pl.ds(r, tm // dil, stride=dil), :] = (
                    ref[0, :, r * d + c * LANES:r * d + (c + 1) * LANES].astype(F32))
        return jnp.concatenate([buf[c] for c in range(d // LANES)], axis=1)

    outs = [token_major(o_refs[g], dils[g], scratch[2 * g]) for g in range(n)]
    lses = [token_major(l_refs[g], dils[g], scratch[2 * g + 1]) for g in range(n)]
    m = functools.reduce(jnp.maximum, lses)
    es = [jnp.exp(l - m) for l in lses]
    inv = 1.0 / functools.reduce(jnp.add, es)
    out_ref[0] = functools.reduce(jnp.add, [e * inv * o for e, o in zip(es, outs)]).astype(out_ref.dtype)


def _dil_merge(outs, lses, dils, tm=512):
    b = outs[0].shape[0]
    d = N_HEADS * HEAD_DIM
    s = outs[0].shape[1] * dils[0]
    specs = [pl.BlockSpec((1, tm // dil, dil * d), lambda bi, i: (bi, i, 0)) for dil in dils]
    return pl.pallas_call(
        functools.partial(_dil_merge_kernel, dils=tuple(dils)),
        grid=(b, s // tm),
        in_specs=specs + specs,
        out_specs=pl.BlockSpec((1, tm, d), lambda bi, i: (bi, i, 0)),
        out_shape=jax.ShapeDtypeStruct((b, s, d), BF16),
        scratch_shapes=[pltpu.VMEM((d // LANES, tm, LANES), F32) for _ in range(2 * len(dils))],
        compiler_params=_params("parallel", "parallel"),
        name="dilated_merge",
    )(*outs, *lses)


def _dil_mixer(x, scale, shift, w_in, rel_table):
    hd = N_HEADS * HEAD_DIM
    w = w_in.astype(BF16)
    outs, lses, dils = [], [], []
    for gi, (window, dil) in enumerate(DIL_GROUPS):
        view = _linear_in(x, scale, shift, w[:, gi * 3 * hd:(gi + 1) * 3 * hd], tn=3 * hd, dil=dil)
        o, lse = _dil_group(view, window, dil, rel_table)
        outs.append(o)
        lses.append(lse)
        dils.append(dil)
    return _dil_merge(outs, lses, dils)


def kernel(x, c, rel_table, mod_w, mod_b, ln_g, ln_b, gla_w_in, gla_w_a2, gla_b_a, gla_norm_g, gla_w_o,
           nsa_w_in, nsa_cmp_pe, nsa_cmp_w1, nsa_cmp_b1, nsa_cmp_w2, nsa_w_o, dil_w_in, dil_w_o,
           ffn_w_up, ffn_conv_w, ffn_conv_b, ffn_w_down):
    shift, scale, gate = _modulation(c, mod_w, mod_b)
    for i in range(DEPTH):
        kind, j = i % N_MIXERS, i // N_MIXERS
        m = 2 * i
        if kind == 0:
            a = _gla_mixer(x, scale[m], shift[m], gla_w_in[j], gla_w_a2[j], gla_b_a[j], gla_norm_g[j])
            w_o = gla_w_o[j]
        elif kind == 1:
            a = _nsa_mixer(x, scale[m], shift[m], nsa_w_in[j], nsa_cmp_pe[j], nsa_cmp_w1[j], nsa_cmp_b1[j],
                           nsa_cmp_w2[j], rel_table)
            w_o = nsa_w_o[j]
        else:
            a = _dil_mixer(x, scale[m], shift[m], dil_w_in[j], rel_table)
            w_o = dil_w_o[j]
        x = _out_norm(a, w_o.astype(BF16), x, gate[m], ln_g[i, 0], ln_b[i, 0])
        a = _ffn_up(x, scale[m + 1], shift[m + 1], ffn_w_up[i].astype(BF16), ffn_conv_w[i], ffn_conv_b[i])
        x = _out_norm(a, ffn_w_down[i].astype(BF16), x, gate[m + 1], ln_g[i, 1], ln_b[i, 1])
    return x
```

```python
import functools
import math

import jax
import jax.numpy as jnp
import numpy as np
from jax import lax
from jax.experimental import pallas as pl
from jax.experimental.pallas import tpu as pltpu

F32 = jnp.float32
BF16 = jnp.bfloat16

D_MODEL = 1024
DEPTH = 4
N_MIXERS = 3
HEAD_DIM = 64
N_HEADS = D_MODEL // HEAD_DIM
REL_BUCKETS = 32
REL_MAX_DIST = 2048
LN_EPS = 1e-5
GLA_HEADS = 4
GLA_DK = D_MODEL // 2 // GLA_HEADS
GLA_DV = D_MODEL // GLA_HEADS
GLA_GATE_RANK = 16
GLA_TAU = 16.0
GLA_CHUNK = 64
NSA_G = 4
NSA_R = N_HEADS // NSA_G
CMP_LEN = 32
CMP_STRIDE = 16
CMP_HIDDEN = 256
SEL_LEN = 64
SEL_TOP = 16
WIN_LEN = 512
DIL_GROUPS = ((128, 1), (512, 4), (2048, 16))
D_FF = 2816
DN_ALPHA = (2 * DEPTH) ** 0.25

NEG = -1e30
VMEM_LIMIT = 56 * 1024 * 1024
LANES = 128
SUB_Q = 128


def _params(*sem):
    return pltpu.CompilerParams(dimension_semantics=sem, vmem_limit_bytes=VMEM_LIMIT)


def _dot(a, b):
    return jnp.dot(a, b, preferred_element_type=F32)


def _dot_nt(a, b):
    return lax.dot_general(a, b, (((1,), (1,)), ((), ())), preferred_element_type=F32)


def _dot_tn(a, b):
    return lax.dot_general(a, b, (((0,), (0,)), ((), ())), preferred_element_type=F32)


def _split(a):
    hi = a.astype(BF16)
    lo = (a - hi.astype(F32)).astype(BF16)
    return hi, lo


def _dot_hi_exact_rhs(a, b_bf16):
    hi, lo = _split(a)
    return _dot(hi, b_bf16) + _dot(lo, b_bf16)


def _silu(x):
    return x * jax.nn.sigmoid(x)


def _bucket_starts():
    n = np.arange(2 * REL_MAX_DIST)
    exact = REL_BUCKETS // 2
    logn = np.log(np.maximum(n, 1).astype(np.float32) / np.float32(exact))
    large = exact + (logn / np.float32(math.log(REL_MAX_DIST / exact)) * np.float32(REL_BUCKETS - exact)).astype(np.int32)
    bucket = np.where(n < exact, n, np.minimum(large, REL_BUCKETS - 1))
    assert np.all(np.diff(bucket) >= 0) and bucket[-1] == REL_BUCKETS - 1
    return [int(np.argmax(bucket >= k)) for k in range(REL_BUCKETS)]


BUCKET_START = _bucket_starts()
FAR_DIST = BUCKET_START[-1]


def _rel_bias(rel_table, dist):
    shape = (rel_table.shape[1],) + (1,) * dist.ndim
    out = jnp.broadcast_to(rel_table[0].reshape(shape), (rel_table.shape[1],) + dist.shape)
    for k in range(1, REL_BUCKETS):
        out = jnp.where(dist[None] >= BUCKET_START[k], rel_table[k].reshape(shape), out)
    return out


def _mod_kernel(c_ref, w_ref, b_ref, o_ref):
    cs = _silu(c_ref[...])
    ch, cl = _split(cs)
    wh, wl = _split(w_ref[0])
    o_ref[0] = _dot(ch, wh) + _dot(ch, wl) + _dot(cl, wh) + b_ref[0]


def _modulation(c, mod_w, mod_b):
    b, d = c.shape
    n_sub = mod_w.shape[0] * mod_w.shape[1]
    rows = 8
    cp = jnp.zeros((rows, d), F32).at[:b].set(c)
    w = mod_w.reshape(n_sub, d, 3 * d)
    bias = mod_b.reshape(n_sub, 1, 3 * d)
    tn = d
    out = pl.pallas_call(
        _mod_kernel,
        grid=(n_sub, 3 * d // tn),
        in_specs=[pl.BlockSpec((rows, d), lambda i, j: (0, 0)),
                  pl.BlockSpec((1, d, tn), lambda i, j: (i, 0, j)),
                  pl.BlockSpec((1, 1, tn), lambda i, j: (i, 0, j))],
        out_specs=pl.BlockSpec((1, rows, tn), lambda i, j: (i, 0, j)),
        out_shape=jax.ShapeDtypeStruct((n_sub, rows, 3 * d), F32),
        compiler_params=_params("parallel", "parallel"),
        name="modulation",
    )(cp, w, bias)
    out = out[:, :b]
    shift, scale, gate = jnp.split(out, 3, axis=-1)
    return shift[:, :, None, :], scale[:, :, None, :], gate[:, :, None, :]


def _linear_in_kernel(x_ref, sc_ref, sh_ref, w_ref, *rest, dil):
    h = (x_ref[0] * (1.0 + sc_ref[0]) + sh_ref[0]).astype(BF16)
    if dil == 1:
        o_ref, = rest
        o_ref[0] = _dot(h, w_ref[...]).astype(o_ref.dtype)
    else:
        perm_ref, o_ref = rest
        res = _dot(_dot(perm_ref[...], h).astype(BF16), w_ref[...])
        rows, n = res.shape[0] // dil, res.shape[1]
        for r in range(dil):
            o_ref[0, :, r * n:(r + 1) * n] = res[r * rows:(r + 1) * rows].astype(o_ref.dtype)


def _linear_in(x, scale, shift, w, tn, tm=512, dil=1):
    b, s, d = x.shape
    n = w.shape[1]
    assert n % tn == 0 and s % tm == 0 and tm % (16 * dil) == 0 and (dil == 1 or n == tn)
    in_specs = [pl.BlockSpec((1, tm, d), lambda j, bi, i: (bi, i, 0)),
                pl.BlockSpec((1, 1, d), lambda j, bi, i: (bi, 0, 0)),
                pl.BlockSpec((1, 1, d), lambda j, bi, i: (bi, 0, 0)),
                pl.BlockSpec((d, tn), lambda j, bi, i: (0, j))]
    args = [x, scale, shift, w]
    if dil > 1:
        out_row = jnp.arange(tm)
        src = (out_row % (tm // dil)) * dil + out_row // (tm // dil)
        in_specs.append(pl.BlockSpec((tm, tm), lambda j, bi, i: (0, 0)))
        args.append((src[:, None] == jnp.arange(tm)[None, :]).astype(BF16))
    return pl.pallas_call(
        functools.partial(_linear_in_kernel, dil=dil),
        grid=(n // tn, b, s // tm),
        in_specs=in_specs,
        out_specs=pl.BlockSpec((1, tm // dil, dil * tn), lambda j, bi, i: (bi, i, j)),
        out_shape=jax.ShapeDtypeStruct((b, s // dil, dil * n), BF16),
        compiler_params=_params("parallel", "parallel", "parallel"),
        name="linear_in",
    )(*args)


OUT_ROWS = 512


def _out_norm_kernel(a_ref, w_ref, x_ref, gate_ref, g_ref, b_ref, o_ref):
    chunks = [slice(r, r + OUT_ROWS) for r in range(0, a_ref.shape[1], OUT_ROWS)]
    ys = [_dot(a_ref[0, sl, :], w_ref[...]) for sl in chunks]
    for sl, y in zip(chunks, ys):
        z = DN_ALPHA * x_ref[0, sl, :] + (1.0 + gate_ref[0]) * y
        mu = jnp.mean(z, axis=-1, keepdims=True)
        zc = z - mu
        var = jnp.mean(zc * zc, axis=-1, keepdims=True)
        o_ref[0, sl, :] = zc * lax.rsqrt(var + LN_EPS) * g_ref[...] + b_ref[...]


def _out_norm(a, w, x, gate, ln_g, ln_b, tm=1024):
    b, s, d = x.shape
    k = a.shape[-1]
    return pl.pallas_call(
        _out_norm_kernel,
        grid=(b, s // tm),
        in_specs=[pl.BlockSpec((1, tm, k), lambda bi, i: (bi, i, 0)),
                  pl.BlockSpec((k, d), lambda bi, i: (0, 0)),
                  pl.BlockSpec((1, tm, d), lambda bi, i: (bi, i, 0)),
                  pl.BlockSpec((1, 1, d), lambda bi, i: (bi, 0, 0)),
                  pl.BlockSpec((1, d), lambda bi, i: (0, 0)),
                  pl.BlockSpec((1, d), lambda bi, i: (0, 0))],
        out_specs=pl.BlockSpec((1, tm, d), lambda bi, i: (bi, i, 0)),
        out_shape=jax.ShapeDtypeStruct((b, s, d), F32),
        compiler_params=_params("parallel", "parallel"),
        name="out_norm",
    )(a, w, x, gate, ln_g.reshape(1, d), ln_b.reshape(1, d))


FFN_HALO = 16
FFN_COLS = 256


def _ffn_up_kernel(x_ref, xh_ref, sc_ref, sh_ref, wu_ref, wg_ref, cw_ref, cb_ref, o_ref, h_ref):
    i = pl.program_id(1)
    sc = 1.0 + sc_ref[0]
    sh = sh_ref[0]
    h_ref[0:FFN_HALO, :] = (xh_ref[0] * sc + sh).astype(BF16)
    h_ref[FFN_HALO:, :] = (x_ref[0] * sc + sh).astype(BF16)
    keep = (i > 0).astype(F32)
    cw = cw_ref[...]
    for c in range(0, o_ref.shape[2], FFN_COLS):
        cols = slice(c, c + FFN_COLS)
        u = _dot(h_ref[...], wu_ref[:, cols])
        u = jnp.concatenate([u[0:FFN_HALO] * keep, u[FFN_HALO:]], axis=0)
        g = _dot(h_ref[FFN_HALO:, :], wg_ref[:, cols])
        a = u * cw[1:2, cols] + pltpu.roll(u, 1, 0) * cw[0:1, cols]
        conv = (u * cw[2:3, cols] + pltpu.roll(a, 1, 0))[FFN_HALO:] + cb_ref[:, cols]
        o_ref[0, :, cols] = (_silu(conv) * g).astype(o_ref.dtype)


def _ffn_up(x, scale, shift, w_up, conv_w, conv_b, tm=512):
    b, s, d = x.shape
    f = conv_w.shape[1]
    assert s % tm == 0 and tm % FFN_HALO == 0 and f % FFN_COLS == 0
    hb = tm // FFN_HALO
    once = pl.Buffered(1)
    return pl.pallas_call(
        _ffn_up_kernel,
        grid=(b, s // tm),
        in_specs=[pl.BlockSpec((1, tm, d), lambda bi, i: (bi, i, 0)),
                  pl.BlockSpec((1, FFN_HALO, d), lambda bi, i: (bi, jnp.maximum(i * hb - 1, 0), 0)),
                  pl.BlockSpec((1, 1, d), lambda bi, i: (bi, 0, 0)),
                  pl.BlockSpec((1, 1, d), lambda bi, i: (bi, 0, 0)),
                  pl.BlockSpec((d, f), lambda bi, i: (0, 0), pipeline_mode=once),
                  pl.BlockSpec((d, f), lambda bi, i: (0, 1), pipeline_mode=once),
                  pl.BlockSpec((3, f), lambda bi, i: (0, 0)),
                  pl.BlockSpec((1, f), lambda bi, i: (0, 0))],
        out_specs=pl.BlockSpec((1, tm, f), lambda bi, i: (bi, i, 0)),
        out_shape=jax.ShapeDtypeStruct((b, s, f), BF16),
        scratch_shapes=[pltpu.VMEM((tm + FFN_HALO, d), BF16)],
        compiler_params=_params("parallel", "parallel"),
        name="ffn_up",
    )(x, x, scale, shift, w_up, w_up, conv_w, conv_b.reshape(1, f))


def _gla_kernel(q_ref, k_ref, v_ref, r_ref, a_ref, wa_ref, ba_ref, ng_ref, o_ref, st_ref):
    tc = q_ref.shape[1]
    nchunk = tc // GLA_CHUNK

    @pl.when(pl.program_id(2) == 0)
    def _():
        st_ref[...] = jnp.zeros_like(st_ref)

    z = _dot(a_ref[0], wa_ref[...]) + ba_ref[...]
    log_a = (jnp.minimum(z, 0.0) - jnp.log1p(jnp.exp(-jnp.abs(z)))) / GLA_TAU
    ci = lax.broadcasted_iota(jnp.int32, (GLA_CHUNK, GLA_CHUNK), 0)
    cj = lax.broadcasted_iota(jnp.int32, (GLA_CHUNK, GLA_CHUNK), 1)
    causal = cj <= ci
    tri = jnp.where(causal, 1.0, 0.0).astype(BF16)
    chunks = [slice(c * GLA_CHUNK, (c + 1) * GLA_CHUNK) for c in range(nchunk)]
    la_hi, la_lo = _split(log_a)
    cums = [_dot(tri, la_hi[sl]) + _dot(tri, la_lo[sl]) for sl in chunks]
    bcum = jnp.concatenate(cums, axis=0)
    btot = jnp.concatenate([jnp.broadcast_to(cm[GLA_CHUNK - 1:], cm.shape) for cm in cums], axis=0)
    qf = q_ref[0].astype(F32) * (GLA_DK ** -0.5)
    kf = k_ref[0].astype(F32)
    q_dec = (qf * jnp.exp(bcum)).astype(BF16)
    k_inv = (kf * jnp.exp(-bcum)).astype(BF16)
    k_dec = (kf * jnp.exp(btot - bcum)).astype(BF16)
    dl = jnp.exp(btot)
    v = v_ref[0]
    att = [jnp.where(causal, _dot_nt(q_dec[sl], k_inv[sl]), 0.0).astype(BF16) for sl in chunks]
    kv = [_dot_tn(v[sl], k_dec[sl]) for sl in chunks]
    o_intra = [_dot(att[c], v[chunks[c]]) for c in range(nchunk)]
    st = st_ref[...]
    o_inter = []
    for c, sl in enumerate(chunks):
        o_inter.append(_dot_nt(q_dec[sl], st.astype(BF16)))
        st = st * dl[c * GLA_CHUNK:c * GLA_CHUNK + 1] + kv[c]
    st_ref[...] = st
    o = jnp.concatenate(o_intra, axis=0) + jnp.concatenate(o_inter, axis=0)
    o = o * lax.rsqrt(jnp.mean(o * o, axis=-1, keepdims=True) + LN_EPS) * ng_ref[...]
    o_ref[0] = (o * _silu(r_ref[0].astype(F32))).astype(o_ref.dtype)


def _gla_core(proj, w_a2, b_a, norm_g, tc=512):
    b, s, _ = proj.shape
    h, dk, dv = GLA_HEADS, GLA_DK, GLA_DV
    assert s % tc == 0
    wa = jnp.zeros((LANES, h * dk), F32).at[:GLA_GATE_RANK].set(w_a2).astype(BF16)
    kb, vb, rb, ab = (h * dk) // dk, (2 * h * dk) // dv, (2 * h * dk + h * dv) // dv, (2 * h * dk + 2 * h * dv) // LANES
    return pl.pallas_call(
        _gla_kernel,
        grid=(b, h, s // tc),
        in_specs=[pl.BlockSpec((1, tc, dk), lambda bi, hi, t: (bi, t, hi)),
                  pl.BlockSpec((1, tc, dk), lambda bi, hi, t: (bi, t, kb + hi)),
                  pl.BlockSpec((1, tc, dv), lambda bi, hi, t: (bi, t, vb + hi)),
                  pl.BlockSpec((1, tc, dv), lambda bi, hi, t: (bi, t, rb + hi)),
                  pl.BlockSpec((1, tc, LANES), lambda bi, hi, t: (bi, t, ab)),
                  pl.BlockSpec((LANES, dk), lambda bi, hi, t: (0, hi)),
                  pl.BlockSpec((1, dk), lambda bi, hi, t: (0, hi)),
                  pl.BlockSpec((1, dv), lambda bi, hi, t: (0, 0))],
        out_specs=pl.BlockSpec((1, tc, dv), lambda bi, hi, t: (bi, t, hi)),
        out_shape=jax.ShapeDtypeStruct((b, s, h * dv), BF16),
        scratch_shapes=[pltpu.VMEM((dv, dk), F32)],
        compiler_params=_params("parallel", "parallel", "arbitrary"),
        name="gla_core",
    )(proj, proj, proj, proj, proj, wa, b_a.reshape(1, h * dk), norm_g.reshape(1, dv))


def _pad_cols(w, n):
    return jnp.pad(w, ((0, 0), (0, n - w.shape[1])))


def _gla_mixer(x, scale, shift, w_in, w_a2, b_a, norm_g):
    n_in = 2 * GLA_HEADS * GLA_DK + 2 * GLA_HEADS * GLA_DV + LANES
    proj = _linear_in(x, scale, shift, _pad_cols(w_in, n_in).astype(BF16), tn=n_in)
    return _gla_core(proj, w_a2, b_a, norm_g)


def _compress_kernel(lo_ref, hi_ref, pe_ref, w1_ref, b1_ref, w2_ref, o_ref):
    half = lo_ref.shape[-1]
    pe = pe_ref[0]
    a_lo = (lo_ref[0, 0, 0].astype(F32) + pe[:, :half]).astype(BF16)
    a_hi = (hi_ref[0, 0, 0].astype(F32) + pe[:, half:]).astype(BF16)
    h1 = _dot(a_lo, w1_ref[0, :half, :]) + _dot(a_hi, w1_ref[0, half:, :]) + b1_ref[0]
    o_ref[0, 0, 0] = _dot(_silu(h1).astype(BF16), w2_ref[0])


def _compress(kv, pe, w1, b1, w2):
    two, b, s, _ = kv.shape
    g, dh = NSA_G, HEAD_DIM
    npc = s // CMP_STRIDE
    half = CMP_STRIDE * dh
    a = kv.reshape(two, b, npc, CMP_STRIDE, g, dh).transpose(0, 1, 4, 2, 3, 5).reshape(two, b, g, npc, half)
    a_hi = jnp.concatenate([a[:, :, :, 1:], jnp.zeros_like(a[:, :, :, :1])], axis=3)
    spec_a = pl.BlockSpec((1, 1, 1, npc, half), lambda c, bi, gi: (c, bi, gi, 0, 0))
    return pl.pallas_call(
        _compress_kernel,
        grid=(two, b, g),
        in_specs=[spec_a, spec_a,
                  pl.BlockSpec((1, 1, 2 * half), lambda c, bi, gi: (c, 0, 0)),
                  pl.BlockSpec((1, 2 * half, CMP_HIDDEN), lambda c, bi, gi: (c, 0, 0)),
                  pl.BlockSpec((1, 1, CMP_HIDDEN), lambda c, bi, gi: (c, 0, 0)),
                  pl.BlockSpec((1, CMP_HIDDEN, dh), lambda c, bi, gi: (c, 0, 0))],
        out_specs=pl.BlockSpec((1, 1, 1, npc, dh), lambda c, bi, gi: (c, bi, gi, 0, 0)),
        out_shape=jax.ShapeDtypeStruct((two, b, g, npc, dh), F32),
        compiler_params=_params("parallel", "parallel", "parallel"),
        name="nsa_compress",
    )(a, a_hi, pe.reshape(two, 1, 2 * half), w1.astype(BF16), b1.reshape(two, 1, CMP_HIDDEN), w2.astype(BF16))


def _stack_heads(q):
    return jnp.concatenate([q[:, r * HEAD_DIM:(r + 1) * HEAD_DIM] for r in range(NSA_R)], axis=0)


def _unstack_heads(o, rows):
    return jnp.concatenate([o[r * rows:(r + 1) * rows] for r in range(NSA_R)], axis=1)


CMP_TQ = 512
LOG2E = 1.4426950408889634


def _nsa_cmp_kernel(q_ref, fq_ref, kc_ref, vc_ref, nb_ref, pb_ref, o_ref, mb_ref, imp_ref):
    i = pl.program_id(2)
    n_tile = pl.num_programs(2) - 1

    @pl.when(i == 0)
    def _():
        _cmp_attend(i, q_ref, fq_ref, kc_ref, vc_ref, nb_ref, pb_ref, o_ref, imp_ref)

    @pl.when((i > 0) & (i < n_tile))
    def _():
        imp = imp_ref[...]
        _cmp_attend(i, q_ref, fq_ref, kc_ref, vc_ref, nb_ref, pb_ref, o_ref, imp_ref)
        _cmp_select(i - 1, imp, mb_ref)

    @pl.when(i == n_tile)
    def _():
        _cmp_select(i - 1, imp_ref[...], mb_ref)


def _cmp_attend(i, q_ref, fq_ref, kc_ref, vc_ref, nb_ref, pb_ref, o_ref, imp_ref):
    npc = nb_ref.shape[2]
    n_sub, rows = CMP_TQ // SUB_Q, NSA_R * SUB_Q
    starts = [pl.multiple_of((i * n_sub + j) * (SUB_Q // CMP_STRIDE), 8) for j in range(n_sub)]

    scores = []
    for j in range(n_sub):
        kwin = kc_ref[0, 0, pl.ds(starts[j], npc), :].astype(BF16)
        qs = _stack_heads(q_ref[0, j * SUB_Q:(j + 1) * SUB_Q, :]).astype(F32) * (HEAD_DIM ** -0.5 * LOG2E)
        qp = jnp.concatenate([qs, fq_ref[...]], axis=1).astype(BF16)
        scores.append(_dot_nt(qp, kwin) + nb_ref[0])
    s = jnp.concatenate(scores, axis=0)
    m = jnp.max(s, axis=1, keepdims=True)
    e = jnp.exp2(s - m)
    den = jnp.sum(e, axis=1, keepdims=True)
    p = e * jnp.where(m > 0.5 * NEG, 1.0 / den, 0.0)
    p_bf = p.astype(BF16)
    for j in range(n_sub):
        vwin = vc_ref[0, 0, pl.ds(starts[j], npc), :].astype(BF16)
        o = _dot(p_bf[j * rows:(j + 1) * rows], vwin)
        o_ref[0, j * SUB_Q:(j + 1) * SUB_Q, :] = _unstack_heads(o, SUB_Q).astype(o_ref.dtype)

    imps = []
    for j in range(n_sub):
        pg = p[j * rows:j * rows + SUB_Q]
        for r in range(1, NSA_R):
            pg = pg + p[j * rows + r * SUB_Q:j * rows + (r + 1) * SUB_Q]
        imps.append(_dot_hi_exact_rhs(pg, pb_ref[pl.ds(starts[j], npc), :].astype(BF16)))
    imp_ref[...] = jnp.concatenate(imps, axis=0)


def _cmp_select(i, imp, mb_ref):
    nblk = mb_ref.shape[3]
    blk = lax.broadcasted_iota(jnp.int32, (CMP_TQ, nblk), 1)
    blk_f = blk.astype(F32)
    t = i * CMP_TQ + lax.broadcasted_iota(jnp.int32, (CMP_TQ, nblk), 0)
    cur = t // SEL_LEN
    forced = (blk == 0) | (blk == cur) | (blk == cur - 1)
    work = jnp.where((blk * SEL_LEN <= t) & jnp.logical_not(forced), imp, NEG)
    free = work > 0.5 * NEG
    for _ in range(SEL_TOP - 3):
        mx = jnp.max(work, axis=1, keepdims=True)
        first = jnp.min(jnp.where(work == mx, blk_f, 1e9), axis=1, keepdims=True)
        work = jnp.where(blk_f == first, 2.0 * NEG, work)
    sel = forced | (free & (work < 1.5 * NEG))
    mb_ref[0, 0] = jnp.where(sel, 0.0, NEG).astype(mb_ref.dtype)


def _nsa_cmp(proj, kcmp, vcmp, rel_table):
    b, s, _ = proj.shape
    g, dh = NSA_G, HEAD_DIM
    npc = s // CMP_STRIDE
    nblk = s // SEL_LEN
    front = npc - SUB_Q // CMP_STRIDE
    assert s % CMP_TQ == 0 and nblk >= SEL_TOP
    n_tile = s // CMP_TQ
    kpad = jnp.pad(kcmp, ((0, 0), (0, 0), (front, npc - front), (0, LANES - dh)))
    kpad = kpad.at[:, :, :front, dh].set(1.0)
    vpad = jnp.pad(vcmp, ((0, 0), (0, 0), (front, npc - front), (0, 0)))
    fq = jnp.zeros((NSA_R * SUB_Q, LANES - dh), F32).at[:, 0].set(NEG)
    qi = jnp.arange(SUB_Q)[:, None]
    cc = jnp.arange(npc)[None, :]
    dd = qi - (CMP_LEN - 1) + CMP_STRIDE * (front - cc)
    nb = jnp.where(dd[None] >= 0, _rel_bias(rel_table, dd) * LOG2E, NEG).reshape(g, NSA_R * SUB_Q, npc)
    diff = jnp.arange(2 * npc)[:, None] - front - (SEL_LEN // CMP_STRIDE) * jnp.arange(nblk)[None, :]
    pool = jnp.where((diff == -1) | (diff == 3), 1.0, jnp.where((diff >= 0) & (diff <= 2), 2.0, 0.0))
    return pl.pallas_call(
        _nsa_cmp_kernel,
        grid=(g, b, n_tile + 1),
        in_specs=[pl.BlockSpec((1, CMP_TQ, NSA_R * dh), lambda gi, bi, i: (bi, jnp.minimum(i, n_tile - 1), gi)),
                  pl.BlockSpec((NSA_R * SUB_Q, LANES - dh), lambda gi, bi, i: (0, 0)),
                  pl.BlockSpec((1, 1, 2 * npc, LANES), lambda gi, bi, i: (bi, gi, 0, 0)),
                  pl.BlockSpec((1, 1, 2 * npc, dh), lambda gi, bi, i: (bi, gi, 0, 0)),
                  pl.BlockSpec((1, NSA_R * SUB_Q, npc), lambda gi, bi, i: (gi, 0, 0)),
                  pl.BlockSpec((2 * npc, nblk), lambda gi, bi, i: (0, 0))],
        out_specs=[pl.BlockSpec((1, CMP_TQ, NSA_R * dh), lambda gi, bi, i: (bi, jnp.minimum(i, n_tile - 1), gi)),
                   pl.BlockSpec((1, 1, CMP_TQ, nblk), lambda gi, bi, i: (bi, gi, jnp.maximum(i - 1, 0), 0))],
        out_shape=[jax.ShapeDtypeStruct((b, s, N_HEADS * dh), BF16),
                   jax.ShapeDtypeStruct((b, g, s, nblk), BF16)],
        scratch_shapes=[pltpu.VMEM((CMP_TQ, nblk), F32)],
        compiler_params=_params("parallel", "parallel", "arbitrary"),
        name="nsa_cmp_select",
    )(proj, fq, kpad, vpad, nb, pool.astype(F32))


SEL_T = 256
SEL_WIDE = 8
SEL_NEAR = -(-(FAR_DIST + SEL_T - 1) // SEL_T)


def _nsa_sel_kernel(q_ref, fq_ref, mb_ref, k_ref, v_ref, bt_ref, o_ref, qp_ref, m_ref, acc_ref):
    i = pl.program_id(2)
    rows = NSA_R * SEL_T
    fw = (k_ref.shape[3] - LANES)
    kh = fw // (SEL_T // SEL_LEN)
    qs = (_stack_heads(q_ref[0]).astype(F32) * (HEAD_DIM ** -0.5 * LOG2E)).astype(BF16)
    mb = mb_ref[0, 0]
    for half in range(2):
        mbh = jnp.concatenate([mb[:, half * fw:(half + 1) * fw]] * NSA_R, axis=0)
        qp_ref[half] = jnp.concatenate([qs, fq_ref[0], mbh], axis=1)
    m_ref[...] = jnp.full_like(m_ref, NEG)
    acc_ref[...] = jnp.zeros_like(acc_ref)

    def update(kt, n_tile, half, bias):
        k0 = pl.multiple_of(kt * SEL_T, SEL_T)
        kk = k_ref[0, 0, pl.ds(k0, n_tile * SEL_T), :]
        vv = v_ref[0, 0, pl.ds(k0, n_tile * SEL_T), :]
        s = _dot_nt(qp_ref[half], kk)
        if bias is not None:
            s = s + bias
        m_prev = m_ref[...]
        m_new = jnp.maximum(m_prev, jnp.max(s, axis=1, keepdims=True))
        alpha = jnp.exp2(m_prev - m_new)
        p = jnp.exp2(s - jnp.concatenate([m_new] * (n_tile * SEL_T // LANES), axis=1))
        acc_ref[...] = acc_ref[...] * alpha + _dot(p.astype(BF16), vv)
        m_ref[...] = m_new

    def far_region(lo, hi, half):
        n = jnp.maximum(hi - lo, 0)

        def wide(j, carry):
            update(lo + SEL_WIDE * j, SEL_WIDE, half, None)
            return carry

        lax.fori_loop(0, n // SEL_WIDE, wide, 0)
        piece = SEL_WIDE // 2
        while piece >= 1:
            @pl.when((n & piece) != 0)
            def _(piece=piece):
                update(lo + (n & ~(2 * piece - 1)), piece, half, None)
            piece //= 2

    n_far = jnp.maximum(i - SEL_NEAR + 1, 0)
    far_region(0, jnp.minimum(n_far, kh), 0)
    far_region(kh, n_far, 1)

    def near(e):
        kt = i - e
        update(kt, 1, jnp.where(kt < kh, 0, 1), bt_ref[0, :, e * SEL_T:(e + 1) * SEL_T])

    @pl.when(i >= SEL_NEAR - 1)
    def _():
        for e in range(SEL_NEAR - 1, -1, -1):
            near(e)

    @pl.when(i < SEL_NEAR - 1)
    def _():
        for e in range(SEL_NEAR - 2, -1, -1):
            @pl.when(i - e >= 0)
            def _(e=e):
                near(e)

    acc = acc_ref[...]
    o = acc[:, :HEAD_DIM] / acc[:, HEAD_DIM:HEAD_DIM + 1]
    o_ref[0] = _unstack_heads(o, SEL_T).astype(o_ref.dtype)


def _nsa_sel(proj, ks, vs, mb, rel_table):
    b, s, _ = proj.shape
    g, dh = NSA_G, HEAD_DIM
    nblk = s // SEL_LEN
    fw = nblk // 2
    rows = NSA_R * SEL_T
    pos = jnp.arange(s)
    onehot = (((pos // SEL_LEN) % fw)[:, None] == jnp.arange(fw)[None, :]).astype(BF16)
    ones2 = jnp.zeros((s, LANES - dh), BF16).at[:, :2].set(1.0)
    kp = jnp.concatenate([ks, jnp.broadcast_to(ones2, (b, g, s, LANES - dh)),
                          jnp.broadcast_to(onehot, (b, g, s, fw))], axis=-1)
    ones1 = jnp.zeros((s, LANES - dh), BF16).at[:, :1].set(1.0)
    vp = jnp.concatenate([vs, jnp.broadcast_to(ones1, (b, g, s, LANES - dh))], axis=-1)
    b_far = rel_table[REL_BUCKETS - 1].reshape(g, NSA_R) * LOG2E
    hi = b_far.astype(BF16)
    lo = (b_far - hi.astype(F32)).astype(BF16)
    fq = jnp.zeros((g, NSA_R, SEL_T, LANES - dh), BF16)
    fq = fq.at[:, :, :, 0].set(hi[:, :, None]).at[:, :, :, 1].set(lo[:, :, None]).reshape(g, rows, LANES - dh)
    qi = jnp.arange(SEL_T)[:, None]
    cc = jnp.arange(SEL_NEAR * SEL_T)[None, :]
    dd = (cc // SEL_T) * SEL_T + qi - (cc % SEL_T)
    bias = _rel_bias(rel_table, dd).reshape(g, NSA_R, SEL_T, -1) * LOG2E - b_far[:, :, None, None]
    bt = jnp.where(dd[None, None] >= 0, bias, NEG).reshape(g, rows, SEL_NEAR * SEL_T)
    return pl.pallas_call(
        _nsa_sel_kernel,
        grid=(g, b, s // SEL_T),
        in_specs=[pl.BlockSpec((1, SEL_T, NSA_R * dh), lambda gi, bi, i: (bi, i, gi)),
                  pl.BlockSpec((1, rows, LANES - dh), lambda gi, bi, i: (gi, 0, 0)),
                  pl.BlockSpec((1, 1, SEL_T, nblk), lambda gi, bi, i: (bi, gi, i, 0)),
                  pl.BlockSpec((1, 1, s, LANES + fw), lambda gi, bi, i: (bi, gi, 0, 0)),
                  pl.BlockSpec((1, 1, s, LANES), lambda gi, bi, i: (bi, gi, 0, 0)),
                  pl.BlockSpec((1, rows, SEL_NEAR * SEL_T), lambda gi, bi, i: (gi, 0, 0),
                               pipeline_mode=pl.Buffered(1))],
        out_specs=pl.BlockSpec((1, SEL_T, NSA_R * dh), lambda gi, bi, i: (bi, i, gi)),
        out_shape=jax.ShapeDtypeStruct((b, s, N_HEADS * dh), BF16),
        scratch_shapes=[pltpu.VMEM((2, rows, LANES + fw), BF16), pltpu.VMEM((rows, LANES), F32),
                        pltpu.VMEM((rows, LANES), F32)],
        compiler_params=_params("parallel", "parallel", "parallel"),
        name="nsa_selected",
    )(proj, fq, mb, kp, vp, bt)


WIN_T = 512


def _softmax_rows(s):
    m = jnp.max(s, axis=1, keepdims=True)
    e = jnp.exp(s - m)
    den = jnp.sum(e, axis=1, keepdims=True)
    return e / den, m + jnp.log(den)


def _nsa_win_kernel(q_ref, kp_ref, kc_ref, vp_ref, vc_ref, bt_ref, o_ref):
    i = pl.program_id(2)
    back = WIN_T
    span = back + SUB_Q
    kcat = jnp.concatenate([kp_ref[0, 0], kc_ref[0, 0]], axis=0)
    vcat = jnp.concatenate([vp_ref[0, 0], vc_ref[0, 0]], axis=0)
    col = lax.broadcasted_iota(jnp.int32, (NSA_R * SUB_Q, span), 1)
    n_sub, rows = WIN_T // SUB_Q, NSA_R * SUB_Q
    scores = []
    for j in range(n_sub):
        qs = _stack_heads(q_ref[0, j * SUB_Q:(j + 1) * SUB_Q, :]) * (HEAD_DIM ** -0.5)
        s = _dot_nt(qs, kcat[j * SUB_Q:j * SUB_Q + span]) + bt_ref[0]
        scores.append(jnp.where((col >= back - j * SUB_Q) | (i > 0), s, NEG))
    p_all, _ = _softmax_rows(jnp.concatenate(scores, axis=0))
    p_all = p_all.astype(BF16)
    for j in range(n_sub):
        o = _dot(p_all[j * rows:(j + 1) * rows], vcat[j * SUB_Q:j * SUB_Q + span])
        o_ref[0, j * SUB_Q:(j + 1) * SUB_Q, :] = _unstack_heads(o, SUB_Q).astype(o_ref.dtype)


def _nsa_win(proj, kw, vw, rel_table):
    b, s, _ = proj.shape
    g, dh = NSA_G, HEAD_DIM
    span = WIN_T + SUB_Q
    qi = jnp.arange(SUB_Q)[:, None]
    cc = jnp.arange(span)[None, :]
    dd = qi + WIN_T - cc
    bias = _rel_bias(rel_table, dd)
    bt = jnp.where((dd[None] >= 0) & (dd[None] <= WIN_LEN - 1), bias, NEG).reshape(g, NSA_R * SUB_Q, span)
    prev = lambda gi, bi, i: (bi, gi, jnp.maximum(i - 1, 0), 0)
    cur = lambda gi, bi, i: (bi, gi, i, 0)
    return pl.pallas_call(
        _nsa_win_kernel,
        grid=(g, b, s // WIN_T),
        in_specs=[pl.BlockSpec((1, WIN_T, NSA_R * dh), lambda gi, bi, i: (bi, i, gi)),
                  pl.BlockSpec((1, 1, WIN_T, dh), prev), pl.BlockSpec((1, 1, WIN_T, dh), cur),
                  pl.BlockSpec((1, 1, WIN_T, dh), prev), pl.BlockSpec((1, 1, WIN_T, dh), cur),
                  pl.BlockSpec((1, NSA_R * SUB_Q, span), lambda gi, bi, i: (gi, 0, 0))],
        out_specs=pl.BlockSpec((1, WIN_T, NSA_R * dh), lambda gi, bi, i: (bi, i, gi)),
        out_shape=jax.ShapeDtypeStruct((b, s, N_HEADS * dh), BF16),
        compiler_params=_params("parallel", "parallel", "parallel"),
        name="nsa_window",
    )(proj, kw, kw, vw, vw, bt)


def _nsa_combine_kernel(oc_ref, os_ref, ow_ref, g_ref, e_ref, o_ref):
    d = o_ref.shape[2]
    sg = jax.nn.sigmoid(g_ref[0].astype(F32))
    ge = _dot_hi_exact_rhs(sg, e_ref[...])
    o = (ge[:, :d] * oc_ref[0].astype(F32) + ge[:, d:2 * d] * os_ref[0].astype(F32)
         + ge[:, 2 * d:] * ow_ref[0].astype(F32))
    o_ref[0] = o.astype(o_ref.dtype)


def _nsa_combine(o_cmp, o_sel, o_win, proj, gate_col, tm=512):
    b, s, d = o_cmp.shape
    c = jnp.arange(LANES)[:, None]
    n = jnp.arange(3 * d)[None, :]
    expand = (c == (n // d) * N_HEADS + (n % d) // HEAD_DIM).astype(BF16)
    spec = pl.BlockSpec((1, tm, d), lambda bi, i: (bi, i, 0))
    return pl.pallas_call(
        _nsa_combine_kernel,
        grid=(b, s // tm),
        in_specs=[spec, spec, spec,
                  pl.BlockSpec((1, tm, LANES), lambda bi, i: (bi, i, gate_col // LANES)),
                  pl.BlockSpec((LANES, 3 * d), lambda bi, i: (0, 0))],
        out_specs=spec,
        out_shape=jax.ShapeDtypeStruct((b, s, d), BF16),
        compiler_params=_params("parallel", "parallel"),
        name="nsa_combine",
    )(o_cmp, o_sel, o_win, proj, expand)


def _nsa_mixer(x, scale, shift, w_in, pe, w1, b1, w2, rel_table):
    b, s, _ = x.shape
    hd, gd = N_HEADS * HEAD_DIM, NSA_G * HEAD_DIM
    gate_col = hd + 6 * gd
    n_in = gate_col + LANES
    proj = _linear_in(x, scale, shift, _pad_cols(w_in, n_in).astype(BF16), tn=n_in)

    def group_major(t):
        return t.reshape(b, s, NSA_G, HEAD_DIM).transpose(0, 2, 1, 3)

    kc, vc, ks, vs, kw, vw = (proj[:, :, hd + n * gd: hd + (n + 1) * gd] for n in range(6))
    cmp = _compress(jnp.stack([kc, vc]), pe.reshape(2, -1), w1, b1, w2)
    o_cmp, mb = _nsa_cmp(proj, cmp[0], cmp[1], rel_table)
    o_sel = _nsa_sel(proj, group_major(ks), group_major(vs), mb, rel_table)
    o_win = _nsa_win(proj, group_major(kw), group_major(vw), rel_table)
    return _nsa_combine(o_cmp, o_sel, o_win, proj, gate_col)


DIL_BACK = 128


def _dil_kernel(q_ref, kp_ref, kc_ref, vp_ref, vc_ref, bt_ref, o_ref, l_ref):
    t = pl.program_id(3)
    tq = q_ref.shape[1]
    span = DIL_BACK + SUB_Q
    kcat = jnp.concatenate([kp_ref[0, tq - DIL_BACK:, :], kc_ref[0]], axis=0)
    vcat = jnp.concatenate([vp_ref[0, tq - DIL_BACK:, :], vc_ref[0]], axis=0)
    lane_q = lax.broadcasted_iota(jnp.int32, (SUB_Q, LANES), 1) // HEAD_DIM
    lane_k = lax.broadcasted_iota(jnp.int32, (span, LANES), 1) // HEAD_DIM
    col = lax.broadcasted_iota(jnp.int32, (SUB_Q, span), 1)
    n_sub, n_head = tq // SUB_Q, LANES // HEAD_DIM
    scores = []
    for j in range(n_sub):
        qj = q_ref[0, j * SUB_Q:(j + 1) * SUB_Q, :] * (HEAD_DIM ** -0.5)
        kk = kcat[j * SUB_Q:j * SUB_Q + span]
        for hh in range(n_head):
            s = _dot_nt(jnp.where(lane_q == hh, qj, 0.0).astype(BF16), kk) + bt_ref[hh]
            if j == 0:
                s = jnp.where((col >= DIL_BACK) | (t > 0), s, NEG)
            scores.append(s)
    p_all, lse_all = _softmax_rows(jnp.concatenate(scores, axis=0))
    p_all = p_all.astype(BF16)
    for j in range(n_sub):
        vv = vcat[j * SUB_Q:j * SUB_Q + span]
        o = jnp.zeros((SUB_Q, LANES), F32)
        lse = jnp.zeros((SUB_Q, LANES), F32)
        for hh in range(n_head):
            r0 = (j * n_head + hh) * SUB_Q
            o = o + _dot(p_all[r0:r0 + SUB_Q], jnp.where(lane_k == hh, vv, 0.0).astype(BF16))
            lse = jnp.where(lane_q == hh, lse_all[r0:r0 + SUB_Q], lse)
        o_ref[0, j * SUB_Q:(j + 1) * SUB_Q, :] = o.astype(o_ref.dtype)
        l_ref[0, j * SUB_Q:(j + 1) * SUB_Q, :] = lse


def _dil_group(view, window, dil, rel_table):
    b, l, n_all = view.shape
    hd = N_HEADS * HEAD_DIM
    assert window // dil == DIL_BACK and n_all == dil * 3 * hd
    tq = min(1024, l)
    assert l % tq == 0 and tq >= DIL_BACK
    hpb = hd // LANES
    qi = jnp.arange(SUB_Q)[:, None]
    cc = jnp.arange(DIL_BACK + SUB_Q)[None, :]
    dd = qi + DIL_BACK - cc
    bt = jnp.where((dd[None] >= 0) & (dd[None] <= DIL_BACK), _rel_bias(rel_table, dd * dil), NEG)

    def spec(comp, prev):
        def index(bi, r, hp, t):
            return (bi, jnp.maximum(t - 1, 0) if prev else t, (r * 3 + comp) * hpb + hp)
        return pl.BlockSpec((1, tq, LANES), index)

    out_spec = pl.BlockSpec((1, tq, LANES), lambda bi, r, hp, t: (bi, t, r * hpb + hp))
    return pl.pallas_call(
        _dil_kernel,
        grid=(b, dil, hpb, l // tq),
        in_specs=[spec(0, False), spec(1, True), spec(1, False), spec(2, True), spec(2, False),
                  pl.BlockSpec((LANES // HEAD_DIM, SUB_Q, DIL_BACK + SUB_Q), lambda bi, r, hp, t: (hp, 0, 0))],
        out_specs=[out_spec, out_spec],
        out_shape=[jax.ShapeDtypeStruct((b, l, dil * hd), BF16), jax.ShapeDtypeStruct((b, l, dil * hd), F32)],
        compiler_params=_params("parallel", "parallel", "parallel", "parallel"),
        name=f"dilated_attention_{dil}",
    )(view, view, view, view, view, bt)


def _dil_merge_kernel(*refs, dils):
    n = len(dils)
    o_refs, l_refs, out_ref, scratch = refs[:n], refs[n:2 * n], refs[2 * n], refs[2 * n + 1:]
    tm, d = out_ref.shape[1], out_ref.shape[2]

    def token_major(ref, dil, buf):
        if dil == 1:
            return ref[0].astype(F32)
        for r in range(dil):
            for c in range(d // LANES):
                buf[c, pl.ds(r, tm // dil, stride=dil), :] = (
                    ref[0, :, r * d + c * LANES:r * d + (c + 1) * LANES].astype(F32))
        return jnp.concatenate([buf[c] for c in range(d // LANES)], axis=1)

    outs = [token_major(o_refs[g], dils[g], scratch[2 * g]) for g in range(n)]
    lses = [token_major(l_refs[g], dils[g], scratch[2 * g + 1]) for g in range(n)]
    m = functools.reduce(jnp.maximum, lses)
    es = [jnp.exp(l - m) for l in lses]
    inv = 1.0 / functools.reduce(jnp.add, es)
    out_ref[0] = functools.reduce(jnp.add, [e * inv * o for e, o in zip(es, outs)]).astype(out_ref.dtype)


def _dil_merge(outs, lses, dils, tm=512):
    b = outs[0].shape[0]
    d = N_HEADS * HEAD_DIM
    s = outs[0].shape[1] * dils[0]
    specs = [pl.BlockSpec((1, tm // dil, dil * d), lambda bi, i: (bi, i, 0)) for dil in dils]
    return pl.pallas_call(
        functools.partial(_dil_merge_kernel, dils=tuple(dils)),
        grid=(b, s // tm),
        in_specs=specs + specs,
        out_specs=pl.BlockSpec((1, tm, d), lambda bi, i: (bi, i, 0)),
        out_shape=jax.ShapeDtypeStruct((b, s, d), BF16),
        scratch_shapes=[pltpu.VMEM((d // LANES, tm, LANES), F32) for _ in range(2 * len(dils))],
        compiler_params=_params("parallel", "parallel"),
        name="dilated_merge",
    )(*outs, *lses)


def _dil_mixer(x, scale, shift, w_in, rel_table):
    hd = N_HEADS * HEAD_DIM
    w = w_in.astype(BF16)
    outs, lses, dils = [], [], []
    for gi, (window, dil) in enumerate(DIL_GROUPS):
        view = _linear_in(x, scale, shift, w[:, gi * 3 * hd:(gi + 1) * 3 * hd], tn=3 * hd, dil=dil)
        o, lse = _dil_group(view, window, dil, rel_table)
        outs.append(o)
        lses.append(lse)
        dils.append(dil)
    return _dil_merge(outs, lses, dils)


def kernel(x, c, rel_table, mod_w, mod_b, ln_g, ln_b, gla_w_in, gla_w_a2, gla_b_a, gla_norm_g, gla_w_o,
           nsa_w_in, nsa_cmp_pe, nsa_cmp_w1, nsa_cmp_b1, nsa_cmp_w2, nsa_w_o, dil_w_in, dil_w_o,
           ffn_w_up, ffn_conv_w, ffn_conv_b, ffn_w_down):
    shift, scale, gate = _modulation(c, mod_w, mod_b)
    for i in range(DEPTH):
        kind, j = i % N_MIXERS, i // N_MIXERS
        m = 2 * i
        if kind == 0:
            a = _gla_mixer(x, scale[m], shift[m], gla_w_in[j], gla_w_a2[j], gla_b_a[j], gla_norm_g[j])
            w_o = gla_w_o[j]
        elif kind == 1:
            a = _nsa_mixer(x, scale[m], shift[m], nsa_w_in[j], nsa_cmp_pe[j], nsa_cmp_w1[j], nsa_cmp_b1[j],
                           nsa_cmp_w2[j], rel_table)
            w_o = nsa_w_o[j]
        else:
            a = _dil_mixer(x, scale[m], shift[m], dil_w_in[j], rel_table)
            w_o = dil_w_o[j]
        x = _out_norm(a, w_o.astype(BF16), x, gate[m], ln_g[i, 0], ln_b[i, 0])
        a = _ffn_up(x, scale[m + 1], shift[m + 1], ffn_w_up[i].astype(BF16), ffn_conv_w[i], ffn_conv_b[i])
        x = _out_norm(a, ffn_w_down[i].astype(BF16), x, gate[m + 1], ln_g[i, 1], ln_b[i, 1])
    return x
```

```python
import functools
import math

import jax
import jax.numpy as jnp
import numpy as np
from jax import lax
from jax.experimental import pallas as pl
from jax.experimental.pallas import tpu as pltpu

F32 = jnp.float32
BF16 = jnp.bfloat16

D_MODEL = 1024
DEPTH = 4
N_MIXERS = 3
HEAD_DIM = 64
N_HEADS = D_MODEL // HEAD_DIM
REL_BUCKETS = 32
REL_MAX_DIST = 2048
LN_EPS = 1e-5
GLA_HEADS = 4
GLA_DK = D_MODEL // 2 // GLA_HEADS
GLA_DV = D_MODEL // GLA_HEADS
GLA_GATE_RANK = 16
GLA_TAU = 16.0
GLA_CHUNK = 64
NSA_G = 4
NSA_R = N_HEADS // NSA_G
CMP_LEN = 32
CMP_STRIDE = 16
CMP_HIDDEN = 256
SEL_LEN = 64
SEL_TOP = 16
WIN_LEN = 512
DIL_GROUPS = ((128, 1), (512, 4), (2048, 16))
D_FF = 2816
DN_ALPHA = (2 * DEPTH) ** 0.25

NEG = -1e30
VMEM_LIMIT = 56 * 1024 * 1024
LANES = 128
SUB_Q = 128


def _params(*sem):
    return pltpu.CompilerParams(dimension_semantics=sem, vmem_limit_bytes=VMEM_LIMIT)


def _dot(a, b):
    return jnp.dot(a, b, preferred_element_type=F32)


def _dot_nt(a, b):
    return lax.dot_general(a, b, (((1,), (1,)), ((), ())), preferred_element_type=F32)


def _dot_tn(a, b):
    return lax.dot_general(a, b, (((0,), (0,)), ((), ())), preferred_element_type=F32)


def _split(a):
    hi = a.astype(BF16)
    lo = (a - hi.astype(F32)).astype(BF16)
    return hi, lo


def _dot_hi_exact_rhs(a, b_bf16):
    hi, lo = _split(a)
    return _dot(hi, b_bf16) + _dot(lo, b_bf16)


def _silu(x):
    return x * jax.nn.sigmoid(x)


def _bucket_starts():
    n = np.arange(2 * REL_MAX_DIST)
    exact = REL_BUCKETS // 2
    logn = np.log(np.maximum(n, 1).astype(np.float32) / np.float32(exact))
    large = exact + (logn / np.float32(math.log(REL_MAX_DIST / exact)) * np.float32(REL_BUCKETS - exact)).astype(np.int32)
    bucket = np.where(n < exact, n, np.minimum(large, REL_BUCKETS - 1))
    assert np.all(np.diff(bucket) >= 0) and bucket[-1] == REL_BUCKETS - 1
    return [int(np.argmax(bucket >= k)) for k in range(REL_BUCKETS)]


BUCKET_START = _bucket_starts()
FAR_DIST = BUCKET_START[-1]


def _rel_bias(rel_table, dist):
    shape = (rel_table.shape[1],) + (1,) * dist.ndim
    out = jnp.broadcast_to(rel_table[0].reshape(shape), (rel_table.shape[1],) + dist.shape)
    for k in range(1, REL_BUCKETS):
        out = jnp.where(dist[None] >= BUCKET_START[k], rel_table[k].reshape(shape), out)
    return out


def _mod_kernel(c_ref, w_ref, b_ref, o_ref):
    cs = _silu(c_ref[...])
    ch, cl = _split(cs)
    wh, wl = _split(w_ref[0])
    o_ref[0] = _dot(ch, wh) + _dot(ch, wl) + _dot(cl, wh) + b_ref[0]


def _modulation(c, mod_w, mod_b):
    b, d = c.shape
    n_sub = mod_w.shape[0] * mod_w.shape[1]
    rows = 8
    cp = jnp.zeros((rows, d), F32).at[:b].set(c)
    w = mod_w.reshape(n_sub, d, 3 * d)
    bias = mod_b.reshape(n_sub, 1, 3 * d)
    tn = d
    out = pl.pallas_call(
        _mod_kernel,
        grid=(n_sub, 3 * d // tn),
        in_specs=[pl.BlockSpec((rows, d), lambda i, j: (0, 0)),
                  pl.BlockSpec((1, d, tn), lambda i, j: (i, 0, j)),
                  pl.BlockSpec((1, 1, tn), lambda i, j: (i, 0, j))],
        out_specs=pl.BlockSpec((1, rows, tn), lambda i, j: (i, 0, j)),
        out_shape=jax.ShapeDtypeStruct((n_sub, rows, 3 * d), F32),
        compiler_params=_params("parallel", "parallel"),
        name="modulation",
    )(cp, w, bias)
    out = out[:, :b]
    shift, scale, gate = jnp.split(out, 3, axis=-1)
    return shift[:, :, None, :], scale[:, :, None, :], gate[:, :, None, :]


def _linear_in_kernel(x_ref, sc_ref, sh_ref, w_ref, *rest, dil):
    h = (x_ref[0] * (1.0 + sc_ref[0]) + sh_ref[0]).astype(BF16)
    if dil == 1:
        o_ref, = rest
        o_ref[0] = _dot(h, w_ref[...]).astype(o_ref.dtype)
    else:
        perm_ref, o_ref = rest
        res = _dot(_dot(perm_ref[...], h).astype(BF16), w_ref[...])
        rows, n = res.shape[0] // dil, res.shape[1]
        for r in range(dil):
            o_ref[0, :, r * n:(r + 1) * n] = res[r * rows:(r + 1) * rows].astype(o_ref.dtype)


def _linear_in(x, scale, shift, w, tn, tm=512, dil=1):
    b, s, d = x.shape
    n = w.shape[1]
    assert n % tn == 0 and s % tm == 0 and tm % (16 * dil) == 0 and (dil == 1 or n == tn)
    in_specs = [pl.BlockSpec((1, tm, d), lambda j, bi, i: (bi, i, 0)),
                pl.BlockSpec((1, 1, d), lambda j, bi, i: (bi, 0, 0)),
                pl.BlockSpec((1, 1, d), lambda j, bi, i: (bi, 0, 0)),
                pl.BlockSpec((d, tn), lambda j, bi, i: (0, j))]
    args = [x, scale, shift, w]
    if dil > 1:
        out_row = jnp.arange(tm)
        src = (out_row % (tm // dil)) * dil + out_row // (tm // dil)
        in_specs.append(pl.BlockSpec((tm, tm), lambda j, bi, i: (0, 0)))
        args.append((src[:, None] == jnp.arange(tm)[None, :]).astype(BF16))
    return pl.pallas_call(
        functools.partial(_linear_in_kernel, dil=dil),
        grid=(n // tn, b, s // tm),
        in_specs=in_specs,
        out_specs=pl.BlockSpec((1, tm // dil, dil * tn), lambda j, bi, i: (bi, i, j)),
        out_shape=jax.ShapeDtypeStruct((b, s // dil, dil * n), BF16),
        compiler_params=_params("parallel", "parallel", "parallel"),
        name="linear_in",
    )(*args)


OUT_ROWS = 512


def _out_norm_kernel(a_ref, w_ref, x_ref, gate_ref, g_ref, b_ref, o_ref):
    chunks = [slice(r, r + OUT_ROWS) for r in range(0, a_ref.shape[1], OUT_ROWS)]
    ys = [_dot(a_ref[0, sl, :], w_ref[...]) for sl in chunks]
    for sl, y in zip(chunks, ys):
        z = DN_ALPHA * x_ref[0, sl, :] + (1.0 + gate_ref[0]) * y
        mu = jnp.mean(z, axis=-1, keepdims=True)
        zc = z - mu
        var = jnp.mean(zc * zc, axis=-1, keepdims=True)
        o_ref[0, sl, :] = zc * lax.rsqrt(var + LN_EPS) * g_ref[...] + b_ref[...]


def _out_norm(a, w, x, gate, ln_g, ln_b, tm=1024):
    b, s, d = x.shape
    k = a.shape[-1]
    return pl.pallas_call(
        _out_norm_kernel,
        grid=(b, s // tm),
        in_specs=[pl.BlockSpec((1, tm, k), lambda bi, i: (bi, i, 0)),
                  pl.BlockSpec((k, d), lambda bi, i: (0, 0)),
                  pl.BlockSpec((1, tm, d), lambda bi, i: (bi, i, 0)),
                  pl.BlockSpec((1, 1, d), lambda bi, i: (bi, 0, 0)),
                  pl.BlockSpec((1, d), lambda bi, i: (0, 0)),
                  pl.BlockSpec((1, d), lambda bi, i: (0, 0))],
        out_specs=pl.BlockSpec((1, tm, d), lambda bi, i: (bi, i, 0)),
        out_shape=jax.ShapeDtypeStruct((b, s, d), F32),
        compiler_params=_params("parallel", "parallel"),
        name="out_norm",
    )(a, w, x, gate, ln_g.reshape(1, d), ln_b.reshape(1, d))


FFN_HALO = 16
FFN_COLS = 256


def _ffn_up_kernel(x_ref, xh_ref, sc_ref, sh_ref, wu_ref, wg_ref, cw_ref, cb_ref, o_ref, h_ref):
    i = pl.program_id(1)
    sc = 1.0 + sc_ref[0]
    sh = sh_ref[0]
    h_ref[0:FFN_HALO, :] = (xh_ref[0] * sc + sh).astype(BF16)
    h_ref[FFN_HALO:, :] = (x_ref[0] * sc + sh).astype(BF16)
    keep = (i > 0).astype(F32)
    cw = cw_ref[...]
    for c in range(0, o_ref.shape[2], FFN_COLS):
        cols = slice(c, c + FFN_COLS)
        u = _dot(h_ref[...], wu_ref[:, cols])
        u = jnp.concatenate([u[0:FFN_HALO] * keep, u[FFN_HALO:]], axis=0)
        g = _dot(h_ref[FFN_HALO:, :], wg_ref[:, cols])
        a = u * cw[1:2, cols] + pltpu.roll(u, 1, 0) * cw[0:1, cols]
        conv = (u * cw[2:3, cols] + pltpu.roll(a, 1, 0))[FFN_HALO:] + cb_ref[:, cols]
        o_ref[0, :, cols] = (_silu(conv) * g).astype(o_ref.dtype)


def _ffn_up(x, scale, shift, w_up, conv_w, conv_b, tm=1024):
    b, s, d = x.shape
    f = conv_w.shape[1]
    assert s % tm == 0 and tm % FFN_HALO == 0 and f % FFN_COLS == 0
    hb = tm // FFN_HALO
    once = pl.Buffered(1)
    return pl.pallas_call(
        _ffn_up_kernel,
        grid=(b, s // tm),
        in_specs=[pl.BlockSpec((1, tm, d), lambda bi, i: (bi, i, 0)),
                  pl.BlockSpec((1, FFN_HALO, d), lambda bi, i: (bi, jnp.maximum(i * hb - 1, 0), 0)),
                  pl.BlockSpec((1, 1, d), lambda bi, i: (bi, 0, 0)),
                  pl.BlockSpec((1, 1, d), lambda bi, i: (bi, 0, 0)),
                  pl.BlockSpec((d, f), lambda bi, i: (0, 0), pipeline_mode=once),
                  pl.BlockSpec((d, f), lambda bi, i: (0, 1), pipeline_mode=once),
                  pl.BlockSpec((3, f), lambda bi, i: (0, 0)),
                  pl.BlockSpec((1, f), lambda bi, i: (0, 0))],
        out_specs=pl.BlockSpec((1, tm, f), lambda bi, i: (bi, i, 0)),
        out_shape=jax.ShapeDtypeStruct((b, s, f), BF16),
        scratch_shapes=[pltpu.VMEM((tm + FFN_HALO, d), BF16)],
        compiler_params=_params("parallel", "parallel"),
        name="ffn_up",
    )(x, x, scale, shift, w_up, w_up, conv_w, conv_b.reshape(1, f))


def _gla_kernel(q_ref, k_ref, v_ref, r_ref, a_ref, wa_ref, ba_ref, ng_ref, o_ref, st_ref):
    tc = q_ref.shape[1]
    nchunk = tc // GLA_CHUNK

    @pl.when(pl.program_id(2) == 0)
    def _():
        st_ref[...] = jnp.zeros_like(st_ref)

    z = _dot(a_ref[0], wa_ref[...]) + ba_ref[...]
    log_a = (jnp.minimum(z, 0.0) - jnp.log1p(jnp.exp(-jnp.abs(z)))) / GLA_TAU
    ci = lax.broadcasted_iota(jnp.int32, (GLA_CHUNK, GLA_CHUNK), 0)
    cj = lax.broadcasted_iota(jnp.int32, (GLA_CHUNK, GLA_CHUNK), 1)
    causal = cj <= ci
    tri = jnp.where(causal, 1.0, 0.0).astype(BF16)
    chunks = [slice(c * GLA_CHUNK, (c + 1) * GLA_CHUNK) for c in range(nchunk)]
    la_hi, la_lo = _split(log_a)
    cums = [_dot(tri, la_hi[sl]) + _dot(tri, la_lo[sl]) for sl in chunks]
    bcum = jnp.concatenate(cums, axis=0)
    btot = jnp.concatenate([jnp.broadcast_to(cm[GLA_CHUNK - 1:], cm.shape) for cm in cums], axis=0)
    qf = q_ref[0].astype(F32) * (GLA_DK ** -0.5)
    kf = k_ref[0].astype(F32)
    q_dec = (qf * jnp.exp(bcum)).astype(BF16)
    k_inv = (kf * jnp.exp(-bcum)).astype(BF16)
    k_dec = (kf * jnp.exp(btot - bcum)).astype(BF16)
    dl = jnp.exp(btot)
    v = v_ref[0]
    att = [jnp.where(causal, _dot_nt(q_dec[sl], k_inv[sl]), 0.0).astype(BF16) for sl in chunks]
    kv = [_dot_tn(v[sl], k_dec[sl]) for sl in chunks]
    o_intra = [_dot(att[c], v[chunks[c]]) for c in range(nchunk)]
    st = st_ref[...]
    o_inter = []
    for c, sl in enumerate(chunks):
        o_inter.append(_dot_nt(q_dec[sl], st.astype(BF16)))
        st = st * dl[c * GLA_CHUNK:c * GLA_CHUNK + 1] + kv[c]
    st_ref[...] = st
    o = jnp.concatenate(o_intra, axis=0) + jnp.concatenate(o_inter, axis=0)
    o = o * lax.rsqrt(jnp.mean(o * o, axis=-1, keepdims=True) + LN_EPS) * ng_ref[...]
    o_ref[0] = (o * _silu(r_ref[0].astype(F32))).astype(o_ref.dtype)


def _gla_core(proj, w_a2, b_a, norm_g, tc=512):
    b, s, _ = proj.shape
    h, dk, dv = GLA_HEADS, GLA_DK, GLA_DV
    assert s % tc == 0
    wa = jnp.zeros((LANES, h * dk), F32).at[:GLA_GATE_RANK].set(w_a2).astype(BF16)
    kb, vb, rb, ab = (h * dk) // dk, (2 * h * dk) // dv, (2 * h * dk + h * dv) // dv, (2 * h * dk + 2 * h * dv) // LANES
    return pl.pallas_call(
        _gla_kernel,
        grid=(b, h, s // tc),
        in_specs=[pl.BlockSpec((1, tc, dk), lambda bi, hi, t: (bi, t, hi)),
                  pl.BlockSpec((1, tc, dk), lambda bi, hi, t: (bi, t, kb + hi)),
                  pl.BlockSpec((1, tc, dv), lambda bi, hi, t: (bi, t, vb + hi)),
                  pl.BlockSpec((1, tc, dv), lambda bi, hi, t: (bi, t, rb + hi)),
                  pl.BlockSpec((1, tc, LANES), lambda bi, hi, t: (bi, t, ab)),
                  pl.BlockSpec((LANES, dk), lambda bi, hi, t: (0, hi)),
                  pl.BlockSpec((1, dk), lambda bi, hi, t: (0, hi)),
                  pl.BlockSpec((1, dv), lambda bi, hi, t: (0, 0))],
        out_specs=pl.BlockSpec((1, tc, dv), lambda bi, hi, t: (bi, t, hi)),
        out_shape=jax.ShapeDtypeStruct((b, s, h * dv), BF16),
        scratch_shapes=[pltpu.VMEM((dv, dk), F32)],
        compiler_params=_params("parallel", "parallel", "arbitrary"),
        name="gla_core",
    )(proj, proj, proj, proj, proj, wa, b_a.reshape(1, h * dk), norm_g.reshape(1, dv))


def _pad_cols(w, n):
    return jnp.pad(w, ((0, 0), (0, n - w.shape[1])))


def _gla_mixer(x, scale, shift, w_in, w_a2, b_a, norm_g):
    n_in = 2 * GLA_HEADS * GLA_DK + 2 * GLA_HEADS * GLA_DV + LANES
    proj = _linear_in(x, scale, shift, _pad_cols(w_in, n_in).astype(BF16), tn=n_in)
    return _gla_core(proj, w_a2, b_a, norm_g)


def _compress_kernel(lo_ref, hi_ref, pe_ref, w1_ref, b1_ref, w2_ref, o_ref):
    half = lo_ref.shape[-1]
    pe = pe_ref[0]
    a_lo = (lo_ref[0, 0, 0].astype(F32) + pe[:, :half]).astype(BF16)
    a_hi = (hi_ref[0, 0, 0].astype(F32) + pe[:, half:]).astype(BF16)
    h1 = _dot(a_lo, w1_ref[0, :half, :]) + _dot(a_hi, w1_ref[0, half:, :]) + b1_ref[0]
    o_ref[0, 0, 0] = _dot(_silu(h1).astype(BF16), w2_ref[0])


def _compress(kv, pe, w1, b1, w2):
    two, b, s, _ = kv.shape
    g, dh = NSA_G, HEAD_DIM
    npc = s // CMP_STRIDE
    half = CMP_STRIDE * dh
    a = kv.reshape(two, b, npc, CMP_STRIDE, g, dh).transpose(0, 1, 4, 2, 3, 5).reshape(two, b, g, npc, half)
    a_hi = jnp.concatenate([a[:, :, :, 1:], jnp.zeros_like(a[:, :, :, :1])], axis=3)
    spec_a = pl.BlockSpec((1, 1, 1, npc, half), lambda c, bi, gi: (c, bi, gi, 0, 0))
    return pl.pallas_call(
        _compress_kernel,
        grid=(two, b, g),
        in_specs=[spec_a, spec_a,
                  pl.BlockSpec((1, 1, 2 * half), lambda c, bi, gi: (c, 0, 0)),
                  pl.BlockSpec((1, 2 * half, CMP_HIDDEN), lambda c, bi, gi: (c, 0, 0)),
                  pl.BlockSpec((1, 1, CMP_HIDDEN), lambda c, bi, gi: (c, 0, 0)),
                  pl.BlockSpec((1, CMP_HIDDEN, dh), lambda c, bi, gi: (c, 0, 0))],
        out_specs=pl.BlockSpec((1, 1, 1, npc, dh), lambda c, bi, gi: (c, bi, gi, 0, 0)),
        out_shape=jax.ShapeDtypeStruct((two, b, g, npc, dh), F32),
        compiler_params=_params("parallel", "parallel", "parallel"),
        name="nsa_compress",
    )(a, a_hi, pe.reshape(two, 1, 2 * half), w1.astype(BF16), b1.reshape(two, 1, CMP_HIDDEN), w2.astype(BF16))


def _stack_heads(q):
    return jnp.concatenate([q[:, r * HEAD_DIM:(r + 1) * HEAD_DIM] for r in range(NSA_R)], axis=0)


def _unstack_heads(o, rows):
    return jnp.concatenate([o[r * rows:(r + 1) * rows] for r in range(NSA_R)], axis=1)


CMP_WIDTHS = 4
CMP_TQ = 512
LOG2E = 1.4426950408889634


def _nsa_cmp_kernel(q_ref, fq_ref, kc_ref, vc_ref, nb_ref, pb_ref, o_ref, mb_ref, imp_ref, *, n_tile):
    i = pl.program_id(2)
    npc = nb_ref.shape[2]
    args = (q_ref, fq_ref, kc_ref, vc_ref, nb_ref, pb_ref, o_ref, imp_ref)
    quarter = n_tile // CMP_WIDTHS
    widths = [(k + 1) * npc // CMP_WIDTHS for k in range(CMP_WIDTHS)]

    @pl.when(i == 0)
    def _():
        _cmp_attend(i, widths[0], *args)

    for k in range(CMP_WIDTHS):
        @pl.when((i > 0) & (i >= k * quarter) & (i < (k + 1) * quarter))
        def _(k=k):
            imp = imp_ref[...]
            _cmp_attend(i, widths[k], *args)
            _cmp_select(i - 1, imp, mb_ref)

    @pl.when(i == n_tile)
    def _():
        _cmp_select(i - 1, imp_ref[...], mb_ref)


def _cmp_attend(i, width, q_ref, fq_ref, kc_ref, vc_ref, nb_ref, pb_ref, o_ref, imp_ref):
    skip = nb_ref.shape[2] - width
    n_sub, rows = CMP_TQ // SUB_Q, NSA_R * SUB_Q
    starts = [pl.multiple_of((i * n_sub + j) * (SUB_Q // CMP_STRIDE), 8) for j in range(n_sub)]

    scores = []
    for j in range(n_sub):
        kwin = kc_ref[0, 0, pl.ds(starts[j] + skip, width), :].astype(BF16)
        qs = _stack_heads(q_ref[0, j * SUB_Q:(j + 1) * SUB_Q, :]).astype(F32) * (HEAD_DIM ** -0.5 * LOG2E)
        qp = jnp.concatenate([qs, fq_ref[...]], axis=1).astype(BF16)
        scores.append(_dot_nt(qp, kwin) + nb_ref[0, :, skip:])
    s = jnp.concatenate(scores, axis=0)
    m = jnp.max(s, axis=1, keepdims=True)
    e = jnp.exp2(s - m)
    den = jnp.sum(e, axis=1, keepdims=True)
    p = e * jnp.where(m > 0.5 * NEG, 1.0 / den, 0.0)
    p_bf = p.astype(BF16)
    for j in range(n_sub):
        vwin = vc_ref[0, 0, pl.ds(starts[j] + skip, width), :].astype(BF16)
        o = _dot(p_bf[j * rows:(j + 1) * rows], vwin)
        o_ref[0, j * SUB_Q:(j + 1) * SUB_Q, :] = _unstack_heads(o, SUB_Q).astype(o_ref.dtype)

    imps = []
    for j in range(n_sub):
        pg = p[j * rows:j * rows + SUB_Q]
        for r in range(1, NSA_R):
            pg = pg + p[j * rows + r * SUB_Q:j * rows + (r + 1) * SUB_Q]
        imps.append(_dot_hi_exact_rhs(pg, pb_ref[pl.ds(starts[j] + skip, width), :].astype(BF16)))
    imp_ref[...] = jnp.concatenate(imps, axis=0)


def _cmp_select(i, imp, mb_ref):
    nblk = mb_ref.shape[3]
    blk = lax.broadcasted_iota(jnp.int32, (CMP_TQ, nblk), 1)
    blk_f = blk.astype(F32)
    t = i * CMP_TQ + lax.broadcasted_iota(jnp.int32, (CMP_TQ, nblk), 0)
    cur = t // SEL_LEN
    forced = (blk == 0) | (blk == cur) | (blk == cur - 1)
    work = jnp.where((blk * SEL_LEN <= t) & jnp.logical_not(forced), imp, NEG)
    free = work > 0.5 * NEG
    for _ in range(SEL_TOP - 3):
        mx = jnp.max(work, axis=1, keepdims=True)
        first = jnp.min(jnp.where(work == mx, blk_f, 1e9), axis=1, keepdims=True)
        work = jnp.where(blk_f == first, 2.0 * NEG, work)
    sel = forced | (free & (work < 1.5 * NEG))
    mb_ref[0, 0] = jnp.where(sel, 0.0, NEG).astype(mb_ref.dtype)


def _nsa_cmp(proj, kcmp, vcmp, rel_table):
    b, s, _ = proj.shape
    g, dh = NSA_G, HEAD_DIM
    npc = s // CMP_STRIDE
    nblk = s // SEL_LEN
    front = npc - SUB_Q // CMP_STRIDE
    assert s % CMP_TQ == 0 and nblk >= SEL_TOP
    n_tile = s // CMP_TQ
    assert n_tile % CMP_WIDTHS == 0
    kpad = jnp.pad(kcmp, ((0, 0), (0, 0), (front, npc - front), (0, LANES - dh)))
    kpad = kpad.at[:, :, :front, dh].set(1.0)
    vpad = jnp.pad(vcmp, ((0, 0), (0, 0), (front, npc - front), (0, 0)))
    fq = jnp.zeros((NSA_R * SUB_Q, LANES - dh), F32).at[:, 0].set(NEG)
    qi = jnp.arange(SUB_Q)[:, None]
    cc = jnp.arange(npc)[None, :]
    dd = qi - (CMP_LEN - 1) + CMP_STRIDE * (front - cc)
    nb = jnp.where(dd[None] >= 0, _rel_bias(rel_table, dd) * LOG2E, NEG).reshape(g, NSA_R * SUB_Q, npc)
    diff = jnp.arange(2 * npc)[:, None] - front - (SEL_LEN // CMP_STRIDE) * jnp.arange(nblk)[None, :]
    pool = jnp.where((diff == -1) | (diff == 3), 1.0, jnp.where((diff >= 0) & (diff <= 2), 2.0, 0.0))
    return pl.pallas_call(
        functools.partial(_nsa_cmp_kernel, n_tile=n_tile),
        grid=(g, b, n_tile + 1),
        in_specs=[pl.BlockSpec((1, CMP_TQ, NSA_R * dh), lambda gi, bi, i: (bi, jnp.minimum(i, n_tile - 1), gi)),
                  pl.BlockSpec((NSA_R * SUB_Q, LANES - dh), lambda gi, bi, i: (0, 0)),
                  pl.BlockSpec((1, 1, 2 * npc, LANES), lambda gi, bi, i: (bi, gi, 0, 0)),
                  pl.BlockSpec((1, 1, 2 * npc, dh), lambda gi, bi, i: (bi, gi, 0, 0)),
                  pl.BlockSpec((1, NSA_R * SUB_Q, npc), lambda gi, bi, i: (gi, 0, 0)),
                  pl.BlockSpec((2 * npc, nblk), lambda gi, bi, i: (0, 0))],
        out_specs=[pl.BlockSpec((1, CMP_TQ, NSA_R * dh), lambda gi, bi, i: (bi, jnp.minimum(i, n_tile - 1), gi)),
                   pl.BlockSpec((1, 1, CMP_TQ, nblk), lambda gi, bi, i: (bi, gi, jnp.maximum(i - 1, 0), 0))],
        out_shape=[jax.ShapeDtypeStruct((b, s, N_HEADS * dh), BF16),
                   jax.ShapeDtypeStruct((b, g, s, nblk), BF16)],
        scratch_shapes=[pltpu.VMEM((CMP_TQ, nblk), F32)],
        compiler_params=_params("parallel", "parallel", "arbitrary"),
        name="nsa_cmp_select",
    )(proj, fq, kpad, vpad, nb, pool.astype(F32))


SEL_T = 256
SEL_WIDE = 8
SEL_NEAR = -(-(FAR_DIST + SEL_T - 1) // SEL_T)


def _nsa_sel_kernel(q_ref, fq_ref, mb_ref, k_ref, v_ref, bt_ref, o_ref, qp_ref, m_ref, acc_ref):
    i = pl.program_id(2)
    rows = NSA_R * SEL_T
    fw = (k_ref.shape[3] - LANES)
    kh = fw // (SEL_T // SEL_LEN)
    qs = (_stack_heads(q_ref[0]).astype(F32) * (HEAD_DIM ** -0.5 * LOG2E)).astype(BF16)
    mb = mb_ref[0, 0]
    for half in range(2):
        mbh = jnp.concatenate([mb[:, half * fw:(half + 1) * fw]] * NSA_R, axis=0)
        qp_ref[half] = jnp.concatenate([qs, fq_ref[0], mbh], axis=1)
    m_ref[...] = jnp.full_like(m_ref, NEG)
    acc_ref[...] = jnp.zeros_like(acc_ref)

    def update(kt, n_tile, half, bias):
        k0 = pl.multiple_of(kt * SEL_T, SEL_T)
        kk = k_ref[0, 0, pl.ds(k0, n_tile * SEL_T), :]
        vv = v_ref[0, 0, pl.ds(k0, n_tile * SEL_T), :]
        s = _dot_nt(qp_ref[half], kk)
        if bias is not None:
            s = s + bias
        m_prev = m_ref[...]
        m_new = jnp.maximum(m_prev, jnp.max(s, axis=1, keepdims=True))
        alpha = jnp.exp2(m_prev - m_new)
        p = jnp.exp2(s - jnp.concatenate([m_new] * (n_tile * SEL_T // LANES), axis=1))
        acc_ref[...] = acc_ref[...] * alpha + _dot(p.astype(BF16), vv)
        m_ref[...] = m_new

    def far_region(lo, hi, half):
        n = jnp.maximum(hi - lo, 0)

        def wide(j, carry):
            update(lo + SEL_WIDE * j, SEL_WIDE, half, None)
            return carry

        lax.fori_loop(0, n // SEL_WIDE, wide, 0)
        piece = SEL_WIDE // 2
        while piece >= 1:
            @pl.when((n & piece) != 0)
            def _(piece=piece):
                update(lo + (n & ~(2 * piece - 1)), piece, half, None)
            piece //= 2

    n_far = jnp.maximum(i - SEL_NEAR + 1, 0)
    far_region(0, jnp.minimum(n_far, kh), 0)
    far_region(kh, n_far, 1)

    def near(e):
        kt = i - e
        update(kt, 1, jnp.where(kt < kh, 0, 1), bt_ref[0, :, e * SEL_T:(e + 1) * SEL_T])

    @pl.when(i >= SEL_NEAR - 1)
    def _():
        for e in range(SEL_NEAR - 1, -1, -1):
            near(e)

    @pl.when(i < SEL_NEAR - 1)
    def _():
        for e in range(SEL_NEAR - 2, -1, -1):
            @pl.when(i - e >= 0)
            def _(e=e):
                near(e)

    acc = acc_ref[...]
    o = acc[:, :HEAD_DIM] / acc[:, HEAD_DIM:HEAD_DIM + 1]
    o_ref[0] = _unstack_heads(o, SEL_T).astype(o_ref.dtype)


def _nsa_sel(proj, ks, vs, mb, rel_table):
    b, s, _ = proj.shape
    g, dh = NSA_G, HEAD_DIM
    nblk = s // SEL_LEN
    fw = nblk // 2
    rows = NSA_R * SEL_T
    pos = jnp.arange(s)
    onehot = (((pos // SEL_LEN) % fw)[:, None] == jnp.arange(fw)[None, :]).astype(BF16)
    ones2 = jnp.zeros((s, LANES - dh), BF16).at[:, :2].set(1.0)
    kp = jnp.concatenate([ks, jnp.broadcast_to(ones2, (b, g, s, LANES - dh)),
                          jnp.broadcast_to(onehot, (b, g, s, fw))], axis=-1)
    ones1 = jnp.zeros((s, LANES - dh), BF16).at[:, :1].set(1.0)
    vp = jnp.concatenate([vs, jnp.broadcast_to(ones1, (b, g, s, LANES - dh))], axis=-1)
    b_far = rel_table[REL_BUCKETS - 1].reshape(g, NSA_R) * LOG2E
    hi = b_far.astype(BF16)
    lo = (b_far - hi.astype(F32)).astype(BF16)
    fq = jnp.zeros((g, NSA_R, SEL_T, LANES - dh), BF16)
    fq = fq.at[:, :, :, 0].set(hi[:, :, None]).at[:, :, :, 1].set(lo[:, :, None]).reshape(g, rows, LANES - dh)
    qi = jnp.arange(SEL_T)[:, None]
    cc = jnp.arange(SEL_NEAR * SEL_T)[None, :]
    dd = (cc // SEL_T) * SEL_T + qi - (cc % SEL_T)
    bias = _rel_bias(rel_table, dd).reshape(g, NSA_R, SEL_T, -1) * LOG2E - b_far[:, :, None, None]
    bt = jnp.where(dd[None, None] >= 0, bias, NEG).reshape(g, rows, SEL_NEAR * SEL_T)
    return pl.pallas_call(
        _nsa_sel_kernel,
        grid=(g, b, s // SEL_T),
        in_specs=[pl.BlockSpec((1, SEL_T, NSA_R * dh), lambda gi, bi, i: (bi, i, gi)),
                  pl.BlockSpec((1, rows, LANES - dh), lambda gi, bi, i: (gi, 0, 0)),
                  pl.BlockSpec((1, 1, SEL_T, nblk), lambda gi, bi, i: (bi, gi, i, 0)),
                  pl.BlockSpec((1, 1, s, LANES + fw), lambda gi, bi, i: (bi, gi, 0, 0)),
                  pl.BlockSpec((1, 1, s, LANES), lambda gi, bi, i: (bi, gi, 0, 0)),
                  pl.BlockSpec((1, rows, SEL_NEAR * SEL_T), lambda gi, bi, i: (gi, 0, 0),
                               pipeline_mode=pl.Buffered(1))],
        out_specs=pl.BlockSpec((1, SEL_T, NSA_R * dh), lambda gi, bi, i: (bi, i, gi)),
        out_shape=jax.ShapeDtypeStruct((b, s, N_HEADS * dh), BF16),
        scratch_shapes=[pltpu.VMEM((2, rows, LANES + fw), BF16), pltpu.VMEM((rows, LANES), F32),
                        pltpu.VMEM((rows, LANES), F32)],
        compiler_params=_params("parallel", "parallel", "parallel"),
        name="nsa_selected",
    )(proj, fq, mb, kp, vp, bt)


WIN_T = 512


def _softmax_rows(s):
    m = jnp.max(s, axis=1, keepdims=True)
    e = jnp.exp(s - m)
    den = jnp.sum(e, axis=1, keepdims=True)
    return e / den, m + jnp.log(den)


def _nsa_win_kernel(q_ref, kp_ref, kc_ref, vp_ref, vc_ref, bt_ref, o_ref):
    i = pl.program_id(2)
    back = WIN_T
    span = back + SUB_Q
    kcat = jnp.concatenate([kp_ref[0, 0], kc_ref[0, 0]], axis=0)
    vcat = jnp.concatenate([vp_ref[0, 0], vc_ref[0, 0]], axis=0)
    col = lax.broadcasted_iota(jnp.int32, (NSA_R * SUB_Q, span), 1)
    n_sub, rows = WIN_T // SUB_Q, NSA_R * SUB_Q
    scores = []
    for j in range(n_sub):
        qs = _stack_heads(q_ref[0, j * SUB_Q:(j + 1) * SUB_Q, :]) * (HEAD_DIM ** -0.5)
        s = _dot_nt(qs, kcat[j * SUB_Q:j * SUB_Q + span]) + bt_ref[0]
        scores.append(jnp.where((col >= back - j * SUB_Q) | (i > 0), s, NEG))
    p_all, _ = _softmax_rows(jnp.concatenate(scores, axis=0))
    p_all = p_all.astype(BF16)
    for j in range(n_sub):
        o = _dot(p_all[j * rows:(j + 1) * rows], vcat[j * SUB_Q:j * SUB_Q + span])
        o_ref[0, j * SUB_Q:(j + 1) * SUB_Q, :] = _unstack_heads(o, SUB_Q).astype(o_ref.dtype)


def _nsa_win(proj, kw, vw, rel_table):
    b, s, _ = proj.shape
    g, dh = NSA_G, HEAD_DIM
    span = WIN_T + SUB_Q
    qi = jnp.arange(SUB_Q)[:, None]
    cc = jnp.arange(span)[None, :]
    dd = qi + WIN_T - cc
    bias = _rel_bias(rel_table, dd)
    bt = jnp.where((dd[None] >= 0) & (dd[None] <= WIN_LEN - 1), bias, NEG).reshape(g, NSA_R * SUB_Q, span)
    prev = lambda gi, bi, i: (bi, gi, jnp.maximum(i - 1, 0), 0)
    cur = lambda gi, bi, i: (bi, gi, i, 0)
    return pl.pallas_call(
        _nsa_win_kernel,
        grid=(g, b, s // WIN_T),
        in_specs=[pl.BlockSpec((1, WIN_T, NSA_R * dh), lambda gi, bi, i: (bi, i, gi)),
                  pl.BlockSpec((1, 1, WIN_T, dh), prev), pl.BlockSpec((1, 1, WIN_T, dh), cur),
                  pl.BlockSpec((1, 1, WIN_T, dh), prev), pl.BlockSpec((1, 1, WIN_T, dh), cur),
                  pl.BlockSpec((1, NSA_R * SUB_Q, span), lambda gi, bi, i: (gi, 0, 0))],
        out_specs=pl.BlockSpec((1, WIN_T, NSA_R * dh), lambda gi, bi, i: (bi, i, gi)),
        out_shape=jax.ShapeDtypeStruct((b, s, N_HEADS * dh), BF16),
        compiler_params=_params("parallel", "parallel", "parallel"),
        name="nsa_window",
    )(proj, kw, kw, vw, vw, bt)


def _nsa_combine_kernel(oc_ref, os_ref, ow_ref, g_ref, e_ref, o_ref):
    d = o_ref.shape[2]
    sg = jax.nn.sigmoid(g_ref[0].astype(F32))
    ge = _dot_hi_exact_rhs(sg, e_ref[...])
    o = (ge[:, :d] * oc_ref[0].astype(F32) + ge[:, d:2 * d] * os_ref[0].astype(F32)
         + ge[:, 2 * d:] * ow_ref[0].astype(F32))
    o_ref[0] = o.astype(o_ref.dtype)


def _nsa_combine(o_cmp, o_sel, o_win, proj, gate_col, tm=512):
    b, s, d = o_cmp.shape
    c = jnp.arange(LANES)[:, None]
    n = jnp.arange(3 * d)[None, :]
    expand = (c == (n // d) * N_HEADS + (n % d) // HEAD_DIM).astype(BF16)
    spec = pl.BlockSpec((1, tm, d), lambda bi, i: (bi, i, 0))
    return pl.pallas_call(
        _nsa_combine_kernel,
        grid=(b, s // tm),
        in_specs=[spec, spec, spec,
                  pl.BlockSpec((1, tm, LANES), lambda bi, i: (bi, i, gate_col // LANES)),
                  pl.BlockSpec((LANES, 3 * d), lambda bi, i: (0, 0))],
        out_specs=spec,
        out_shape=jax.ShapeDtypeStruct((b, s, d), BF16),
        compiler_params=_params("parallel", "parallel"),
        name="nsa_combine",
    )(o_cmp, o_sel, o_win, proj, expand)


def _nsa_mixer(x, scale, shift, w_in, pe, w1, b1, w2, rel_table):
    b, s, _ = x.shape
    hd, gd = N_HEADS * HEAD_DIM, NSA_G * HEAD_DIM
    gate_col = hd + 6 * gd
    n_in = gate_col + LANES
    proj = _linear_in(x, scale, shift, _pad_cols(w_in, n_in).astype(BF16), tn=n_in)

    def group_major(t):
        return t.reshape(b, s, NSA_G, HEAD_DIM).transpose(0, 2, 1, 3)

    kc, vc, ks, vs, kw, vw = (proj[:, :, hd + n * gd: hd + (n + 1) * gd] for n in range(6))
    cmp = _compress(jnp.stack([kc, vc]), pe.reshape(2, -1), w1, b1, w2)
    o_cmp, mb = _nsa_cmp(proj, cmp[0], cmp[1], rel_table)
    o_sel = _nsa_sel(proj, group_major(ks), group_major(vs), mb, rel_table)
    o_win = _nsa_win(proj, group_major(kw), group_major(vw), rel_table)
    return _nsa_combine(o_cmp, o_sel, o_win, proj, gate_col)


DIL_BACK = 128
DIL_W = 256


def _dil_kernel(q_ref, kp_ref, kc_ref, vp_ref, vc_ref, bt_ref, o_ref, l_ref):
    t = pl.program_id(3)
    tq = q_ref.shape[1]
    span = DIL_BACK + SUB_Q
    kcat = jnp.concatenate([kp_ref[0, tq - DIL_BACK:, :], kc_ref[0]], axis=0)
    vcat = jnp.concatenate([vp_ref[0, tq - DIL_BACK:, :], vc_ref[0]], axis=0)
    lane_q = lax.broadcasted_iota(jnp.int32, (SUB_Q, DIL_W), 1) // HEAD_DIM
    lane_k = lax.broadcasted_iota(jnp.int32, (span, DIL_W), 1) // HEAD_DIM
    col = lax.broadcasted_iota(jnp.int32, (SUB_Q, span), 1)
    n_sub, n_head = tq // SUB_Q, DIL_W // HEAD_DIM
    scores = []
    for j in range(n_sub):
        qj = q_ref[0, j * SUB_Q:(j + 1) * SUB_Q, :] * (HEAD_DIM ** -0.5)
        kk = kcat[j * SUB_Q:j * SUB_Q + span]
        for hh in range(n_head):
            s = _dot_nt(jnp.where(lane_q == hh, qj, 0.0).astype(BF16), kk) + bt_ref[hh]
            if j == 0:
                s = jnp.where((col >= DIL_BACK) | (t > 0), s, NEG)
            scores.append(s)
    p_all, lse_all = _softmax_rows(jnp.concatenate(scores, axis=0))
    p_all = p_all.astype(BF16)
    for j in range(n_sub):
        vv = vcat[j * SUB_Q:j * SUB_Q + span]
        o = jnp.zeros((SUB_Q, DIL_W), F32)
        lse = jnp.zeros((SUB_Q, DIL_W), F32)
        for hh in range(n_head):
            r0 = (j * n_head + hh) * SUB_Q
            o = o + _dot(p_all[r0:r0 + SUB_Q], jnp.where(lane_k == hh, vv, 0.0).astype(BF16))
            lse = jnp.where(lane_q == hh, lse_all[r0:r0 + SUB_Q], lse)
        o_ref[0, j * SUB_Q:(j + 1) * SUB_Q, :] = o.astype(o_ref.dtype)
        l_ref[0, j * SUB_Q:(j + 1) * SUB_Q, :] = lse


def _dil_group(view, window, dil, rel_table):
    b, l, n_all = view.shape
    hd = N_HEADS * HEAD_DIM
    assert window // dil == DIL_BACK and n_all == dil * 3 * hd
    tq = min(1024, l)
    assert l % tq == 0 and tq >= DIL_BACK
    hpb = hd // DIL_W
    qi = jnp.arange(SUB_Q)[:, None]
    cc = jnp.arange(DIL_BACK + SUB_Q)[None, :]
    dd = qi + DIL_BACK - cc
    bt = jnp.where((dd[None] >= 0) & (dd[None] <= DIL_BACK), _rel_bias(rel_table, dd * dil), NEG)

    def spec(comp, prev):
        def index(bi, r, hp, t):
            return (bi, jnp.maximum(t - 1, 0) if prev else t, (r * 3 + comp) * hpb + hp)
        return pl.BlockSpec((1, tq, DIL_W), index)

    out_spec = pl.BlockSpec((1, tq, DIL_W), lambda bi, r, hp, t: (bi, t, r * hpb + hp))
    return pl.pallas_call(
        _dil_kernel,
        grid=(b, dil, hpb, l // tq),
        in_specs=[spec(0, False), spec(1, True), spec(1, False), spec(2, True), spec(2, False),
                  pl.BlockSpec((DIL_W // HEAD_DIM, SUB_Q, DIL_BACK + SUB_Q), lambda bi, r, hp, t: (hp, 0, 0))],
        out_specs=[out_spec, out_spec],
        out_shape=[jax.ShapeDtypeStruct((b, l, dil * hd), BF16), jax.ShapeDtypeStruct((b, l, dil * hd), F32)],
        compiler_params=_params("parallel", "parallel", "parallel", "parallel"),
        name=f"dilated_attention_{dil}",
    )(view, view, view, view, view, bt)


def _dil_merge_kernel(*refs, dils):
    n = len(dils)
    o_refs, l_refs, out_ref, scratch = refs[:n], refs[n:2 * n], refs[2 * n], refs[2 * n + 1:]
    tm, d = out_ref.shape[1], out_ref.shape[2]

    def token_major(ref, dil, buf):
        if dil == 1:
            return ref[0].astype(F32)
        for r in range(dil):
            for c in range(d // LANES):
                buf[c, pl.ds(r, tm // dil, stride=dil), :] = (
                    ref[0, :, r * d + c * LANES:r * d + (c + 1) * LANES].astype(F32))
        return jnp.concatenate([buf[c] for c in range(d // LANES)], axis=1)

    outs = [token_major(o_refs[g], dils[g], scratch[2 * g]) for g in range(n)]
    lses = [token_major(l_refs[g], dils[g], scratch[2 * g + 1]) for g in range(n)]
    m = functools.reduce(jnp.maximum, lses)
    es = [jnp.exp(l - m) for l in lses]
    inv = 1.0 / functools.reduce(jnp.add, es)
    out_ref[0] = functools.reduce(jnp.add, [e * inv * o for e, o in zip(es, outs)]).astype(out_ref.dtype)


def _dil_merge(outs, lses, dils, tm=512):
    b = outs[0].shape[0]
    d = N_HEADS * HEAD_DIM
    s = outs[0].shape[1] * dils[0]
    specs = [pl.BlockSpec((1, tm // dil, dil * d), lambda bi, i: (bi, i, 0)) for dil in dils]
    return pl.pallas_call(
        functools.partial(_dil_merge_kernel, dils=tuple(dils)),
        grid=(b, s // tm),
        in_specs=specs + specs,
        out_specs=pl.BlockSpec((1, tm, d), lambda bi, i: (bi, i, 0)),
        out_shape=jax.ShapeDtypeStruct((b, s, d), BF16),
        scratch_shapes=[pltpu.VMEM((d // LANES, tm, LANES), F32) for _ in range(2 * len(dils))],
        compiler_params=_params("parallel", "parallel"),
        name="dilated_merge",
    )(*outs, *lses)


def _dil_mixer(x, scale, shift, w_in, rel_table):
    hd = N_HEADS * HEAD_DIM
    w = w_in.astype(BF16)
    outs, lses, dils = [], [], []
    for gi, (window, dil) in enumerate(DIL_GROUPS):
        view = _linear_in(x, scale, shift, w[:, gi * 3 * hd:(gi + 1) * 3 * hd], tn=3 * hd, dil=dil)
        o, lse = _dil_group(view, window, dil, rel_table)
        outs.append(o)
        lses.append(lse)
        dils.append(dil)
    return _dil_merge(outs, lses, dils)


def kernel(x, c, rel_table, mod_w, mod_b, ln_g, ln_b, gla_w_in, gla_w_a2, gla_b_a, gla_norm_g, gla_w_o,
           nsa_w_in, nsa_cmp_pe, nsa_cmp_w1, nsa_cmp_b1, nsa_cmp_w2, nsa_w_o, dil_w_in, dil_w_o,
           ffn_w_up, ffn_conv_w, ffn_conv_b, ffn_w_down):
    shift, scale, gate = _modulation(c, mod_w, mod_b)
    for i in range(DEPTH):
        kind, j = i % N_MIXERS, i // N_MIXERS
        m = 2 * i
        if kind == 0:
            a = _gla_mixer(x, scale[m], shift[m], gla_w_in[j], gla_w_a2[j], gla_b_a[j], gla_norm_g[j])
            w_o = gla_w_o[j]
        elif kind == 1:
            a = _nsa_mixer(x, scale[m], shift[m], nsa_w_in[j], nsa_cmp_pe[j], nsa_cmp_w1[j], nsa_cmp_b1[j],
                           nsa_cmp_w2[j], rel_table)
            w_o = nsa_w_o[j]
        else:
            a = _dil_mixer(x, scale[m], shift[m], dil_w_in[j], rel_table)
            w_o = dil_w_o[j]
        x = _out_norm(a, w_o.astype(BF16), x, gate[m], ln_g[i, 0], ln_b[i, 0])
        a = _ffn_up(x, scale[m + 1], shift[m + 1], ffn_w_up[i].astype(BF16), ffn_conv_w[i], ffn_conv_b[i])
        x = _out_norm(a, ffn_w_down[i].astype(BF16), x, gate[m + 1], ln_g[i, 1], ln_b[i, 1])
    return x
```

```python
import functools
import math

import jax
import jax.numpy as jnp
import numpy as np
from jax import lax
from jax.experimental import pallas as pl
from jax.experimental.pallas import tpu as pltpu

F32 = jnp.float32
BF16 = jnp.bfloat16

D_MODEL = 1024
DEPTH = 4
N_MIXERS = 3
HEAD_DIM = 64
N_HEADS = D_MODEL // HEAD_DIM
REL_BUCKETS = 32
REL_MAX_DIST = 2048
LN_EPS = 1e-5
GLA_HEADS = 4
GLA_DK = D_MODEL // 2 // GLA_HEADS
GLA_DV = D_MODEL // GLA_HEADS
GLA_GATE_RANK = 16
GLA_TAU = 16.0
GLA_CHUNK = 64
NSA_G = 4
NSA_R = N_HEADS // NSA_G
CMP_LEN = 32
CMP_STRIDE = 16
CMP_HIDDEN = 256
SEL_LEN = 64
SEL_TOP = 16
WIN_LEN = 512
DIL_GROUPS = ((128, 1), (512, 4), (2048, 16))
D_FF = 2816
DN_ALPHA = (2 * DEPTH) ** 0.25

NEG = -1e30
VMEM_LIMIT = 56 * 1024 * 1024
LANES = 128
SUB_Q = 128


def _params(*sem):
    return pltpu.CompilerParams(dimension_semantics=sem, vmem_limit_bytes=VMEM_LIMIT)


def _dot(a, b):
    return jnp.dot(a, b, preferred_element_type=F32)


def _dot_nt(a, b):
    return lax.dot_general(a, b, (((1,), (1,)), ((), ())), preferred_element_type=F32)


def _dot_tn(a, b):
    return lax.dot_general(a, b, (((0,), (0,)), ((), ())), preferred_element_type=F32)


def _split(a):
    hi = a.astype(BF16)
    lo = (a - hi.astype(F32)).astype(BF16)
    return hi, lo


def _dot_hi_exact_rhs(a, b_bf16):
    hi, lo = _split(a)
    return _dot(hi, b_bf16) + _dot(lo, b_bf16)


def _silu(x):
    return x * jax.nn.sigmoid(x)


def _bucket_starts():
    n = np.arange(2 * REL_MAX_DIST)
    exact = REL_BUCKETS // 2
    logn = np.log(np.maximum(n, 1).astype(np.float32) / np.float32(exact))
    large = exact + (logn / np.float32(math.log(REL_MAX_DIST / exact)) * np.float32(REL_BUCKETS - exact)).astype(np.int32)
    bucket = np.where(n < exact, n, np.minimum(large, REL_BUCKETS - 1))
    assert np.all(np.diff(bucket) >= 0) and bucket[-1] == REL_BUCKETS - 1
    return [int(np.argmax(bucket >= k)) for k in range(REL_BUCKETS)]


BUCKET_START = _bucket_starts()
FAR_DIST = BUCKET_START[-1]


def _rel_bias(rel_table, dist):
    shape = (rel_table.shape[1],) + (1,) * dist.ndim
    out = jnp.broadcast_to(rel_table[0].reshape(shape), (rel_table.shape[1],) + dist.shape)
    for k in range(1, REL_BUCKETS):
        out = jnp.where(dist[None] >= BUCKET_START[k], rel_table[k].reshape(shape), out)
    return out


def _mod_kernel(c_ref, w_ref, b_ref, o_ref):
    cs = _silu(c_ref[...])
    ch, cl = _split(cs)
    wh, wl = _split(w_ref[0])
    o_ref[0] = _dot(ch, wh) + _dot(ch, wl) + _dot(cl, wh) + b_ref[0]


def _modulation(c, mod_w, mod_b):
    b, d = c.shape
    n_sub = mod_w.shape[0] * mod_w.shape[1]
    rows = 8
    cp = jnp.zeros((rows, d), F32).at[:b].set(c)
    w = mod_w.reshape(n_sub, d, 3 * d)
    bias = mod_b.reshape(n_sub, 1, 3 * d)
    tn = d
    out = pl.pallas_call(
        _mod_kernel,
        grid=(n_sub, 3 * d // tn),
        in_specs=[pl.BlockSpec((rows, d), lambda i, j: (0, 0)),
                  pl.BlockSpec((1, d, tn), lambda i, j: (i, 0, j)),
                  pl.BlockSpec((1, 1, tn), lambda i, j: (i, 0, j))],
        out_specs=pl.BlockSpec((1, rows, tn), lambda i, j: (i, 0, j)),
        out_shape=jax.ShapeDtypeStruct((n_sub, rows, 3 * d), F32),
        compiler_params=_params("parallel", "parallel"),
        name="modulation",
    )(cp, w, bias)
    out = out[:, :b]
    shift, scale, gate = jnp.split(out, 3, axis=-1)
    return shift[:, :, None, :], scale[:, :, None, :], gate[:, :, None, :]


def _linear_in_kernel(x_ref, sc_ref, sh_ref, w_ref, *rest, dil):
    h = (x_ref[0] * (1.0 + sc_ref[0]) + sh_ref[0]).astype(BF16)
    if dil == 1:
        o_ref, = rest
        o_ref[0] = _dot(h, w_ref[...]).astype(o_ref.dtype)
    else:
        perm_ref, o_ref = rest
        res = _dot(_dot(perm_ref[...], h).astype(BF16), w_ref[...])
        rows, n = res.shape[0] // dil, res.shape[1]
        for r in range(dil):
            o_ref[0, :, r * n:(r + 1) * n] = res[r * rows:(r + 1) * rows].astype(o_ref.dtype)


def _linear_in(x, scale, shift, w, tn, tm=512, dil=1):
    b, s, d = x.shape
    n = w.shape[1]
    assert n % tn == 0 and s % tm == 0 and tm % (16 * dil) == 0 and (dil == 1 or n == tn)
    in_specs = [pl.BlockSpec((1, tm, d), lambda j, bi, i: (bi, i, 0)),
                pl.BlockSpec((1, 1, d), lambda j, bi, i: (bi, 0, 0)),
                pl.BlockSpec((1, 1, d), lambda j, bi, i: (bi, 0, 0)),
                pl.BlockSpec((d, tn), lambda j, bi, i: (0, j))]
    args = [x, scale, shift, w]
    if dil > 1:
        out_row = jnp.arange(tm)
        src = (out_row % (tm // dil)) * dil + out_row // (tm // dil)
        in_specs.append(pl.BlockSpec((tm, tm), lambda j, bi, i: (0, 0)))
        args.append((src[:, None] == jnp.arange(tm)[None, :]).astype(BF16))
    return pl.pallas_call(
        functools.partial(_linear_in_kernel, dil=dil),
        grid=(n // tn, b, s // tm),
        in_specs=in_specs,
        out_specs=pl.BlockSpec((1, tm // dil, dil * tn), lambda j, bi, i: (bi, i, j)),
        out_shape=jax.ShapeDtypeStruct((b, s // dil, dil * n), BF16),
        compiler_params=_params("parallel", "parallel", "parallel"),
        name="linear_in",
    )(*args)


OUT_ROWS = 512


def _out_norm_kernel(a_ref, w_ref, x_ref, gate_ref, g_ref, b_ref, o_ref):
    chunks = [slice(r, r + OUT_ROWS) for r in range(0, a_ref.shape[1], OUT_ROWS)]
    ys = [_dot(a_ref[0, sl, :], w_ref[...]) for sl in chunks]
    for sl, y in zip(chunks, ys):
        z = DN_ALPHA * x_ref[0, sl, :] + (1.0 + gate_ref[0]) * y
        mu = jnp.mean(z, axis=-1, keepdims=True)
        zc = z - mu
        var = jnp.mean(zc * zc, axis=-1, keepdims=True)
        o_ref[0, sl, :] = zc * lax.rsqrt(var + LN_EPS) * g_ref[...] + b_ref[...]


def _out_norm(a, w, x, gate, ln_g, ln_b):
    b, s, d = x.shape
    k = a.shape[-1]
    tm = min(s, 2048 if k <= d else 1024)
    return pl.pallas_call(
        _out_norm_kernel,
        grid=(b, s // tm),
        in_specs=[pl.BlockSpec((1, tm, k), lambda bi, i: (bi, i, 0)),
                  pl.BlockSpec((k, d), lambda bi, i: (0, 0)),
                  pl.BlockSpec((1, tm, d), lambda bi, i: (bi, i, 0)),
                  pl.BlockSpec((1, 1, d), lambda bi, i: (bi, 0, 0)),
                  pl.BlockSpec((1, d), lambda bi, i: (0, 0)),
                  pl.BlockSpec((1, d), lambda bi, i: (0, 0))],
        out_specs=pl.BlockSpec((1, tm, d), lambda bi, i: (bi, i, 0)),
        out_shape=jax.ShapeDtypeStruct((b, s, d), F32),
        compiler_params=_params("parallel", "parallel"),
        name="out_norm",
    )(a, w, x, gate, ln_g.reshape(1, d), ln_b.reshape(1, d))


FFN_HALO = 16
FFN_COLS = 256


def _ffn_up_kernel(x_ref, xh_ref, sc_ref, sh_ref, wu_ref, wg_ref, cw_ref, cb_ref, o_ref, h_ref):
    i = pl.program_id(1)
    sc = 1.0 + sc_ref[0]
    sh = sh_ref[0]
    h_ref[0:FFN_HALO, :] = (xh_ref[0] * sc + sh).astype(BF16)
    h_ref[FFN_HALO:, :] = (x_ref[0] * sc + sh).astype(BF16)
    keep = (i > 0).astype(F32)
    cw = cw_ref[...]
    for c in range(0, o_ref.shape[2], FFN_COLS):
        cols = slice(c, c + FFN_COLS)
        u = _dot(h_ref[...], wu_ref[:, cols])
        u = jnp.concatenate([u[0:FFN_HALO] * keep, u[FFN_HALO:]], axis=0)
        g = _dot(h_ref[FFN_HALO:, :], wg_ref[:, cols])
        a = u * cw[1:2, cols] + pltpu.roll(u, 1, 0) * cw[0:1, cols]
        conv = (u * cw[2:3, cols] + pltpu.roll(a, 1, 0))[FFN_HALO:] + cb_ref[:, cols]
        o_ref[0, :, cols] = (_silu(conv) * g).astype(o_ref.dtype)


def _ffn_up(x, scale, shift, w_up, conv_w, conv_b, tm=1024):
    b, s, d = x.shape
    f = conv_w.shape[1]
    assert s % tm == 0 and tm % FFN_HALO == 0 and f % FFN_COLS == 0
    hb = tm // FFN_HALO
    once = pl.Buffered(1)
    return pl.pallas_call(
        _ffn_up_kernel,
        grid=(b, s // tm),
        in_specs=[pl.BlockSpec((1, tm, d), lambda bi, i: (bi, i, 0)),
                  pl.BlockSpec((1, FFN_HALO, d), lambda bi, i: (bi, jnp.maximum(i * hb - 1, 0), 0)),
                  pl.BlockSpec((1, 1, d), lambda bi, i: (bi, 0, 0)),
                  pl.BlockSpec((1, 1, d), lambda bi, i: (bi, 0, 0)),
                  pl.BlockSpec((d, f), lambda bi, i: (0, 0), pipeline_mode=once),
                  pl.BlockSpec((d, f), lambda bi, i: (0, 1), pipeline_mode=once),
                  pl.BlockSpec((3, f), lambda bi, i: (0, 0)),
                  pl.BlockSpec((1, f), lambda bi, i: (0, 0))],
        out_specs=pl.BlockSpec((1, tm, f), lambda bi, i: (bi, i, 0)),
        out_shape=jax.ShapeDtypeStruct((b, s, f), BF16),
        scratch_shapes=[pltpu.VMEM((tm + FFN_HALO, d), BF16)],
        compiler_params=_params("parallel", "parallel"),
        name="ffn_up",
    )(x, x, scale, shift, w_up, w_up, conv_w, conv_b.reshape(1, f))


def _gla_kernel(q_ref, k_ref, v_ref, r_ref, a_ref, wa_ref, ba_ref, ng_ref, o_ref, st_ref):
    tc = q_ref.shape[1]
    nchunk = tc // GLA_CHUNK

    @pl.when(pl.program_id(2) == 0)
    def _():
        st_ref[...] = jnp.zeros_like(st_ref)

    z = _dot(a_ref[0], wa_ref[...]) + ba_ref[...]
    log_a = (jnp.minimum(z, 0.0) - jnp.log1p(jnp.exp(-jnp.abs(z)))) / GLA_TAU
    ci = lax.broadcasted_iota(jnp.int32, (GLA_CHUNK, GLA_CHUNK), 0)
    cj = lax.broadcasted_iota(jnp.int32, (GLA_CHUNK, GLA_CHUNK), 1)
    causal = cj <= ci
    tri = jnp.where(causal, 1.0, 0.0).astype(BF16)
    chunks = [slice(c * GLA_CHUNK, (c + 1) * GLA_CHUNK) for c in range(nchunk)]
    la_hi, la_lo = _split(log_a)
    cums = [_dot(tri, la_hi[sl]) + _dot(tri, la_lo[sl]) for sl in chunks]
    bcum = jnp.concatenate(cums, axis=0)
    btot = jnp.concatenate([jnp.broadcast_to(cm[GLA_CHUNK - 1:], cm.shape) for cm in cums], axis=0)
    qf = q_ref[0].astype(F32) * (GLA_DK ** -0.5)
    kf = k_ref[0].astype(F32)
    q_dec = (qf * jnp.exp(bcum)).astype(BF16)
    k_inv = (kf * jnp.exp(-bcum)).astype(BF16)
    k_dec = (kf * jnp.exp(btot - bcum)).astype(BF16)
    dl = jnp.exp(btot)
    v = v_ref[0]
    att = [jnp.where(causal, _dot_nt(q_dec[sl], k_inv[sl]), 0.0).astype(BF16) for sl in chunks]
    kv = [_dot_tn(v[sl], k_dec[sl]) for sl in chunks]
    o_intra = [_dot(att[c], v[chunks[c]]) for c in range(nchunk)]
    st = st_ref[...]
    o_inter = []
    for c, sl in enumerate(chunks):
        o_inter.append(_dot_nt(q_dec[sl], st.astype(BF16)))
        st = st * dl[c * GLA_CHUNK:c * GLA_CHUNK + 1] + kv[c]
    st_ref[...] = st
    o = jnp.concatenate(o_intra, axis=0) + jnp.concatenate(o_inter, axis=0)
    o = o * lax.rsqrt(jnp.mean(o * o, axis=-1, keepdims=True) + LN_EPS) * ng_ref[...]
    o_ref[0] = (o * _silu(r_ref[0].astype(F32))).astype(o_ref.dtype)


def _gla_core(proj, w_a2, b_a, norm_g, tc=1024):
    b, s, _ = proj.shape
    h, dk, dv = GLA_HEADS, GLA_DK, GLA_DV
    assert s % tc == 0
    wa = jnp.zeros((LANES, h * dk), F32).at[:GLA_GATE_RANK].set(w_a2).astype(BF16)
    kb, vb, rb, ab = (h * dk) // dk, (2 * h * dk) // dv, (2 * h * dk + h * dv) // dv, (2 * h * dk + 2 * h * dv) // LANES
    return pl.pallas_call(
        _gla_kernel,
        grid=(b, h, s // tc),
        in_specs=[pl.BlockSpec((1, tc, dk), lambda bi, hi, t: (bi, t, hi)),
                  pl.BlockSpec((1, tc, dk), lambda bi, hi, t: (bi, t, kb + hi)),
                  pl.BlockSpec((1, tc, dv), lambda bi, hi, t: (bi, t, vb + hi)),
                  pl.BlockSpec((1, tc, dv), lambda bi, hi, t: (bi, t, rb + hi)),
                  pl.BlockSpec((1, tc, LANES), lambda bi, hi, t: (bi, t, ab)),
                  pl.BlockSpec((LANES, dk), lambda bi, hi, t: (0, hi)),
                  pl.BlockSpec((1, dk), lambda bi, hi, t: (0, hi)),
                  pl.BlockSpec((1, dv), lambda bi, hi, t: (0, 0))],
        out_specs=pl.BlockSpec((1, tc, dv), lambda bi, hi, t: (bi, t, hi)),
        out_shape=jax.ShapeDtypeStruct((b, s, h * dv), BF16),
        scratch_shapes=[pltpu.VMEM((dv, dk), F32)],
        compiler_params=_params("parallel", "parallel", "arbitrary"),
        name="gla_core",
    )(proj, proj, proj, proj, proj, wa, b_a.reshape(1, h * dk), norm_g.reshape(1, dv))


def _pad_cols(w, n):
    return jnp.pad(w, ((0, 0), (0, n - w.shape[1])))


def _gla_mixer(x, scale, shift, w_in, w_a2, b_a, norm_g):
    n_in = 2 * GLA_HEADS * GLA_DK + 2 * GLA_HEADS * GLA_DV + LANES
    proj = _linear_in(x, scale, shift, _pad_cols(w_in, n_in).astype(BF16), tn=n_in)
    return _gla_core(proj, w_a2, b_a, norm_g)


def _compress_kernel(lo_ref, hi_ref, pe_ref, w1_ref, b1_ref, w2_ref, o_ref):
    half = lo_ref.shape[-1]
    pe = pe_ref[0]
    a_lo = (lo_ref[0, 0, 0].astype(F32) + pe[:, :half]).astype(BF16)
    a_hi = (hi_ref[0, 0, 0].astype(F32) + pe[:, half:]).astype(BF16)
    h1 = _dot(a_lo, w1_ref[0, :half, :]) + _dot(a_hi, w1_ref[0, half:, :]) + b1_ref[0]
    o_ref[0, 0, 0] = _dot(_silu(h1).astype(BF16), w2_ref[0])


def _compress(kv, pe, w1, b1, w2):
    two, b, s, _ = kv.shape
    g, dh = NSA_G, HEAD_DIM
    npc = s // CMP_STRIDE
    half = CMP_STRIDE * dh
    a = kv.reshape(two, b, npc, CMP_STRIDE, g, dh).transpose(0, 1, 4, 2, 3, 5).reshape(two, b, g, npc, half)
    a_hi = jnp.concatenate([a[:, :, :, 1:], jnp.zeros_like(a[:, :, :, :1])], axis=3)
    spec_a = pl.BlockSpec((1, 1, 1, npc, half), lambda c, bi, gi: (c, bi, gi, 0, 0))
    return pl.pallas_call(
        _compress_kernel,
        grid=(two, b, g),
        in_specs=[spec_a, spec_a,
                  pl.BlockSpec((1, 1, 2 * half), lambda c, bi, gi: (c, 0, 0)),
                  pl.BlockSpec((1, 2 * half, CMP_HIDDEN), lambda c, bi, gi: (c, 0, 0)),
                  pl.BlockSpec((1, 1, CMP_HIDDEN), lambda c, bi, gi: (c, 0, 0)),
                  pl.BlockSpec((1, CMP_HIDDEN, dh), lambda c, bi, gi: (c, 0, 0))],
        out_specs=pl.BlockSpec((1, 1, 1, npc, dh), lambda c, bi, gi: (c, bi, gi, 0, 0)),
        out_shape=jax.ShapeDtypeStruct((two, b, g, npc, dh), F32),
        compiler_params=_params("parallel", "parallel", "parallel"),
        name="nsa_compress",
    )(a, a_hi, pe.reshape(two, 1, 2 * half), w1.astype(BF16), b1.reshape(two, 1, CMP_HIDDEN), w2.astype(BF16))


def _stack_heads(q):
    return jnp.concatenate([q[:, r * HEAD_DIM:(r + 1) * HEAD_DIM] for r in range(NSA_R)], axis=0)


def _unstack_heads(o, rows):
    return jnp.concatenate([o[r * rows:(r + 1) * rows] for r in range(NSA_R)], axis=1)


CMP_WIDTHS = 4
CMP_TQ = 512
LOG2E = 1.4426950408889634


def _nsa_cmp_kernel(q_ref, fq_ref, kc_ref, vc_ref, nb_ref, pb_ref, o_ref, mb_ref, imp_ref, *, n_tile):
    i = pl.program_id(2)
    npc = nb_ref.shape[2]
    args = (q_ref, fq_ref, kc_ref, vc_ref, nb_ref, pb_ref, o_ref, imp_ref)
    quarter = n_tile // CMP_WIDTHS
    widths = [(k + 1) * npc // CMP_WIDTHS for k in range(CMP_WIDTHS)]

    @pl.when(i == 0)
    def _():
        _cmp_attend(i, widths[0], *args)

    for k in range(CMP_WIDTHS):
        @pl.when((i > 0) & (i >= k * quarter) & (i < (k + 1) * quarter))
        def _(k=k):
            imp = imp_ref[...]
            _cmp_attend(i, widths[k], *args)
            _cmp_select(i - 1, imp, mb_ref)

    @pl.when(i == n_tile)
    def _():
        _cmp_select(i - 1, imp_ref[...], mb_ref)


def _cmp_attend(i, width, q_ref, fq_ref, kc_ref, vc_ref, nb_ref, pb_ref, o_ref, imp_ref):
    skip = nb_ref.shape[2] - width
    n_sub, rows = CMP_TQ // SUB_Q, NSA_R * SUB_Q
    starts = [pl.multiple_of((i * n_sub + j) * (SUB_Q // CMP_STRIDE), 8) for j in range(n_sub)]

    scores = []
    for j in range(n_sub):
        kwin = kc_ref[0, 0, pl.ds(starts[j] + skip, width), :].astype(BF16)
        qs = _stack_heads(q_ref[0, j * SUB_Q:(j + 1) * SUB_Q, :]).astype(F32) * (HEAD_DIM ** -0.5 * LOG2E)
        qp = jnp.concatenate([qs, fq_ref[...]], axis=1).astype(BF16)
        scores.append(_dot_nt(qp, kwin) + nb_ref[0, :, skip:])
    s = jnp.concatenate(scores, axis=0)
    m = jnp.max(s, axis=1, keepdims=True)
    e = jnp.exp2(s - m)
    den = jnp.sum(e, axis=1, keepdims=True)
    p = e * jnp.where(m > 0.5 * NEG, 1.0 / den, 0.0)
    p_bf = p.astype(BF16)
    for j in range(n_sub):
        vwin = vc_ref[0, 0, pl.ds(starts[j] + skip, width), :].astype(BF16)
        o = _dot(p_bf[j * rows:(j + 1) * rows], vwin)
        o_ref[0, j * SUB_Q:(j + 1) * SUB_Q, :] = _unstack_heads(o, SUB_Q).astype(o_ref.dtype)

    imps = []
    for j in range(n_sub):
        pg = p[j * rows:j * rows + SUB_Q]
        for r in range(1, NSA_R):
            pg = pg + p[j * rows + r * SUB_Q:j * rows + (r + 1) * SUB_Q]
        imps.append(_dot_hi_exact_rhs(pg, pb_ref[pl.ds(starts[j] + skip, width), :].astype(BF16)))
    imp_ref[...] = jnp.concatenate(imps, axis=0)


def _cmp_select(i, imp, mb_ref):
    nblk = mb_ref.shape[3]
    blk = lax.broadcasted_iota(jnp.int32, (CMP_TQ, nblk), 1)
    blk_f = blk.astype(F32)
    t = i * CMP_TQ + lax.broadcasted_iota(jnp.int32, (CMP_TQ, nblk), 0)
    cur = t // SEL_LEN
    forced = (blk == 0) | (blk == cur) | (blk == cur - 1)
    work = jnp.where((blk * SEL_LEN <= t) & jnp.logical_not(forced), imp, NEG)
    free = work > 0.5 * NEG
    for _ in range(SEL_TOP - 3):
        mx = jnp.max(work, axis=1, keepdims=True)
        first = jnp.min(jnp.where(work == mx, blk_f, 1e9), axis=1, keepdims=True)
        work = jnp.where(blk_f == first, 2.0 * NEG, work)
    sel = forced | (free & (work < 1.5 * NEG))
    mb_ref[0, 0] = jnp.where(sel, 0.0, NEG).astype(mb_ref.dtype)


def _nsa_cmp(proj, kcmp, vcmp, rel_table):
    b, s, _ = proj.shape
    g, dh = NSA_G, HEAD_DIM
    npc = s // CMP_STRIDE
    nblk = s // SEL_LEN
    front = npc - SUB_Q // CMP_STRIDE
    assert s % CMP_TQ == 0 and nblk >= SEL_TOP
    n_tile = s // CMP_TQ
    assert n_tile % CMP_WIDTHS == 0
    kpad = jnp.pad(kcmp, ((0, 0), (0, 0), (front, npc - front), (0, LANES - dh)))
    kpad = kpad.at[:, :, :front, dh].set(1.0)
    vpad = jnp.pad(vcmp, ((0, 0), (0, 0), (front, npc - front), (0, 0)))
    fq = jnp.zeros((NSA_R * SUB_Q, LANES - dh), F32).at[:, 0].set(NEG)
    qi = jnp.arange(SUB_Q)[:, None]
    cc = jnp.arange(npc)[None, :]
    dd = qi - (CMP_LEN - 1) + CMP_STRIDE * (front - cc)
    nb = jnp.where(dd[None] >= 0, _rel_bias(rel_table, dd) * LOG2E, NEG).reshape(g, NSA_R * SUB_Q, npc)
    diff = jnp.arange(2 * npc)[:, None] - front - (SEL_LEN // CMP_STRIDE) * jnp.arange(nblk)[None, :]
    pool = jnp.where((diff == -1) | (diff == 3), 1.0, jnp.where((diff >= 0) & (diff <= 2), 2.0, 0.0))
    return pl.pallas_call(
        functools.partial(_nsa_cmp_kernel, n_tile=n_tile),
        grid=(g, b, n_tile + 1),
        in_specs=[pl.BlockSpec((1, CMP_TQ, NSA_R * dh), lambda gi, bi, i: (bi, jnp.minimum(i, n_tile - 1), gi)),
                  pl.BlockSpec((NSA_R * SUB_Q, LANES - dh), lambda gi, bi, i: (0, 0)),
                  pl.BlockSpec((1, 1, 2 * npc, LANES), lambda gi, bi, i: (bi, gi, 0, 0)),
                  pl.BlockSpec((1, 1, 2 * npc, dh), lambda gi, bi, i: (bi, gi, 0, 0)),
                  pl.BlockSpec((1, NSA_R * SUB_Q, npc), lambda gi, bi, i: (gi, 0, 0)),
                  pl.BlockSpec((2 * npc, nblk), lambda gi, bi, i: (0, 0))],
        out_specs=[pl.BlockSpec((1, CMP_TQ, NSA_R * dh), lambda gi, bi, i: (bi, jnp.minimum(i, n_tile - 1), gi)),
                   pl.BlockSpec((1, 1, CMP_TQ, nblk), lambda gi, bi, i: (bi, gi, jnp.maximum(i - 1, 0), 0))],
        out_shape=[jax.ShapeDtypeStruct((b, s, N_HEADS * dh), BF16),
                   jax.ShapeDtypeStruct((b, g, s, nblk), BF16)],
        scratch_shapes=[pltpu.VMEM((CMP_TQ, nblk), F32)],
        compiler_params=_params("parallel", "parallel", "arbitrary"),
        name="nsa_cmp_select",
    )(proj, fq, kpad, vpad, nb, pool.astype(F32))


SEL_T = 256
SEL_WIDE = 8
SEL_NEAR = -(-(FAR_DIST + SEL_T - 1) // SEL_T)


def _nsa_sel_kernel(q_ref, fq_ref, mb_ref, k_ref, v_ref, bt_ref, o_ref, qp_ref, m_ref, acc_ref):
    i = pl.program_id(2)
    rows = NSA_R * SEL_T
    fw = (k_ref.shape[3] - LANES)
    kh = fw // (SEL_T // SEL_LEN)
    qs = (_stack_heads(q_ref[0]).astype(F32) * (HEAD_DIM ** -0.5 * LOG2E)).astype(BF16)
    mb = mb_ref[0, 0]
    for half in range(2):
        mbh = jnp.concatenate([mb[:, half * fw:(half + 1) * fw]] * NSA_R, axis=0)
        qp_ref[half] = jnp.concatenate([qs, fq_ref[0], mbh], axis=1)
    m_ref[...] = jnp.full_like(m_ref, NEG)
    acc_ref[...] = jnp.zeros_like(acc_ref)

    def update(kt, n_tile, half, bias):
        k0 = pl.multiple_of(kt * SEL_T, SEL_T)
        kk = k_ref[0, 0, pl.ds(k0, n_tile * SEL_T), :]
        vv = v_ref[0, 0, pl.ds(k0, n_tile * SEL_T), :]
        s = _dot_nt(qp_ref[half], kk)
        if bias is not None:
            s = s + bias
        m_prev = m_ref[...]
        m_new = jnp.maximum(m_prev, jnp.max(s, axis=1, keepdims=True))
        alpha = jnp.exp2(m_prev - m_new)
        p = jnp.exp2(s - jnp.concatenate([m_new] * (n_tile * SEL_T // LANES), axis=1))
        acc_ref[...] = acc_ref[...] * alpha + _dot(p.astype(BF16), vv)
        m_ref[...] = m_new

    def far_region(lo, hi, half):
        n = jnp.maximum(hi - lo, 0)

        def wide(j, carry):
            update(lo + SEL_WIDE * j, SEL_WIDE, half, None)
            return carry

        lax.fori_loop(0, n // SEL_WIDE, wide, 0)
        piece = SEL_WIDE // 2
        while piece >= 1:
            @pl.when((n & piece) != 0)
            def _(piece=piece):
                update(lo + (n & ~(2 * piece - 1)), piece, half, None)
            piece //= 2

    n_far = jnp.maximum(i - SEL_NEAR + 1, 0)
    far_region(0, jnp.minimum(n_far, kh), 0)
    far_region(kh, n_far, 1)

    def near(e):
        kt = i - e
        update(kt, 1, jnp.where(kt < kh, 0, 1), bt_ref[0, :, e * SEL_T:(e + 1) * SEL_T])

    @pl.when(i >= SEL_NEAR - 1)
    def _():
        for e in range(SEL_NEAR - 1, -1, -1):
            near(e)

    @pl.when(i < SEL_NEAR - 1)
    def _():
        for e in range(SEL_NEAR - 2, -1, -1):
            @pl.when(i - e >= 0)
            def _(e=e):
                near(e)

    acc = acc_ref[...]
    o = acc[:, :HEAD_DIM] / acc[:, HEAD_DIM:HEAD_DIM + 1]
    o_ref[0] = _unstack_heads(o, SEL_T).astype(o_ref.dtype)


def _nsa_sel(proj, ks, vs, mb, rel_table):
    b, s, _ = proj.shape
    g, dh = NSA_G, HEAD_DIM
    nblk = s // SEL_LEN
    fw = nblk // 2
    rows = NSA_R * SEL_T
    pos = jnp.arange(s)
    onehot = (((pos // SEL_LEN) % fw)[:, None] == jnp.arange(fw)[None, :]).astype(BF16)
    ones2 = jnp.zeros((s, LANES - dh), BF16).at[:, :2].set(1.0)
    kp = jnp.concatenate([ks, jnp.broadcast_to(ones2, (b, g, s, LANES - dh)),
                          jnp.broadcast_to(onehot, (b, g, s, fw))], axis=-1)
    ones1 = jnp.zeros((s, LANES - dh), BF16).at[:, :1].set(1.0)
    vp = jnp.concatenate([vs, jnp.broadcast_to(ones1, (b, g, s, LANES - dh))], axis=-1)
    b_far = rel_table[REL_BUCKETS - 1].reshape(g, NSA_R) * LOG2E
    hi = b_far.astype(BF16)
    lo = (b_far - hi.astype(F32)).astype(BF16)
    fq = jnp.zeros((g, NSA_R, SEL_T, LANES - dh), BF16)
    fq = fq.at[:, :, :, 0].set(hi[:, :, None]).at[:, :, :, 1].set(lo[:, :, None]).reshape(g, rows, LANES - dh)
    qi = jnp.arange(SEL_T)[:, None]
    cc = jnp.arange(SEL_NEAR * SEL_T)[None, :]
    dd = (cc // SEL_T) * SEL_T + qi - (cc % SEL_T)
    bias = _rel_bias(rel_table, dd).reshape(g, NSA_R, SEL_T, -1) * LOG2E - b_far[:, :, None, None]
    bt = jnp.where(dd[None, None] >= 0, bias, NEG).reshape(g, rows, SEL_NEAR * SEL_T)
    return pl.pallas_call(
        _nsa_sel_kernel,
        grid=(g, b, s // SEL_T),
        in_specs=[pl.BlockSpec((1, SEL_T, NSA_R * dh), lambda gi, bi, i: (bi, i, gi)),
                  pl.BlockSpec((1, rows, LANES - dh), lambda gi, bi, i: (gi, 0, 0)),
                  pl.BlockSpec((1, 1, SEL_T, nblk), lambda gi, bi, i: (bi, gi, i, 0)),
                  pl.BlockSpec((1, 1, s, LANES + fw), lambda gi, bi, i: (bi, gi, 0, 0)),
                  pl.BlockSpec((1, 1, s, LANES), lambda gi, bi, i: (bi, gi, 0, 0)),
                  pl.BlockSpec((1, rows, SEL_NEAR * SEL_T), lambda gi, bi, i: (gi, 0, 0),
                               pipeline_mode=pl.Buffered(1))],
        out_specs=pl.BlockSpec((1, SEL_T, NSA_R * dh), lambda gi, bi, i: (bi, i, gi)),
        out_shape=jax.ShapeDtypeStruct((b, s, N_HEADS * dh), BF16),
        scratch_shapes=[pltpu.VMEM((2, rows, LANES + fw), BF16), pltpu.VMEM((rows, LANES), F32),
                        pltpu.VMEM((rows, LANES), F32)],
        compiler_params=_params("parallel", "parallel", "parallel"),
        name="nsa_selected",
    )(proj, fq, mb, kp, vp, bt)


WIN_T = 512


def _softmax2_rows(s):
    m = jnp.max(s, axis=1, keepdims=True)
    e = jnp.exp2(s - m)
    den = jnp.sum(e, axis=1, keepdims=True)
    return e, 1.0 / den, m * (1.0 / LOG2E) + jnp.log(den)


def _nsa_win_kernel(q_ref, kp_ref, kc_ref, vp_ref, vc_ref, bt_ref, o_ref):
    i = pl.program_id(2)
    back = WIN_T
    span = back + SUB_Q
    n_sub, rows = WIN_T // SUB_Q, NSA_R * SUB_Q

    def body(first_tile):
        kcat = jnp.concatenate([kp_ref[0, 0], kc_ref[0, 0]], axis=0)
        vcat = jnp.concatenate([vp_ref[0, 0], vc_ref[0, 0]], axis=0)
        col = lax.broadcasted_iota(jnp.int32, (rows, span), 1)
        scores = []
        for j in range(n_sub):
            qs = _stack_heads(q_ref[0, j * SUB_Q:(j + 1) * SUB_Q, :]).astype(F32) * (HEAD_DIM ** -0.5 * LOG2E)
            s = _dot_nt(qs.astype(BF16), kcat[j * SUB_Q:j * SUB_Q + span]) + bt_ref[0]
            if first_tile:
                s = jnp.where(col >= back - j * SUB_Q, s, NEG)
            scores.append(s)
        e, inv, _ = _softmax2_rows(jnp.concatenate(scores, axis=0))
        e = e.astype(BF16)
        for j in range(n_sub):
            o = _dot(e[j * rows:(j + 1) * rows], vcat[j * SUB_Q:j * SUB_Q + span]) * inv[j * rows:(j + 1) * rows]
            o_ref[0, j * SUB_Q:(j + 1) * SUB_Q, :] = _unstack_heads(o, SUB_Q).astype(o_ref.dtype)

    @pl.when(i == 0)
    def _():
        body(True)

    @pl.when(i > 0)
    def _():
        body(False)


def _nsa_win(proj, kw, vw, rel_table):
    b, s, _ = proj.shape
    g, dh = NSA_G, HEAD_DIM
    span = WIN_T + SUB_Q
    qi = jnp.arange(SUB_Q)[:, None]
    cc = jnp.arange(span)[None, :]
    dd = qi + WIN_T - cc
    bias = _rel_bias(rel_table, dd)
    bt = jnp.where((dd[None] >= 0) & (dd[None] <= WIN_LEN - 1), bias * LOG2E, NEG).reshape(g, NSA_R * SUB_Q, span)
    prev = lambda gi, bi, i: (bi, gi, jnp.maximum(i - 1, 0), 0)
    cur = lambda gi, bi, i: (bi, gi, i, 0)
    return pl.pallas_call(
        _nsa_win_kernel,
        grid=(g, b, s // WIN_T),
        in_specs=[pl.BlockSpec((1, WIN_T, NSA_R * dh), lambda gi, bi, i: (bi, i, gi)),
                  pl.BlockSpec((1, 1, WIN_T, dh), prev), pl.BlockSpec((1, 1, WIN_T, dh), cur),
                  pl.BlockSpec((1, 1, WIN_T, dh), prev), pl.BlockSpec((1, 1, WIN_T, dh), cur),
                  pl.BlockSpec((1, NSA_R * SUB_Q, span), lambda gi, bi, i: (gi, 0, 0))],
        out_specs=pl.BlockSpec((1, WIN_T, NSA_R * dh), lambda gi, bi, i: (bi, i, gi)),
        out_shape=jax.ShapeDtypeStruct((b, s, N_HEADS * dh), BF16),
        compiler_params=_params("parallel", "parallel", "parallel"),
        name="nsa_window",
    )(proj, kw, kw, vw, vw, bt)


def _nsa_combine_kernel(oc_ref, os_ref, ow_ref, g_ref, e_ref, o_ref):
    d = o_ref.shape[2]
    sg = jax.nn.sigmoid(g_ref[0].astype(F32))
    ge = _dot_hi_exact_rhs(sg, e_ref[...])
    o = (ge[:, :d] * oc_ref[0].astype(F32) + ge[:, d:2 * d] * os_ref[0].astype(F32)
         + ge[:, 2 * d:] * ow_ref[0].astype(F32))
    o_ref[0] = o.astype(o_ref.dtype)


def _nsa_combine(o_cmp, o_sel, o_win, proj, gate_col, tm=512):
    b, s, d = o_cmp.shape
    c = jnp.arange(LANES)[:, None]
    n = jnp.arange(3 * d)[None, :]
    expand = (c == (n // d) * N_HEADS + (n % d) // HEAD_DIM).astype(BF16)
    spec = pl.BlockSpec((1, tm, d), lambda bi, i: (bi, i, 0))
    return pl.pallas_call(
        _nsa_combine_kernel,
        grid=(b, s // tm),
        in_specs=[spec, spec, spec,
                  pl.BlockSpec((1, tm, LANES), lambda bi, i: (bi, i, gate_col // LANES)),
                  pl.BlockSpec((LANES, 3 * d), lambda bi, i: (0, 0))],
        out_specs=spec,
        out_shape=jax.ShapeDtypeStruct((b, s, d), BF16),
        compiler_params=_params("parallel", "parallel"),
        name="nsa_combine",
    )(o_cmp, o_sel, o_win, proj, expand)


def _nsa_mixer(x, scale, shift, w_in, pe, w1, b1, w2, rel_table):
    b, s, _ = x.shape
    hd, gd = N_HEADS * HEAD_DIM, NSA_G * HEAD_DIM
    gate_col = hd + 6 * gd
    n_in = gate_col + LANES
    proj = _linear_in(x, scale, shift, _pad_cols(w_in, n_in).astype(BF16), tn=n_in)

    def group_major(t):
        return t.reshape(b, s, NSA_G, HEAD_DIM).transpose(0, 2, 1, 3)

    kc, vc, ks, vs, kw, vw = (proj[:, :, hd + n * gd: hd + (n + 1) * gd] for n in range(6))
    cmp = _compress(jnp.stack([kc, vc]), pe.reshape(2, -1), w1, b1, w2)
    o_cmp, mb = _nsa_cmp(proj, cmp[0], cmp[1], rel_table)
    o_sel = _nsa_sel(proj, group_major(ks), group_major(vs), mb, rel_table)
    o_win = _nsa_win(proj, group_major(kw), group_major(vw), rel_table)
    return _nsa_combine(o_cmp, o_sel, o_win, proj, gate_col)


DIL_BACK = 128
DIL_W = 256


def _dil_kernel(q_ref, kp_ref, kc_ref, vp_ref, vc_ref, bt_ref, o_ref, l_ref):
    t = pl.program_id(3)
    tq = q_ref.shape[1]
    span = DIL_BACK + SUB_Q
    kcat = jnp.concatenate([kp_ref[0, tq - DIL_BACK:, :], kc_ref[0]], axis=0)
    vcat = jnp.concatenate([vp_ref[0, tq - DIL_BACK:, :], vc_ref[0]], axis=0)
    lane_q = lax.broadcasted_iota(jnp.int32, (SUB_Q, DIL_W), 1) // HEAD_DIM
    lane_k = lax.broadcasted_iota(jnp.int32, (span, DIL_W), 1) // HEAD_DIM
    col = lax.broadcasted_iota(jnp.int32, (SUB_Q, span), 1)
    n_sub, n_head = tq // SUB_Q, DIL_W // HEAD_DIM
    scores = []
    for j in range(n_sub):
        qj = (q_ref[0, j * SUB_Q:(j + 1) * SUB_Q, :].astype(F32) * (HEAD_DIM ** -0.5 * LOG2E)).astype(BF16)
        kk = kcat[j * SUB_Q:j * SUB_Q + span]
        for hh in range(n_head):
            s = _dot_nt(jnp.where(lane_q == hh, qj, 0.0).astype(BF16), kk) + bt_ref[hh]
            if j == 0:
                s = jnp.where((col >= DIL_BACK) | (t > 0), s, NEG)
            scores.append(s)
    e_all, inv_all, lse_all = _softmax2_rows(jnp.concatenate(scores, axis=0))
    e_all = e_all.astype(BF16)
    for j in range(n_sub):
        vv = vcat[j * SUB_Q:j * SUB_Q + span]
        o = jnp.zeros((SUB_Q, DIL_W), F32)
        inv = jnp.zeros((SUB_Q, DIL_W), F32)
        lse = jnp.zeros((SUB_Q, DIL_W), F32)
        for hh in range(n_head):
            r0 = (j * n_head + hh) * SUB_Q
            o = o + _dot(e_all[r0:r0 + SUB_Q], jnp.where(lane_k == hh, vv, 0.0).astype(BF16))
            inv = jnp.where(lane_q == hh, inv_all[r0:r0 + SUB_Q], inv)
            lse = jnp.where(lane_q == hh, lse_all[r0:r0 + SUB_Q], lse)
        o_ref[0, j * SUB_Q:(j + 1) * SUB_Q, :] = (o * inv).astype(o_ref.dtype)
        l_ref[0, j * SUB_Q:(j + 1) * SUB_Q, :] = lse


def _dil_group(view, window, dil, rel_table):
    b, l, n_all = view.shape
    hd = N_HEADS * HEAD_DIM
    assert window // dil == DIL_BACK and n_all == dil * 3 * hd
    tq = min(1024, l)
    assert l % tq == 0 and tq >= DIL_BACK
    hpb = hd // DIL_W
    qi = jnp.arange(SUB_Q)[:, None]
    cc = jnp.arange(DIL_BACK + SUB_Q)[None, :]
    dd = qi + DIL_BACK - cc
    bt = jnp.where((dd[None] >= 0) & (dd[None] <= DIL_BACK), _rel_bias(rel_table, dd * dil) * LOG2E, NEG)

    def spec(comp, prev):
        def index(bi, r, hp, t):
            return (bi, jnp.maximum(t - 1, 0) if prev else t, (r * 3 + comp) * hpb + hp)
        return pl.BlockSpec((1, tq, DIL_W), index)

    out_spec = pl.BlockSpec((1, tq, DIL_W), lambda bi, r, hp, t: (bi, t, r * hpb + hp))
    return pl.pallas_call(
        _dil_kernel,
        grid=(b, dil, hpb, l // tq),
        in_specs=[spec(0, False), spec(1, True), spec(1, False), spec(2, True), spec(2, False),
                  pl.BlockSpec((DIL_W // HEAD_DIM, SUB_Q, DIL_BACK + SUB_Q), lambda bi, r, hp, t: (hp, 0, 0))],
        out_specs=[out_spec, out_spec],
        out_shape=[jax.ShapeDtypeStruct((b, l, dil * hd), BF16), jax.ShapeDtypeStruct((b, l, dil * hd), F32)],
        compiler_params=_params("parallel", "parallel", "parallel", "parallel"),
        name=f"dilated_attention_{dil}",
    )(view, view, view, view, view, bt)


def _dil_merge_kernel(*refs, dils):
    n = len(dils)
    o_refs, l_refs, out_ref, scratch = refs[:n], refs[n:2 * n], refs[2 * n], refs[2 * n + 1:]
    tm, d = out_ref.shape[1], out_ref.shape[2]

    def token_major(ref, dil, buf):
        if dil == 1:
            return ref[0].astype(F32)
        for r in range(dil):
            for c in range(d // LANES):
                buf[c, pl.ds(r, tm // dil, stride=dil), :] = (
                    ref[0, :, r * d + c * LANES:r * d + (c + 1) * LANES].astype(F32))
        return jnp.concatenate([buf[c] for c in range(d // LANES)], axis=1)

    outs = [token_major(o_refs[g], dils[g], scratch[2 * g]) for g in range(n)]
    lses = [token_major(l_refs[g], dils[g], scratch[2 * g + 1]) for g in range(n)]
    m = functools.reduce(jnp.maximum, lses)
    es = [jnp.exp(l - m) for l in lses]
    inv = 1.0 / functools.reduce(jnp.add, es)
    out_ref[0] = functools.reduce(jnp.add, [e * inv * o for e, o in zip(es, outs)]).astype(out_ref.dtype)


def _dil_merge(outs, lses, dils, tm=512):
    b = outs[0].shape[0]
    d = N_HEADS * HEAD_DIM
    s = outs[0].shape[1] * dils[0]
    specs = [pl.BlockSpec((1, tm // dil, dil * d), lambda bi, i: (bi, i, 0)) for dil in dils]
    return pl.pallas_call(
        functools.partial(_dil_merge_kernel, dils=tuple(dils)),
        grid=(b, s // tm),
        in_specs=specs + specs,
        out_specs=pl.BlockSpec((1, tm, d), lambda bi, i: (bi, i, 0)),
        out_shape=jax.ShapeDtypeStruct((b, s, d), BF16),
        scratch_shapes=[pltpu.VMEM((d // LANES, tm, LANES), F32) for _ in range(2 * len(dils))],
        compiler_params=_params("parallel", "parallel"),
        name="dilated_merge",
    )(*outs, *lses)


def _dil_mixer(x, scale, shift, w_in, rel_table):
    hd = N_HEADS * HEAD_DIM
    w = w_in.astype(BF16)
    outs, lses, dils = [], [], []
    for gi, (window, dil) in enumerate(DIL_GROUPS):
        view = _linear_in(x, scale, shift, w[:, gi * 3 * hd:(gi + 1) * 3 * hd], tn=3 * hd, dil=dil)
        o, lse = _dil_group(view, window, dil, rel_table)
        outs.append(o)
        lses.append(lse)
        dils.append(dil)
    return _dil_merge(outs, lses, dils)


def kernel(x, c, rel_table, mod_w, mod_b, ln_g, ln_b, gla_w_in, gla_w_a2, gla_b_a, gla_norm_g, gla_w_o,
           nsa_w_in, nsa_cmp_pe, nsa_cmp_w1, nsa_cmp_b1, nsa_cmp_w2, nsa_w_o, dil_w_in, dil_w_o,
           ffn_w_up, ffn_conv_w, ffn_conv_b, ffn_w_down):
    shift, scale, gate = _modulation(c, mod_w, mod_b)
    for i in range(DEPTH):
        kind, j = i % N_MIXERS, i // N_MIXERS
        m = 2 * i
        if kind == 0:
            a = _gla_mixer(x, scale[m], shift[m], gla_w_in[j], gla_w_a2[j], gla_b_a[j], gla_norm_g[j])
            w_o = gla_w_o[j]
        elif kind == 1:
            a = _nsa_mixer(x, scale[m], shift[m], nsa_w_in[j], nsa_cmp_pe[j], nsa_cmp_w1[j], nsa_cmp_b1[j],
                           nsa_cmp_w2[j], rel_table)
            w_o = nsa_w_o[j]
        else:
            a = _dil_mixer(x, scale[m], shift[m], dil_w_in[j], rel_table)
            w_o = dil_w_o[j]
        x = _out_norm(a, w_o.astype(BF16), x, gate[m], ln_g[i, 0], ln_b[i, 0])
        a = _ffn_up(x, scale[m + 1], shift[m + 1], ffn_w_up[i].astype(BF16), ffn_conv_w[i], ffn_conv_b[i])
        x = _out_norm(a, ffn_w_down[i].astype(BF16), x, gate[m + 1], ln_g[i, 1], ln_b[i, 1])
    return x
```

```python
import functools
import math

import jax
import jax.numpy as jnp
import numpy as np
from jax import lax
from jax.experimental import pallas as pl
from jax.experimental.pallas import tpu as pltpu

F32 = jnp.float32
BF16 = jnp.bfloat16

D_MODEL = 1024
DEPTH = 4
N_MIXERS = 3
HEAD_DIM = 64
N_HEADS = D_MODEL // HEAD_DIM
REL_BUCKETS = 32
REL_MAX_DIST = 2048
LN_EPS = 1e-5
GLA_HEADS = 4
GLA_DK = D_MODEL // 2 // GLA_HEADS
GLA_DV = D_MODEL // GLA_HEADS
GLA_GATE_RANK = 16
GLA_TAU = 16.0
GLA_CHUNK = 64
NSA_G = 4
NSA_R = N_HEADS // NSA_G
CMP_LEN = 32
CMP_STRIDE = 16
CMP_HIDDEN = 256
SEL_LEN = 64
SEL_TOP = 16
WIN_LEN = 512
DIL_GROUPS = ((128, 1), (512, 4), (2048, 16))
D_FF = 2816
DN_ALPHA = (2 * DEPTH) ** 0.25

NEG = -1e30
VMEM_LIMIT = 56 * 1024 * 1024
LANES = 128
SUB_Q = 128


def _params(*sem):
    return pltpu.CompilerParams(dimension_semantics=sem, vmem_limit_bytes=VMEM_LIMIT)


def _dot(a, b):
    return jnp.dot(a, b, preferred_element_type=F32)


def _dot_nt(a, b):
    return lax.dot_general(a, b, (((1,), (1,)), ((), ())), preferred_element_type=F32)


def _dot_tn(a, b):
    return lax.dot_general(a, b, (((0,), (0,)), ((), ())), preferred_element_type=F32)


def _split(a):
    hi = a.astype(BF16)
    lo = (a - hi.astype(F32)).astype(BF16)
    return hi, lo


def _dot_hi_exact_rhs(a, b_bf16):
    hi, lo = _split(a)
    return _dot(hi, b_bf16) + _dot(lo, b_bf16)


def _silu(x):
    return x * jax.nn.sigmoid(x)


def _bucket_starts():
    n = np.arange(2 * REL_MAX_DIST)
    exact = REL_BUCKETS // 2
    logn = np.log(np.maximum(n, 1).astype(np.float32) / np.float32(exact))
    large = exact + (logn / np.float32(math.log(REL_MAX_DIST / exact)) * np.float32(REL_BUCKETS - exact)).astype(np.int32)
    bucket = np.where(n < exact, n, np.minimum(large, REL_BUCKETS - 1))
    assert np.all(np.diff(bucket) >= 0) and bucket[-1] == REL_BUCKETS - 1
    return [int(np.argmax(bucket >= k)) for k in range(REL_BUCKETS)]


BUCKET_START = _bucket_starts()
FAR_DIST = BUCKET_START[-1]


def _rel_bias(rel_table, dist):
    shape = (rel_table.shape[1],) + (1,) * dist.ndim
    out = jnp.broadcast_to(rel_table[0].reshape(shape), (rel_table.shape[1],) + dist.shape)
    for k in range(1, REL_BUCKETS):
        out = jnp.where(dist[None] >= BUCKET_START[k], rel_table[k].reshape(shape), out)
    return out


def _mod_kernel(c_ref, w_ref, b_ref, o_ref):
    cs = _silu(c_ref[...])
    ch, cl = _split(cs)
    wh, wl = _split(w_ref[0])
    o_ref[0] = _dot(ch, wh) + _dot(ch, wl) + _dot(cl, wh) + b_ref[0]


def _modulation(c, mod_w, mod_b):
    b, d = c.shape
    n_sub = mod_w.shape[0] * mod_w.shape[1]
    rows = 8
    cp = jnp.zeros((rows, d), F32).at[:b].set(c)
    w = mod_w.reshape(n_sub, d, 3 * d)
    bias = mod_b.reshape(n_sub, 1, 3 * d)
    tn = d
    out = pl.pallas_call(
        _mod_kernel,
        grid=(n_sub, 3 * d // tn),
        in_specs=[pl.BlockSpec((rows, d), lambda i, j: (0, 0)),
                  pl.BlockSpec((1, d, tn), lambda i, j: (i, 0, j)),
                  pl.BlockSpec((1, 1, tn), lambda i, j: (i, 0, j))],
        out_specs=pl.BlockSpec((1, rows, tn), lambda i, j: (i, 0, j)),
        out_shape=jax.ShapeDtypeStruct((n_sub, rows, 3 * d), F32),
        compiler_params=_params("parallel", "parallel"),
        name="modulation",
    )(cp, w, bias)
    out = out[:, :b]
    shift, scale, gate = jnp.split(out, 3, axis=-1)
    return shift[:, :, None, :], scale[:, :, None, :], gate[:, :, None, :]


def _linear_in_kernel(x_ref, sc_ref, sh_ref, w_ref, *rest, dil):
    h = (x_ref[0] * (1.0 + sc_ref[0]) + sh_ref[0]).astype(BF16)
    if dil == 1:
        o_ref, = rest
        o_ref[0] = _dot(h, w_ref[...]).astype(o_ref.dtype)
    else:
        perm_ref, o_ref = rest
        res = _dot(_dot(perm_ref[...], h).astype(BF16), w_ref[...])
        rows, n = res.shape[0] // dil, res.shape[1]
        for r in range(dil):
            o_ref[0, :, r * n:(r + 1) * n] = res[r * rows:(r + 1) * rows].astype(o_ref.dtype)


def _linear_in(x, scale, shift, w, tn, dil=1):
    b, s, d = x.shape
    n = w.shape[1]
    tm = min(s, 1024 if dil == 1 else 512)
    assert n % tn == 0 and s % tm == 0 and tm % (16 * dil) == 0 and (dil == 1 or n == tn)
    in_specs = [pl.BlockSpec((1, tm, d), lambda j, bi, i: (bi, i, 0)),
                pl.BlockSpec((1, 1, d), lambda j, bi, i: (bi, 0, 0)),
                pl.BlockSpec((1, 1, d), lambda j, bi, i: (bi, 0, 0)),
                pl.BlockSpec((d, tn), lambda j, bi, i: (0, j), pipeline_mode=pl.Buffered(1) if n == tn else None)]
    args = [x, scale, shift, w]
    if dil > 1:
        out_row = jnp.arange(tm)
        src = (out_row % (tm // dil)) * dil + out_row // (tm // dil)
        in_specs.append(pl.BlockSpec((tm, tm), lambda j, bi, i: (0, 0)))
        args.append((src[:, None] == jnp.arange(tm)[None, :]).astype(BF16))
    return pl.pallas_call(
        functools.partial(_linear_in_kernel, dil=dil),
        grid=(n // tn, b, s // tm),
        in_specs=in_specs,
        out_specs=pl.BlockSpec((1, tm // dil, dil * tn), lambda j, bi, i: (bi, i, j)),
        out_shape=jax.ShapeDtypeStruct((b, s // dil, dil * n), BF16),
        compiler_params=_params("parallel", "parallel", "parallel"),
        name="linear_in",
    )(*args)


OUT_ROWS = 512


def _out_norm_kernel(a_ref, w_ref, x_ref, gate_ref, g_ref, b_ref, o_ref):
    chunks = [slice(r, r + OUT_ROWS) for r in range(0, a_ref.shape[1], OUT_ROWS)]
    ys = [_dot(a_ref[0, sl, :], w_ref[...]) for sl in chunks]
    for sl, y in zip(chunks, ys):
        z = DN_ALPHA * x_ref[0, sl, :] + (1.0 + gate_ref[0]) * y
        mu = jnp.mean(z, axis=-1, keepdims=True)
        zc = z - mu
        var = jnp.mean(zc * zc, axis=-1, keepdims=True)
        o_ref[0, sl, :] = zc * lax.rsqrt(var + LN_EPS) * g_ref[...] + b_ref[...]


def _out_norm(a, w, x, gate, ln_g, ln_b):
    b, s, d = x.shape
    k = a.shape[-1]
    tm = min(s, 2048 if k <= d else 1024)
    return pl.pallas_call(
        _out_norm_kernel,
        grid=(b, s // tm),
        in_specs=[pl.BlockSpec((1, tm, k), lambda bi, i: (bi, i, 0)),
                  pl.BlockSpec((k, d), lambda bi, i: (0, 0)),
                  pl.BlockSpec((1, tm, d), lambda bi, i: (bi, i, 0)),
                  pl.BlockSpec((1, 1, d), lambda bi, i: (bi, 0, 0)),
                  pl.BlockSpec((1, d), lambda bi, i: (0, 0)),
                  pl.BlockSpec((1, d), lambda bi, i: (0, 0))],
        out_specs=pl.BlockSpec((1, tm, d), lambda bi, i: (bi, i, 0)),
        out_shape=jax.ShapeDtypeStruct((b, s, d), F32),
        compiler_params=_params("parallel", "parallel"),
        name="out_norm",
    )(a, w, x, gate, ln_g.reshape(1, d), ln_b.reshape(1, d))


FFN_HALO = 16
FFN_COLS = 256


def _ffn_up_kernel(x_ref, xh_ref, sc_ref, sh_ref, wu_ref, wg_ref, cw_ref, cb_ref, o_ref, h_ref):
    i = pl.program_id(1)
    sc = 1.0 + sc_ref[0]
    sh = sh_ref[0]
    h_ref[0:FFN_HALO, :] = (xh_ref[0] * sc + sh).astype(BF16)
    h_ref[FFN_HALO:, :] = (x_ref[0] * sc + sh).astype(BF16)
    keep = (i > 0).astype(F32)
    cw = cw_ref[...]
    for c in range(0, o_ref.shape[2], FFN_COLS):
        cols = slice(c, c + FFN_COLS)
        u = _dot(h_ref[...], wu_ref[:, cols])
        u = jnp.concatenate([u[0:FFN_HALO] * keep, u[FFN_HALO:]], axis=0)
        g = _dot(h_ref[FFN_HALO:, :], wg_ref[:, cols])
        a = u * cw[1:2, cols] + pltpu.roll(u, 1, 0) * cw[0:1, cols]
        conv = (u * cw[2:3, cols] + pltpu.roll(a, 1, 0))[FFN_HALO:] + cb_ref[:, cols]
        o_ref[0, :, cols] = (_silu(conv) * g).astype(o_ref.dtype)


def _ffn_up(x, scale, shift, w_up, conv_w, conv_b, tm=1024):
    b, s, d = x.shape
    f = conv_w.shape[1]
    assert s % tm == 0 and tm % FFN_HALO == 0 and f % FFN_COLS == 0
    hb = tm // FFN_HALO
    once = pl.Buffered(1)
    return pl.pallas_call(
        _ffn_up_kernel,
        grid=(b, s // tm),
        in_specs=[pl.BlockSpec((1, tm, d), lambda bi, i: (bi, i, 0)),
                  pl.BlockSpec((1, FFN_HALO, d), lambda bi, i: (bi, jnp.maximum(i * hb - 1, 0), 0)),
                  pl.BlockSpec((1, 1, d), lambda bi, i: (bi, 0, 0)),
                  pl.BlockSpec((1, 1, d), lambda bi, i: (bi, 0, 0)),
                  pl.BlockSpec((d, f), lambda bi, i: (0, 0), pipeline_mode=once),
                  pl.BlockSpec((d, f), lambda bi, i: (0, 1), pipeline_mode=once),
                  pl.BlockSpec((3, f), lambda bi, i: (0, 0)),
                  pl.BlockSpec((1, f), lambda bi, i: (0, 0))],
        out_specs=pl.BlockSpec((1, tm, f), lambda bi, i: (bi, i, 0)),
        out_shape=jax.ShapeDtypeStruct((b, s, f), BF16),
        scratch_shapes=[pltpu.VMEM((tm + FFN_HALO, d), BF16)],
        compiler_params=_params("parallel", "parallel"),
        name="ffn_up",
    )(x, x, scale, shift, w_up, w_up, conv_w, conv_b.reshape(1, f))


def _gla_kernel(q_ref, k_ref, v_ref, r_ref, a_ref, wa_ref, ba_ref, ng_ref, o_ref, st_ref):
    tc = q_ref.shape[1]
    nchunk = tc // GLA_CHUNK

    @pl.when(pl.program_id(2) == 0)
    def _():
        st_ref[...] = jnp.zeros_like(st_ref)

    z = _dot(a_ref[0], wa_ref[...]) + ba_ref[...]
    log_a = (jnp.minimum(z, 0.0) - jnp.log1p(jnp.exp(-jnp.abs(z)))) / GLA_TAU
    ci = lax.broadcasted_iota(jnp.int32, (GLA_CHUNK, GLA_CHUNK), 0)
    cj = lax.broadcasted_iota(jnp.int32, (GLA_CHUNK, GLA_CHUNK), 1)
    causal = cj <= ci
    tri = jnp.where(causal, 1.0, 0.0).astype(BF16)
    chunks = [slice(c * GLA_CHUNK, (c + 1) * GLA_CHUNK) for c in range(nchunk)]
    la_hi, la_lo = _split(log_a)
    cums = [_dot(tri, la_hi[sl]) + _dot(tri, la_lo[sl]) for sl in chunks]
    bcum = jnp.concatenate(cums, axis=0)
    btot = jnp.concatenate([jnp.broadcast_to(cm[GLA_CHUNK - 1:], cm.shape) for cm in cums], axis=0)
    qf = q_ref[0].astype(F32) * (GLA_DK ** -0.5)
    kf = k_ref[0].astype(F32)
    q_dec = (qf * jnp.exp(bcum)).astype(BF16)
    k_inv = (kf * jnp.exp(-bcum)).astype(BF16)
    k_dec = (kf * jnp.exp(btot - bcum)).astype(BF16)
    dl = jnp.exp(btot)
    v = v_ref[0]
    att = [jnp.where(causal, _dot_nt(q_dec[sl], k_inv[sl]), 0.0).astype(BF16) for sl in chunks]
    kv = [_dot_tn(v[sl], k_dec[sl]) for sl in chunks]
    o_intra = [_dot(att[c], v[chunks[c]]) for c in range(nchunk)]
    st = st_ref[...]
    o_inter = []
    for c, sl in enumerate(chunks):
        o_inter.append(_dot_nt(q_dec[sl], st.astype(BF16)))
        st = st * dl[c * GLA_CHUNK:c * GLA_CHUNK + 1] + kv[c]
    st_ref[...] = st
    o = jnp.concatenate(o_intra, axis=0) + jnp.concatenate(o_inter, axis=0)
    o = o * lax.rsqrt(jnp.mean(o * o, axis=-1, keepdims=True) + LN_EPS) * ng_ref[...]
    o_ref[0] = (o * _silu(r_ref[0].astype(F32))).astype(o_ref.dtype)


def _gla_core(proj, w_a2, b_a, norm_g, tc=1024):
    b, s, _ = proj.shape
    h, dk, dv = GLA_HEADS, GLA_DK, GLA_DV
    assert s % tc == 0
    wa = jnp.zeros((LANES, h * dk), F32).at[:GLA_GATE_RANK].set(w_a2).astype(BF16)
    kb, vb, rb, ab = (h * dk) // dk, (2 * h * dk) // dv, (2 * h * dk + h * dv) // dv, (2 * h * dk + 2 * h * dv) // LANES
    return pl.pallas_call(
        _gla_kernel,
        grid=(b, h, s // tc),
        in_specs=[pl.BlockSpec((1, tc, dk), lambda bi, hi, t: (bi, t, hi)),
                  pl.BlockSpec((1, tc, dk), lambda bi, hi, t: (bi, t, kb + hi)),
                  pl.BlockSpec((1, tc, dv), lambda bi, hi, t: (bi, t, vb + hi)),
                  pl.BlockSpec((1, tc, dv), lambda bi, hi, t: (bi, t, rb + hi)),
                  pl.BlockSpec((1, tc, LANES), lambda bi, hi, t: (bi, t, ab)),
                  pl.BlockSpec((LANES, dk), lambda bi, hi, t: (0, hi)),
                  pl.BlockSpec((1, dk), lambda bi, hi, t: (0, hi)),
                  pl.BlockSpec((1, dv), lambda bi, hi, t: (0, 0))],
        out_specs=pl.BlockSpec((1, tc, dv), lambda bi, hi, t: (bi, t, hi)),
        out_shape=jax.ShapeDtypeStruct((b, s, h * dv), BF16),
        scratch_shapes=[pltpu.VMEM((dv, dk), F32)],
        compiler_params=_params("parallel", "parallel", "arbitrary"),
        name="gla_core",
    )(proj, proj, proj, proj, proj, wa, b_a.reshape(1, h * dk), norm_g.reshape(1, dv))


def _pad_cols(w, n):
    return jnp.pad(w, ((0, 0), (0, n - w.shape[1])))


def _gla_mixer(x, scale, shift, w_in, w_a2, b_a, norm_g):
    n_in = 2 * GLA_HEADS * GLA_DK + 2 * GLA_HEADS * GLA_DV + LANES
    proj = _linear_in(x, scale, shift, _pad_cols(w_in, n_in).astype(BF16), tn=n_in)
    return _gla_core(proj, w_a2, b_a, norm_g)


def _compress_kernel(lo_ref, hi_ref, pe_ref, w1_ref, b1_ref, w2_ref, o_ref):
    half = lo_ref.shape[-1]
    pe = pe_ref[0]
    a_lo = (lo_ref[0, 0, 0].astype(F32) + pe[:, :half]).astype(BF16)
    a_hi = (hi_ref[0, 0, 0].astype(F32) + pe[:, half:]).astype(BF16)
    h1 = _dot(a_lo, w1_ref[0, :half, :]) + _dot(a_hi, w1_ref[0, half:, :]) + b1_ref[0]
    o_ref[0, 0, 0] = _dot(_silu(h1).astype(BF16), w2_ref[0])


def _compress(kv, pe, w1, b1, w2):
    two, b, s, _ = kv.shape
    g, dh = NSA_G, HEAD_DIM
    npc = s // CMP_STRIDE
    half = CMP_STRIDE * dh
    a = kv.reshape(two, b, npc, CMP_STRIDE, g, dh).transpose(0, 1, 4, 2, 3, 5).reshape(two, b, g, npc, half)
    a_hi = jnp.concatenate([a[:, :, :, 1:], jnp.zeros_like(a[:, :, :, :1])], axis=3)
    spec_a = pl.BlockSpec((1, 1, 1, npc, half), lambda c, bi, gi: (c, bi, gi, 0, 0))
    return pl.pallas_call(
        _compress_kernel,
        grid=(two, b, g),
        in_specs=[spec_a, spec_a,
                  pl.BlockSpec((1, 1, 2 * half), lambda c, bi, gi: (c, 0, 0)),
                  pl.BlockSpec((1, 2 * half, CMP_HIDDEN), lambda c, bi, gi: (c, 0, 0)),
                  pl.BlockSpec((1, 1, CMP_HIDDEN), lambda c, bi, gi: (c, 0, 0)),
                  pl.BlockSpec((1, CMP_HIDDEN, dh), lambda c, bi, gi: (c, 0, 0))],
        out_specs=pl.BlockSpec((1, 1, 1, npc, dh), lambda c, bi, gi: (c, bi, gi, 0, 0)),
        out_shape=jax.ShapeDtypeStruct((two, b, g, npc, dh), F32),
        compiler_params=_params("parallel", "parallel", "parallel"),
        name="nsa_compress",
    )(a, a_hi, pe.reshape(two, 1, 2 * half), w1.astype(BF16), b1.reshape(two, 1, CMP_HIDDEN), w2.astype(BF16))


def _stack_heads(q):
    return jnp.concatenate([q[:, r * HEAD_DIM:(r + 1) * HEAD_DIM] for r in range(NSA_R)], axis=0)


def _unstack_heads(o, rows):
    return jnp.concatenate([o[r * rows:(r + 1) * rows] for r in range(NSA_R)], axis=1)


CMP_WIDTHS = 4
CMP_TQ = 512
LOG2E = 1.4426950408889634


def _nsa_cmp_kernel(q_ref, fq_ref, kc_ref, vc_ref, nb_ref, pb_ref, o_ref, mb_ref, imp_ref, *, n_tile):
    i = pl.program_id(2)
    npc = nb_ref.shape[2]
    args = (q_ref, fq_ref, kc_ref, vc_ref, nb_ref, pb_ref, o_ref, imp_ref)
    quarter = n_tile // CMP_WIDTHS
    widths = [(k + 1) * npc // CMP_WIDTHS for k in range(CMP_WIDTHS)]
    nblk = mb_ref.shape[3]
    blocks = [min(nblk, -(-((k + 1) * nblk // CMP_WIDTHS) // LANES) * LANES) for k in range(CMP_WIDTHS)]

    @pl.when(i == 0)
    def _():
        _cmp_attend(i, widths[0], *args)

    for k in range(CMP_WIDTHS):
        @pl.when((i > 0) & (i >= k * quarter) & (i < (k + 1) * quarter))
        def _(k=k):
            imp = imp_ref[...]
            _cmp_attend(i, widths[k], *args)
            _cmp_select(i - 1, imp, mb_ref, blocks[k])

    @pl.when(i == n_tile)
    def _():
        _cmp_select(i - 1, imp_ref[...], mb_ref, nblk)


def _cmp_attend(i, width, q_ref, fq_ref, kc_ref, vc_ref, nb_ref, pb_ref, o_ref, imp_ref):
    skip = nb_ref.shape[2] - width
    n_sub, rows = CMP_TQ // SUB_Q, NSA_R * SUB_Q
    starts = [pl.multiple_of((i * n_sub + j) * (SUB_Q // CMP_STRIDE), 8) for j in range(n_sub)]

    scores = []
    for j in range(n_sub):
        kwin = kc_ref[0, 0, pl.ds(starts[j] + skip, width), :].astype(BF16)
        qs = _stack_heads(q_ref[0, j * SUB_Q:(j + 1) * SUB_Q, :]).astype(F32) * (HEAD_DIM ** -0.5 * LOG2E)
        qp = jnp.concatenate([qs, fq_ref[...]], axis=1).astype(BF16)
        scores.append(_dot_nt(qp, kwin) + nb_ref[0, :, skip:])
    s = jnp.concatenate(scores, axis=0)
    m = jnp.max(s, axis=1, keepdims=True)
    e = jnp.exp2(s - m)
    den = jnp.sum(e, axis=1, keepdims=True)
    p = e * jnp.where(m > 0.5 * NEG, 1.0 / den, 0.0)
    p_bf = p.astype(BF16)
    for j in range(n_sub):
        vwin = vc_ref[0, 0, pl.ds(starts[j] + skip, width), :].astype(BF16)
        o = _dot(p_bf[j * rows:(j + 1) * rows], vwin)
        o_ref[0, j * SUB_Q:(j + 1) * SUB_Q, :] = _unstack_heads(o, SUB_Q).astype(o_ref.dtype)

    imps = []
    for j in range(n_sub):
        pg = p[j * rows:j * rows + SUB_Q]
        for r in range(1, NSA_R):
            pg = pg + p[j * rows + r * SUB_Q:j * rows + (r + 1) * SUB_Q]
        imps.append(_dot_hi_exact_rhs(pg, pb_ref[pl.ds(starts[j] + skip, width), :].astype(BF16)))
    imp_ref[...] = jnp.concatenate(imps, axis=0)


def _cmp_select(i, imp, mb_ref, nblk):
    if nblk < mb_ref.shape[3]:
        imp = imp[:, :nblk]
        mb_ref[0, 0, :, nblk:] = jnp.full((CMP_TQ, mb_ref.shape[3] - nblk), NEG, mb_ref.dtype)
    blk = lax.broadcasted_iota(jnp.int32, (CMP_TQ, nblk), 1)
    blk_f = blk.astype(F32)
    t = i * CMP_TQ + lax.broadcasted_iota(jnp.int32, (CMP_TQ, nblk), 0)
    cur = t // SEL_LEN
    forced = (blk == 0) | (blk == cur) | (blk == cur - 1)
    work = jnp.where((blk * SEL_LEN <= t) & jnp.logical_not(forced), imp, NEG)
    free = work > 0.5 * NEG
    for _ in range(SEL_TOP - 3):
        mx = jnp.max(work, axis=1, keepdims=True)
        first = jnp.min(jnp.where(work == mx, blk_f, 1e9), axis=1, keepdims=True)
        work = jnp.where(blk_f == first, 2.0 * NEG, work)
    sel = forced | (free & (work < 1.5 * NEG))
    mb_ref[0, 0, :, :nblk] = jnp.where(sel, 0.0, NEG).astype(mb_ref.dtype)


def _nsa_cmp(proj, kcmp, vcmp, rel_table):
    b, s, _ = proj.shape
    g, dh = NSA_G, HEAD_DIM
    npc = s // CMP_STRIDE
    nblk = s // SEL_LEN
    front = npc - SUB_Q // CMP_STRIDE
    assert s % CMP_TQ == 0 and nblk >= SEL_TOP
    n_tile = s // CMP_TQ
    assert n_tile % CMP_WIDTHS == 0
    kpad = jnp.pad(kcmp, ((0, 0), (0, 0), (front, npc - front), (0, LANES - dh)))
    kpad = kpad.at[:, :, :front, dh].set(1.0)
    vpad = jnp.pad(vcmp, ((0, 0), (0, 0), (front, npc - front), (0, 0)))
    fq = jnp.zeros((NSA_R * SUB_Q, LANES - dh), F32).at[:, 0].set(NEG)
    qi = jnp.arange(SUB_Q)[:, None]
    cc = jnp.arange(npc)[None, :]
    dd = qi - (CMP_LEN - 1) + CMP_STRIDE * (front - cc)
    nb = jnp.where(dd[None] >= 0, _rel_bias(rel_table, dd) * LOG2E, NEG).reshape(g, NSA_R * SUB_Q, npc)
    diff = jnp.arange(2 * npc)[:, None] - front - (SEL_LEN // CMP_STRIDE) * jnp.arange(nblk)[None, :]
    pool = jnp.where((diff == -1) | (diff == 3), 1.0, jnp.where((diff >= 0) & (diff <= 2), 2.0, 0.0))
    return pl.pallas_call(
        functools.partial(_nsa_cmp_kernel, n_tile=n_tile),
        grid=(g, b, n_tile + 1),
        in_specs=[pl.BlockSpec((1, CMP_TQ, NSA_R * dh), lambda gi, bi, i: (bi, jnp.minimum(i, n_tile - 1), gi)),
                  pl.BlockSpec((NSA_R * SUB_Q, LANES - dh), lambda gi, bi, i: (0, 0)),
                  pl.BlockSpec((1, 1, 2 * npc, LANES), lambda gi, bi, i: (bi, gi, 0, 0)),
                  pl.BlockSpec((1, 1, 2 * npc, dh), lambda gi, bi, i: (bi, gi, 0, 0)),
                  pl.BlockSpec((1, NSA_R * SUB_Q, npc), lambda gi, bi, i: (gi, 0, 0)),
                  pl.BlockSpec((2 * npc, nblk), lambda gi, bi, i: (0, 0))],
        out_specs=[pl.BlockSpec((1, CMP_TQ, NSA_R * dh), lambda gi, bi, i: (bi, jnp.minimum(i, n_tile - 1), gi)),
                   pl.BlockSpec((1, 1, CMP_TQ, nblk), lambda gi, bi, i: (bi, gi, jnp.maximum(i - 1, 0), 0))],
        out_shape=[jax.ShapeDtypeStruct((b, s, N_HEADS * dh), BF16),
                   jax.ShapeDtypeStruct((b, g, s, nblk), BF16)],
        scratch_shapes=[pltpu.VMEM((CMP_TQ, nblk), F32)],
        compiler_params=_params("parallel", "parallel", "arbitrary"),
        name="nsa_cmp_select",
    )(proj, fq, kpad, vpad, nb, pool.astype(F32))


SEL_T = 256
SEL_WIDE = 8
SEL_NEAR = -(-(FAR_DIST + SEL_T - 1) // SEL_T)


def _nsa_sel_kernel(q_ref, fq_ref, mb_ref, k_ref, v_ref, bt_ref, o_ref, qp_ref, m_ref, acc_ref):
    i = pl.program_id(2)
    rows = NSA_R * SEL_T
    fw = (k_ref.shape[3] - LANES)
    kh = fw // (SEL_T // SEL_LEN)
    qs = (_stack_heads(q_ref[0]).astype(F32) * (HEAD_DIM ** -0.5 * LOG2E)).astype(BF16)
    mb = mb_ref[0, 0]
    for half in range(2):
        mbh = jnp.concatenate([mb[:, half * fw:(half + 1) * fw]] * NSA_R, axis=0)
        qp_ref[half] = jnp.concatenate([qs, fq_ref[0], mbh], axis=1)
    m_ref[...] = jnp.full_like(m_ref, NEG)
    acc_ref[...] = jnp.zeros_like(acc_ref)

    def update(kt, n_tile, half, bias):
        k0 = pl.multiple_of(kt * SEL_T, SEL_T)
        kk = k_ref[0, 0, pl.ds(k0, n_tile * SEL_T), :]
        vv = v_ref[0, 0, pl.ds(k0, n_tile * SEL_T), :]
        s = _dot_nt(qp_ref[half], kk)
        if bias is not None:
            s = s + bias
        m_prev = m_ref[...]
        m_new = jnp.maximum(m_prev, jnp.max(s, axis=1, keepdims=True))
        alpha = jnp.exp2(m_prev - m_new)
        p = jnp.exp2(s - jnp.concatenate([m_new] * (n_tile * SEL_T // LANES), axis=1))
        acc_ref[...] = acc_ref[...] * alpha + _dot(p.astype(BF16), vv)
        m_ref[...] = m_new

    def far_region(lo, hi, half):
        n = jnp.maximum(hi - lo, 0)

        def wide(j, carry):
            update(lo + SEL_WIDE * j, SEL_WIDE, half, None)
            return carry

        lax.fori_loop(0, n // SEL_WIDE, wide, 0)
        piece = SEL_WIDE // 2
        while piece >= 1:
            @pl.when((n & piece) != 0)
            def _(piece=piece):
                update(lo + (n & ~(2 * piece - 1)), piece, half, None)
            piece //= 2

    n_far = jnp.maximum(i - SEL_NEAR + 1, 0)
    far_region(0, jnp.minimum(n_far, kh), 0)
    far_region(kh, n_far, 1)

    def near(e):
        kt = i - e
        update(kt, 1, jnp.where(kt < kh, 0, 1), bt_ref[0, :, e * SEL_T:(e + 1) * SEL_T])

    @pl.when(i >= SEL_NEAR - 1)
    def _():
        for e in range(SEL_NEAR - 1, -1, -1):
            near(e)

    @pl.when(i < SEL_NEAR - 1)
    def _():
        for e in range(SEL_NEAR - 2, -1, -1):
            @pl.when(i - e >= 0)
            def _(e=e):
                near(e)

    acc = acc_ref[...]
    o = acc[:, :HEAD_DIM] / acc[:, HEAD_DIM:HEAD_DIM + 1]
    o_ref[0] = _unstack_heads(o, SEL_T).astype(o_ref.dtype)


def _nsa_sel(proj, ks, vs, mb, rel_table):
    b, s, _ = proj.shape
    g, dh = NSA_G, HEAD_DIM
    nblk = s // SEL_LEN
    fw = nblk // 2
    rows = NSA_R * SEL_T
    pos = jnp.arange(s)
    onehot = (((pos // SEL_LEN) % fw)[:, None] == jnp.arange(fw)[None, :]).astype(BF16)
    ones2 = jnp.zeros((s, LANES - dh), BF16).at[:, :2].set(1.0)
    kp = jnp.concatenate([ks, jnp.broadcast_to(ones2, (b, g, s, LANES - dh)),
                          jnp.broadcast_to(onehot, (b, g, s, fw))], axis=-1)
    ones1 = jnp.zeros((s, LANES - dh), BF16).at[:, :1].set(1.0)
    vp = jnp.concatenate([vs, jnp.broadcast_to(ones1, (b, g, s, LANES - dh))], axis=-1)
    b_far = rel_table[REL_BUCKETS - 1].reshape(g, NSA_R) * LOG2E
    hi = b_far.astype(BF16)
    lo = (b_far - hi.astype(F32)).astype(BF16)
    fq = jnp.zeros((g, NSA_R, SEL_T, LANES - dh), BF16)
    fq = fq.at[:, :, :, 0].set(hi[:, :, None]).at[:, :, :, 1].set(lo[:, :, None]).reshape(g, rows, LANES - dh)
    qi = jnp.arange(SEL_T)[:, None]
    cc = jnp.arange(SEL_NEAR * SEL_T)[None, :]
    dd = (cc // SEL_T) * SEL_T + qi - (cc % SEL_T)
    bias = _rel_bias(rel_table, dd).reshape(g, NSA_R, SEL_T, -1) * LOG2E - b_far[:, :, None, None]
    bt = jnp.where(dd[None, None] >= 0, bias, NEG).reshape(g, rows, SEL_NEAR * SEL_T)
    return pl.pallas_call(
        _nsa_sel_kernel,
        grid=(g, b, s // SEL_T),
        in_specs=[pl.BlockSpec((1, SEL_T, NSA_R * dh), lambda gi, bi, i: (bi, i, gi)),
                  pl.BlockSpec((1, rows, LANES - dh), lambda gi, bi, i: (gi, 0, 0)),
                  pl.BlockSpec((1, 1, SEL_T, nblk), lambda gi, bi, i: (bi, gi, i, 0)),
                  pl.BlockSpec((1, 1, s, LANES + fw), lambda gi, bi, i: (bi, gi, 0, 0)),
                  pl.BlockSpec((1, 1, s, LANES), lambda gi, bi, i: (bi, gi, 0, 0)),
                  pl.BlockSpec((1, rows, SEL_NEAR * SEL_T), lambda gi, bi, i: (gi, 0, 0),
                               pipeline_mode=pl.Buffered(1))],
        out_specs=pl.BlockSpec((1, SEL_T, NSA_R * dh), lambda gi, bi, i: (bi, i, gi)),
        out_shape=jax.ShapeDtypeStruct((b, s, N_HEADS * dh), BF16),
        scratch_shapes=[pltpu.VMEM((2, rows, LANES + fw), BF16), pltpu.VMEM((rows, LANES), F32),
                        pltpu.VMEM((rows, LANES), F32)],
        compiler_params=_params("parallel", "parallel", "parallel"),
        name="nsa_selected",
    )(proj, fq, mb, kp, vp, bt)


WIN_T = 512


def _softmax2_rows(s):
    m = jnp.max(s, axis=1, keepdims=True)
    e = jnp.exp2(s - m)
    den = jnp.sum(e, axis=1, keepdims=True)
    return e, 1.0 / den, m * (1.0 / LOG2E) + jnp.log(den)


def _nsa_win_kernel(q_ref, kp_ref, kc_ref, vp_ref, vc_ref, bt_ref, o_ref):
    i = pl.program_id(2)
    back = WIN_T
    span = back + SUB_Q
    n_sub, rows = WIN_T // SUB_Q, NSA_R * SUB_Q

    def body(first_tile):
        kcat = jnp.concatenate([kp_ref[0, 0], kc_ref[0, 0]], axis=0)
        vcat = jnp.concatenate([vp_ref[0, 0], vc_ref[0, 0]], axis=0)
        col = lax.broadcasted_iota(jnp.int32, (rows, span), 1)
        scores = []
        for j in range(n_sub):
            qs = _stack_heads(q_ref[0, j * SUB_Q:(j + 1) * SUB_Q, :]).astype(F32) * (HEAD_DIM ** -0.5 * LOG2E)
            s = _dot_nt(qs.astype(BF16), kcat[j * SUB_Q:j * SUB_Q + span]) + bt_ref[0]
            if first_tile:
                s = jnp.where(col >= back - j * SUB_Q, s, NEG)
            scores.append(s)
        e, inv, _ = _softmax2_rows(jnp.concatenate(scores, axis=0))
        e = e.astype(BF16)
        for j in range(n_sub):
            o = _dot(e[j * rows:(j + 1) * rows], vcat[j * SUB_Q:j * SUB_Q + span]) * inv[j * rows:(j + 1) * rows]
            o_ref[0, j * SUB_Q:(j + 1) * SUB_Q, :] = _unstack_heads(o, SUB_Q).astype(o_ref.dtype)

    @pl.when(i == 0)
    def _():
        body(True)

    @pl.when(i > 0)
    def _():
        body(False)


def _nsa_win(proj, kw, vw, rel_table):
    b, s, _ = proj.shape
    g, dh = NSA_G, HEAD_DIM
    span = WIN_T + SUB_Q
    qi = jnp.arange(SUB_Q)[:, None]
    cc = jnp.arange(span)[None, :]
    dd = qi + WIN_T - cc
    bias = _rel_bias(rel_table, dd)
    bt = jnp.where((dd[None] >= 0) & (dd[None] <= WIN_LEN - 1), bias * LOG2E, NEG).reshape(g, NSA_R * SUB_Q, span)
    prev = lambda gi, bi, i: (bi, gi, jnp.maximum(i - 1, 0), 0)
    cur = lambda gi, bi, i: (bi, gi, i, 0)
    return pl.pallas_call(
        _nsa_win_kernel,
        grid=(g, b, s // WIN_T),
        in_specs=[pl.BlockSpec((1, WIN_T, NSA_R * dh), lambda gi, bi, i: (bi, i, gi)),
                  pl.BlockSpec((1, 1, WIN_T, dh), prev), pl.BlockSpec((1, 1, WIN_T, dh), cur),
                  pl.BlockSpec((1, 1, WIN_T, dh), prev), pl.BlockSpec((1, 1, WIN_T, dh), cur),
                  pl.BlockSpec((1, NSA_R * SUB_Q, span), lambda gi, bi, i: (gi, 0, 0))],
        out_specs=pl.BlockSpec((1, WIN_T, NSA_R * dh), lambda gi, bi, i: (bi, i, gi)),
        out_shape=jax.ShapeDtypeStruct((b, s, N_HEADS * dh), BF16),
        compiler_params=_params("parallel", "parallel", "parallel"),
        name="nsa_window",
    )(proj, kw, kw, vw, vw, bt)


def _nsa_combine_kernel(oc_ref, os_ref, ow_ref, g_ref, e_ref, o_ref):
    d = o_ref.shape[2]
    sg = jax.nn.sigmoid(g_ref[0].astype(F32))
    ge = _dot_hi_exact_rhs(sg, e_ref[...])
    o = (ge[:, :d] * oc_ref[0].astype(F32) + ge[:, d:2 * d] * os_ref[0].astype(F32)
         + ge[:, 2 * d:] * ow_ref[0].astype(F32))
    o_ref[0] = o.astype(o_ref.dtype)


def _nsa_combine(o_cmp, o_sel, o_win, proj, gate_col, tm=512):
    b, s, d = o_cmp.shape
    c = jnp.arange(LANES)[:, None]
    n = jnp.arange(3 * d)[None, :]
    expand = (c == (n // d) * N_HEADS + (n % d) // HEAD_DIM).astype(BF16)
    spec = pl.BlockSpec((1, tm, d), lambda bi, i: (bi, i, 0))
    return pl.pallas_call(
        _nsa_combine_kernel,
        grid=(b, s // tm),
        in_specs=[spec, spec, spec,
                  pl.BlockSpec((1, tm, LANES), lambda bi, i: (bi, i, gate_col // LANES)),
                  pl.BlockSpec((LANES, 3 * d), lambda bi, i: (0, 0))],
        out_specs=spec,
        out_shape=jax.ShapeDtypeStruct((b, s, d), BF16),
        compiler_params=_params("parallel", "parallel"),
        name="nsa_combine",
    )(o_cmp, o_sel, o_win, proj, expand)


def _nsa_mixer(x, scale, shift, w_in, pe, w1, b1, w2, rel_table):
    b, s, _ = x.shape
    hd, gd = N_HEADS * HEAD_DIM, NSA_G * HEAD_DIM
    gate_col = hd + 6 * gd
    n_in = gate_col + LANES
    proj = _linear_in(x, scale, shift, _pad_cols(w_in, n_in).astype(BF16), tn=n_in)

    def group_major(t):
        return t.reshape(b, s, NSA_G, HEAD_DIM).transpose(0, 2, 1, 3)

    kc, vc, ks, vs, kw, vw = (proj[:, :, hd + n * gd: hd + (n + 1) * gd] for n in range(6))
    cmp = _compress(jnp.stack([kc, vc]), pe.reshape(2, -1), w1, b1, w2)
    o_cmp, mb = _nsa_cmp(proj, cmp[0], cmp[1], rel_table)
    o_sel = _nsa_sel(proj, group_major(ks), group_major(vs), mb, rel_table)
    o_win = _nsa_win(proj, group_major(kw), group_major(vw), rel_table)
    return _nsa_combine(o_cmp, o_sel, o_win, proj, gate_col)


DIL_BACK = 128
DIL_W = 256


def _dil_kernel(q_ref, kp_ref, kc_ref, vp_ref, vc_ref, bt_ref, o_ref, l_ref):
    t = pl.program_id(3)
    tq = q_ref.shape[1]
    span = DIL_BACK + SUB_Q
    kcat = jnp.concatenate([kp_ref[0, tq - DIL_BACK:, :], kc_ref[0]], axis=0)
    vcat = jnp.concatenate([vp_ref[0, tq - DIL_BACK:, :], vc_ref[0]], axis=0)
    lane_q = lax.broadcasted_iota(jnp.int32, (SUB_Q, DIL_W), 1) // HEAD_DIM
    lane_k = lax.broadcasted_iota(jnp.int32, (span, DIL_W), 1) // HEAD_DIM
    col = lax.broadcasted_iota(jnp.int32, (SUB_Q, span), 1)
    n_sub, n_head = tq // SUB_Q, DIL_W // HEAD_DIM
    scores = []
    for j in range(n_sub):
        qj = (q_ref[0, j * SUB_Q:(j + 1) * SUB_Q, :].astype(F32) * (HEAD_DIM ** -0.5 * LOG2E)).astype(BF16)
        kk = kcat[j * SUB_Q:j * SUB_Q + span]
        for hh in range(n_head):
            s = _dot_nt(jnp.where(lane_q == hh, qj, 0.0).astype(BF16), kk) + bt_ref[hh]
            if j == 0:
                s = jnp.where((col >= DIL_BACK) | (t > 0), s, NEG)
            scores.append(s)
    e_all, inv_all, lse_all = _softmax2_rows(jnp.concatenate(scores, axis=0))
    e_all = e_all.astype(BF16)
    for j in range(n_sub):
        vv = vcat[j * SUB_Q:j * SUB_Q + span]
        o = jnp.zeros((SUB_Q, DIL_W), F32)
        inv = jnp.zeros((SUB_Q, DIL_W), F32)
        lse = jnp.zeros((SUB_Q, DIL_W), F32)
        for hh in range(n_head):
            r0 = (j * n_head + hh) * SUB_Q
            o = o + _dot(e_all[r0:r0 + SUB_Q], jnp.where(lane_k == hh, vv, 0.0).astype(BF16))
            inv = jnp.where(lane_q == hh, inv_all[r0:r0 + SUB_Q], inv)
            lse = jnp.where(lane_q == hh, lse_all[r0:r0 + SUB_Q], lse)
        o_ref[0, j * SUB_Q:(j + 1) * SUB_Q, :] = (o * inv).astype(o_ref.dtype)
        l_ref[0, j * SUB_Q:(j + 1) * SUB_Q, :] = lse


def _dil_group(view, window, dil, rel_table):
    b, l, n_all = view.shape
    hd = N_HEADS * HEAD_DIM
    assert window // dil == DIL_BACK and n_all == dil * 3 * hd
    tq = min(1024, l)
    assert l % tq == 0 and tq >= DIL_BACK
    hpb = hd // DIL_W
    qi = jnp.arange(SUB_Q)[:, None]
    cc = jnp.arange(DIL_BACK + SUB_Q)[None, :]
    dd = qi + DIL_BACK - cc
    bt = jnp.where((dd[None] >= 0) & (dd[None] <= DIL_BACK), _rel_bias(rel_table, dd * dil) * LOG2E, NEG)

    def spec(comp, prev):
        def index(bi, r, hp, t):
            return (bi, jnp.maximum(t - 1, 0) if prev else t, (r * 3 + comp) * hpb + hp)
        return pl.BlockSpec((1, tq, DIL_W), index)

    out_spec = pl.BlockSpec((1, tq, DIL_W), lambda bi, r, hp, t: (bi, t, r * hpb + hp))
    return pl.pallas_call(
        _dil_kernel,
        grid=(b, dil, hpb, l // tq),
        in_specs=[spec(0, False), spec(1, True), spec(1, False), spec(2, True), spec(2, False),
                  pl.BlockSpec((DIL_W // HEAD_DIM, SUB_Q, DIL_BACK + SUB_Q), lambda bi, r, hp, t: (hp, 0, 0))],
        out_specs=[out_spec, out_spec],
        out_shape=[jax.ShapeDtypeStruct((b, l, dil * hd), BF16), jax.ShapeDtypeStruct((b, l, dil * hd), F32)],
        compiler_params=_params("parallel", "parallel", "parallel", "parallel"),
        name=f"dilated_attention_{dil}",
    )(view, view, view, view, view, bt)


def _dil_merge_kernel(*refs, dils):
    n = len(dils)
    o_refs, l_refs, out_ref, scratch = refs[:n], refs[n:2 * n], refs[2 * n], refs[2 * n + 1:]
    tm, d = out_ref.shape[1], out_ref.shape[2]

    def token_major(ref, dil, buf):
        if dil == 1:
            return ref[0].astype(F32)
        for r in range(dil):
            for c in range(d // LANES):
                buf[c, pl.ds(r, tm // dil, stride=dil), :] = (
                    ref[0, :, r * d + c * LANES:r * d + (c + 1) * LANES].astype(F32))
        return jnp.concatenate([buf[c] for c in range(d // LANES)], axis=1)

    outs = [token_major(o_refs[g], dils[g], scratch[2 * g]) for g in range(n)]
    lses = [token_major(l_refs[g], dils[g], scratch[2 * g + 1]) for g in range(n)]
    m = functools.reduce(jnp.maximum, lses)
    es = [jnp.exp(l - m) for l in lses]
    inv = 1.0 / functools.reduce(jnp.add, es)
    out_ref[0] = functools.reduce(jnp.add, [e * inv * o for e, o in zip(es, outs)]).astype(out_ref.dtype)


def _dil_merge(outs, lses, dils, tm=512):
    b = outs[0].shape[0]
    d = N_HEADS * HEAD_DIM
    s = outs[0].shape[1] * dils[0]
    specs = [pl.BlockSpec((1, tm // dil, dil * d), lambda bi, i: (bi, i, 0)) for dil in dils]
    return pl.pallas_call(
        functools.partial(_dil_merge_kernel, dils=tuple(dils)),
        grid=(b, s // tm),
        in_specs=specs + specs,
        out_specs=pl.BlockSpec((1, tm, d), lambda bi, i: (bi, i, 0)),
        out_shape=jax.ShapeDtypeStruct((b, s, d), BF16),
        scratch_shapes=[pltpu.VMEM((d // LANES, tm, LANES), F32) for _ in range(2 * len(dils))],
        compiler_params=_params("parallel", "parallel"),
        name="dilated_merge",
    )(*outs, *lses)


def _dil_mixer(x, scale, shift, w_in, rel_table):
    hd = N_HEADS * HEAD_DIM
    w = w_in.astype(BF16)
    outs, lses, dils = [], [], []
    for gi, (window, dil) in enumerate(DIL_GROUPS):
        view = _linear_in(x, scale, shift, w[:, gi * 3 * hd:(gi + 1) * 3 * hd], tn=3 * hd, dil=dil)
        o, lse = _dil_group(view, window, dil, rel_table)
        outs.append(o)
        lses.append(lse)
        dils.append(dil)
    return _dil_merge(outs, lses, dils)


def kernel(x, c, rel_table, mod_w, mod_b, ln_g, ln_b, gla_w_in, gla_w_a2, gla_b_a, gla_norm_g, gla_w_o,
           nsa_w_in, nsa_cmp_pe, nsa_cmp_w1, nsa_cmp_b1, nsa_cmp_w2, nsa_w_o, dil_w_in, dil_w_o,
           ffn_w_up, ffn_conv_w, ffn_conv_b, ffn_w_down):
    shift, scale, gate = _modulation(c, mod_w, mod_b)
    for i in range(DEPTH):
        kind, j = i % N_MIXERS, i // N_MIXERS
        m = 2 * i
        if kind == 0:
            a = _gla_mixer(x, scale[m], shift[m], gla_w_in[j], gla_w_a2[j], gla_b_a[j], gla_norm_g[j])
            w_o = gla_w_o[j]
        elif kind == 1:
            a = _nsa_mixer(x, scale[m], shift[m], nsa_w_in[j], nsa_cmp_pe[j], nsa_cmp_w1[j], nsa_cmp_b1[j],
                           nsa_cmp_w2[j], rel_table)
            w_o = nsa_w_o[j]
        else:
            a = _dil_mixer(x, scale[m], shift[m], dil_w_in[j], rel_table)
            w_o = dil_w_o[j]
        x = _out_norm(a, w_o.astype(BF16), x, gate[m], ln_g[i, 0], ln_b[i, 0])
        a = _ffn_up(x, scale[m + 1], shift[m + 1], ffn_w_up[i].astype(BF16), ffn_conv_w[i], ffn_conv_b[i])
        x = _out_norm(a, ffn_w_down[i].astype(BF16), x, gate[m + 1], ln_g[i, 1], ln_b[i, 1])
    return x
```

```python
import functools
import math

import jax
import jax.numpy as jnp
import numpy as np
from jax import lax
from jax.experimental import pallas as pl
from jax.experimental.pallas import tpu as pltpu

F32 = jnp.float32
BF16 = jnp.bfloat16

D_MODEL = 1024
DEPTH = 4
N_MIXERS = 3
HEAD_DIM = 64
N_HEADS = D_MODEL // HEAD_DIM
REL_BUCKETS = 32
REL_MAX_DIST = 2048
LN_EPS = 1e-5
GLA_HEADS = 4
GLA_DK = D_MODEL // 2 // GLA_HEADS
GLA_DV = D_MODEL // GLA_HEADS
GLA_GATE_RANK = 16
GLA_TAU = 16.0
GLA_CHUNK = 64
NSA_G = 4
NSA_R = N_HEADS // NSA_G
CMP_LEN = 32
CMP_STRIDE = 16
CMP_HIDDEN = 256
SEL_LEN = 64
SEL_TOP = 16
WIN_LEN = 512
DIL_GROUPS = ((128, 1), (512, 4), (2048, 16))
D_FF = 2816
DN_ALPHA = (2 * DEPTH) ** 0.25

NEG = -1e30
VMEM_LIMIT = 56 * 1024 * 1024
LANES = 128
SUB_Q = 128


def _params(*sem):
    return pltpu.CompilerParams(dimension_semantics=sem, vmem_limit_bytes=VMEM_LIMIT)


def _dot(a, b):
    return jnp.dot(a, b, preferred_element_type=F32)


def _dot_nt(a, b):
    return lax.dot_general(a, b, (((1,), (1,)), ((), ())), preferred_element_type=F32)


def _dot_tn(a, b):
    return lax.dot_general(a, b, (((0,), (0,)), ((), ())), preferred_element_type=F32)


def _split(a):
    hi = a.astype(BF16)
    lo = (a - hi.astype(F32)).astype(BF16)
    return hi, lo


def _dot_hi_exact_rhs(a, b_bf16):
    hi, lo = _split(a)
    return _dot(hi, b_bf16) + _dot(lo, b_bf16)


def _silu(x):
    return x * jax.nn.sigmoid(x)


def _bucket_starts():
    n = np.arange(2 * REL_MAX_DIST)
    exact = REL_BUCKETS // 2
    logn = np.log(np.maximum(n, 1).astype(np.float32) / np.float32(exact))
    large = exact + (logn / np.float32(math.log(REL_MAX_DIST / exact)) * np.float32(REL_BUCKETS - exact)).astype(np.int32)
    bucket = np.where(n < exact, n, np.minimum(large, REL_BUCKETS - 1))
    assert np.all(np.diff(bucket) >= 0) and bucket[-1] == REL_BUCKETS - 1
    return [int(np.argmax(bucket >= k)) for k in range(REL_BUCKETS)]


BUCKET_START = _bucket_starts()
FAR_DIST = BUCKET_START[-1]


def _rel_bias(rel_table, dist):
    shape = (rel_table.shape[1],) + (1,) * dist.ndim
    out = jnp.broadcast_to(rel_table[0].reshape(shape), (rel_table.shape[1],) + dist.shape)
    for k in range(1, REL_BUCKETS):
        out = jnp.where(dist[None] >= BUCKET_START[k], rel_table[k].reshape(shape), out)
    return out


def _mod_kernel(c_ref, w_ref, b_ref, o_ref):
    cs = _silu(c_ref[...])
    ch, cl = _split(cs)
    wh, wl = _split(w_ref[0])
    o_ref[0] = _dot(ch, wh) + _dot(ch, wl) + _dot(cl, wh) + b_ref[0]


def _modulation(c, mod_w, mod_b):
    b, d = c.shape
    n_sub = mod_w.shape[0] * mod_w.shape[1]
    rows = 8
    cp = jnp.zeros((rows, d), F32).at[:b].set(c)
    w = mod_w.reshape(n_sub, d, 3 * d)
    bias = mod_b.reshape(n_sub, 1, 3 * d)
    tn = d
    out = pl.pallas_call(
        _mod_kernel,
        grid=(n_sub, 3 * d // tn),
        in_specs=[pl.BlockSpec((rows, d), lambda i, j: (0, 0)),
                  pl.BlockSpec((1, d, tn), lambda i, j: (i, 0, j)),
                  pl.BlockSpec((1, 1, tn), lambda i, j: (i, 0, j))],
        out_specs=pl.BlockSpec((1, rows, tn), lambda i, j: (i, 0, j)),
        out_shape=jax.ShapeDtypeStruct((n_sub, rows, 3 * d), F32),
        compiler_params=_params("parallel", "parallel"),
        name="modulation",
    )(cp, w, bias)
    out = out[:, :b]
    shift, scale, gate = jnp.split(out, 3, axis=-1)
    return shift[:, :, None, :], scale[:, :, None, :], gate[:, :, None, :]


def _linear_in_kernel(x_ref, sc_ref, sh_ref, w_ref, *rest, dil):
    h = (x_ref[0] * (1.0 + sc_ref[0]) + sh_ref[0]).astype(BF16)
    if dil == 1:
        o_ref, = rest
        o_ref[0] = _dot(h, w_ref[...]).astype(o_ref.dtype)
    else:
        perm_ref, o_ref = rest
        res = _dot(_dot(perm_ref[...], h).astype(BF16), w_ref[...])
        rows, n = res.shape[0] // dil, res.shape[1]
        for r in range(dil):
            o_ref[0, :, r * n:(r + 1) * n] = res[r * rows:(r + 1) * rows].astype(o_ref.dtype)


def _linear_in(x, scale, shift, w, tn, dil=1):
    b, s, d = x.shape
    n = w.shape[1]
    tm = min(s, 1024 if dil == 1 else 512)
    assert n % tn == 0 and s % tm == 0 and tm % (16 * dil) == 0 and (dil == 1 or n == tn)
    in_specs = [pl.BlockSpec((1, tm, d), lambda j, bi, i: (bi, i, 0)),
                pl.BlockSpec((1, 1, d), lambda j, bi, i: (bi, 0, 0)),
                pl.BlockSpec((1, 1, d), lambda j, bi, i: (bi, 0, 0)),
                pl.BlockSpec((d, tn), lambda j, bi, i: (0, j), pipeline_mode=pl.Buffered(1) if n == tn else None)]
    args = [x, scale, shift, w]
    if dil > 1:
        out_row = jnp.arange(tm)
        src = (out_row % (tm // dil)) * dil + out_row // (tm // dil)
        in_specs.append(pl.BlockSpec((tm, tm), lambda j, bi, i: (0, 0)))
        args.append((src[:, None] == jnp.arange(tm)[None, :]).astype(BF16))
    return pl.pallas_call(
        functools.partial(_linear_in_kernel, dil=dil),
        grid=(n // tn, b, s // tm),
        in_specs=in_specs,
        out_specs=pl.BlockSpec((1, tm // dil, dil * tn), lambda j, bi, i: (bi, i, j)),
        out_shape=jax.ShapeDtypeStruct((b, s // dil, dil * n), BF16),
        compiler_params=_params("parallel", "parallel", "parallel"),
        name="linear_in",
    )(*args)


OUT_ROWS = 512


def _out_norm_kernel(a_ref, w_ref, x_ref, gate_ref, g_ref, b_ref, o_ref):
    chunks = [slice(r, r + OUT_ROWS) for r in range(0, a_ref.shape[1], OUT_ROWS)]
    ys = [_dot(a_ref[0, sl, :], w_ref[...]) for sl in chunks]
    for sl, y in zip(chunks, ys):
        z = DN_ALPHA * x_ref[0, sl, :] + (1.0 + gate_ref[0]) * y
        mu = jnp.mean(z, axis=-1, keepdims=True)
        zc = z - mu
        var = jnp.mean(zc * zc, axis=-1, keepdims=True)
        o_ref[0, sl, :] = zc * lax.rsqrt(var + LN_EPS) * g_ref[...] + b_ref[...]


def _out_norm(a, w, x, gate, ln_g, ln_b):
    b, s, d = x.shape
    k = a.shape[-1]
    tm = min(s, 2048 if k <= d else 1024)
    return pl.pallas_call(
        _out_norm_kernel,
        grid=(b, s // tm),
        in_specs=[pl.BlockSpec((1, tm, k), lambda bi, i: (bi, i, 0)),
                  pl.BlockSpec((k, d), lambda bi, i: (0, 0)),
                  pl.BlockSpec((1, tm, d), lambda bi, i: (bi, i, 0)),
                  pl.BlockSpec((1, 1, d), lambda bi, i: (bi, 0, 0)),
                  pl.BlockSpec((1, d), lambda bi, i: (0, 0)),
                  pl.BlockSpec((1, d), lambda bi, i: (0, 0))],
        out_specs=pl.BlockSpec((1, tm, d), lambda bi, i: (bi, i, 0)),
        out_shape=jax.ShapeDtypeStruct((b, s, d), F32),
        compiler_params=_params("parallel", "parallel"),
        name="out_norm",
    )(a, w, x, gate, ln_g.reshape(1, d), ln_b.reshape(1, d))


FFN_HALO = 16
FFN_COLS = 256


def _ffn_up_kernel(x_ref, xh_ref, sc_ref, sh_ref, wu_ref, wg_ref, cw_ref, cb_ref, o_ref, h_ref):
    i = pl.program_id(1)
    sc = 1.0 + sc_ref[0]
    sh = sh_ref[0]
    h_ref[0:FFN_HALO, :] = (xh_ref[0] * sc + sh).astype(BF16)
    h_ref[FFN_HALO:, :] = (x_ref[0] * sc + sh).astype(BF16)
    keep = (i > 0).astype(F32)
    cw = cw_ref[...]
    for c in range(0, o_ref.shape[2], FFN_COLS):
        cols = slice(c, c + FFN_COLS)
        u = _dot(h_ref[...], wu_ref[:, cols])
        u = jnp.concatenate([u[0:FFN_HALO] * keep, u[FFN_HALO:]], axis=0)
        g = _dot(h_ref[FFN_HALO:, :], wg_ref[:, cols])
        a = u * cw[1:2, cols] + pltpu.roll(u, 1, 0) * cw[0:1, cols]
        conv = (u * cw[2:3, cols] + pltpu.roll(a, 1, 0))[FFN_HALO:] + cb_ref[:, cols]
        o_ref[0, :, cols] = (_silu(conv) * g).astype(o_ref.dtype)


def _ffn_up(x, scale, shift, w_up, conv_w, conv_b, tm=1024):
    b, s, d = x.shape
    f = conv_w.shape[1]
    assert s % tm == 0 and tm % FFN_HALO == 0 and f % FFN_COLS == 0
    hb = tm // FFN_HALO
    once = pl.Buffered(1)
    return pl.pallas_call(
        _ffn_up_kernel,
        grid=(b, s // tm),
        in_specs=[pl.BlockSpec((1, tm, d), lambda bi, i: (bi, i, 0)),
                  pl.BlockSpec((1, FFN_HALO, d), lambda bi, i: (bi, jnp.maximum(i * hb - 1, 0), 0)),
                  pl.BlockSpec((1, 1, d), lambda bi, i: (bi, 0, 0)),
                  pl.BlockSpec((1, 1, d), lambda bi, i: (bi, 0, 0)),
                  pl.BlockSpec((d, f), lambda bi, i: (0, 0), pipeline_mode=once),
                  pl.BlockSpec((d, f), lambda bi, i: (0, 1), pipeline_mode=once),
                  pl.BlockSpec((3, f), lambda bi, i: (0, 0)),
                  pl.BlockSpec((1, f), lambda bi, i: (0, 0))],
        out_specs=pl.BlockSpec((1, tm, f), lambda bi, i: (bi, i, 0)),
        out_shape=jax.ShapeDtypeStruct((b, s, f), BF16),
        scratch_shapes=[pltpu.VMEM((tm + FFN_HALO, d), BF16)],
        compiler_params=_params("parallel", "parallel"),
        name="ffn_up",
    )(x, x, scale, shift, w_up, w_up, conv_w, conv_b.reshape(1, f))


def _gla_kernel(q_ref, k_ref, v_ref, r_ref, a_ref, wa_ref, ba_ref, ng_ref, o_ref, st_ref):
    tc = q_ref.shape[1]
    nchunk = tc // GLA_CHUNK

    @pl.when(pl.program_id(2) == 0)
    def _():
        st_ref[...] = jnp.zeros_like(st_ref)

    z = _dot(a_ref[0], wa_ref[...]) + ba_ref[...]
    log_a = (jnp.minimum(z, 0.0) - jnp.log1p(jnp.exp(-jnp.abs(z)))) / GLA_TAU
    ci = lax.broadcasted_iota(jnp.int32, (GLA_CHUNK, GLA_CHUNK), 0)
    cj = lax.broadcasted_iota(jnp.int32, (GLA_CHUNK, GLA_CHUNK), 1)
    causal = cj <= ci
    tri = jnp.where(causal, 1.0, 0.0).astype(BF16)
    chunks = [slice(c * GLA_CHUNK, (c + 1) * GLA_CHUNK) for c in range(nchunk)]
    la_hi, la_lo = _split(log_a)
    cums = [_dot(tri, la_hi[sl]) + _dot(tri, la_lo[sl]) for sl in chunks]
    bcum = jnp.concatenate(cums, axis=0)
    btot = jnp.concatenate([jnp.broadcast_to(cm[GLA_CHUNK - 1:], cm.shape) for cm in cums], axis=0)
    qf = q_ref[0].astype(F32) * (GLA_DK ** -0.5)
    kf = k_ref[0].astype(F32)
    q_dec = (qf * jnp.exp(bcum)).astype(BF16)
    k_inv = (kf * jnp.exp(-bcum)).astype(BF16)
    k_dec = (kf * jnp.exp(btot - bcum)).astype(BF16)
    dl = jnp.exp(btot)
    v = v_ref[0]
    att = [jnp.where(causal, _dot_nt(q_dec[sl], k_inv[sl]), 0.0).astype(BF16) for sl in chunks]
    kv = [_dot_tn(v[sl], k_dec[sl]) for sl in chunks]
    o_intra = [_dot(att[c], v[chunks[c]]) for c in range(nchunk)]
    st = st_ref[...]
    o_inter = []
    for c, sl in enumerate(chunks):
        o_inter.append(_dot_nt(q_dec[sl], st.astype(BF16)))
        st = st * dl[c * GLA_CHUNK:c * GLA_CHUNK + 1] + kv[c]
    st_ref[...] = st
    o = jnp.concatenate(o_intra, axis=0) + jnp.concatenate(o_inter, axis=0)
    o = o * lax.rsqrt(jnp.mean(o * o, axis=-1, keepdims=True) + LN_EPS) * ng_ref[...]
    o_ref[0] = (o * _silu(r_ref[0].astype(F32))).astype(o_ref.dtype)


def _gla_core(proj, w_a2, b_a, norm_g, tc=1024):
    b, s, _ = proj.shape
    h, dk, dv = GLA_HEADS, GLA_DK, GLA_DV
    assert s % tc == 0
    wa = jnp.zeros((LANES, h * dk), F32).at[:GLA_GATE_RANK].set(w_a2).astype(BF16)
    kb, vb, rb, ab = (h * dk) // dk, (2 * h * dk) // dv, (2 * h * dk + h * dv) // dv, (2 * h * dk + 2 * h * dv) // LANES
    return pl.pallas_call(
        _gla_kernel,
        grid=(b, h, s // tc),
        in_specs=[pl.BlockSpec((1, tc, dk), lambda bi, hi, t: (bi, t, hi)),
                  pl.BlockSpec((1, tc, dk), lambda bi, hi, t: (bi, t, kb + hi)),
                  pl.BlockSpec((1, tc, dv), lambda bi, hi, t: (bi, t, vb + hi)),
                  pl.BlockSpec((1, tc, dv), lambda bi, hi, t: (bi, t, rb + hi)),
                  pl.BlockSpec((1, tc, LANES), lambda bi, hi, t: (bi, t, ab)),
                  pl.BlockSpec((LANES, dk), lambda bi, hi, t: (0, hi)),
                  pl.BlockSpec((1, dk), lambda bi, hi, t: (0, hi)),
                  pl.BlockSpec((1, dv), lambda bi, hi, t: (0, 0))],
        out_specs=pl.BlockSpec((1, tc, dv), lambda bi, hi, t: (bi, t, hi)),
        out_shape=jax.ShapeDtypeStruct((b, s, h * dv), BF16),
        scratch_shapes=[pltpu.VMEM((dv, dk), F32)],
        compiler_params=_params("parallel", "parallel", "arbitrary"),
        name="gla_core",
    )(proj, proj, proj, proj, proj, wa, b_a.reshape(1, h * dk), norm_g.reshape(1, dv))


def _pad_cols(w, n):
    return jnp.pad(w, ((0, 0), (0, n - w.shape[1])))


def _gla_mixer(x, scale, shift, w_in, w_a2, b_a, norm_g):
    n_in = 2 * GLA_HEADS * GLA_DK + 2 * GLA_HEADS * GLA_DV + LANES
    proj = _linear_in(x, scale, shift, _pad_cols(w_in, n_in).astype(BF16), tn=n_in)
    return _gla_core(proj, w_a2, b_a, norm_g)


def _compress_kernel(lo_ref, hi_ref, pe_ref, w1_ref, b1_ref, w2_ref, o_ref):
    half = lo_ref.shape[-1]
    pe = pe_ref[0]
    a_lo = (lo_ref[0, 0, 0].astype(F32) + pe[:, :half]).astype(BF16)
    a_hi = (hi_ref[0, 0, 0].astype(F32) + pe[:, half:]).astype(BF16)
    h1 = _dot(a_lo, w1_ref[0, :half, :]) + _dot(a_hi, w1_ref[0, half:, :]) + b1_ref[0]
    o_ref[0, 0, 0] = _dot(_silu(h1).astype(BF16), w2_ref[0])


def _compress(kv, pe, w1, b1, w2):
    two, b, s, _ = kv.shape
    g, dh = NSA_G, HEAD_DIM
    npc = s // CMP_STRIDE
    half = CMP_STRIDE * dh
    a = kv.reshape(two, b, npc, CMP_STRIDE, g, dh).transpose(0, 1, 4, 2, 3, 5).reshape(two, b, g, npc, half)
    a_hi = jnp.concatenate([a[:, :, :, 1:], jnp.zeros_like(a[:, :, :, :1])], axis=3)
    spec_a = pl.BlockSpec((1, 1, 1, npc, half), lambda c, bi, gi: (c, bi, gi, 0, 0))
    return pl.pallas_call(
        _compress_kernel,
        grid=(two, b, g),
        in_specs=[spec_a, spec_a,
                  pl.BlockSpec((1, 1, 2 * half), lambda c, bi, gi: (c, 0, 0)),
                  pl.BlockSpec((1, 2 * half, CMP_HIDDEN), lambda c, bi, gi: (c, 0, 0)),
                  pl.BlockSpec((1, 1, CMP_HIDDEN), lambda c, bi, gi: (c, 0, 0)),
                  pl.BlockSpec((1, CMP_HIDDEN, dh), lambda c, bi, gi: (c, 0, 0))],
        out_specs=pl.BlockSpec((1, 1, 1, npc, dh), lambda c, bi, gi: (c, bi, gi, 0, 0)),
        out_shape=jax.ShapeDtypeStruct((two, b, g, npc, dh), F32),
        compiler_params=_params("parallel", "parallel", "parallel"),
        name="nsa_compress",
    )(a, a_hi, pe.reshape(two, 1, 2 * half), w1.astype(BF16), b1.reshape(two, 1, CMP_HIDDEN), w2.astype(BF16))


def _stack_heads(q):
    return jnp.concatenate([q[:, r * HEAD_DIM:(r + 1) * HEAD_DIM] for r in range(NSA_R)], axis=0)


def _unstack_heads(o, rows):
    return jnp.concatenate([o[r * rows:(r + 1) * rows] for r in range(NSA_R)], axis=1)


CMP_WIDTHS = 4
CMP_TQ = 512
LOG2E = 1.4426950408889634


def _nsa_cmp_kernel(q_ref, fq_ref, kc_ref, vc_ref, nb_ref, pb_ref, o_ref, mb_ref, imp_ref, *, n_tile):
    i = pl.program_id(2)
    npc = nb_ref.shape[2]
    args = (q_ref, fq_ref, kc_ref, vc_ref, nb_ref, pb_ref, o_ref, imp_ref)
    quarter = n_tile // CMP_WIDTHS
    widths = [(k + 1) * npc // CMP_WIDTHS for k in range(CMP_WIDTHS)]
    nblk = mb_ref.shape[3]
    blocks = [min(nblk, -(-((k + 1) * nblk // CMP_WIDTHS) // LANES) * LANES) for k in range(CMP_WIDTHS)]

    @pl.when(i == 0)
    def _():
        _cmp_attend(i, widths[0], *args)

    for k in range(CMP_WIDTHS):
        @pl.when((i > 0) & (i >= k * quarter) & (i < (k + 1) * quarter))
        def _(k=k):
            sel = _Selection(i - 1, imp_ref[...], mb_ref, blocks[k])
            per_phase = -(-(SEL_TOP - 3) // CMP_PHASES)
            _cmp_attend(i, widths[k], *args, after_phase=lambda: sel.rounds(per_phase))
            sel.finish()

    @pl.when(i == n_tile)
    def _():
        _Selection(i - 1, imp_ref[...], mb_ref, nblk).finish()


CMP_PHASES = 4


def _cmp_attend(i, width, q_ref, fq_ref, kc_ref, vc_ref, nb_ref, pb_ref, o_ref, imp_ref, after_phase=lambda: None):
    skip = nb_ref.shape[2] - width
    n_sub, rows = CMP_TQ // SUB_Q, NSA_R * SUB_Q
    starts = [pl.multiple_of((i * n_sub + j) * (SUB_Q // CMP_STRIDE), 8) for j in range(n_sub)]

    scores = []
    for j in range(n_sub):
        kwin = kc_ref[0, 0, pl.ds(starts[j] + skip, width), :].astype(BF16)
        qs = _stack_heads(q_ref[0, j * SUB_Q:(j + 1) * SUB_Q, :]).astype(F32) * (HEAD_DIM ** -0.5 * LOG2E)
        qp = jnp.concatenate([qs, fq_ref[...]], axis=1).astype(BF16)
        scores.append(_dot_nt(qp, kwin) + nb_ref[0, :, skip:])
    after_phase()
    s = jnp.concatenate(scores, axis=0)
    m = jnp.max(s, axis=1, keepdims=True)
    e = jnp.exp2(s - m)
    den = jnp.sum(e, axis=1, keepdims=True)
    p = e * jnp.where(m > 0.5 * NEG, 1.0 / den, 0.0)
    p_bf = p.astype(BF16)
    after_phase()
    for j in range(n_sub):
        vwin = vc_ref[0, 0, pl.ds(starts[j] + skip, width), :].astype(BF16)
        o = _dot(p_bf[j * rows:(j + 1) * rows], vwin)
        o_ref[0, j * SUB_Q:(j + 1) * SUB_Q, :] = _unstack_heads(o, SUB_Q).astype(o_ref.dtype)
    after_phase()

    imps = []
    for j in range(n_sub):
        pg = p[j * rows:j * rows + SUB_Q]
        for r in range(1, NSA_R):
            pg = pg + p[j * rows + r * SUB_Q:j * rows + (r + 1) * SUB_Q]
        imps.append(_dot_hi_exact_rhs(pg, pb_ref[pl.ds(starts[j] + skip, width), :].astype(BF16)))
    imp_ref[...] = jnp.concatenate(imps, axis=0)
    after_phase()


class _Selection:
    def __init__(self, i, imp, mb_ref, nblk):
        self.mb_ref, self.nblk = mb_ref, nblk
        if nblk < mb_ref.shape[3]:
            imp = imp[:, :nblk]
            mb_ref[0, 0, :, nblk:] = jnp.full((CMP_TQ, mb_ref.shape[3] - nblk), NEG, mb_ref.dtype)
        blk = lax.broadcasted_iota(jnp.int32, (CMP_TQ, nblk), 1)
        self.blk_f = blk.astype(F32)
        t = i * CMP_TQ + lax.broadcasted_iota(jnp.int32, (CMP_TQ, nblk), 0)
        cur = t // SEL_LEN
        self.forced = (blk == 0) | (blk == cur) | (blk == cur - 1)
        self.work = jnp.where((blk * SEL_LEN <= t) & jnp.logical_not(self.forced), imp, NEG)
        self.free = self.work > 0.5 * NEG
        self.left = SEL_TOP - 3

    def rounds(self, n):
        n = min(n, self.left)
        for _ in range(n):
            mx = jnp.max(self.work, axis=1, keepdims=True)
            first = jnp.min(jnp.where(self.work == mx, self.blk_f, 1e9), axis=1, keepdims=True)
            self.work = jnp.where(self.blk_f == first, 2.0 * NEG, self.work)
        self.left -= n

    def finish(self):
        self.rounds(self.left)
        sel = self.forced | (self.free & (self.work < 1.5 * NEG))
        self.mb_ref[0, 0, :, :self.nblk] = jnp.where(sel, 0.0, NEG).astype(self.mb_ref.dtype)


def _nsa_cmp(proj, kcmp, vcmp, rel_table):
    b, s, _ = proj.shape
    g, dh = NSA_G, HEAD_DIM
    npc = s // CMP_STRIDE
    nblk = s // SEL_LEN
    front = npc - SUB_Q // CMP_STRIDE
    assert s % CMP_TQ == 0 and nblk >= SEL_TOP
    n_tile = s // CMP_TQ
    assert n_tile % CMP_WIDTHS == 0
    kpad = jnp.pad(kcmp, ((0, 0), (0, 0), (front, npc - front), (0, LANES - dh)))
    kpad = kpad.at[:, :, :front, dh].set(1.0)
    vpad = jnp.pad(vcmp, ((0, 0), (0, 0), (front, npc - front), (0, 0)))
    fq = jnp.zeros((NSA_R * SUB_Q, LANES - dh), F32).at[:, 0].set(NEG)
    qi = jnp.arange(SUB_Q)[:, None]
    cc = jnp.arange(npc)[None, :]
    dd = qi - (CMP_LEN - 1) + CMP_STRIDE * (front - cc)
    nb = jnp.where(dd[None] >= 0, _rel_bias(rel_table, dd) * LOG2E, NEG).reshape(g, NSA_R * SUB_Q, npc)
    diff = jnp.arange(2 * npc)[:, None] - front - (SEL_LEN // CMP_STRIDE) * jnp.arange(nblk)[None, :]
    pool = jnp.where((diff == -1) | (diff == 3), 1.0, jnp.where((diff >= 0) & (diff <= 2), 2.0, 0.0))
    return pl.pallas_call(
        functools.partial(_nsa_cmp_kernel, n_tile=n_tile),
        grid=(g, b, n_tile + 1),
        in_specs=[pl.BlockSpec((1, CMP_TQ, NSA_R * dh), lambda gi, bi, i: (bi, jnp.minimum(i, n_tile - 1), gi)),
                  pl.BlockSpec((NSA_R * SUB_Q, LANES - dh), lambda gi, bi, i: (0, 0)),
                  pl.BlockSpec((1, 1, 2 * npc, LANES), lambda gi, bi, i: (bi, gi, 0, 0)),
                  pl.BlockSpec((1, 1, 2 * npc, dh), lambda gi, bi, i: (bi, gi, 0, 0)),
                  pl.BlockSpec((1, NSA_R * SUB_Q, npc), lambda gi, bi, i: (gi, 0, 0)),
                  pl.BlockSpec((2 * npc, nblk), lambda gi, bi, i: (0, 0))],
        out_specs=[pl.BlockSpec((1, CMP_TQ, NSA_R * dh), lambda gi, bi, i: (bi, jnp.minimum(i, n_tile - 1), gi)),
                   pl.BlockSpec((1, 1, CMP_TQ, nblk), lambda gi, bi, i: (bi, gi, jnp.maximum(i - 1, 0), 0))],
        out_shape=[jax.ShapeDtypeStruct((b, s, N_HEADS * dh), BF16),
                   jax.ShapeDtypeStruct((b, g, s, nblk), BF16)],
        scratch_shapes=[pltpu.VMEM((CMP_TQ, nblk), F32)],
        compiler_params=_params("parallel", "parallel", "arbitrary"),
        name="nsa_cmp_select",
    )(proj, fq, kpad, vpad, nb, pool.astype(F32))


SEL_T = 256
SEL_WIDE = 8
SEL_NEAR = -(-(FAR_DIST + SEL_T - 1) // SEL_T)


def _nsa_sel_kernel(q_ref, fq_ref, mb_ref, k_ref, v_ref, bt_ref, o_ref, qp_ref, m_ref, acc_ref):
    i = pl.program_id(2)
    rows = NSA_R * SEL_T
    fw = (k_ref.shape[3] - LANES)
    kh = fw // (SEL_T // SEL_LEN)
    qs = (_stack_heads(q_ref[0]).astype(F32) * (HEAD_DIM ** -0.5 * LOG2E)).astype(BF16)
    mb = mb_ref[0, 0]
    for half in range(2):
        mbh = jnp.concatenate([mb[:, half * fw:(half + 1) * fw]] * NSA_R, axis=0)
        qp_ref[half] = jnp.concatenate([qs, fq_ref[0], mbh], axis=1)
    m_ref[...] = jnp.full_like(m_ref, NEG)
    acc_ref[...] = jnp.zeros_like(acc_ref)

    def update(kt, n_tile, half, bias):
        k0 = pl.multiple_of(kt * SEL_T, SEL_T)
        kk = k_ref[0, 0, pl.ds(k0, n_tile * SEL_T), :]
        vv = v_ref[0, 0, pl.ds(k0, n_tile * SEL_T), :]
        s = _dot_nt(qp_ref[half], kk)
        if bias is not None:
            s = s + bias
        m_prev = m_ref[...]
        m_new = jnp.maximum(m_prev, jnp.max(s, axis=1, keepdims=True))
        alpha = jnp.exp2(m_prev - m_new)
        p = jnp.exp2(s - jnp.concatenate([m_new] * (n_tile * SEL_T // LANES), axis=1))
        acc_ref[...] = acc_ref[...] * alpha + _dot(p.astype(BF16), vv)
        m_ref[...] = m_new

    def far_region(lo, hi, half):
        n = jnp.maximum(hi - lo, 0)

        def wide(j, carry):
            update(lo + SEL_WIDE * j, SEL_WIDE, half, None)
            return carry

        lax.fori_loop(0, n // SEL_WIDE, wide, 0)
        piece = SEL_WIDE // 2
        while piece >= 1:
            @pl.when((n & piece) != 0)
            def _(piece=piece):
                update(lo + (n & ~(2 * piece - 1)), piece, half, None)
            piece //= 2

    n_far = jnp.maximum(i - SEL_NEAR + 1, 0)
    far_region(0, jnp.minimum(n_far, kh), 0)
    far_region(kh, n_far, 1)

    def near(e):
        kt = i - e
        update(kt, 1, jnp.where(kt < kh, 0, 1), bt_ref[0, :, e * SEL_T:(e + 1) * SEL_T])

    @pl.when(i >= SEL_NEAR - 1)
    def _():
        for e in range(SEL_NEAR - 1, -1, -1):
            near(e)

    @pl.when(i < SEL_NEAR - 1)
    def _():
        for e in range(SEL_NEAR - 2, -1, -1):
            @pl.when(i - e >= 0)
            def _(e=e):
                near(e)

    acc = acc_ref[...]
    o = acc[:, :HEAD_DIM] / acc[:, HEAD_DIM:HEAD_DIM + 1]
    o_ref[0] = _unstack_heads(o, SEL_T).astype(o_ref.dtype)


def _nsa_sel(proj, ks, vs, mb, rel_table):
    b, s, _ = proj.shape
    g, dh = NSA_G, HEAD_DIM
    nblk = s // SEL_LEN
    fw = nblk // 2
    rows = NSA_R * SEL_T
    pos = jnp.arange(s)
    onehot = (((pos // SEL_LEN) % fw)[:, None] == jnp.arange(fw)[None, :]).astype(BF16)
    ones2 = jnp.zeros((s, LANES - dh), BF16).at[:, :2].set(1.0)
    kp = jnp.concatenate([ks, jnp.broadcast_to(ones2, (b, g, s, LANES - dh)),
                          jnp.broadcast_to(onehot, (b, g, s, fw))], axis=-1)
    ones1 = jnp.zeros((s, LANES - dh), BF16).at[:, :1].set(1.0)
    vp = jnp.concatenate([vs, jnp.broadcast_to(ones1, (b, g, s, LANES - dh))], axis=-1)
    b_far = rel_table[REL_BUCKETS - 1].reshape(g, NSA_R) * LOG2E
    hi = b_far.astype(BF16)
    lo = (b_far - hi.astype(F32)).astype(BF16)
    fq = jnp.zeros((g, NSA_R, SEL_T, LANES - dh), BF16)
    fq = fq.at[:, :, :, 0].set(hi[:, :, None]).at[:, :, :, 1].set(lo[:, :, None]).reshape(g, rows, LANES - dh)
    qi = jnp.arange(SEL_T)[:, None]
    cc = jnp.arange(SEL_NEAR * SEL_T)[None, :]
    dd = (cc // SEL_T) * SEL_T + qi - (cc % SEL_T)
    bias = _rel_bias(rel_table, dd).reshape(g, NSA_R, SEL_T, -1) * LOG2E - b_far[:, :, None, None]
    bt = jnp.where(dd[None, None] >= 0, bias, NEG).reshape(g, rows, SEL_NEAR * SEL_T)
    return pl.pallas_call(
        _nsa_sel_kernel,
        grid=(g, b, s // SEL_T),
        in_specs=[pl.BlockSpec((1, SEL_T, NSA_R * dh), lambda gi, bi, i: (bi, i, gi)),
                  pl.BlockSpec((1, rows, LANES - dh), lambda gi, bi, i: (gi, 0, 0)),
                  pl.BlockSpec((1, 1, SEL_T, nblk), lambda gi, bi, i: (bi, gi, i, 0)),
                  pl.BlockSpec((1, 1, s, LANES + fw), lambda gi, bi, i: (bi, gi, 0, 0)),
                  pl.BlockSpec((1, 1, s, LANES), lambda gi, bi, i: (bi, gi, 0, 0)),
                  pl.BlockSpec((1, rows, SEL_NEAR * SEL_T), lambda gi, bi, i: (gi, 0, 0),
                               pipeline_mode=pl.Buffered(1))],
        out_specs=pl.BlockSpec((1, SEL_T, NSA_R * dh), lambda gi, bi, i: (bi, i, gi)),
        out_shape=jax.ShapeDtypeStruct((b, s, N_HEADS * dh), BF16),
        scratch_shapes=[pltpu.VMEM((2, rows, LANES + fw), BF16), pltpu.VMEM((rows, LANES), F32),
                        pltpu.VMEM((rows, LANES), F32)],
        compiler_params=_params("parallel", "parallel", "parallel"),
        name="nsa_selected",
    )(proj, fq, mb, kp, vp, bt)


WIN_T = 512


def _softmax2_rows(s):
    m = jnp.max(s, axis=1, keepdims=True)
    e = jnp.exp2(s - m)
    den = jnp.sum(e, axis=1, keepdims=True)
    return e, 1.0 / den, m * (1.0 / LOG2E) + jnp.log(den)


def _nsa_win_kernel(q_ref, kp_ref, kc_ref, vp_ref, vc_ref, bt_ref, o_ref):
    i = pl.program_id(2)
    back = WIN_T
    span = back + SUB_Q
    n_sub, rows = WIN_T // SUB_Q, NSA_R * SUB_Q

    def body(first_tile):
        kcat = jnp.concatenate([kp_ref[0, 0], kc_ref[0, 0]], axis=0)
        vcat = jnp.concatenate([vp_ref[0, 0], vc_ref[0, 0]], axis=0)
        col = lax.broadcasted_iota(jnp.int32, (rows, span), 1)
        scores = []
        for j in range(n_sub):
            qs = _stack_heads(q_ref[0, j * SUB_Q:(j + 1) * SUB_Q, :]).astype(F32) * (HEAD_DIM ** -0.5 * LOG2E)
            s = _dot_nt(qs.astype(BF16), kcat[j * SUB_Q:j * SUB_Q + span]) + bt_ref[0]
            if first_tile:
                s = jnp.where(col >= back - j * SUB_Q, s, NEG)
            scores.append(s)
        e, inv, _ = _softmax2_rows(jnp.concatenate(scores, axis=0))
        e = e.astype(BF16)
        for j in range(n_sub):
            o = _dot(e[j * rows:(j + 1) * rows], vcat[j * SUB_Q:j * SUB_Q + span]) * inv[j * rows:(j + 1) * rows]
            o_ref[0, j * SUB_Q:(j + 1) * SUB_Q, :] = _unstack_heads(o, SUB_Q).astype(o_ref.dtype)

    @pl.when(i == 0)
    def _():
        body(True)

    @pl.when(i > 0)
    def _():
        body(False)


def _nsa_win(proj, kw, vw, rel_table):
    b, s, _ = proj.shape
    g, dh = NSA_G, HEAD_DIM
    span = WIN_T + SUB_Q
    qi = jnp.arange(SUB_Q)[:, None]
    cc = jnp.arange(span)[None, :]
    dd = qi + WIN_T - cc
    bias = _rel_bias(rel_table, dd)
    bt = jnp.where((dd[None] >= 0) & (dd[None] <= WIN_LEN - 1), bias * LOG2E, NEG).reshape(g, NSA_R * SUB_Q, span)
    prev = lambda gi, bi, i: (bi, gi, jnp.maximum(i - 1, 0), 0)
    cur = lambda gi, bi, i: (bi, gi, i, 0)
    return pl.pallas_call(
        _nsa_win_kernel,
        grid=(g, b, s // WIN_T),
        in_specs=[pl.BlockSpec((1, WIN_T, NSA_R * dh), lambda gi, bi, i: (bi, i, gi)),
                  pl.BlockSpec((1, 1, WIN_T, dh), prev), pl.BlockSpec((1, 1, WIN_T, dh), cur),
                  pl.BlockSpec((1, 1, WIN_T, dh), prev), pl.BlockSpec((1, 1, WIN_T, dh), cur),
                  pl.BlockSpec((1, NSA_R * SUB_Q, span), lambda gi, bi, i: (gi, 0, 0))],
        out_specs=pl.BlockSpec((1, WIN_T, NSA_R * dh), lambda gi, bi, i: (bi, i, gi)),
        out_shape=jax.ShapeDtypeStruct((b, s, N_HEADS * dh), BF16),
        compiler_params=_params("parallel", "parallel", "parallel"),
        name="nsa_window",
    )(proj, kw, kw, vw, vw, bt)


def _nsa_combine_kernel(oc_ref, os_ref, ow_ref, g_ref, e_ref, o_ref):
    d = o_ref.shape[2]
    sg = jax.nn.sigmoid(g_ref[0].astype(F32))
    ge = _dot_hi_exact_rhs(sg, e_ref[...])
    o = (ge[:, :d] * oc_ref[0].astype(F32) + ge[:, d:2 * d] * os_ref[0].astype(F32)
         + ge[:, 2 * d:] * ow_ref[0].astype(F32))
    o_ref[0] = o.astype(o_ref.dtype)


def _nsa_combine(o_cmp, o_sel, o_win, proj, gate_col, tm=512):
    b, s, d = o_cmp.shape
    c = jnp.arange(LANES)[:, None]
    n = jnp.arange(3 * d)[None, :]
    expand = (c == (n // d) * N_HEADS + (n % d) // HEAD_DIM).astype(BF16)
    spec = pl.BlockSpec((1, tm, d), lambda bi, i: (bi, i, 0))
    return pl.pallas_call(
        _nsa_combine_kernel,
        grid=(b, s // tm),
        in_specs=[spec, spec, spec,
                  pl.BlockSpec((1, tm, LANES), lambda bi, i: (bi, i, gate_col // LANES)),
                  pl.BlockSpec((LANES, 3 * d), lambda bi, i: (0, 0))],
        out_specs=spec,
        out_shape=jax.ShapeDtypeStruct((b, s, d), BF16),
        compiler_params=_params("parallel", "parallel"),
        name="nsa_combine",
    )(o_cmp, o_sel, o_win, proj, expand)


def _nsa_mixer(x, scale, shift, w_in, pe, w1, b1, w2, rel_table):
    b, s, _ = x.shape
    hd, gd = N_HEADS * HEAD_DIM, NSA_G * HEAD_DIM
    gate_col = hd + 6 * gd
    n_in = gate_col + LANES
    proj = _linear_in(x, scale, shift, _pad_cols(w_in, n_in).astype(BF16), tn=n_in)

    def group_major(t):
        return t.reshape(b, s, NSA_G, HEAD_DIM).transpose(0, 2, 1, 3)

    kc, vc, ks, vs, kw, vw = (proj[:, :, hd + n * gd: hd + (n + 1) * gd] for n in range(6))
    cmp = _compress(jnp.stack([kc, vc]), pe.reshape(2, -1), w1, b1, w2)
    o_cmp, mb = _nsa_cmp(proj, cmp[0], cmp[1], rel_table)
    o_sel = _nsa_sel(proj, group_major(ks), group_major(vs), mb, rel_table)
    o_win = _nsa_win(proj, group_major(kw), group_major(vw), rel_table)
    return _nsa_combine(o_cmp, o_sel, o_win, proj, gate_col)


DIL_BACK = 128
DIL_W = 256


def _dil_kernel(q_ref, kp_ref, kc_ref, vp_ref, vc_ref, bt_ref, o_ref, l_ref):
    t = pl.program_id(3)
    tq = q_ref.shape[1]
    span = DIL_BACK + SUB_Q
    kcat = jnp.concatenate([kp_ref[0, tq - DIL_BACK:, :], kc_ref[0]], axis=0)
    vcat = jnp.concatenate([vp_ref[0, tq - DIL_BACK:, :], vc_ref[0]], axis=0)
    lane_q = lax.broadcasted_iota(jnp.int32, (SUB_Q, DIL_W), 1) // HEAD_DIM
    lane_k = lax.broadcasted_iota(jnp.int32, (span, DIL_W), 1) // HEAD_DIM
    col = lax.broadcasted_iota(jnp.int32, (SUB_Q, span), 1)
    n_sub, n_head = tq // SUB_Q, DIL_W // HEAD_DIM
    scores = []
    for j in range(n_sub):
        qj = (q_ref[0, j * SUB_Q:(j + 1) * SUB_Q, :].astype(F32) * (HEAD_DIM ** -0.5 * LOG2E)).astype(BF16)
        kk = kcat[j * SUB_Q:j * SUB_Q + span]
        for hh in range(n_head):
            s = _dot_nt(jnp.where(lane_q == hh, qj, 0.0).astype(BF16), kk) + bt_ref[hh]
            if j == 0:
                s = jnp.where((col >= DIL_BACK) | (t > 0), s, NEG)
            scores.append(s)
    e_all, inv_all, lse_all = _softmax2_rows(jnp.concatenate(scores, axis=0))
    e_all = e_all.astype(BF16)
    for j in range(n_sub):
        vv = vcat[j * SUB_Q:j * SUB_Q + span]
        o = jnp.zeros((SUB_Q, DIL_W), F32)
        inv = jnp.zeros((SUB_Q, DIL_W), F32)
        lse = jnp.zeros((SUB_Q, DIL_W), F32)
        for hh in range(n_head):
            r0 = (j * n_head + hh) * SUB_Q
            o = o + _dot(e_all[r0:r0 + SUB_Q], jnp.where(lane_k == hh, vv, 0.0).astype(BF16))
            inv = jnp.where(lane_q == hh, inv_all[r0:r0 + SUB_Q], inv)
            lse = jnp.where(lane_q == hh, lse_all[r0:r0 + SUB_Q], lse)
        o_ref[0, j * SUB_Q:(j + 1) * SUB_Q, :] = (o * inv).astype(o_ref.dtype)
        l_ref[0, j * SUB_Q:(j + 1) * SUB_Q, :] = lse


def _dil_group(view, window, dil, rel_table):
    b, l, n_all = view.shape
    hd = N_HEADS * HEAD_DIM
    assert window // dil == DIL_BACK and n_all == dil * 3 * hd
    tq = min(1024, l)
    assert l % tq == 0 and tq >= DIL_BACK
    hpb = hd // DIL_W
    qi = jnp.arange(SUB_Q)[:, None]
    cc = jnp.arange(DIL_BACK + SUB_Q)[None, :]
    dd = qi + DIL_BACK - cc
    bt = jnp.where((dd[None] >= 0) & (dd[None] <= DIL_BACK), _rel_bias(rel_table, dd * dil) * LOG2E, NEG)

    def spec(comp, prev):
        def index(bi, r, hp, t):
            return (bi, jnp.maximum(t - 1, 0) if prev else t, (r * 3 + comp) * hpb + hp)
        return pl.BlockSpec((1, tq, DIL_W), index)

    out_spec = pl.BlockSpec((1, tq, DIL_W), lambda bi, r, hp, t: (bi, t, r * hpb + hp))
    return pl.pallas_call(
        _dil_kernel,
        grid=(b, dil, hpb, l // tq),
        in_specs=[spec(0, False), spec(1, True), spec(1, False), spec(2, True), spec(2, False),
                  pl.BlockSpec((DIL_W // HEAD_DIM, SUB_Q, DIL_BACK + SUB_Q), lambda bi, r, hp, t: (hp, 0, 0))],
        out_specs=[out_spec, out_spec],
        out_shape=[jax.ShapeDtypeStruct((b, l, dil * hd), BF16), jax.ShapeDtypeStruct((b, l, dil * hd), F32)],
        compiler_params=_params("parallel", "parallel", "parallel", "parallel"),
        name=f"dilated_attention_{dil}",
    )(view, view, view, view, view, bt)


def _dil_merge_kernel(*refs, dils):
    n = len(dils)
    o_refs, l_refs, out_ref, scratch = refs[:n], refs[n:2 * n], refs[2 * n], refs[2 * n + 1:]
    tm, d = out_ref.shape[1], out_ref.shape[2]

    def token_major(ref, dil, buf):
        if dil == 1:
            return ref[0].astype(F32)
        for r in range(dil):
            for c in range(d // LANES):
                buf[c, pl.ds(r, tm // dil, stride=dil), :] = (
                    ref[0, :, r * d + c * LANES:r * d + (c + 1) * LANES].astype(F32))
        return jnp.concatenate([buf[c] for c in range(d // LANES)], axis=1)

    outs = [token_major(o_refs[g], dils[g], scratch[2 * g]) for g in range(n)]
    lses = [token_major(l_refs[g], dils[g], scratch[2 * g + 1]) for g in range(n)]
    m = functools.reduce(jnp.maximum, lses)
    es = [jnp.exp(l - m) for l in lses]
    inv = 1.0 / functools.reduce(jnp.add, es)
    out_ref[0] = functools.reduce(jnp.add, [e * inv * o for e, o in zip(es, outs)]).astype(out_ref.dtype)


def _dil_merge(outs, lses, dils, tm=512):
    b = outs[0].shape[0]
    d = N_HEADS * HEAD_DIM
    s = outs[0].shape[1] * dils[0]
    specs = [pl.BlockSpec((1, tm // dil, dil * d), lambda bi, i: (bi, i, 0)) for dil in dils]
    return pl.pallas_call(
        functools.partial(_dil_merge_kernel, dils=tuple(dils)),
        grid=(b, s // tm),
        in_specs=specs + specs,
        out_specs=pl.BlockSpec((1, tm, d), lambda bi, i: (bi, i, 0)),
        out_shape=jax.ShapeDtypeStruct((b, s, d), BF16),
        scratch_shapes=[pltpu.VMEM((d // LANES, tm, LANES), F32) for _ in range(2 * len(dils))],
        compiler_params=_params("parallel", "parallel"),
        name="dilated_merge",
    )(*outs, *lses)


def _dil_mixer(x, scale, shift, w_in, rel_table):
    hd = N_HEADS * HEAD_DIM
    w = w_in.astype(BF16)
    outs, lses, dils = [], [], []
    for gi, (window, dil) in enumerate(DIL_GROUPS):
        view = _linear_in(x, scale, shift, w[:, gi * 3 * hd:(gi + 1) * 3 * hd], tn=3 * hd, dil=dil)
        o, lse = _dil_group(view, window, dil, rel_table)
        outs.append(o)
        lses.append(lse)
        dils.append(dil)
    return _dil_merge(outs, lses, dils)


def kernel(x, c, rel_table, mod_w, mod_b, ln_g, ln_b, gla_w_in, gla_w_a2, gla_b_a, gla_norm_g, gla_w_o,
           nsa_w_in, nsa_cmp_pe, nsa_cmp_w1, nsa_cmp_b1, nsa_cmp_w2, nsa_w_o, dil_w_in, dil_w_o,
           ffn_w_up, ffn_conv_w, ffn_conv_b, ffn_w_down):
    shift, scale, gate = _modulation(c, mod_w, mod_b)
    for i in range(DEPTH):
        kind, j = i % N_MIXERS, i // N_MIXERS
        m = 2 * i
        if kind == 0:
            a = _gla_mixer(x, scale[m], shift[m], gla_w_in[j], gla_w_a2[j], gla_b_a[j], gla_norm_g[j])
            w_o = gla_w_o[j]
        elif kind == 1:
            a = _nsa_mixer(x, scale[m], shift[m], nsa_w_in[j], nsa_cmp_pe[j], nsa_cmp_w1[j], nsa_cmp_b1[j],
                           nsa_cmp_w2[j], rel_table)
            w_o = nsa_w_o[j]
        else:
            a = _dil_mixer(x, scale[m], shift[m], dil_w_in[j], rel_table)
            w_o = dil_w_o[j]
        x = _out_norm(a, w_o.astype(BF16), x, gate[m], ln_g[i, 0], ln_b[i, 0])
        a = _ffn_up(x, scale[m + 1], shift[m + 1], ffn_w_up[i].astype(BF16), ffn_conv_w[i], ffn_conv_b[i])
        x = _out_norm(a, ffn_w_down[i].astype(BF16), x, gate[m + 1], ln_g[i, 1], ln_b[i, 1])
    return x
```

```python
import functools
import math

import jax
import jax.numpy as jnp
import numpy as np
from jax import lax
from jax.experimental import pallas as pl
from jax.experimental.pallas import tpu as pltpu

F32 = jnp.float32
BF16 = jnp.bfloat16

D_MODEL = 1024
DEPTH = 4
N_MIXERS = 3
HEAD_DIM = 64
N_HEADS = D_MODEL // HEAD_DIM
REL_BUCKETS = 32
REL_MAX_DIST = 2048
LN_EPS = 1e-5
GLA_HEADS = 4
GLA_DK = D_MODEL // 2 // GLA_HEADS
GLA_DV = D_MODEL // GLA_HEADS
GLA_GATE_RANK = 16
GLA_TAU = 16.0
GLA_CHUNK = 64
NSA_G = 4
NSA_R = N_HEADS // NSA_G
CMP_LEN = 32
CMP_STRIDE = 16
CMP_HIDDEN = 256
SEL_LEN = 64
SEL_TOP = 16
WIN_LEN = 512
DIL_GROUPS = ((128, 1), (512, 4), (2048, 16))
D_FF = 2816
DN_ALPHA = (2 * DEPTH) ** 0.25

NEG = -1e30
VMEM_LIMIT = 56 * 1024 * 1024
LANES = 128
SUB_Q = 128


def _params(*sem):
    return pltpu.CompilerParams(dimension_semantics=sem, vmem_limit_bytes=VMEM_LIMIT)


def _dot(a, b):
    return jnp.dot(a, b, preferred_element_type=F32)


def _dot_nt(a, b):
    return lax.dot_general(a, b, (((1,), (1,)), ((), ())), preferred_element_type=F32)


def _dot_tn(a, b):
    return lax.dot_general(a, b, (((0,), (0,)), ((), ())), preferred_element_type=F32)


def _split(a):
    hi = a.astype(BF16)
    lo = (a - hi.astype(F32)).astype(BF16)
    return hi, lo


def _dot_hi_exact_rhs(a, b_bf16):
    hi, lo = _split(a)
    return _dot(hi, b_bf16) + _dot(lo, b_bf16)


def _silu(x):
    return x * jax.nn.sigmoid(x)


def _bucket_starts():
    n = np.arange(2 * REL_MAX_DIST)
    exact = REL_BUCKETS // 2
    logn = np.log(np.maximum(n, 1).astype(np.float32) / np.float32(exact))
    large = exact + (logn / np.float32(math.log(REL_MAX_DIST / exact)) * np.float32(REL_BUCKETS - exact)).astype(np.int32)
    bucket = np.where(n < exact, n, np.minimum(large, REL_BUCKETS - 1))
    assert np.all(np.diff(bucket) >= 0) and bucket[-1] == REL_BUCKETS - 1
    return [int(np.argmax(bucket >= k)) for k in range(REL_BUCKETS)]


BUCKET_START = _bucket_starts()
FAR_DIST = BUCKET_START[-1]


def _rel_bias(rel_table, dist):
    shape = (rel_table.shape[1],) + (1,) * dist.ndim
    out = jnp.broadcast_to(rel_table[0].reshape(shape), (rel_table.shape[1],) + dist.shape)
    for k in range(1, REL_BUCKETS):
        out = jnp.where(dist[None] >= BUCKET_START[k], rel_table[k].reshape(shape), out)
    return out


def _mod_kernel(c_ref, w_ref, b_ref, o_ref):
    cs = _silu(c_ref[...])
    ch, cl = _split(cs)
    wh, wl = _split(w_ref[0])
    o_ref[0] = _dot(ch, wh) + _dot(ch, wl) + _dot(cl, wh) + b_ref[0]


def _modulation(c, mod_w, mod_b):
    b, d = c.shape
    n_sub = mod_w.shape[0] * mod_w.shape[1]
    rows = 8
    cp = jnp.zeros((rows, d), F32).at[:b].set(c)
    w = mod_w.reshape(n_sub, d, 3 * d)
    bias = mod_b.reshape(n_sub, 1, 3 * d)
    tn = d
    out = pl.pallas_call(
        _mod_kernel,
        grid=(n_sub, 3 * d // tn),
        in_specs=[pl.BlockSpec((rows, d), lambda i, j: (0, 0)),
                  pl.BlockSpec((1, d, tn), lambda i, j: (i, 0, j)),
                  pl.BlockSpec((1, 1, tn), lambda i, j: (i, 0, j))],
        out_specs=pl.BlockSpec((1, rows, tn), lambda i, j: (i, 0, j)),
        out_shape=jax.ShapeDtypeStruct((n_sub, rows, 3 * d), F32),
        compiler_params=_params("parallel", "parallel"),
        name="modulation",
    )(cp, w, bias)
    out = out[:, :b]
    shift, scale, gate = jnp.split(out, 3, axis=-1)
    return shift[:, :, None, :], scale[:, :, None, :], gate[:, :, None, :]


def _linear_in_kernel(x_ref, sc_ref, sh_ref, w_ref, *rest, dil):
    h = (x_ref[0] * (1.0 + sc_ref[0]) + sh_ref[0]).astype(BF16)
    if dil == 1:
        o_ref, = rest
        o_ref[0] = _dot(h, w_ref[...]).astype(o_ref.dtype)
    else:
        perm_ref, o_ref = rest
        res = _dot(_dot(perm_ref[...], h).astype(BF16), w_ref[...])
        rows, n = res.shape[0] // dil, res.shape[1]
        for r in range(dil):
            o_ref[0, :, r * n:(r + 1) * n] = res[r * rows:(r + 1) * rows].astype(o_ref.dtype)


def _linear_in(x, scale, shift, w, tn, dil=1):
    b, s, d = x.shape
    n = w.shape[1]
    tm = min(s, 1024 if dil == 1 else 512)
    assert n % tn == 0 and s % tm == 0 and tm % (16 * dil) == 0 and (dil == 1 or n == tn)
    in_specs = [pl.BlockSpec((1, tm, d), lambda j, bi, i: (bi, i, 0)),
                pl.BlockSpec((1, 1, d), lambda j, bi, i: (bi, 0, 0)),
                pl.BlockSpec((1, 1, d), lambda j, bi, i: (bi, 0, 0)),
                pl.BlockSpec((d, tn), lambda j, bi, i: (0, j), pipeline_mode=pl.Buffered(1) if n == tn else None)]
    args = [x, scale, shift, w]
    if dil > 1:
        out_row = jnp.arange(tm)
        src = (out_row % (tm // dil)) * dil + out_row // (tm // dil)
        in_specs.append(pl.BlockSpec((tm, tm), lambda j, bi, i: (0, 0)))
        args.append((src[:, None] == jnp.arange(tm)[None, :]).astype(BF16))
    return pl.pallas_call(
        functools.partial(_linear_in_kernel, dil=dil),
        grid=(n // tn, b, s // tm),
        in_specs=in_specs,
        out_specs=pl.BlockSpec((1, tm // dil, dil * tn), lambda j, bi, i: (bi, i, j)),
        out_shape=jax.ShapeDtypeStruct((b, s // dil, dil * n), BF16),
        compiler_params=_params("parallel", "parallel", "parallel"),
        name="linear_in",
    )(*args)


OUT_ROWS = 512


def _out_norm_kernel(a_ref, w_ref, x_ref, gate_ref, g_ref, b_ref, o_ref):
    chunks = [slice(r, r + OUT_ROWS) for r in range(0, a_ref.shape[1], OUT_ROWS)]
    ys = [_dot(a_ref[0, sl, :], w_ref[...]) for sl in chunks]
    for sl, y in zip(chunks, ys):
        z = DN_ALPHA * x_ref[0, sl, :] + (1.0 + gate_ref[0]) * y
        mu = jnp.mean(z, axis=-1, keepdims=True)
        zc = z - mu
        var = jnp.mean(zc * zc, axis=-1, keepdims=True)
        o_ref[0, sl, :] = zc * lax.rsqrt(var + LN_EPS) * g_ref[...] + b_ref[...]


def _out_norm(a, w, x, gate, ln_g, ln_b):
    b, s, d = x.shape
    k = a.shape[-1]
    tm = min(s, 2048 if k <= d else 1024)
    return pl.pallas_call(
        _out_norm_kernel,
        grid=(b, s // tm),
        in_specs=[pl.BlockSpec((1, tm, k), lambda bi, i: (bi, i, 0)),
                  pl.BlockSpec((k, d), lambda bi, i: (0, 0)),
                  pl.BlockSpec((1, tm, d), lambda bi, i: (bi, i, 0)),
                  pl.BlockSpec((1, 1, d), lambda bi, i: (bi, 0, 0)),
                  pl.BlockSpec((1, d), lambda bi, i: (0, 0)),
                  pl.BlockSpec((1, d), lambda bi, i: (0, 0))],
        out_specs=pl.BlockSpec((1, tm, d), lambda bi, i: (bi, i, 0)),
        out_shape=jax.ShapeDtypeStruct((b, s, d), F32),
        compiler_params=_params("parallel", "parallel"),
        name="out_norm",
    )(a, w, x, gate, ln_g.reshape(1, d), ln_b.reshape(1, d))


FFN_HALO = 16
FFN_COLS = 256


def _ffn_up_kernel(x_ref, xh_ref, sc_ref, sh_ref, wu_ref, wg_ref, cw_ref, cb_ref, o_ref, h_ref):
    i = pl.program_id(1)
    sc = 1.0 + sc_ref[0]
    sh = sh_ref[0]
    h_ref[0:FFN_HALO, :] = (xh_ref[0] * sc + sh).astype(BF16)
    h_ref[FFN_HALO:, :] = (x_ref[0] * sc + sh).astype(BF16)
    keep = (i > 0).astype(F32)
    cw = cw_ref[...]
    for c in range(0, o_ref.shape[2], FFN_COLS):
        cols = slice(c, c + FFN_COLS)
        u = _dot(h_ref[...], wu_ref[:, cols])
        u = jnp.concatenate([u[0:FFN_HALO] * keep, u[FFN_HALO:]], axis=0)
        g = _dot(h_ref[FFN_HALO:, :], wg_ref[:, cols])
        a = u * cw[1:2, cols] + pltpu.roll(u, 1, 0) * cw[0:1, cols]
        conv = (u * cw[2:3, cols] + pltpu.roll(a, 1, 0))[FFN_HALO:] + cb_ref[:, cols]
        o_ref[0, :, cols] = (_silu(conv) * g).astype(o_ref.dtype)


def _ffn_up(x, scale, shift, w_up, conv_w, conv_b, tm=1024):
    b, s, d = x.shape
    f = conv_w.shape[1]
    assert s % tm == 0 and tm % FFN_HALO == 0 and f % FFN_COLS == 0
    hb = tm // FFN_HALO
    once = pl.Buffered(1)
    return pl.pallas_call(
        _ffn_up_kernel,
        grid=(b, s // tm),
        in_specs=[pl.BlockSpec((1, tm, d), lambda bi, i: (bi, i, 0)),
                  pl.BlockSpec((1, FFN_HALO, d), lambda bi, i: (bi, jnp.maximum(i * hb - 1, 0), 0)),
                  pl.BlockSpec((1, 1, d), lambda bi, i: (bi, 0, 0)),
                  pl.BlockSpec((1, 1, d), lambda bi, i: (bi, 0, 0)),
                  pl.BlockSpec((d, f), lambda bi, i: (0, 0), pipeline_mode=once),
                  pl.BlockSpec((d, f), lambda bi, i: (0, 1), pipeline_mode=once),
                  pl.BlockSpec((3, f), lambda bi, i: (0, 0)),
                  pl.BlockSpec((1, f), lambda bi, i: (0, 0))],
        out_specs=pl.BlockSpec((1, tm, f), lambda bi, i: (bi, i, 0)),
        out_shape=jax.ShapeDtypeStruct((b, s, f), BF16),
        scratch_shapes=[pltpu.VMEM((tm + FFN_HALO, d), BF16)],
        compiler_params=_params("parallel", "parallel"),
        name="ffn_up",
    )(x, x, scale, shift, w_up, w_up, conv_w, conv_b.reshape(1, f))


def _gla_kernel(q_ref, k_ref, v_ref, r_ref, a_ref, wa_ref, ba_ref, ng_ref, o_ref, st_ref):
    tc = q_ref.shape[1]
    nchunk = tc // GLA_CHUNK

    @pl.when(pl.program_id(2) == 0)
    def _():
        st_ref[...] = jnp.zeros_like(st_ref)

    z = _dot(a_ref[0], wa_ref[...]) + ba_ref[...]
    log_a = (jnp.minimum(z, 0.0) - jnp.log1p(jnp.exp(-jnp.abs(z)))) / GLA_TAU
    ci = lax.broadcasted_iota(jnp.int32, (GLA_CHUNK, GLA_CHUNK), 0)
    cj = lax.broadcasted_iota(jnp.int32, (GLA_CHUNK, GLA_CHUNK), 1)
    causal = cj <= ci
    tri = jnp.where(causal, 1.0, 0.0).astype(BF16)
    chunks = [slice(c * GLA_CHUNK, (c + 1) * GLA_CHUNK) for c in range(nchunk)]
    la_hi, la_lo = _split(log_a)
    cums = [_dot(tri, la_hi[sl]) + _dot(tri, la_lo[sl]) for sl in chunks]
    bcum = jnp.concatenate(cums, axis=0)
    btot = jnp.concatenate([jnp.broadcast_to(cm[GLA_CHUNK - 1:], cm.shape) for cm in cums], axis=0)
    qf = q_ref[0].astype(F32) * (GLA_DK ** -0.5)
    kf = k_ref[0].astype(F32)
    q_dec = (qf * jnp.exp(bcum)).astype(BF16)
    k_inv = (kf * jnp.exp(-bcum)).astype(BF16)
    k_dec = (kf * jnp.exp(btot - bcum)).astype(BF16)
    dl = jnp.exp(btot)
    v = v_ref[0]
    att = [jnp.where(causal, _dot_nt(q_dec[sl], k_inv[sl]), 0.0).astype(BF16) for sl in chunks]
    kv = [_dot_tn(v[sl], k_dec[sl]) for sl in chunks]
    o_intra = [_dot(att[c], v[chunks[c]]) for c in range(nchunk)]
    st = st_ref[...]
    o_inter = []
    for c, sl in enumerate(chunks):
        o_inter.append(_dot_nt(q_dec[sl], st.astype(BF16)))
        st = st * dl[c * GLA_CHUNK:c * GLA_CHUNK + 1] + kv[c]
    st_ref[...] = st
    o = jnp.concatenate(o_intra, axis=0) + jnp.concatenate(o_inter, axis=0)
    o = o * lax.rsqrt(jnp.mean(o * o, axis=-1, keepdims=True) + LN_EPS) * ng_ref[...]
    o_ref[0] = (o * _silu(r_ref[0].astype(F32))).astype(o_ref.dtype)


def _gla_core(proj, w_a2, b_a, norm_g, tc=1024):
    b, s, _ = proj.shape
    h, dk, dv = GLA_HEADS, GLA_DK, GLA_DV
    assert s % tc == 0
    wa = jnp.zeros((LANES, h * dk), F32).at[:GLA_GATE_RANK].set(w_a2).astype(BF16)
    kb, vb, rb, ab = (h * dk) // dk, (2 * h * dk) // dv, (2 * h * dk + h * dv) // dv, (2 * h * dk + 2 * h * dv) // LANES
    return pl.pallas_call(
        _gla_kernel,
        grid=(b, h, s // tc),
        in_specs=[pl.BlockSpec((1, tc, dk), lambda bi, hi, t: (bi, t, hi)),
                  pl.BlockSpec((1, tc, dk), lambda bi, hi, t: (bi, t, kb + hi)),
                  pl.BlockSpec((1, tc, dv), lambda bi, hi, t: (bi, t, vb + hi)),
                  pl.BlockSpec((1, tc, dv), lambda bi, hi, t: (bi, t, rb + hi)),
                  pl.BlockSpec((1, tc, LANES), lambda bi, hi, t: (bi, t, ab)),
                  pl.BlockSpec((LANES, dk), lambda bi, hi, t: (0, hi)),
                  pl.BlockSpec((1, dk), lambda bi, hi, t: (0, hi)),
                  pl.BlockSpec((1, dv), lambda bi, hi, t: (0, 0))],
        out_specs=pl.BlockSpec((1, tc, dv), lambda bi, hi, t: (bi, t, hi)),
        out_shape=jax.ShapeDtypeStruct((b, s, h * dv), BF16),
        scratch_shapes=[pltpu.VMEM((dv, dk), F32)],
        compiler_params=_params("parallel", "parallel", "arbitrary"),
        name="gla_core",
    )(proj, proj, proj, proj, proj, wa, b_a.reshape(1, h * dk), norm_g.reshape(1, dv))


def _pad_cols(w, n):
    return jnp.pad(w, ((0, 0), (0, n - w.shape[1])))


def _gla_mixer(x, scale, shift, w_in, w_a2, b_a, norm_g):
    n_in = 2 * GLA_HEADS * GLA_DK + 2 * GLA_HEADS * GLA_DV + LANES
    proj = _linear_in(x, scale, shift, _pad_cols(w_in, n_in).astype(BF16), tn=n_in)
    return _gla_core(proj, w_a2, b_a, norm_g)


def _compress_kernel(lo_ref, hi_ref, pe_ref, w1_ref, b1_ref, w2_ref, o_ref):
    half = lo_ref.shape[-1]
    pe = pe_ref[0]
    a_lo = (lo_ref[0, 0, 0].astype(F32) + pe[:, :half]).astype(BF16)
    a_hi = (hi_ref[0, 0, 0].astype(F32) + pe[:, half:]).astype(BF16)
    h1 = _dot(a_lo, w1_ref[0, :half, :]) + _dot(a_hi, w1_ref[0, half:, :]) + b1_ref[0]
    o_ref[0, 0, 0] = _dot(_silu(h1).astype(BF16), w2_ref[0])


def _compress(kv, pe, w1, b1, w2):
    two, b, s, _ = kv.shape
    g, dh = NSA_G, HEAD_DIM
    npc = s // CMP_STRIDE
    half = CMP_STRIDE * dh
    a = kv.reshape(two, b, npc, CMP_STRIDE, g, dh).transpose(0, 1, 4, 2, 3, 5).reshape(two, b, g, npc, half)
    a_hi = jnp.concatenate([a[:, :, :, 1:], jnp.zeros_like(a[:, :, :, :1])], axis=3)
    spec_a = pl.BlockSpec((1, 1, 1, npc, half), lambda c, bi, gi: (c, bi, gi, 0, 0))
    return pl.pallas_call(
        _compress_kernel,
        grid=(two, b, g),
        in_specs=[spec_a, spec_a,
                  pl.BlockSpec((1, 1, 2 * half), lambda c, bi, gi: (c, 0, 0)),
                  pl.BlockSpec((1, 2 * half, CMP_HIDDEN), lambda c, bi, gi: (c, 0, 0)),
                  pl.BlockSpec((1, 1, CMP_HIDDEN), lambda c, bi, gi: (c, 0, 0)),
                  pl.BlockSpec((1, CMP_HIDDEN, dh), lambda c, bi, gi: (c, 0, 0))],
        out_specs=pl.BlockSpec((1, 1, 1, npc, dh), lambda c, bi, gi: (c, bi, gi, 0, 0)),
        out_shape=jax.ShapeDtypeStruct((two, b, g, npc, dh), F32),
        compiler_params=_params("parallel", "parallel", "parallel"),
        name="nsa_compress",
    )(a, a_hi, pe.reshape(two, 1, 2 * half), w1.astype(BF16), b1.reshape(two, 1, CMP_HIDDEN), w2.astype(BF16))


def _stack_heads(q):
    return jnp.concatenate([q[:, r * HEAD_DIM:(r + 1) * HEAD_DIM] for r in range(NSA_R)], axis=0)


def _unstack_heads(o, rows):
    return jnp.concatenate([o[r * rows:(r + 1) * rows] for r in range(NSA_R)], axis=1)


CMP_WIDTHS = 4
CMP_TQ = 512
LOG2E = 1.4426950408889634


def _nsa_cmp_kernel(q_ref, fq_ref, kc_ref, vc_ref, nb_ref, pb_ref, o_ref, mb_ref, imp_ref, *, n_tile):
    i = pl.program_id(2)
    npc = nb_ref.shape[2]
    args = (q_ref, fq_ref, kc_ref, vc_ref, nb_ref, pb_ref, o_ref, imp_ref)
    quarter = n_tile // CMP_WIDTHS
    widths = [(k + 1) * npc // CMP_WIDTHS for k in range(CMP_WIDTHS)]
    nblk = mb_ref.shape[3]
    blocks = [min(nblk, -(-((k + 1) * nblk // CMP_WIDTHS) // LANES) * LANES) for k in range(CMP_WIDTHS)]

    @pl.when(i == 0)
    def _():
        _cmp_attend(i, widths[0], *args)

    for k in range(CMP_WIDTHS):
        @pl.when((i > 0) & (i >= k * quarter) & (i < (k + 1) * quarter))
        def _(k=k):
            sel = _Selection(i - 1, imp_ref[...], mb_ref, blocks[k])
            per_phase = -(-(SEL_TOP - 3) // CMP_PHASES)
            _cmp_attend(i, widths[k], *args, after_phase=lambda: sel.rounds(per_phase))
            sel.finish()

    @pl.when(i == n_tile)
    def _():
        _Selection(i - 1, imp_ref[...], mb_ref, nblk).finish()


CMP_PHASES = 4


def _cmp_attend(i, width, q_ref, fq_ref, kc_ref, vc_ref, nb_ref, pb_ref, o_ref, imp_ref, after_phase=lambda: None):
    skip = nb_ref.shape[2] - width
    n_sub, rows = CMP_TQ // SUB_Q, NSA_R * SUB_Q
    starts = [pl.multiple_of((i * n_sub + j) * (SUB_Q // CMP_STRIDE), 8) for j in range(n_sub)]

    scores = []
    for j in range(n_sub):
        kwin = kc_ref[0, 0, pl.ds(starts[j] + skip, width), :].astype(BF16)
        qs = _stack_heads(q_ref[0, j * SUB_Q:(j + 1) * SUB_Q, :]).astype(F32) * (HEAD_DIM ** -0.5 * LOG2E)
        qp = jnp.concatenate([qs, fq_ref[...]], axis=1).astype(BF16)
        scores.append(_dot_nt(qp, kwin) + nb_ref[0, :, skip:])
    after_phase()
    s = jnp.concatenate(scores, axis=0)
    m = jnp.max(s, axis=1, keepdims=True)
    e = jnp.exp2(s - m)
    den = jnp.sum(e, axis=1, keepdims=True)
    p = e * jnp.where(m > 0.5 * NEG, 1.0 / den, 0.0)
    p_bf = p.astype(BF16)
    after_phase()
    for j in range(n_sub):
        vwin = vc_ref[0, 0, pl.ds(starts[j] + skip, width), :].astype(BF16)
        o = _dot(p_bf[j * rows:(j + 1) * rows], vwin)
        o_ref[0, j * SUB_Q:(j + 1) * SUB_Q, :] = _unstack_heads(o, SUB_Q).astype(o_ref.dtype)
    after_phase()

    imps = []
    for j in range(n_sub):
        pg = p[j * rows:j * rows + SUB_Q]
        for r in range(1, NSA_R):
            pg = pg + p[j * rows + r * SUB_Q:j * rows + (r + 1) * SUB_Q]
        imps.append(_dot_hi_exact_rhs(pg, pb_ref[pl.ds(starts[j] + skip, width), :].astype(BF16)))
    imp_ref[...] = jnp.concatenate(imps, axis=0)
    after_phase()


class _Selection:
    def __init__(self, i, imp, mb_ref, nblk):
        self.mb_ref, self.nblk = mb_ref, nblk
        if nblk < mb_ref.shape[3]:
            imp = imp[:, :nblk]
            mb_ref[0, 0, :, nblk:] = jnp.full((CMP_TQ, mb_ref.shape[3] - nblk), NEG, mb_ref.dtype)
        blk = lax.broadcasted_iota(jnp.int32, (CMP_TQ, nblk), 1)
        self.blk_f = blk.astype(F32)
        t = i * CMP_TQ + lax.broadcasted_iota(jnp.int32, (CMP_TQ, nblk), 0)
        cur = t // SEL_LEN
        self.forced = (blk == 0) | (blk == cur) | (blk == cur - 1)
        self.work = jnp.where((blk * SEL_LEN <= t) & jnp.logical_not(self.forced), imp, NEG)
        self.free = self.work > 0.5 * NEG
        self.left = SEL_TOP - 3

    def rounds(self, n):
        n = min(n, self.left)
        for _ in range(n):
            mx = jnp.max(self.work, axis=1, keepdims=True)
            first = jnp.min(jnp.where(self.work == mx, self.blk_f, 1e9), axis=1, keepdims=True)
            self.work = jnp.where(self.blk_f == first, 2.0 * NEG, self.work)
        self.left -= n

    def finish(self):
        self.rounds(self.left)
        sel = self.forced | (self.free & (self.work < 1.5 * NEG))
        self.mb_ref[0, 0, :, :self.nblk] = jnp.where(sel, 0.0, NEG).astype(self.mb_ref.dtype)


def _nsa_cmp(proj, kcmp, vcmp, rel_table):
    b, s, _ = proj.shape
    g, dh = NSA_G, HEAD_DIM
    npc = s // CMP_STRIDE
    nblk = s // SEL_LEN
    front = npc - SUB_Q // CMP_STRIDE
    assert s % CMP_TQ == 0 and nblk >= SEL_TOP
    n_tile = s // CMP_TQ
    assert n_tile % CMP_WIDTHS == 0
    kpad = jnp.pad(kcmp, ((0, 0), (0, 0), (front, npc - front), (0, LANES - dh)))
    kpad = kpad.at[:, :, :front, dh].set(1.0)
    vpad = jnp.pad(vcmp, ((0, 0), (0, 0), (front, npc - front), (0, 0)))
    fq = jnp.zeros((NSA_R * SUB_Q, LANES - dh), F32).at[:, 0].set(NEG)
    qi = jnp.arange(SUB_Q)[:, None]
    cc = jnp.arange(npc)[None, :]
    dd = qi - (CMP_LEN - 1) + CMP_STRIDE * (front - cc)
    nb = jnp.where(dd[None] >= 0, _rel_bias(rel_table, dd) * LOG2E, NEG).reshape(g, NSA_R * SUB_Q, npc)
    diff = jnp.arange(2 * npc)[:, None] - front - (SEL_LEN // CMP_STRIDE) * jnp.arange(nblk)[None, :]
    pool = jnp.where((diff == -1) | (diff == 3), 1.0, jnp.where((diff >= 0) & (diff <= 2), 2.0, 0.0))
    return pl.pallas_call(
        functools.partial(_nsa_cmp_kernel, n_tile=n_tile),
        grid=(g, b, n_tile + 1),
        in_specs=[pl.BlockSpec((1, CMP_TQ, NSA_R * dh), lambda gi, bi, i: (bi, jnp.minimum(i, n_tile - 1), gi)),
                  pl.BlockSpec((NSA_R * SUB_Q, LANES - dh), lambda gi, bi, i: (0, 0)),
                  pl.BlockSpec((1, 1, 2 * npc, LANES), lambda gi, bi, i: (bi, gi, 0, 0)),
                  pl.BlockSpec((1, 1, 2 * npc, dh), lambda gi, bi, i: (bi, gi, 0, 0)),
                  pl.BlockSpec((1, NSA_R * SUB_Q, npc), lambda gi, bi, i: (gi, 0, 0)),
                  pl.BlockSpec((2 * npc, nblk), lambda gi, bi, i: (0, 0))],
        out_specs=[pl.BlockSpec((1, CMP_TQ, NSA_R * dh), lambda gi, bi, i: (bi, jnp.minimum(i, n_tile - 1), gi)),
                   pl.BlockSpec((1, 1, CMP_TQ, nblk), lambda gi, bi, i: (bi, gi, jnp.maximum(i - 1, 0), 0))],
        out_shape=[jax.ShapeDtypeStruct((b, s, N_HEADS * dh), BF16),
                   jax.ShapeDtypeStruct((b, g, s, nblk), BF16)],
        scratch_shapes=[pltpu.VMEM((CMP_TQ, nblk), F32)],
        compiler_params=_params("parallel", "parallel", "arbitrary"),
        name="nsa_cmp_select",
    )(proj, fq, kpad, vpad, nb, pool.astype(F32))


SEL_T = 256
SEL_WIDE = 8
SEL_NEAR = -(-(FAR_DIST + SEL_T - 1) // SEL_T)


def _nsa_sel_kernel(q_ref, fq_ref, mb_ref, k_ref, v_ref, bt_ref, o_ref, qp_ref, m_ref, acc_ref):
    i = pl.program_id(2)
    rows = NSA_R * SEL_T
    fw = (k_ref.shape[3] - LANES)
    kh = fw // (SEL_T // SEL_LEN)
    qs = (_stack_heads(q_ref[0]).astype(F32) * (HEAD_DIM ** -0.5 * LOG2E)).astype(BF16)
    mb = mb_ref[0, 0]
    for half in range(2):
        mbh = jnp.concatenate([mb[:, half * fw:(half + 1) * fw]] * NSA_R, axis=0)
        qp_ref[half] = jnp.concatenate([qs, fq_ref[0], mbh], axis=1)
    m_ref[...] = jnp.full_like(m_ref, NEG)
    acc_ref[...] = jnp.zeros_like(acc_ref)

    def update(kt, n_tile, half, bias):
        k0 = pl.multiple_of(kt * SEL_T, SEL_T)
        kk = k_ref[0, 0, pl.ds(k0, n_tile * SEL_T), :]
        vv = v_ref[0, 0, pl.ds(k0, n_tile * SEL_T), :]
        s = _dot_nt(qp_ref[half], kk)
        if bias is not None:
            s = s + bias
        m_prev = m_ref[...]
        m_new = jnp.maximum(m_prev, jnp.max(s, axis=1, keepdims=True))
        alpha = jnp.exp2(m_prev - m_new)
        p = jnp.exp2(s - jnp.concatenate([m_new] * (n_tile * SEL_T // LANES), axis=1))
        acc_ref[...] = acc_ref[...] * alpha + _dot(p.astype(BF16), vv)
        m_ref[...] = m_new

    def far_region(lo, hi, half):
        n = jnp.maximum(hi - lo, 0)

        def wide(j, carry):
            update(lo + SEL_WIDE * j, SEL_WIDE, half, None)
            return carry

        lax.fori_loop(0, n // SEL_WIDE, wide, 0)
        piece = SEL_WIDE // 2
        while piece >= 1:
            @pl.when((n & piece) != 0)
            def _(piece=piece):
                update(lo + (n & ~(2 * piece - 1)), piece, half, None)
            piece //= 2

    n_far = jnp.maximum(i - SEL_NEAR + 1, 0)
    far_region(0, jnp.minimum(n_far, kh), 0)
    far_region(kh, n_far, 1)

    def near(e):
        kt = i - e
        update(kt, 1, jnp.where(kt < kh, 0, 1), bt_ref[0, :, e * SEL_T:(e + 1) * SEL_T])

    @pl.when(i >= SEL_NEAR - 1)
    def _():
        for e in range(SEL_NEAR - 1, -1, -1):
            near(e)

    @pl.when(i < SEL_NEAR - 1)
    def _():
        for e in range(SEL_NEAR - 2, -1, -1):
            @pl.when(i - e >= 0)
            def _(e=e):
                near(e)

    acc = acc_ref[...]
    o = acc[:, :HEAD_DIM] / acc[:, HEAD_DIM:HEAD_DIM + 1]
    o_ref[0] = _unstack_heads(o, SEL_T).astype(o_ref.dtype)


def _nsa_sel(proj, ks, vs, mb, rel_table):
    b, s, _ = proj.shape
    g, dh = NSA_G, HEAD_DIM
    nblk = s // SEL_LEN
    fw = nblk // 2
    rows = NSA_R * SEL_T
    pos = jnp.arange(s)
    onehot = (((pos // SEL_LEN) % fw)[:, None] == jnp.arange(fw)[None, :]).astype(BF16)
    ones2 = jnp.zeros((s, LANES - dh), BF16).at[:, :2].set(1.0)
    kp = jnp.concatenate([ks, jnp.broadcast_to(ones2, (b, g, s, LANES - dh)),
                          jnp.broadcast_to(onehot, (b, g, s, fw))], axis=-1)
    ones1 = jnp.zeros((s, LANES - dh), BF16).at[:, :1].set(1.0)
    vp = jnp.concatenate([vs, jnp.broadcast_to(ones1, (b, g, s, LANES - dh))], axis=-1)
    b_far = rel_table[REL_BUCKETS - 1].reshape(g, NSA_R) * LOG2E
    hi = b_far.astype(BF16)
    lo = (b_far - hi.astype(F32)).astype(BF16)
    fq = jnp.zeros((g, NSA_R, SEL_T, LANES - dh), BF16)
    fq = fq.at[:, :, :, 0].set(hi[:, :, None]).at[:, :, :, 1].set(lo[:, :, None]).reshape(g, rows, LANES - dh)
    t = SEL_T
    dist = jnp.arange(-(t - 1), SEL_NEAR * t)
    f1 = jnp.where(dist[None] >= 0, _rel_bias(rel_table, dist) * LOG2E - b_far.reshape(-1, 1), NEG)
    u = jnp.stack([f1[:, e * t:e * t + 2 * t - 1] for e in range(SEL_NEAR)], axis=1)
    z = jnp.pad(u[..., ::-1], ((0, 0), (0, 0), (0, 1)))
    skew = jnp.tile(z, (1, 1, t))[..., :t * (2 * t - 1)].reshape(N_HEADS, SEL_NEAR, t, 2 * t - 1)
    bt = skew[..., t - 1:].transpose(0, 2, 1, 3).reshape(g, rows, SEL_NEAR * t)
    return pl.pallas_call(
        _nsa_sel_kernel,
        grid=(g, b, s // SEL_T),
        in_specs=[pl.BlockSpec((1, SEL_T, NSA_R * dh), lambda gi, bi, i: (bi, i, gi)),
                  pl.BlockSpec((1, rows, LANES - dh), lambda gi, bi, i: (gi, 0, 0)),
                  pl.BlockSpec((1, 1, SEL_T, nblk), lambda gi, bi, i: (bi, gi, i, 0)),
                  pl.BlockSpec((1, 1, s, LANES + fw), lambda gi, bi, i: (bi, gi, 0, 0)),
                  pl.BlockSpec((1, 1, s, LANES), lambda gi, bi, i: (bi, gi, 0, 0)),
                  pl.BlockSpec((1, rows, SEL_NEAR * SEL_T), lambda gi, bi, i: (gi, 0, 0),
                               pipeline_mode=pl.Buffered(1))],
        out_specs=pl.BlockSpec((1, SEL_T, NSA_R * dh), lambda gi, bi, i: (bi, i, gi)),
        out_shape=jax.ShapeDtypeStruct((b, s, N_HEADS * dh), BF16),
        scratch_shapes=[pltpu.VMEM((2, rows, LANES + fw), BF16), pltpu.VMEM((rows, LANES), F32),
                        pltpu.VMEM((rows, LANES), F32)],
        compiler_params=_params("parallel", "parallel", "parallel"),
        name="nsa_selected",
    )(proj, fq, mb, kp, vp, bt)


WIN_T = 512


def _softmax2_rows(s):
    m = jnp.max(s, axis=1, keepdims=True)
    e = jnp.exp2(s - m)
    den = jnp.sum(e, axis=1, keepdims=True)
    return e, 1.0 / den, m * (1.0 / LOG2E) + jnp.log(den)


def _nsa_win_kernel(q_ref, kp_ref, kc_ref, vp_ref, vc_ref, bt_ref, o_ref):
    i = pl.program_id(2)
    back = WIN_T
    span = back + SUB_Q
    n_sub, rows = WIN_T // SUB_Q, NSA_R * SUB_Q

    def body(first_tile):
        kcat = jnp.concatenate([kp_ref[0, 0], kc_ref[0, 0]], axis=0)
        vcat = jnp.concatenate([vp_ref[0, 0], vc_ref[0, 0]], axis=0)
        col = lax.broadcasted_iota(jnp.int32, (rows, span), 1)
        scores = []
        for j in range(n_sub):
            qs = _stack_heads(q_ref[0, j * SUB_Q:(j + 1) * SUB_Q, :]).astype(F32) * (HEAD_DIM ** -0.5 * LOG2E)
            s = _dot_nt(qs.astype(BF16), kcat[j * SUB_Q:j * SUB_Q + span]) + bt_ref[0]
            if first_tile:
                s = jnp.where(col >= back - j * SUB_Q, s, NEG)
            scores.append(s)
        e, inv, _ = _softmax2_rows(jnp.concatenate(scores, axis=0))
        e = e.astype(BF16)
        for j in range(n_sub):
            o = _dot(e[j * rows:(j + 1) * rows], vcat[j * SUB_Q:j * SUB_Q + span]) * inv[j * rows:(j + 1) * rows]
            o_ref[0, j * SUB_Q:(j + 1) * SUB_Q, :] = _unstack_heads(o, SUB_Q).astype(o_ref.dtype)

    @pl.when(i == 0)
    def _():
        body(True)

    @pl.when(i > 0)
    def _():
        body(False)


def _nsa_win(proj, kw, vw, rel_table):
    b, s, _ = proj.shape
    g, dh = NSA_G, HEAD_DIM
    span = WIN_T + SUB_Q
    qi = jnp.arange(SUB_Q)[:, None]
    cc = jnp.arange(span)[None, :]
    dd = qi + WIN_T - cc
    bias = _rel_bias(rel_table, dd)
    bt = jnp.where((dd[None] >= 0) & (dd[None] <= WIN_LEN - 1), bias * LOG2E, NEG).reshape(g, NSA_R * SUB_Q, span)
    prev = lambda gi, bi, i: (bi, gi, jnp.maximum(i - 1, 0), 0)
    cur = lambda gi, bi, i: (bi, gi, i, 0)
    return pl.pallas_call(
        _nsa_win_kernel,
        grid=(g, b, s // WIN_T),
        in_specs=[pl.BlockSpec((1, WIN_T, NSA_R * dh), lambda gi, bi, i: (bi, i, gi)),
                  pl.BlockSpec((1, 1, WIN_T, dh), prev), pl.BlockSpec((1, 1, WIN_T, dh), cur),
                  pl.BlockSpec((1, 1, WIN_T, dh), prev), pl.BlockSpec((1, 1, WIN_T, dh), cur),
                  pl.BlockSpec((1, NSA_R * SUB_Q, span), lambda gi, bi, i: (gi, 0, 0))],
        out_specs=pl.BlockSpec((1, WIN_T, NSA_R * dh), lambda gi, bi, i: (bi, i, gi)),
        out_shape=jax.ShapeDtypeStruct((b, s, N_HEADS * dh), BF16),
        compiler_params=_params("parallel", "parallel", "parallel"),
        name="nsa_window",
    )(proj, kw, kw, vw, vw, bt)


def _nsa_combine_kernel(oc_ref, os_ref, ow_ref, g_ref, e_ref, o_ref):
    d = o_ref.shape[2]
    sg = jax.nn.sigmoid(g_ref[0].astype(F32))
    ge = _dot_hi_exact_rhs(sg, e_ref[...])
    o = (ge[:, :d] * oc_ref[0].astype(F32) + ge[:, d:2 * d] * os_ref[0].astype(F32)
         + ge[:, 2 * d:] * ow_ref[0].astype(F32))
    o_ref[0] = o.astype(o_ref.dtype)


def _nsa_combine(o_cmp, o_sel, o_win, proj, gate_col, tm=512):
    b, s, d = o_cmp.shape
    c = jnp.arange(LANES)[:, None]
    n = jnp.arange(3 * d)[None, :]
    expand = (c == (n // d) * N_HEADS + (n % d) // HEAD_DIM).astype(BF16)
    spec = pl.BlockSpec((1, tm, d), lambda bi, i: (bi, i, 0))
    return pl.pallas_call(
        _nsa_combine_kernel,
        grid=(b, s // tm),
        in_specs=[spec, spec, spec,
                  pl.BlockSpec((1, tm, LANES), lambda bi, i: (bi, i, gate_col // LANES)),
                  pl.BlockSpec((LANES, 3 * d), lambda bi, i: (0, 0))],
        out_specs=spec,
        out_shape=jax.ShapeDtypeStruct((b, s, d), BF16),
        compiler_params=_params("parallel", "parallel"),
        name="nsa_combine",
    )(o_cmp, o_sel, o_win, proj, expand)


def _nsa_mixer(x, scale, shift, w_in, pe, w1, b1, w2, rel_table):
    b, s, _ = x.shape
    hd, gd = N_HEADS * HEAD_DIM, NSA_G * HEAD_DIM
    gate_col = hd + 6 * gd
    n_in = gate_col + LANES
    proj = _linear_in(x, scale, shift, _pad_cols(w_in, n_in).astype(BF16), tn=n_in)

    def group_major(t):
        return t.reshape(b, s, NSA_G, HEAD_DIM).transpose(0, 2, 1, 3)

    kc, vc, ks, vs, kw, vw = (proj[:, :, hd + n * gd: hd + (n + 1) * gd] for n in range(6))
    cmp = _compress(jnp.stack([kc, vc]), pe.reshape(2, -1), w1, b1, w2)
    o_cmp, mb = _nsa_cmp(proj, cmp[0], cmp[1], rel_table)
    o_sel = _nsa_sel(proj, group_major(ks), group_major(vs), mb, rel_table)
    o_win = _nsa_win(proj, group_major(kw), group_major(vw), rel_table)
    return _nsa_combine(o_cmp, o_sel, o_win, proj, gate_col)


DIL_BACK = 128
DIL_W = 256


def _dil_kernel(q_ref, kp_ref, kc_ref, vp_ref, vc_ref, bt_ref, o_ref, l_ref):
    t = pl.program_id(3)
    tq = q_ref.shape[1]
    span = DIL_BACK + SUB_Q
    kcat = jnp.concatenate([kp_ref[0, tq - DIL_BACK:, :], kc_ref[0]], axis=0)
    vcat = jnp.concatenate([vp_ref[0, tq - DIL_BACK:, :], vc_ref[0]], axis=0)
    lane_q = lax.broadcasted_iota(jnp.int32, (SUB_Q, DIL_W), 1) // HEAD_DIM
    lane_k = lax.broadcasted_iota(jnp.int32, (span, DIL_W), 1) // HEAD_DIM
    col = lax.broadcasted_iota(jnp.int32, (SUB_Q, span), 1)
    n_sub, n_head = tq // SUB_Q, DIL_W // HEAD_DIM
    scores = []
    for j in range(n_sub):
        qj = (q_ref[0, j * SUB_Q:(j + 1) * SUB_Q, :].astype(F32) * (HEAD_DIM ** -0.5 * LOG2E)).astype(BF16)
        kk = kcat[j * SUB_Q:j * SUB_Q + span]
        for hh in range(n_head):
            s = _dot_nt(jnp.where(lane_q == hh, qj, 0.0).astype(BF16), kk) + bt_ref[hh]
            if j == 0:
                s = jnp.where((col >= DIL_BACK) | (t > 0), s, NEG)
            scores.append(s)
    e_all, inv_all, lse_all = _softmax2_rows(jnp.concatenate(scores, axis=0))
    e_all = e_all.astype(BF16)
    for j in range(n_sub):
        vv = vcat[j * SUB_Q:j * SUB_Q + span]
        o = jnp.zeros((SUB_Q, DIL_W), F32)
        inv = jnp.zeros((SUB_Q, DIL_W), F32)
        lse = jnp.zeros((SUB_Q, DIL_W), F32)
        for hh in range(n_head):
            r0 = (j * n_head + hh) * SUB_Q
            o = o + _dot(e_all[r0:r0 + SUB_Q], jnp.where(lane_k == hh, vv, 0.0).astype(BF16))
            inv = jnp.where(lane_q == hh, inv_all[r0:r0 + SUB_Q], inv)
            lse = jnp.where(lane_q == hh, lse_all[r0:r0 + SUB_Q], lse)
        o_ref[0, j * SUB_Q:(j + 1) * SUB_Q, :] = (o * inv).astype(o_ref.dtype)
        l_ref[0, j * SUB_Q:(j + 1) * SUB_Q, :] = lse


def _dil_group(view, window, dil, rel_table):
    b, l, n_all = view.shape
    hd = N_HEADS * HEAD_DIM
    assert window // dil == DIL_BACK and n_all == dil * 3 * hd
    tq = min(1024, l)
    assert l % tq == 0 and tq >= DIL_BACK
    hpb = hd // DIL_W
    qi = jnp.arange(SUB_Q)[:, None]
    cc = jnp.arange(DIL_BACK + SUB_Q)[None, :]
    dd = qi + DIL_BACK - cc
    bt = jnp.where((dd[None] >= 0) & (dd[None] <= DIL_BACK), _rel_bias(rel_table, dd * dil) * LOG2E, NEG)

    def spec(comp, prev):
        def index(bi, r, hp, t):
            return (bi, jnp.maximum(t - 1, 0) if prev else t, (r * 3 + comp) * hpb + hp)
        return pl.BlockSpec((1, tq, DIL_W), index)

    out_spec = pl.BlockSpec((1, tq, DIL_W), lambda bi, r, hp, t: (bi, t, r * hpb + hp))
    return pl.pallas_call(
        _dil_kernel,
        grid=(b, dil, hpb, l // tq),
        in_specs=[spec(0, False), spec(1, True), spec(1, False), spec(2, True), spec(2, False),
                  pl.BlockSpec((DIL_W // HEAD_DIM, SUB_Q, DIL_BACK + SUB_Q), lambda bi, r, hp, t: (hp, 0, 0))],
        out_specs=[out_spec, out_spec],
        out_shape=[jax.ShapeDtypeStruct((b, l, dil * hd), BF16), jax.ShapeDtypeStruct((b, l, dil * hd), F32)],
        compiler_params=_params("parallel", "parallel", "parallel", "parallel"),
        name=f"dilated_attention_{dil}",
    )(view, view, view, view, view, bt)


def _dil_merge_kernel(*refs, dils):
    n = len(dils)
    o_refs, l_refs, out_ref, scratch = refs[:n], refs[n:2 * n], refs[2 * n], refs[2 * n + 1:]
    tm, d = out_ref.shape[1], out_ref.shape[2]

    def token_major(ref, dil, buf):
        if dil == 1:
            return ref[0].astype(F32)
        for r in range(dil):
            for c in range(d // LANES):
                buf[c, pl.ds(r, tm // dil, stride=dil), :] = (
                    ref[0, :, r * d + c * LANES:r * d + (c + 1) * LANES].astype(F32))
        return jnp.concatenate([buf[c] for c in range(d // LANES)], axis=1)

    outs = [token_major(o_refs[g], dils[g], scratch[2 * g]) for g in range(n)]
    lses = [token_major(l_refs[g], dils[g], scratch[2 * g + 1]) for g in range(n)]
    m = functools.reduce(jnp.maximum, lses)
    es = [jnp.exp(l - m) for l in lses]
    inv = 1.0 / functools.reduce(jnp.add, es)
    out_ref[0] = functools.reduce(jnp.add, [e * inv * o for e, o in zip(es, outs)]).astype(out_ref.dtype)


def _dil_merge(outs, lses, dils, tm=512):
    b = outs[0].shape[0]
    d = N_HEADS * HEAD_DIM
    s = outs[0].shape[1] * dils[0]
    specs = [pl.BlockSpec((1, tm // dil, dil * d), lambda bi, i: (bi, i, 0)) for dil in dils]
    return pl.pallas_call(
        functools.partial(_dil_merge_kernel, dils=tuple(dils)),
        grid=(b, s // tm),
        in_specs=specs + specs,
        out_specs=pl.BlockSpec((1, tm, d), lambda bi, i: (bi, i, 0)),
        out_shape=jax.ShapeDtypeStruct((b, s, d), BF16),
        scratch_shapes=[pltpu.VMEM((d // LANES, tm, LANES), F32) for _ in range(2 * len(dils))],
        compiler_params=_params("parallel", "parallel"),
        name="dilated_merge",
    )(*outs, *lses)


def _dil_mixer(x, scale, shift, w_in, rel_table):
    hd = N_HEADS * HEAD_DIM
    w = w_in.astype(BF16)
    outs, lses, dils = [], [], []
    for gi, (window, dil) in enumerate(DIL_GROUPS):
        view = _linear_in(x, scale, shift, w[:, gi * 3 * hd:(gi + 1) * 3 * hd], tn=3 * hd, dil=dil)
        o, lse = _dil_group(view, window, dil, rel_table)
        outs.append(o)
        lses.append(lse)
        dils.append(dil)
    return _dil_merge(outs, lses, dils)


def kernel(x, c, rel_table, mod_w, mod_b, ln_g, ln_b, gla_w_in, gla_w_a2, gla_b_a, gla_norm_g, gla_w_o,
           nsa_w_in, nsa_cmp_pe, nsa_cmp_w1, nsa_cmp_b1, nsa_cmp_w2, nsa_w_o, dil_w_in, dil_w_o,
           ffn_w_up, ffn_conv_w, ffn_conv_b, ffn_w_down):
    shift, scale, gate = _modulation(c, mod_w, mod_b)
    for i in range(DEPTH):
        kind, j = i % N_MIXERS, i // N_MIXERS
        m = 2 * i
        if kind == 0:
            a = _gla_mixer(x, scale[m], shift[m], gla_w_in[j], gla_w_a2[j], gla_b_a[j], gla_norm_g[j])
            w_o = gla_w_o[j]
        elif kind == 1:
            a = _nsa_mixer(x, scale[m], shift[m], nsa_w_in[j], nsa_cmp_pe[j], nsa_cmp_w1[j], nsa_cmp_b1[j],
                           nsa_cmp_w2[j], rel_table)
            w_o = nsa_w_o[j]
        else:
            a = _dil_mixer(x, scale[m], shift[m], dil_w_in[j], rel_table)
            w_o = dil_w_o[j]
        x = _out_norm(a, w_o.astype(BF16), x, gate[m], ln_g[i, 0], ln_b[i, 0])
        a = _ffn_up(x, scale[m + 1], shift[m + 1], ffn_w_up[i].astype(BF16), ffn_conv_w[i], ffn_conv_b[i])
        x = _out_norm(a, ffn_w_down[i].astype(BF16), x, gate[m + 1], ln_g[i, 1], ln_b[i, 1])
    return x
```

```python
import functools
import math

import jax
import jax.numpy as jnp
import numpy as np
from jax import lax
from jax.experimental import pallas as pl
from jax.experimental.pallas import tpu as pltpu

F32 = jnp.float32
BF16 = jnp.bfloat16

D_MODEL = 1024
DEPTH = 4
N_MIXERS = 3
HEAD_DIM = 64
N_HEADS = D_MODEL // HEAD_DIM
REL_BUCKETS = 32
REL_MAX_DIST = 2048
LN_EPS = 1e-5
GLA_HEADS = 4
GLA_DK = D_MODEL // 2 // GLA_HEADS
GLA_DV = D_MODEL // GLA_HEADS
GLA_GATE_RANK = 16
GLA_TAU = 16.0
GLA_CHUNK = 64
NSA_G = 4
NSA_R = N_HEADS // NSA_G
CMP_LEN = 32
CMP_STRIDE = 16
CMP_HIDDEN = 256
SEL_LEN = 64
SEL_TOP = 16
WIN_LEN = 512
DIL_GROUPS = ((128, 1), (512, 4), (2048, 16))
D_FF = 2816
DN_ALPHA = (2 * DEPTH) ** 0.25

NEG = -1e30
VMEM_LIMIT = 56 * 1024 * 1024
LANES = 128
SUB_Q = 128


def _params(*sem):
    return pltpu.CompilerParams(dimension_semantics=sem, vmem_limit_bytes=VMEM_LIMIT)


def _dot(a, b):
    return jnp.dot(a, b, preferred_element_type=F32)


def _dot_nt(a, b):
    return lax.dot_general(a, b, (((1,), (1,)), ((), ())), preferred_element_type=F32)


def _dot_tn(a, b):
    return lax.dot_general(a, b, (((0,), (0,)), ((), ())), preferred_element_type=F32)


def _split(a):
    hi = a.astype(BF16)
    lo = (a - hi.astype(F32)).astype(BF16)
    return hi, lo


def _dot_hi_exact_rhs(a, b_bf16):
    hi, lo = _split(a)
    return _dot(hi, b_bf16) + _dot(lo, b_bf16)


def _silu(x):
    return x * jax.nn.sigmoid(x)


def _bucket_starts():
    n = np.arange(2 * REL_MAX_DIST)
    exact = REL_BUCKETS // 2
    logn = np.log(np.maximum(n, 1).astype(np.float32) / np.float32(exact))
    large = exact + (logn / np.float32(math.log(REL_MAX_DIST / exact)) * np.float32(REL_BUCKETS - exact)).astype(np.int32)
    bucket = np.where(n < exact, n, np.minimum(large, REL_BUCKETS - 1))
    assert np.all(np.diff(bucket) >= 0) and bucket[-1] == REL_BUCKETS - 1
    return [int(np.argmax(bucket >= k)) for k in range(REL_BUCKETS)]


BUCKET_START = _bucket_starts()
FAR_DIST = BUCKET_START[-1]


def _rel_bias(rel_table, dist):
    shape = (rel_table.shape[1],) + (1,) * dist.ndim
    out = jnp.broadcast_to(rel_table[0].reshape(shape), (rel_table.shape[1],) + dist.shape)
    for k in range(1, REL_BUCKETS):
        out = jnp.where(dist[None] >= BUCKET_START[k], rel_table[k].reshape(shape), out)
    return out


def _mod_kernel(c_ref, w_ref, b_ref, o_ref):
    cs = _silu(c_ref[...])
    ch, cl = _split(cs)
    wh, wl = _split(w_ref[0])
    o_ref[0] = _dot(ch, wh) + _dot(ch, wl) + _dot(cl, wh) + b_ref[0]


def _modulation(c, mod_w, mod_b):
    b, d = c.shape
    n_sub = mod_w.shape[0] * mod_w.shape[1]
    rows = 8
    cp = jnp.zeros((rows, d), F32).at[:b].set(c)
    w = mod_w.reshape(n_sub, d, 3 * d)
    bias = mod_b.reshape(n_sub, 1, 3 * d)
    tn = d
    out = pl.pallas_call(
        _mod_kernel,
        grid=(n_sub, 3 * d // tn),
        in_specs=[pl.BlockSpec((rows, d), lambda i, j: (0, 0)),
                  pl.BlockSpec((1, d, tn), lambda i, j: (i, 0, j)),
                  pl.BlockSpec((1, 1, tn), lambda i, j: (i, 0, j))],
        out_specs=pl.BlockSpec((1, rows, tn), lambda i, j: (i, 0, j)),
        out_shape=jax.ShapeDtypeStruct((n_sub, rows, 3 * d), F32),
        compiler_params=_params("parallel", "parallel"),
        name="modulation",
    )(cp, w, bias)
    out = out[:, :b]
    shift, scale, gate = jnp.split(out, 3, axis=-1)
    return shift[:, :, None, :], scale[:, :, None, :], gate[:, :, None, :]


def _linear_in_kernel(x_ref, sc_ref, sh_ref, w_ref, *rest, dil):
    h = (x_ref[0] * (1.0 + sc_ref[0]) + sh_ref[0]).astype(BF16)
    if dil == 1:
        o_ref, = rest
        o_ref[0] = _dot(h, w_ref[...]).astype(o_ref.dtype)
    else:
        perm_ref, o_ref = rest
        res = _dot(_dot(perm_ref[...], h).astype(BF16), w_ref[...])
        rows, n = res.shape[0] // dil, res.shape[1]
        for r in range(dil):
            o_ref[0, :, r * n:(r + 1) * n] = res[r * rows:(r + 1) * rows].astype(o_ref.dtype)


def _linear_in(x, scale, shift, w, tn, dil=1):
    b, s, d = x.shape
    n = w.shape[1]
    tm = min(s, 1024 if dil == 1 else 512)
    assert n % tn == 0 and s % tm == 0 and tm % (16 * dil) == 0 and (dil == 1 or n == tn)
    in_specs = [pl.BlockSpec((1, tm, d), lambda j, bi, i: (bi, i, 0)),
                pl.BlockSpec((1, 1, d), lambda j, bi, i: (bi, 0, 0)),
                pl.BlockSpec((1, 1, d), lambda j, bi, i: (bi, 0, 0)),
                pl.BlockSpec((d, tn), lambda j, bi, i: (0, j), pipeline_mode=pl.Buffered(1) if n == tn else None)]
    args = [x, scale, shift, w]
    if dil > 1:
        out_row = jnp.arange(tm)
        src = (out_row % (tm // dil)) * dil + out_row // (tm // dil)
        in_specs.append(pl.BlockSpec((tm, tm), lambda j, bi, i: (0, 0)))
        args.append((src[:, None] == jnp.arange(tm)[None, :]).astype(BF16))
    return pl.pallas_call(
        functools.partial(_linear_in_kernel, dil=dil),
        grid=(n // tn, b, s // tm),
        in_specs=in_specs,
        out_specs=pl.BlockSpec((1, tm // dil, dil * tn), lambda j, bi, i: (bi, i, j)),
        out_shape=jax.ShapeDtypeStruct((b, s // dil, dil * n), BF16),
        compiler_params=_params("parallel", "parallel", "parallel"),
        name="linear_in",
    )(*args)


OUT_ROWS = 512


def _out_norm_kernel(a_ref, w_ref, x_ref, gate_ref, g_ref, b_ref, o_ref):
    chunks = [slice(r, r + OUT_ROWS) for r in range(0, a_ref.shape[1], OUT_ROWS)]
    ys = [_dot(a_ref[0, sl, :], w_ref[...]) for sl in chunks]
    for sl, y in zip(chunks, ys):
        z = DN_ALPHA * x_ref[0, sl, :] + (1.0 + gate_ref[0]) * y
        mu = jnp.mean(z, axis=-1, keepdims=True)
        zc = z - mu
        var = jnp.mean(zc * zc, axis=-1, keepdims=True)
        o_ref[0, sl, :] = zc * lax.rsqrt(var + LN_EPS) * g_ref[...] + b_ref[...]


def _out_norm(a, w, x, gate, ln_g, ln_b):
    b, s, d = x.shape
    k = a.shape[-1]
    tm = min(s, 2048 if k <= d else 1024)
    return pl.pallas_call(
        _out_norm_kernel,
        grid=(b, s // tm),
        in_specs=[pl.BlockSpec((1, tm, k), lambda bi, i: (bi, i, 0)),
                  pl.BlockSpec((k, d), lambda bi, i: (0, 0)),
                  pl.BlockSpec((1, tm, d), lambda bi, i: (bi, i, 0)),
                  pl.BlockSpec((1, 1, d), lambda bi, i: (bi, 0, 0)),
                  pl.BlockSpec((1, d), lambda bi, i: (0, 0)),
                  pl.BlockSpec((1, d), lambda bi, i: (0, 0))],
        out_specs=pl.BlockSpec((1, tm, d), lambda bi, i: (bi, i, 0)),
        out_shape=jax.ShapeDtypeStruct((b, s, d), F32),
        compiler_params=_params("parallel", "parallel"),
        name="out_norm",
    )(a, w, x, gate, ln_g.reshape(1, d), ln_b.reshape(1, d))


FFN_HALO = 16
FFN_COLS = 256


def _ffn_up_kernel(x_ref, xh_ref, sc_ref, sh_ref, wu_ref, wg_ref, cw_ref, cb_ref, o_ref, h_ref):
    i = pl.program_id(1)
    sc = 1.0 + sc_ref[0]
    sh = sh_ref[0]
    h_ref[0:FFN_HALO, :] = (xh_ref[0] * sc + sh).astype(BF16)
    h_ref[FFN_HALO:, :] = (x_ref[0] * sc + sh).astype(BF16)
    keep = (i > 0).astype(F32)
    cw = cw_ref[...]
    for c in range(0, o_ref.shape[2], FFN_COLS):
        cols = slice(c, c + FFN_COLS)
        u = _dot(h_ref[...], wu_ref[:, cols])
        u = jnp.concatenate([u[0:FFN_HALO] * keep, u[FFN_HALO:]], axis=0)
        g = _dot(h_ref[FFN_HALO:, :], wg_ref[:, cols])
        a = u * cw[1:2, cols] + pltpu.roll(u, 1, 0) * cw[0:1, cols]
        conv = (u * cw[2:3, cols] + pltpu.roll(a, 1, 0))[FFN_HALO:] + cb_ref[:, cols]
        o_ref[0, :, cols] = (_silu(conv) * g).astype(o_ref.dtype)


def _ffn_up(x, scale, shift, w_up, conv_w, conv_b, tm=1024):
    b, s, d = x.shape
    f = conv_w.shape[1]
    assert s % tm == 0 and tm % FFN_HALO == 0 and f % FFN_COLS == 0
    hb = tm // FFN_HALO
    once = pl.Buffered(1)
    return pl.pallas_call(
        _ffn_up_kernel,
        grid=(b, s // tm),
        in_specs=[pl.BlockSpec((1, tm, d), lambda bi, i: (bi, i, 0)),
                  pl.BlockSpec((1, FFN_HALO, d), lambda bi, i: (bi, jnp.maximum(i * hb - 1, 0), 0)),
                  pl.BlockSpec((1, 1, d), lambda bi, i: (bi, 0, 0)),
                  pl.BlockSpec((1, 1, d), lambda bi, i: (bi, 0, 0)),
                  pl.BlockSpec((d, f), lambda bi, i: (0, 0), pipeline_mode=once),
                  pl.BlockSpec((d, f), lambda bi, i: (0, 1), pipeline_mode=once),
                  pl.BlockSpec((3, f), lambda bi, i: (0, 0)),
                  pl.BlockSpec((1, f), lambda bi, i: (0, 0))],
        out_specs=pl.BlockSpec((1, tm, f), lambda bi, i: (bi, i, 0)),
        out_shape=jax.ShapeDtypeStruct((b, s, f), BF16),
        scratch_shapes=[pltpu.VMEM((tm + FFN_HALO, d), BF16)],
        compiler_params=_params("parallel", "parallel"),
        name="ffn_up",
    )(x, x, scale, shift, w_up, w_up, conv_w, conv_b.reshape(1, f))


def _gla_kernel(q_ref, k_ref, v_ref, r_ref, a_ref, wa_ref, ba_ref, ng_ref, o_ref, st_ref):
    tc = q_ref.shape[1]
    nchunk = tc // GLA_CHUNK

    @pl.when(pl.program_id(2) == 0)
    def _():
        st_ref[...] = jnp.zeros_like(st_ref)

    z = _dot(a_ref[0], wa_ref[...]) + ba_ref[...]
    log_a = (jnp.minimum(z, 0.0) - jnp.log1p(jnp.exp(-jnp.abs(z)))) / GLA_TAU
    ci = lax.broadcasted_iota(jnp.int32, (GLA_CHUNK, GLA_CHUNK), 0)
    cj = lax.broadcasted_iota(jnp.int32, (GLA_CHUNK, GLA_CHUNK), 1)
    causal = cj <= ci
    tri = jnp.where(causal, 1.0, 0.0).astype(BF16)
    chunks = [slice(c * GLA_CHUNK, (c + 1) * GLA_CHUNK) for c in range(nchunk)]
    la_hi, la_lo = _split(log_a)
    cums = [_dot(tri, la_hi[sl]) + _dot(tri, la_lo[sl]) for sl in chunks]
    bcum = jnp.concatenate(cums, axis=0)
    btot = jnp.concatenate([jnp.broadcast_to(cm[GLA_CHUNK - 1:], cm.shape) for cm in cums], axis=0)
    qf = q_ref[0].astype(F32) * (GLA_DK ** -0.5)
    kf = k_ref[0].astype(F32)
    q_dec = (qf * jnp.exp(bcum)).astype(BF16)
    k_inv = (kf * jnp.exp(-bcum)).astype(BF16)
    k_dec = (kf * jnp.exp(btot - bcum)).astype(BF16)
    dl = jnp.exp(btot)
    v = v_ref[0]
    att = [jnp.where(causal, _dot_nt(q_dec[sl], k_inv[sl]), 0.0).astype(BF16) for sl in chunks]
    kv = [_dot_tn(v[sl], k_dec[sl]) for sl in chunks]
    o_intra = [_dot(att[c], v[chunks[c]]) for c in range(nchunk)]
    st = st_ref[...]
    o_inter = []
    for c, sl in enumerate(chunks):
        o_inter.append(_dot_nt(q_dec[sl], st.astype(BF16)))
        st = st * dl[c * GLA_CHUNK:c * GLA_CHUNK + 1] + kv[c]
    st_ref[...] = st
    o = jnp.concatenate(o_intra, axis=0) + jnp.concatenate(o_inter, axis=0)
    o = o * lax.rsqrt(jnp.mean(o * o, axis=-1, keepdims=True) + LN_EPS) * ng_ref[...]
    o_ref[0] = (o * _silu(r_ref[0].astype(F32))).astype(o_ref.dtype)


def _gla_core(proj, w_a2, b_a, norm_g, tc=1024):
    b, s, _ = proj.shape
    h, dk, dv = GLA_HEADS, GLA_DK, GLA_DV
    assert s % tc == 0
    wa = jnp.zeros((LANES, h * dk), F32).at[:GLA_GATE_RANK].set(w_a2).astype(BF16)
    kb, vb, rb, ab = (h * dk) // dk, (2 * h * dk) // dv, (2 * h * dk + h * dv) // dv, (2 * h * dk + 2 * h * dv) // LANES
    return pl.pallas_call(
        _gla_kernel,
        grid=(b, h, s // tc),
        in_specs=[pl.BlockSpec((1, tc, dk), lambda bi, hi, t: (bi, t, hi)),
                  pl.BlockSpec((1, tc, dk), lambda bi, hi, t: (bi, t, kb + hi)),
                  pl.BlockSpec((1, tc, dv), lambda bi, hi, t: (bi, t, vb + hi)),
                  pl.BlockSpec((1, tc, dv), lambda bi, hi, t: (bi, t, rb + hi)),
                  pl.BlockSpec((1, tc, LANES), lambda bi, hi, t: (bi, t, ab)),
                  pl.BlockSpec((LANES, dk), lambda bi, hi, t: (0, hi)),
                  pl.BlockSpec((1, dk), lambda bi, hi, t: (0, hi)),
                  pl.BlockSpec((1, dv), lambda bi, hi, t: (0, 0))],
        out_specs=pl.BlockSpec((1, tc, dv), lambda bi, hi, t: (bi, t, hi)),
        out_shape=jax.ShapeDtypeStruct((b, s, h * dv), BF16),
        scratch_shapes=[pltpu.VMEM((dv, dk), F32)],
        compiler_params=_params("parallel", "parallel", "arbitrary"),
        name="gla_core",
    )(proj, proj, proj, proj, proj, wa, b_a.reshape(1, h * dk), norm_g.reshape(1, dv))


def _pad_cols(w, n):
    return jnp.pad(w, ((0, 0), (0, n - w.shape[1])))


def _gla_mixer(x, scale, shift, w_in, w_a2, b_a, norm_g):
    n_in = 2 * GLA_HEADS * GLA_DK + 2 * GLA_HEADS * GLA_DV + LANES
    proj = _linear_in(x, scale, shift, _pad_cols(w_in, n_in).astype(BF16), tn=n_in)
    return _gla_core(proj, w_a2, b_a, norm_g)


def _compress_kernel(lo_ref, hi_ref, pe_ref, w1_ref, b1_ref, w2_ref, o_ref):
    half = lo_ref.shape[-1]
    pe = pe_ref[0]
    a_lo = (lo_ref[0, 0, 0].astype(F32) + pe[:, :half]).astype(BF16)
    a_hi = (hi_ref[0, 0, 0].astype(F32) + pe[:, half:]).astype(BF16)
    h1 = _dot(a_lo, w1_ref[0, :half, :]) + _dot(a_hi, w1_ref[0, half:, :]) + b1_ref[0]
    o_ref[0, 0, 0] = _dot(_silu(h1).astype(BF16), w2_ref[0])


def _compress(kv, pe, w1, b1, w2):
    two, b, s, _ = kv.shape
    g, dh = NSA_G, HEAD_DIM
    npc = s // CMP_STRIDE
    half = CMP_STRIDE * dh
    a = kv.reshape(two, b, npc, CMP_STRIDE, g, dh).transpose(0, 1, 4, 2, 3, 5).reshape(two, b, g, npc, half)
    a_hi = jnp.concatenate([a[:, :, :, 1:], jnp.zeros_like(a[:, :, :, :1])], axis=3)
    spec_a = pl.BlockSpec((1, 1, 1, npc, half), lambda c, bi, gi: (c, bi, gi, 0, 0))
    return pl.pallas_call(
        _compress_kernel,
        grid=(two, b, g),
        in_specs=[spec_a, spec_a,
                  pl.BlockSpec((1, 1, 2 * half), lambda c, bi, gi: (c, 0, 0)),
                  pl.BlockSpec((1, 2 * half, CMP_HIDDEN), lambda c, bi, gi: (c, 0, 0)),
                  pl.BlockSpec((1, 1, CMP_HIDDEN), lambda c, bi, gi: (c, 0, 0)),
                  pl.BlockSpec((1, CMP_HIDDEN, dh), lambda c, bi, gi: (c, 0, 0))],
        out_specs=pl.BlockSpec((1, 1, 1, npc, dh), lambda c, bi, gi: (c, bi, gi, 0, 0)),
        out_shape=jax.ShapeDtypeStruct((two, b, g, npc, dh), F32),
        compiler_params=_params("parallel", "parallel", "parallel"),
        name="nsa_compress",
    )(a, a_hi, pe.reshape(two, 1, 2 * half), w1.astype(BF16), b1.reshape(two, 1, CMP_HIDDEN), w2.astype(BF16))


def _stack_heads(q):
    return jnp.concatenate([q[:, r * HEAD_DIM:(r + 1) * HEAD_DIM] for r in range(NSA_R)], axis=0)


def _unstack_heads(o, rows):
    return jnp.concatenate([o[r * rows:(r + 1) * rows] for r in range(NSA_R)], axis=1)


CMP_WIDTHS = 4
CMP_TQ = 512
LOG2E = 1.4426950408889634


def _nsa_cmp_kernel(q_ref, fq_ref, kc_ref, vc_ref, nb_ref, pb_ref, o_ref, mb_ref, imp_ref, *, n_tile):
    i = pl.program_id(2)
    npc = kc_ref.shape[2] // 2
    args = (q_ref, fq_ref, kc_ref, vc_ref, nb_ref, pb_ref, o_ref, imp_ref)
    quarter = n_tile // CMP_WIDTHS
    widths = [(k + 1) * npc // CMP_WIDTHS for k in range(CMP_WIDTHS)]
    nblk = mb_ref.shape[3]
    blocks = [min(nblk, -(-((k + 1) * nblk // CMP_WIDTHS) // LANES) * LANES) for k in range(CMP_WIDTHS)]

    @pl.when(i == 0)
    def _():
        _cmp_attend(i, widths[0], *args)

    for k in range(CMP_WIDTHS):
        @pl.when((i > 0) & (i >= k * quarter) & (i < (k + 1) * quarter))
        def _(k=k):
            sel = _Selection(i - 1, imp_ref[...], mb_ref, blocks[k])
            per_phase = -(-(SEL_TOP - 3) // CMP_PHASES)
            _cmp_attend(i, widths[k], *args, after_phase=lambda: sel.rounds(per_phase))
            sel.finish()

    @pl.when(i == n_tile)
    def _():
        _Selection(i - 1, imp_ref[...], mb_ref, nblk).finish()


CMP_PHASES = 4
CMP_NEAR = 128


def _cmp_attend(i, width, q_ref, fq_ref, kc_ref, vc_ref, nb_ref, pb_ref, o_ref, imp_ref, after_phase=lambda: None):
    skip = kc_ref.shape[2] // 2 - width
    near = min(nb_ref.shape[2], width)
    n_sub, rows = CMP_TQ // SUB_Q, NSA_R * SUB_Q
    starts = [pl.multiple_of((i * n_sub + j) * (SUB_Q // CMP_STRIDE), 8) for j in range(n_sub)]

    scores = []
    for j in range(n_sub):
        kwin = kc_ref[0, 0, pl.ds(starts[j] + skip, width), :].astype(BF16)
        qs = _stack_heads(q_ref[0, j * SUB_Q:(j + 1) * SUB_Q, :]).astype(F32) * (HEAD_DIM ** -0.5 * LOG2E)
        qp = jnp.concatenate([qs, fq_ref[0]], axis=1).astype(BF16)
        sj = _dot_nt(qp, kwin)
        sj_near = sj[:, width - near:] + nb_ref[0, :, nb_ref.shape[2] - near:]
        scores.append(sj_near if near == width else jnp.concatenate([sj[:, :width - near], sj_near], axis=1))
    after_phase()
    s = jnp.concatenate(scores, axis=0)
    m = jnp.max(s, axis=1, keepdims=True)
    e = jnp.exp2(s - m)
    den = jnp.sum(e, axis=1, keepdims=True)
    p = e * jnp.where(m > 0.5 * NEG, 1.0 / den, 0.0)
    p_bf = p.astype(BF16)
    after_phase()
    for j in range(n_sub):
        vwin = vc_ref[0, 0, pl.ds(starts[j] + skip, width), :].astype(BF16)
        o = _dot(p_bf[j * rows:(j + 1) * rows], vwin)
        o_ref[0, j * SUB_Q:(j + 1) * SUB_Q, :] = _unstack_heads(o, SUB_Q).astype(o_ref.dtype)
    after_phase()

    imps = []
    for j in range(n_sub):
        pg = p[j * rows:j * rows + SUB_Q]
        for r in range(1, NSA_R):
            pg = pg + p[j * rows + r * SUB_Q:j * rows + (r + 1) * SUB_Q]
        imps.append(_dot_hi_exact_rhs(pg, pb_ref[pl.ds(starts[j] + skip, width), :].astype(BF16)))
    imp_ref[...] = jnp.concatenate(imps, axis=0)
    after_phase()


class _Selection:
    def __init__(self, i, imp, mb_ref, nblk):
        self.mb_ref, self.nblk = mb_ref, nblk
        if nblk < mb_ref.shape[3]:
            imp = imp[:, :nblk]
            mb_ref[0, 0, :, nblk:] = jnp.full((CMP_TQ, mb_ref.shape[3] - nblk), NEG, mb_ref.dtype)
        blk = lax.broadcasted_iota(jnp.int32, (CMP_TQ, nblk), 1)
        self.blk_f = blk.astype(F32)
        t = i * CMP_TQ + lax.broadcasted_iota(jnp.int32, (CMP_TQ, nblk), 0)
        cur = t // SEL_LEN
        self.forced = (blk == 0) | (blk == cur) | (blk == cur - 1)
        self.work = jnp.where((blk * SEL_LEN <= t) & jnp.logical_not(self.forced), imp, NEG)
        self.free = self.work > 0.5 * NEG
        self.left = SEL_TOP - 3

    def rounds(self, n):
        n = min(n, self.left)
        for _ in range(n):
            mx = jnp.max(self.work, axis=1, keepdims=True)
            first = jnp.min(jnp.where(self.work == mx, self.blk_f, 1e9), axis=1, keepdims=True)
            self.work = jnp.where(self.blk_f == first, 2.0 * NEG, self.work)
        self.left -= n

    def finish(self):
        self.rounds(self.left)
        sel = self.forced | (self.free & (self.work < 1.5 * NEG))
        self.mb_ref[0, 0, :, :self.nblk] = jnp.where(sel, 0.0, NEG).astype(self.mb_ref.dtype)


def _nsa_cmp(proj, kcmp, vcmp, rel_table):
    b, s, _ = proj.shape
    g, dh = NSA_G, HEAD_DIM
    npc = s // CMP_STRIDE
    nblk = s // SEL_LEN
    front = npc - SUB_Q // CMP_STRIDE
    assert s % CMP_TQ == 0 and nblk >= SEL_TOP
    n_tile = s // CMP_TQ
    assert n_tile % CMP_WIDTHS == 0
    kpad = jnp.pad(kcmp, ((0, 0), (0, 0), (front, npc - front), (0, LANES - dh)))
    kpad = kpad.at[:, :, :front, dh].set(1.0)
    kpad = kpad.at[:, :, :, dh + 1:dh + 3].set(1.0)
    vpad = jnp.pad(vcmp, ((0, 0), (0, 0), (front, npc - front), (0, 0)))
    b_far = rel_table[REL_BUCKETS - 1].reshape(g, NSA_R, 1) * LOG2E
    hi = b_far.astype(BF16).astype(F32)
    fq = jnp.zeros((g, NSA_R, SUB_Q, LANES - dh), F32).at[..., 0].set(NEG).at[..., 1].set(hi).at[..., 2].set(b_far - hi)
    fq = fq.reshape(g, NSA_R * SUB_Q, LANES - dh)
    n_near = min(npc, CMP_NEAR)
    assert n_near == npc or CMP_STRIDE * (n_near - SUB_Q // CMP_STRIDE + 1) - (CMP_LEN - 1) >= FAR_DIST
    qi = jnp.arange(SUB_Q)[:, None]
    cc = jnp.arange(npc - n_near, npc)[None, :]
    dd = qi - (CMP_LEN - 1) + CMP_STRIDE * (front - cc)
    nb = _rel_bias(rel_table, dd).reshape(g, NSA_R, SUB_Q, n_near) * LOG2E - b_far[..., None]
    nb = jnp.where(dd[None, None] >= 0, nb, NEG).reshape(g, NSA_R * SUB_Q, n_near)
    diff = jnp.arange(2 * npc)[:, None] - front - (SEL_LEN // CMP_STRIDE) * jnp.arange(nblk)[None, :]
    pool = jnp.where((diff == -1) | (diff == 3), 1.0, jnp.where((diff >= 0) & (diff <= 2), 2.0, 0.0))
    return pl.pallas_call(
        functools.partial(_nsa_cmp_kernel, n_tile=n_tile),
        grid=(g, b, n_tile + 1),
        in_specs=[pl.BlockSpec((1, CMP_TQ, NSA_R * dh), lambda gi, bi, i: (bi, jnp.minimum(i, n_tile - 1), gi)),
                  pl.BlockSpec((1, NSA_R * SUB_Q, LANES - dh), lambda gi, bi, i: (gi, 0, 0)),
                  pl.BlockSpec((1, 1, 2 * npc, LANES), lambda gi, bi, i: (bi, gi, 0, 0)),
                  pl.BlockSpec((1, 1, 2 * npc, dh), lambda gi, bi, i: (bi, gi, 0, 0)),
                  pl.BlockSpec((1, NSA_R * SUB_Q, n_near), lambda gi, bi, i: (gi, 0, 0)),
                  pl.BlockSpec((2 * npc, nblk), lambda gi, bi, i: (0, 0))],
        out_specs=[pl.BlockSpec((1, CMP_TQ, NSA_R * dh), lambda gi, bi, i: (bi, jnp.minimum(i, n_tile - 1), gi)),
                   pl.BlockSpec((1, 1, CMP_TQ, nblk), lambda gi, bi, i: (bi, gi, jnp.maximum(i - 1, 0), 0))],
        out_shape=[jax.ShapeDtypeStruct((b, s, N_HEADS * dh), BF16),
                   jax.ShapeDtypeStruct((b, g, s, nblk), BF16)],
        scratch_shapes=[pltpu.VMEM((CMP_TQ, nblk), F32)],
        compiler_params=_params("parallel", "parallel", "arbitrary"),
        name="nsa_cmp_select",
    )(proj, fq, kpad, vpad, nb, pool.astype(F32))


SEL_T = 256
SEL_WIDE = 8
SEL_NEAR = -(-(FAR_DIST + SEL_T - 1) // SEL_T)


def _nsa_sel_kernel(q_ref, fq_ref, mb_ref, k_ref, v_ref, bt_ref, o_ref, qp_ref, m_ref, acc_ref):
    i = pl.program_id(2)
    rows = NSA_R * SEL_T
    fw = (k_ref.shape[3] - LANES)
    kh = fw // (SEL_T // SEL_LEN)
    qs = (_stack_heads(q_ref[0]).astype(F32) * (HEAD_DIM ** -0.5 * LOG2E)).astype(BF16)
    mb = mb_ref[0, 0]
    for half in range(2):
        mbh = jnp.concatenate([mb[:, half * fw:(half + 1) * fw]] * NSA_R, axis=0)
        qp_ref[half] = jnp.concatenate([qs, fq_ref[0], mbh], axis=1)
    m_ref[...] = jnp.full_like(m_ref, NEG)
    acc_ref[...] = jnp.zeros_like(acc_ref)

    def update(kt, n_tile, half, bias):
        k0 = pl.multiple_of(kt * SEL_T, SEL_T)
        kk = k_ref[0, 0, pl.ds(k0, n_tile * SEL_T), :]
        vv = v_ref[0, 0, pl.ds(k0, n_tile * SEL_T), :]
        s = _dot_nt(qp_ref[half], kk)
        if bias is not None:
            s = s + bias
        m_prev = m_ref[...]
        m_new = jnp.maximum(m_prev, jnp.max(s, axis=1, keepdims=True))
        alpha = jnp.exp2(m_prev - m_new)
        p = jnp.exp2(s - jnp.concatenate([m_new] * (n_tile * SEL_T // LANES), axis=1))
        acc_ref[...] = acc_ref[...] * alpha + _dot(p.astype(BF16), vv)
        m_ref[...] = m_new

    def far_region(lo, hi, half):
        n = jnp.maximum(hi - lo, 0)

        def wide(j, carry):
            update(lo + SEL_WIDE * j, SEL_WIDE, half, None)
            return carry

        lax.fori_loop(0, n // SEL_WIDE, wide, 0)
        piece = SEL_WIDE // 2
        while piece >= 1:
            @pl.when((n & piece) != 0)
            def _(piece=piece):
                update(lo + (n & ~(2 * piece - 1)), piece, half, None)
            piece //= 2

    n_far = jnp.maximum(i - SEL_NEAR + 1, 0)
    far_region(0, jnp.minimum(n_far, kh), 0)
    far_region(kh, n_far, 1)

    def near(e):
        kt = i - e
        update(kt, 1, jnp.where(kt < kh, 0, 1), bt_ref[0, :, e * SEL_T:(e + 1) * SEL_T])

    @pl.when(i >= SEL_NEAR - 1)
    def _():
        for e in range(SEL_NEAR - 1, -1, -1):
            near(e)

    @pl.when(i < SEL_NEAR - 1)
    def _():
        for e in range(SEL_NEAR - 2, -1, -1):
            @pl.when(i - e >= 0)
            def _(e=e):
                near(e)

    acc = acc_ref[...]
    o = acc[:, :HEAD_DIM] / acc[:, HEAD_DIM:HEAD_DIM + 1]
    o_ref[0] = _unstack_heads(o, SEL_T).astype(o_ref.dtype)


def _nsa_sel(proj, ks, vs, mb, rel_table):
    b, s, _ = proj.shape
    g, dh = NSA_G, HEAD_DIM
    nblk = s // SEL_LEN
    fw = nblk // 2
    rows = NSA_R * SEL_T
    pos = jnp.arange(s)
    onehot = (((pos // SEL_LEN) % fw)[:, None] == jnp.arange(fw)[None, :]).astype(BF16)
    ones2 = jnp.zeros((s, LANES - dh), BF16).at[:, :2].set(1.0)
    kp = jnp.concatenate([ks, jnp.broadcast_to(ones2, (b, g, s, LANES - dh)),
                          jnp.broadcast_to(onehot, (b, g, s, fw))], axis=-1)
    ones1 = jnp.zeros((s, LANES - dh), BF16).at[:, :1].set(1.0)
    vp = jnp.concatenate([vs, jnp.broadcast_to(ones1, (b, g, s, LANES - dh))], axis=-1)
    b_far = rel_table[REL_BUCKETS - 1].reshape(g, NSA_R) * LOG2E
    hi = b_far.astype(BF16)
    lo = (b_far - hi.astype(F32)).astype(BF16)
    fq = jnp.zeros((g, NSA_R, SEL_T, LANES - dh), BF16)
    fq = fq.at[:, :, :, 0].set(hi[:, :, None]).at[:, :, :, 1].set(lo[:, :, None]).reshape(g, rows, LANES - dh)
    qi = jnp.arange(SEL_T)[:, None]
    cc = jnp.arange(SEL_NEAR * SEL_T)[None, :]
    dd = (cc // SEL_T) * SEL_T + qi - (cc % SEL_T)
    bias = _rel_bias(rel_table, dd).reshape(g, NSA_R, SEL_T, -1) * LOG2E - b_far[:, :, None, None]
    bt = jnp.where(dd[None, None] >= 0, bias, NEG).reshape(g, rows, SEL_NEAR * SEL_T)
    return pl.pallas_call(
        _nsa_sel_kernel,
        grid=(g, b, s // SEL_T),
        in_specs=[pl.BlockSpec((1, SEL_T, NSA_R * dh), lambda gi, bi, i: (bi, i, gi)),
                  pl.BlockSpec((1, rows, LANES - dh), lambda gi, bi, i: (gi, 0, 0)),
                  pl.BlockSpec((1, 1, SEL_T, nblk), lambda gi, bi, i: (bi, gi, i, 0)),
                  pl.BlockSpec((1, 1, s, LANES + fw), lambda gi, bi, i: (bi, gi, 0, 0)),
                  pl.BlockSpec((1, 1, s, LANES), lambda gi, bi, i: (bi, gi, 0, 0)),
                  pl.BlockSpec((1, rows, SEL_NEAR * SEL_T), lambda gi, bi, i: (gi, 0, 0),
                               pipeline_mode=pl.Buffered(1))],
        out_specs=pl.BlockSpec((1, SEL_T, NSA_R * dh), lambda gi, bi, i: (bi, i, gi)),
        out_shape=jax.ShapeDtypeStruct((b, s, N_HEADS * dh), BF16),
        scratch_shapes=[pltpu.VMEM((2, rows, LANES + fw), BF16), pltpu.VMEM((rows, LANES), F32),
                        pltpu.VMEM((rows, LANES), F32)],
        compiler_params=_params("parallel", "parallel", "parallel"),
        name="nsa_selected",
    )(proj, fq, mb, kp, vp, bt)


WIN_T = 512


def _softmax2_rows(s):
    m = jnp.max(s, axis=1, keepdims=True)
    e = jnp.exp2(s - m)
    den = jnp.sum(e, axis=1, keepdims=True)
    return e, 1.0 / den, m * (1.0 / LOG2E) + jnp.log(den)


def _nsa_win_kernel(q_ref, kp_ref, kc_ref, vp_ref, vc_ref, bt_ref, o_ref):
    i = pl.program_id(2)
    back = WIN_T
    span = back + SUB_Q
    n_sub, rows = WIN_T // SUB_Q, NSA_R * SUB_Q

    def body(first_tile):
        kcat = jnp.concatenate([kp_ref[0, 0], kc_ref[0, 0]], axis=0)
        vcat = jnp.concatenate([vp_ref[0, 0], vc_ref[0, 0]], axis=0)
        col = lax.broadcasted_iota(jnp.int32, (rows, span), 1)
        scores = []
        for j in range(n_sub):
            qs = _stack_heads(q_ref[0, j * SUB_Q:(j + 1) * SUB_Q, :]).astype(F32) * (HEAD_DIM ** -0.5 * LOG2E)
            s = _dot_nt(qs.astype(BF16), kcat[j * SUB_Q:j * SUB_Q + span]) + bt_ref[0]
            if first_tile:
                s = jnp.where(col >= back - j * SUB_Q, s, NEG)
            scores.append(s)
        e, inv, _ = _softmax2_rows(jnp.concatenate(scores, axis=0))
        e = e.astype(BF16)
        for j in range(n_sub):
            o = _dot(e[j * rows:(j + 1) * rows], vcat[j * SUB_Q:j * SUB_Q + span]) * inv[j * rows:(j + 1) * rows]
            o_ref[0, j * SUB_Q:(j + 1) * SUB_Q, :] = _unstack_heads(o, SUB_Q).astype(o_ref.dtype)

    @pl.when(i == 0)
    def _():
        body(True)

    @pl.when(i > 0)
    def _():
        body(False)


def _nsa_win(proj, kw, vw, rel_table):
    b, s, _ = proj.shape
    g, dh = NSA_G, HEAD_DIM
    span = WIN_T + SUB_Q
    qi = jnp.arange(SUB_Q)[:, None]
    cc = jnp.arange(span)[None, :]
    dd = qi + WIN_T - cc
    bias = _rel_bias(rel_table, dd)
    bt = jnp.where((dd[None] >= 0) & (dd[None] <= WIN_LEN - 1), bias * LOG2E, NEG).reshape(g, NSA_R * SUB_Q, span)
    prev = lambda gi, bi, i: (bi, gi, jnp.maximum(i - 1, 0), 0)
    cur = lambda gi, bi, i: (bi, gi, i, 0)
    return pl.pallas_call(
        _nsa_win_kernel,
        grid=(g, b, s // WIN_T),
        in_specs=[pl.BlockSpec((1, WIN_T, NSA_R * dh), lambda gi, bi, i: (bi, i, gi)),
                  pl.BlockSpec((1, 1, WIN_T, dh), prev), pl.BlockSpec((1, 1, WIN_T, dh), cur),
                  pl.BlockSpec((1, 1, WIN_T, dh), prev), pl.BlockSpec((1, 1, WIN_T, dh), cur),
                  pl.BlockSpec((1, NSA_R * SUB_Q, span), lambda gi, bi, i: (gi, 0, 0))],
        out_specs=pl.BlockSpec((1, WIN_T, NSA_R * dh), lambda gi, bi, i: (bi, i, gi)),
        out_shape=jax.ShapeDtypeStruct((b, s, N_HEADS * dh), BF16),
        compiler_params=_params("parallel", "parallel", "parallel"),
        name="nsa_window",
    )(proj, kw, kw, vw, vw, bt)


def _nsa_combine_kernel(oc_ref, os_ref, ow_ref, g_ref, e_ref, o_ref):
    d = o_ref.shape[2]
    sg = jax.nn.sigmoid(g_ref[0].astype(F32))
    ge = _dot_hi_exact_rhs(sg, e_ref[...])
    o = (ge[:, :d] * oc_ref[0].astype(F32) + ge[:, d:2 * d] * os_ref[0].astype(F32)
         + ge[:, 2 * d:] * ow_ref[0].astype(F32))
    o_ref[0] = o.astype(o_ref.dtype)


def _nsa_combine(o_cmp, o_sel, o_win, proj, gate_col, tm=512):
    b, s, d = o_cmp.shape
    c = jnp.arange(LANES)[:, None]
    n = jnp.arange(3 * d)[None, :]
    expand = (c == (n // d) * N_HEADS + (n % d) // HEAD_DIM).astype(BF16)
    spec = pl.BlockSpec((1, tm, d), lambda bi, i: (bi, i, 0))
    return pl.pallas_call(
        _nsa_combine_kernel,
        grid=(b, s // tm),
        in_specs=[spec, spec, spec,
                  pl.BlockSpec((1, tm, LANES), lambda bi, i: (bi, i, gate_col // LANES)),
                  pl.BlockSpec((LANES, 3 * d), lambda bi, i: (0, 0))],
        out_specs=spec,
        out_shape=jax.ShapeDtypeStruct((b, s, d), BF16),
        compiler_params=_params("parallel", "parallel"),
        name="nsa_combine",
    )(o_cmp, o_sel, o_win, proj, expand)


def _nsa_mixer(x, scale, shift, w_in, pe, w1, b1, w2, rel_table):
    b, s, _ = x.shape
    hd, gd = N_HEADS * HEAD_DIM, NSA_G * HEAD_DIM
    gate_col = hd + 6 * gd
    n_in = gate_col + LANES
    proj = _linear_in(x, scale, shift, _pad_cols(w_in, n_in).astype(BF16), tn=n_in)

    def group_major(t):
        return t.reshape(b, s, NSA_G, HEAD_DIM).transpose(0, 2, 1, 3)

    kc, vc, ks, vs, kw, vw = (proj[:, :, hd + n * gd: hd + (n + 1) * gd] for n in range(6))
    cmp = _compress(jnp.stack([kc, vc]), pe.reshape(2, -1), w1, b1, w2)
    o_cmp, mb = _nsa_cmp(proj, cmp[0], cmp[1], rel_table)
    o_sel = _nsa_sel(proj, group_major(ks), group_major(vs), mb, rel_table)
    o_win = _nsa_win(proj, group_major(kw), group_major(vw), rel_table)
    return _nsa_combine(o_cmp, o_sel, o_win, proj, gate_col)


DIL_BACK = 128
DIL_W = 256


def _dil_kernel(q_ref, kp_ref, kc_ref, vp_ref, vc_ref, bt_ref, o_ref, l_ref):
    t = pl.program_id(3)
    tq = q_ref.shape[1]
    span = DIL_BACK + SUB_Q
    kcat = jnp.concatenate([kp_ref[0, tq - DIL_BACK:, :], kc_ref[0]], axis=0)
    vcat = jnp.concatenate([vp_ref[0, tq - DIL_BACK:, :], vc_ref[0]], axis=0)
    lane_q = lax.broadcasted_iota(jnp.int32, (SUB_Q, DIL_W), 1) // HEAD_DIM
    lane_k = lax.broadcasted_iota(jnp.int32, (span, DIL_W), 1) // HEAD_DIM
    col = lax.broadcasted_iota(jnp.int32, (SUB_Q, span), 1)
    n_sub, n_head = tq // SUB_Q, DIL_W // HEAD_DIM
    scores = []
    for j in range(n_sub):
        qj = (q_ref[0, j * SUB_Q:(j + 1) * SUB_Q, :].astype(F32) * (HEAD_DIM ** -0.5 * LOG2E)).astype(BF16)
        kk = kcat[j * SUB_Q:j * SUB_Q + span]
        for hh in range(n_head):
            s = _dot_nt(jnp.where(lane_q == hh, qj, 0.0).astype(BF16), kk) + bt_ref[hh]
            if j == 0:
                s = jnp.where((col >= DIL_BACK) | (t > 0), s, NEG)
            scores.append(s)
    e_all, inv_all, lse_all = _softmax2_rows(jnp.concatenate(scores, axis=0))
    e_all = e_all.astype(BF16)
    for j in range(n_sub):
        vv = vcat[j * SUB_Q:j * SUB_Q + span]
        o = jnp.zeros((SUB_Q, DIL_W), F32)
        inv = jnp.zeros((SUB_Q, DIL_W), F32)
        lse = jnp.zeros((SUB_Q, DIL_W), F32)
        for hh in range(n_head):
            r0 = (j * n_head + hh) * SUB_Q
            o = o + _dot(e_all[r0:r0 + SUB_Q], jnp.where(lane_k == hh, vv, 0.0).astype(BF16))
            inv = jnp.where(lane_q == hh, inv_all[r0:r0 + SUB_Q], inv)
            lse = jnp.where(lane_q == hh, lse_all[r0:r0 + SUB_Q], lse)
        o_ref[0, j * SUB_Q:(j + 1) * SUB_Q, :] = (o * inv).astype(o_ref.dtype)
        l_ref[0, j * SUB_Q:(j + 1) * SUB_Q, :] = lse


def _dil_group(view, window, dil, rel_table):
    b, l, n_all = view.shape
    hd = N_HEADS * HEAD_DIM
    assert window // dil == DIL_BACK and n_all == dil * 3 * hd
    tq = min(1024, l)
    assert l % tq == 0 and tq >= DIL_BACK
    hpb = hd // DIL_W
    qi = jnp.arange(SUB_Q)[:, None]
    cc = jnp.arange(DIL_BACK + SUB_Q)[None, :]
    dd = qi + DIL_BACK - cc
    bt = jnp.where((dd[None] >= 0) & (dd[None] <= DIL_BACK), _rel_bias(rel_table, dd * dil) * LOG2E, NEG)

    def spec(comp, prev):
        def index(bi, r, hp, t):
            return (bi, jnp.maximum(t - 1, 0) if prev else t, (r * 3 + comp) * hpb + hp)
        return pl.BlockSpec((1, tq, DIL_W), index)

    out_spec = pl.BlockSpec((1, tq, DIL_W), lambda bi, r, hp, t: (bi, t, r * hpb + hp))
    return pl.pallas_call(
        _dil_kernel,
        grid=(b, dil, hpb, l // tq),
        in_specs=[spec(0, False), spec(1, True), spec(1, False), spec(2, True), spec(2, False),
                  pl.BlockSpec((DIL_W // HEAD_DIM, SUB_Q, DIL_BACK + SUB_Q), lambda bi, r, hp, t: (hp, 0, 0))],
        out_specs=[out_spec, out_spec],
        out_shape=[jax.ShapeDtypeStruct((b, l, dil * hd), BF16), jax.ShapeDtypeStruct((b, l, dil * hd), F32)],
        compiler_params=_params("parallel", "parallel", "parallel", "parallel"),
        name=f"dilated_attention_{dil}",
    )(view, view, view, view, view, bt)


def _dil_merge_kernel(*refs, dils):
    n = len(dils)
    o_refs, l_refs, out_ref, scratch = refs[:n], refs[n:2 * n], refs[2 * n], refs[2 * n + 1:]
    tm, d = out_ref.shape[1], out_ref.shape[2]

    def token_major(ref, dil, buf):
        if dil == 1:
            return ref[0].astype(F32)
        for r in range(dil):
            for c in range(d // LANES):
                buf[c, pl.ds(r, tm // dil, stride=dil), :] = (
                    ref[0, :, r * d + c * LANES:r * d + (c + 1) * LANES].astype(F32))
        return jnp.concatenate([buf[c] for c in range(d // LANES)], axis=1)

    outs = [token_major(o_refs[g], dils[g], scratch[2 * g]) for g in range(n)]
    lses = [token_major(l_refs[g], dils[g], scratch[2 * g + 1]) for g in range(n)]
    m = functools.reduce(jnp.maximum, lses)
    es = [jnp.exp(l - m) for l in lses]
    inv = 1.0 / functools.reduce(jnp.add, es)
    out_ref[0] = functools.reduce(jnp.add, [e * inv * o for e, o in zip(es, outs)]).astype(out_ref.dtype)


def _dil_merge(outs, lses, dils, tm=512):
    b = outs[0].shape[0]
    d = N_HEADS * HEAD_DIM
    s = outs[0].shape[1] * dils[0]
    specs = [pl.BlockSpec((1, tm // dil, dil * d), lambda bi, i: (bi, i, 0)) for dil in dils]
    return pl.pallas_call(
        functools.partial(_dil_merge_kernel, dils=tuple(dils)),
        grid=(b, s // tm),
        in_specs=specs + specs,
        out_specs=pl.BlockSpec((1, tm, d), lambda bi, i: (bi, i, 0)),
        out_shape=jax.ShapeDtypeStruct((b, s, d), BF16),
        scratch_shapes=[pltpu.VMEM((d // LANES, tm, LANES), F32) for _ in range(2 * len(dils))],
        compiler_params=_params("parallel", "parallel"),
        name="dilated_merge",
    )(*outs, *lses)


def _dil_mixer(x, scale, shift, w_in, rel_table):
    hd = N_HEADS * HEAD_DIM
    w = w_in.astype(BF16)
    outs, lses, dils = [], [], []
    for gi, (window, dil) in enumerate(DIL_GROUPS):
        view = _linear_in(x, scale, shift, w[:, gi * 3 * hd:(gi + 1) * 3 * hd], tn=3 * hd, dil=dil)
        o, lse = _dil_group(view, window, dil, rel_table)
        outs.append(o)
        lses.append(lse)
        dils.append(dil)
    return _dil_merge(outs, lses, dils)


def kernel(x, c, rel_table, mod_w, mod_b, ln_g, ln_b, gla_w_in, gla_w_a2, gla_b_a, gla_norm_g, gla_w_o,
           nsa_w_in, nsa_cmp_pe, nsa_cmp_w1, nsa_cmp_b1, nsa_cmp_w2, nsa_w_o, dil_w_in, dil_w_o,
           ffn_w_up, ffn_conv_w, ffn_conv_b, ffn_w_down):
    shift, scale, gate = _modulation(c, mod_w, mod_b)
    for i in range(DEPTH):
        kind, j = i % N_MIXERS, i // N_MIXERS
        m = 2 * i
        if kind == 0:
            a = _gla_mixer(x, scale[m], shift[m], gla_w_in[j], gla_w_a2[j], gla_b_a[j], gla_norm_g[j])
            w_o = gla_w_o[j]
        elif kind == 1:
            a = _nsa_mixer(x, scale[m], shift[m], nsa_w_in[j], nsa_cmp_pe[j], nsa_cmp_w1[j], nsa_cmp_b1[j],
                           nsa_cmp_w2[j], rel_table)
            w_o = nsa_w_o[j]
        else:
            a = _dil_mixer(x, scale[m], shift[m], dil_w_in[j], rel_table)
            w_o = dil_w_o[j]
        x = _out_norm(a, w_o.astype(BF16), x, gate[m], ln_g[i, 0], ln_b[i, 0])
        a = _ffn_up(x, scale[m + 1], shift[m + 1], ffn_w_up[i].astype(BF16), ffn_conv_w[i], ffn_conv_b[i])
        x = _out_norm(a, ffn_w_down[i].astype(BF16), x, gate[m + 1], ln_g[i, 1], ln_b[i, 1])
    return x
```

```python
import functools
import math

import jax
import jax.numpy as jnp
import numpy as np
from jax import lax
from jax.experimental import pallas as pl
from jax.experimental.pallas import tpu as pltpu

F32 = jnp.float32
BF16 = jnp.bfloat16

D_MODEL = 1024
DEPTH = 4
N_MIXERS = 3
HEAD_DIM = 64
N_HEADS = D_MODEL // HEAD_DIM
REL_BUCKETS = 32
REL_MAX_DIST = 2048
LN_EPS = 1e-5
GLA_HEADS = 4
GLA_DK = D_MODEL // 2 // GLA_HEADS
GLA_DV = D_MODEL // GLA_HEADS
GLA_GATE_RANK = 16
GLA_TAU = 16.0
GLA_CHUNK = 64
NSA_G = 4
NSA_R = N_HEADS // NSA_G
CMP_LEN = 32
CMP_STRIDE = 16
CMP_HIDDEN = 256
SEL_LEN = 64
SEL_TOP = 16
WIN_LEN = 512
DIL_GROUPS = ((128, 1), (512, 4), (2048, 16))
D_FF = 2816
DN_ALPHA = (2 * DEPTH) ** 0.25

NEG = -1e30
VMEM_LIMIT = 56 * 1024 * 1024
LANES = 128
SUB_Q = 128


def _params(*sem):
    return pltpu.CompilerParams(dimension_semantics=sem, vmem_limit_bytes=VMEM_LIMIT)


def _dot(a, b):
    return jnp.dot(a, b, preferred_element_type=F32)


def _dot_nt(a, b):
    return lax.dot_general(a, b, (((1,), (1,)), ((), ())), preferred_element_type=F32)


def _dot_tn(a, b):
    return lax.dot_general(a, b, (((0,), (0,)), ((), ())), preferred_element_type=F32)


def _split(a):
    hi = a.astype(BF16)
    lo = (a - hi.astype(F32)).astype(BF16)
    return hi, lo


def _dot_hi_exact_rhs(a, b_bf16):
    hi, lo = _split(a)
    return _dot(hi, b_bf16) + _dot(lo, b_bf16)


def _silu(x):
    return x * jax.nn.sigmoid(x)


def _bucket_starts():
    n = np.arange(2 * REL_MAX_DIST)
    exact = REL_BUCKETS // 2
    logn = np.log(np.maximum(n, 1).astype(np.float32) / np.float32(exact))
    large = exact + (logn / np.float32(math.log(REL_MAX_DIST / exact)) * np.float32(REL_BUCKETS - exact)).astype(np.int32)
    bucket = np.where(n < exact, n, np.minimum(large, REL_BUCKETS - 1))
    assert np.all(np.diff(bucket) >= 0) and bucket[-1] == REL_BUCKETS - 1
    return [int(np.argmax(bucket >= k)) for k in range(REL_BUCKETS)]


BUCKET_START = _bucket_starts()
FAR_DIST = BUCKET_START[-1]


def _rel_bias(rel_table, dist):
    shape = (rel_table.shape[1],) + (1,) * dist.ndim
    out = jnp.broadcast_to(rel_table[0].reshape(shape), (rel_table.shape[1],) + dist.shape)
    for k in range(1, REL_BUCKETS):
        out = jnp.where(dist[None] >= BUCKET_START[k], rel_table[k].reshape(shape), out)
    return out


def _mod_kernel(c_ref, w_ref, b_ref, o_ref):
    cs = _silu(c_ref[...])
    ch, cl = _split(cs)
    wh, wl = _split(w_ref[0])
    o_ref[0] = _dot(ch, wh) + _dot(ch, wl) + _dot(cl, wh) + b_ref[0]


def _modulation(c, mod_w, mod_b):
    b, d = c.shape
    n_sub = mod_w.shape[0] * mod_w.shape[1]
    rows = 8
    cp = jnp.zeros((rows, d), F32).at[:b].set(c)
    w = mod_w.reshape(n_sub, d, 3 * d)
    bias = mod_b.reshape(n_sub, 1, 3 * d)
    tn = d
    out = pl.pallas_call(
        _mod_kernel,
        grid=(n_sub, 3 * d // tn),
        in_specs=[pl.BlockSpec((rows, d), lambda i, j: (0, 0)),
                  pl.BlockSpec((1, d, tn), lambda i, j: (i, 0, j)),
                  pl.BlockSpec((1, 1, tn), lambda i, j: (i, 0, j))],
        out_specs=pl.BlockSpec((1, rows, tn), lambda i, j: (i, 0, j)),
        out_shape=jax.ShapeDtypeStruct((n_sub, rows, 3 * d), F32),
        compiler_params=_params("parallel", "parallel"),
        name="modulation",
    )(cp, w, bias)
    out = out[:, :b]
    shift, scale, gate = jnp.split(out, 3, axis=-1)
    return shift[:, :, None, :], scale[:, :, None, :], gate[:, :, None, :]


def _linear_in_kernel(x_ref, sc_ref, sh_ref, w_ref, *rest, dil):
    h = (x_ref[0] * (1.0 + sc_ref[0]) + sh_ref[0]).astype(BF16)
    if dil == 1:
        o_ref, = rest
        o_ref[0] = _dot(h, w_ref[...]).astype(o_ref.dtype)
    else:
        perm_ref, o_ref = rest
        res = _dot(_dot(perm_ref[...], h).astype(BF16), w_ref[...])
        rows, n = res.shape[0] // dil, res.shape[1]
        for r in range(dil):
            o_ref[0, :, r * n:(r + 1) * n] = res[r * rows:(r + 1) * rows].astype(o_ref.dtype)


def _linear_in(x, scale, shift, w, tn, dil=1):
    b, s, d = x.shape
    n = w.shape[1]
    tm = min(s, 1024 if dil == 1 else 512)
    assert n % tn == 0 and s % tm == 0 and tm % (16 * dil) == 0 and (dil == 1 or n == tn)
    in_specs = [pl.BlockSpec((1, tm, d), lambda j, bi, i: (bi, i, 0)),
                pl.BlockSpec((1, 1, d), lambda j, bi, i: (bi, 0, 0)),
                pl.BlockSpec((1, 1, d), lambda j, bi, i: (bi, 0, 0)),
                pl.BlockSpec((d, tn), lambda j, bi, i: (0, j), pipeline_mode=pl.Buffered(1) if n == tn else None)]
    args = [x, scale, shift, w]
    if dil > 1:
        out_row = jnp.arange(tm)
        src = (out_row % (tm // dil)) * dil + out_row // (tm // dil)
        in_specs.append(pl.BlockSpec((tm, tm), lambda j, bi, i: (0, 0)))
        args.append((src[:, None] == jnp.arange(tm)[None, :]).astype(BF16))
    return pl.pallas_call(
        functools.partial(_linear_in_kernel, dil=dil),
        grid=(n // tn, b, s // tm),
        in_specs=in_specs,
        out_specs=pl.BlockSpec((1, tm // dil, dil * tn), lambda j, bi, i: (bi, i, j)),
        out_shape=jax.ShapeDtypeStruct((b, s // dil, dil * n), BF16),
        compiler_params=_params("parallel", "parallel", "parallel"),
        name="linear_in",
    )(*args)


OUT_ROWS = 512


def _out_norm_kernel(a_ref, w_ref, x_ref, gate_ref, g_ref, b_ref, o_ref):
    chunks = [slice(r, r + OUT_ROWS) for r in range(0, a_ref.shape[1], OUT_ROWS)]
    ys = [_dot(a_ref[0, sl, :], w_ref[...]) for sl in chunks]
    for sl, y in zip(chunks, ys):
        z = DN_ALPHA * x_ref[0, sl, :] + (1.0 + gate_ref[0]) * y
        mu = jnp.mean(z, axis=-1, keepdims=True)
        zc = z - mu
        var = jnp.mean(zc * zc, axis=-1, keepdims=True)
        o_ref[0, sl, :] = zc * lax.rsqrt(var + LN_EPS) * g_ref[...] + b_ref[...]


def _out_norm(a, w, x, gate, ln_g, ln_b):
    b, s, d = x.shape
    k = a.shape[-1]
    tm = min(s, 2048 if k <= d else 1024)
    return pl.pallas_call(
        _out_norm_kernel,
        grid=(b, s // tm),
        in_specs=[pl.BlockSpec((1, tm, k), lambda bi, i: (bi, i, 0)),
                  pl.BlockSpec((k, d), lambda bi, i: (0, 0)),
                  pl.BlockSpec((1, tm, d), lambda bi, i: (bi, i, 0)),
                  pl.BlockSpec((1, 1, d), lambda bi, i: (bi, 0, 0)),
                  pl.BlockSpec((1, d), lambda bi, i: (0, 0)),
                  pl.BlockSpec((1, d), lambda bi, i: (0, 0))],
        out_specs=pl.BlockSpec((1, tm, d), lambda bi, i: (bi, i, 0)),
        out_shape=jax.ShapeDtypeStruct((b, s, d), F32),
        compiler_params=_params("parallel", "parallel"),
        name="out_norm",
    )(a, w, x, gate, ln_g.reshape(1, d), ln_b.reshape(1, d))


FFN_HALO = 16
FFN_COLS = 256


def _ffn_up_kernel(x_ref, xh_ref, sc_ref, sh_ref, wu_ref, wg_ref, cw_ref, cb_ref, o_ref, h_ref):
    i = pl.program_id(1)
    sc = 1.0 + sc_ref[0]
    sh = sh_ref[0]
    h_ref[0:FFN_HALO, :] = (xh_ref[0] * sc + sh).astype(BF16)
    h_ref[FFN_HALO:, :] = (x_ref[0] * sc + sh).astype(BF16)
    keep = (i > 0).astype(F32)
    cw = cw_ref[...]
    for c in range(0, o_ref.shape[2], FFN_COLS):
        cols = slice(c, c + FFN_COLS)
        u = _dot(h_ref[...], wu_ref[:, cols])
        u = jnp.concatenate([u[0:FFN_HALO] * keep, u[FFN_HALO:]], axis=0)
        g = _dot(h_ref[FFN_HALO:, :], wg_ref[:, cols])
        a = u * cw[1:2, cols] + pltpu.roll(u, 1, 0) * cw[0:1, cols]
        conv = (u * cw[2:3, cols] + pltpu.roll(a, 1, 0))[FFN_HALO:] + cb_ref[:, cols]
        o_ref[0, :, cols] = (_silu(conv) * g).astype(o_ref.dtype)


def _ffn_up(x, scale, shift, w_up, conv_w, conv_b, tm=1024):
    b, s, d = x.shape
    f = conv_w.shape[1]
    assert s % tm == 0 and tm % FFN_HALO == 0 and f % FFN_COLS == 0
    hb = tm // FFN_HALO
    once = pl.Buffered(1)
    return pl.pallas_call(
        _ffn_up_kernel,
        grid=(b, s // tm),
        in_specs=[pl.BlockSpec((1, tm, d), lambda bi, i: (bi, i, 0)),
                  pl.BlockSpec((1, FFN_HALO, d), lambda bi, i: (bi, jnp.maximum(i * hb - 1, 0), 0)),
                  pl.BlockSpec((1, 1, d), lambda bi, i: (bi, 0, 0)),
                  pl.BlockSpec((1, 1, d), lambda bi, i: (bi, 0, 0)),
                  pl.BlockSpec((d, f), lambda bi, i: (0, 0), pipeline_mode=once),
                  pl.BlockSpec((d, f), lambda bi, i: (0, 1), pipeline_mode=once),
                  pl.BlockSpec((3, f), lambda bi, i: (0, 0)),
                  pl.BlockSpec((1, f), lambda bi, i: (0, 0))],
        out_specs=pl.BlockSpec((1, tm, f), lambda bi, i: (bi, i, 0)),
        out_shape=jax.ShapeDtypeStruct((b, s, f), BF16),
        scratch_shapes=[pltpu.VMEM((tm + FFN_HALO, d), BF16)],
        compiler_params=_params("parallel", "parallel"),
        name="ffn_up",
    )(x, x, scale, shift, w_up, w_up, conv_w, conv_b.reshape(1, f))


def _gla_kernel(q_ref, k_ref, v_ref, r_ref, a_ref, wa_ref, ba_ref, ng_ref, o_ref, st_ref):
    tc = q_ref.shape[1]
    nchunk = tc // GLA_CHUNK

    @pl.when(pl.program_id(2) == 0)
    def _():
        st_ref[...] = jnp.zeros_like(st_ref)

    z = _dot(a_ref[0], wa_ref[...]) + ba_ref[...]
    log_a = (jnp.minimum(z, 0.0) - jnp.log1p(jnp.exp(-jnp.abs(z)))) / GLA_TAU
    ci = lax.broadcasted_iota(jnp.int32, (GLA_CHUNK, GLA_CHUNK), 0)
    cj = lax.broadcasted_iota(jnp.int32, (GLA_CHUNK, GLA_CHUNK), 1)
    causal = cj <= ci
    tri = jnp.where(causal, 1.0, 0.0).astype(BF16)
    chunks = [slice(c * GLA_CHUNK, (c + 1) * GLA_CHUNK) for c in range(nchunk)]
    la_hi, la_lo = _split(log_a)
    cums = [_dot(tri, la_hi[sl]) + _dot(tri, la_lo[sl]) for sl in chunks]
    bcum = jnp.concatenate(cums, axis=0)
    btot = jnp.concatenate([jnp.broadcast_to(cm[GLA_CHUNK - 1:], cm.shape) for cm in cums], axis=0)
    qf = q_ref[0].astype(F32) * (GLA_DK ** -0.5)
    kf = k_ref[0].astype(F32)
    q_dec = (qf * jnp.exp(bcum)).astype(BF16)
    k_inv = (kf * jnp.exp(-bcum)).astype(BF16)
    k_dec = (kf * jnp.exp(btot - bcum)).astype(BF16)
    dl = jnp.exp(btot)
    v = v_ref[0]
    att = [jnp.where(causal, _dot_nt(q_dec[sl], k_inv[sl]), 0.0).astype(BF16) for sl in chunks]
    kv = [_dot_tn(v[sl], k_dec[sl]) for sl in chunks]
    o_intra = [_dot(att[c], v[chunks[c]]) for c in range(nchunk)]
    st = st_ref[...]
    o_inter = []
    for c, sl in enumerate(chunks):
        o_inter.append(_dot_nt(q_dec[sl], st.astype(BF16)))
        st = st * dl[c * GLA_CHUNK:c * GLA_CHUNK + 1] + kv[c]
    st_ref[...] = st
    o = jnp.concatenate(o_intra, axis=0) + jnp.concatenate(o_inter, axis=0)
    o = o * lax.rsqrt(jnp.mean(o * o, axis=-1, keepdims=True) + LN_EPS) * ng_ref[...]
    o_ref[0] = (o * _silu(r_ref[0].astype(F32))).astype(o_ref.dtype)


def _gla_core(proj, w_a2, b_a, norm_g, tc=1024):
    b, s, _ = proj.shape
    h, dk, dv = GLA_HEADS, GLA_DK, GLA_DV
    assert s % tc == 0
    wa = jnp.zeros((LANES, h * dk), F32).at[:GLA_GATE_RANK].set(w_a2).astype(BF16)
    kb, vb, rb, ab = (h * dk) // dk, (2 * h * dk) // dv, (2 * h * dk + h * dv) // dv, (2 * h * dk + 2 * h * dv) // LANES
    return pl.pallas_call(
        _gla_kernel,
        grid=(b, h, s // tc),
        in_specs=[pl.BlockSpec((1, tc, dk), lambda bi, hi, t: (bi, t, hi)),
                  pl.BlockSpec((1, tc, dk), lambda bi, hi, t: (bi, t, kb + hi)),
                  pl.BlockSpec((1, tc, dv), lambda bi, hi, t: (bi, t, vb + hi)),
                  pl.BlockSpec((1, tc, dv), lambda bi, hi, t: (bi, t, rb + hi)),
                  pl.BlockSpec((1, tc, LANES), lambda bi, hi, t: (bi, t, ab)),
                  pl.BlockSpec((LANES, dk), lambda bi, hi, t: (0, hi)),
                  pl.BlockSpec((1, dk), lambda bi, hi, t: (0, hi)),
                  pl.BlockSpec((1, dv), lambda bi, hi, t: (0, 0))],
        out_specs=pl.BlockSpec((1, tc, dv), lambda bi, hi, t: (bi, t, hi)),
        out_shape=jax.ShapeDtypeStruct((b, s, h * dv), BF16),
        scratch_shapes=[pltpu.VMEM((dv, dk), F32)],
        compiler_params=_params("parallel", "parallel", "arbitrary"),
        name="gla_core",
    )(proj, proj, proj, proj, proj, wa, b_a.reshape(1, h * dk), norm_g.reshape(1, dv))


def _pad_cols(w, n):
    return jnp.pad(w, ((0, 0), (0, n - w.shape[1])))


def _gla_mixer(x, scale, shift, w_in, w_a2, b_a, norm_g):
    n_in = 2 * GLA_HEADS * GLA_DK + 2 * GLA_HEADS * GLA_DV + LANES
    proj = _linear_in(x, scale, shift, _pad_cols(w_in, n_in).astype(BF16), tn=n_in)
    return _gla_core(proj, w_a2, b_a, norm_g)


def _compress_kernel(lo_ref, hi_ref, pe_ref, w1_ref, b1_ref, w2_ref, o_ref):
    half = lo_ref.shape[-1]
    pe = pe_ref[0]
    a_lo = (lo_ref[0, 0, 0].astype(F32) + pe[:, :half]).astype(BF16)
    a_hi = (hi_ref[0, 0, 0].astype(F32) + pe[:, half:]).astype(BF16)
    h1 = _dot(a_lo, w1_ref[0, :half, :]) + _dot(a_hi, w1_ref[0, half:, :]) + b1_ref[0]
    o_ref[0, 0, 0] = _dot(_silu(h1).astype(BF16), w2_ref[0])


def _compress(kv, pe, w1, b1, w2):
    two, b, s, _ = kv.shape
    g, dh = NSA_G, HEAD_DIM
    npc = s // CMP_STRIDE
    half = CMP_STRIDE * dh
    a = kv.reshape(two, b, npc, CMP_STRIDE, g, dh).transpose(0, 1, 4, 2, 3, 5).reshape(two, b, g, npc, half)
    a_hi = jnp.concatenate([a[:, :, :, 1:], jnp.zeros_like(a[:, :, :, :1])], axis=3)
    spec_a = pl.BlockSpec((1, 1, 1, npc, half), lambda c, bi, gi: (c, bi, gi, 0, 0))
    return pl.pallas_call(
        _compress_kernel,
        grid=(two, b, g),
        in_specs=[spec_a, spec_a,
                  pl.BlockSpec((1, 1, 2 * half), lambda c, bi, gi: (c, 0, 0)),
                  pl.BlockSpec((1, 2 * half, CMP_HIDDEN), lambda c, bi, gi: (c, 0, 0)),
                  pl.BlockSpec((1, 1, CMP_HIDDEN), lambda c, bi, gi: (c, 0, 0)),
                  pl.BlockSpec((1, CMP_HIDDEN, dh), lambda c, bi, gi: (c, 0, 0))],
        out_specs=pl.BlockSpec((1, 1, 1, npc, dh), lambda c, bi, gi: (c, bi, gi, 0, 0)),
        out_shape=jax.ShapeDtypeStruct((two, b, g, npc, dh), F32),
        compiler_params=_params("parallel", "parallel", "parallel"),
        name="nsa_compress",
    )(a, a_hi, pe.reshape(two, 1, 2 * half), w1.astype(BF16), b1.reshape(two, 1, CMP_HIDDEN), w2.astype(BF16))


def _stack_heads(q):
    return jnp.concatenate([q[:, r * HEAD_DIM:(r + 1) * HEAD_DIM] for r in range(NSA_R)], axis=0)


def _unstack_heads(o, rows):
    return jnp.concatenate([o[r * rows:(r + 1) * rows] for r in range(NSA_R)], axis=1)


CMP_WIDTHS = 8
CMP_TQ = 512
LOG2E = 1.4426950408889634


def _nsa_cmp_kernel(q_ref, fq_ref, kc_ref, vc_ref, nb_ref, pb_ref, o_ref, mb_ref, imp_ref, *, n_tile):
    i = pl.program_id(2)
    npc = kc_ref.shape[2] // 2
    args = (q_ref, fq_ref, kc_ref, vc_ref, nb_ref, pb_ref, o_ref, imp_ref)
    quarter = n_tile // CMP_WIDTHS
    widths = [(k + 1) * npc // CMP_WIDTHS for k in range(CMP_WIDTHS)]
    nblk = mb_ref.shape[3]
    blocks = [min(nblk, -(-((k + 1) * nblk // CMP_WIDTHS) // LANES) * LANES) for k in range(CMP_WIDTHS)]

    @pl.when(i == 0)
    def _():
        _cmp_attend(i, widths[0], *args)

    for k in range(CMP_WIDTHS):
        @pl.when((i > 0) & (i >= k * quarter) & (i < (k + 1) * quarter))
        def _(k=k):
            sel = _Selection(i - 1, imp_ref[...], mb_ref, blocks[k])
            per_phase = -(-(SEL_TOP - 3) // CMP_PHASES)
            _cmp_attend(i, widths[k], *args, after_phase=lambda: sel.rounds(per_phase))
            sel.finish()

    @pl.when(i == n_tile)
    def _():
        _Selection(i - 1, imp_ref[...], mb_ref, nblk).finish()


CMP_PHASES = 4
CMP_NEAR = 128


def _cmp_attend(i, width, q_ref, fq_ref, kc_ref, vc_ref, nb_ref, pb_ref, o_ref, imp_ref, after_phase=lambda: None):
    skip = kc_ref.shape[2] // 2 - width
    near = min(nb_ref.shape[2], width)
    n_sub, rows = CMP_TQ // SUB_Q, NSA_R * SUB_Q
    starts = [pl.multiple_of((i * n_sub + j) * (SUB_Q // CMP_STRIDE), 8) for j in range(n_sub)]

    scores = []
    for j in range(n_sub):
        kwin = kc_ref[0, 0, pl.ds(starts[j] + skip, width), :].astype(BF16)
        qs = _stack_heads(q_ref[0, j * SUB_Q:(j + 1) * SUB_Q, :]).astype(F32) * (HEAD_DIM ** -0.5 * LOG2E)
        qp = jnp.concatenate([qs, fq_ref[0]], axis=1).astype(BF16)
        sj = _dot_nt(qp, kwin)
        sj_near = sj[:, width - near:] + nb_ref[0, :, nb_ref.shape[2] - near:]
        scores.append(sj_near if near == width else jnp.concatenate([sj[:, :width - near], sj_near], axis=1))
    after_phase()
    s = jnp.concatenate(scores, axis=0)
    m = jnp.max(s, axis=1, keepdims=True)
    e = jnp.exp2(s - m)
    den = jnp.sum(e, axis=1, keepdims=True)
    p = e * jnp.where(m > 0.5 * NEG, 1.0 / den, 0.0)
    p_bf = p.astype(BF16)
    after_phase()
    for j in range(n_sub):
        vwin = vc_ref[0, 0, pl.ds(starts[j] + skip, width), :].astype(BF16)
        o = _dot(p_bf[j * rows:(j + 1) * rows], vwin)
        o_ref[0, j * SUB_Q:(j + 1) * SUB_Q, :] = _unstack_heads(o, SUB_Q).astype(o_ref.dtype)
    after_phase()

    imps = []
    for j in range(n_sub):
        pg = p[j * rows:j * rows + SUB_Q]
        for r in range(1, NSA_R):
            pg = pg + p[j * rows + r * SUB_Q:j * rows + (r + 1) * SUB_Q]
        imps.append(_dot_hi_exact_rhs(pg, pb_ref[pl.ds(starts[j] + skip, width), :].astype(BF16)))
    imp_ref[...] = jnp.concatenate(imps, axis=0)
    after_phase()


class _Selection:
    def __init__(self, i, imp, mb_ref, nblk):
        self.mb_ref, self.nblk = mb_ref, nblk
        if nblk < mb_ref.shape[3]:
            imp = imp[:, :nblk]
            mb_ref[0, 0, :, nblk:] = jnp.full((CMP_TQ, mb_ref.shape[3] - nblk), NEG, mb_ref.dtype)
        blk = lax.broadcasted_iota(jnp.int32, (CMP_TQ, nblk), 1)
        self.blk_f = blk.astype(F32)
        t = i * CMP_TQ + lax.broadcasted_iota(jnp.int32, (CMP_TQ, nblk), 0)
        cur = t // SEL_LEN
        self.forced = (blk == 0) | (blk == cur) | (blk == cur - 1)
        self.work = jnp.where((blk * SEL_LEN <= t) & jnp.logical_not(self.forced), imp, NEG)
        self.free = self.work > 0.5 * NEG
        self.left = SEL_TOP - 3

    def rounds(self, n):
        n = min(n, self.left)
        for _ in range(n):
            mx = jnp.max(self.work, axis=1, keepdims=True)
            first = jnp.min(jnp.where(self.work == mx, self.blk_f, 1e9), axis=1, keepdims=True)
            self.work = jnp.where(self.blk_f == first, 2.0 * NEG, self.work)
        self.left -= n

    def finish(self):
        self.rounds(self.left)
        sel = self.forced | (self.free & (self.work < 1.5 * NEG))
        self.mb_ref[0, 0, :, :self.nblk] = jnp.where(sel, 0.0, NEG).astype(self.mb_ref.dtype)


def _nsa_cmp(proj, kcmp, vcmp, rel_table):
    b, s, _ = proj.shape
    g, dh = NSA_G, HEAD_DIM
    npc = s // CMP_STRIDE
    nblk = s // SEL_LEN
    front = npc - SUB_Q // CMP_STRIDE
    assert s % CMP_TQ == 0 and nblk >= SEL_TOP
    n_tile = s // CMP_TQ
    assert n_tile % CMP_WIDTHS == 0
    kpad = jnp.pad(kcmp, ((0, 0), (0, 0), (front, npc - front), (0, LANES - dh)))
    kpad = kpad.at[:, :, :front, dh].set(1.0)
    kpad = kpad.at[:, :, :, dh + 1:dh + 3].set(1.0)
    vpad = jnp.pad(vcmp, ((0, 0), (0, 0), (front, npc - front), (0, 0)))
    b_far = rel_table[REL_BUCKETS - 1].reshape(g, NSA_R, 1) * LOG2E
    hi = b_far.astype(BF16).astype(F32)
    fq = jnp.zeros((g, NSA_R, SUB_Q, LANES - dh), F32).at[..., 0].set(NEG).at[..., 1].set(hi).at[..., 2].set(b_far - hi)
    fq = fq.reshape(g, NSA_R * SUB_Q, LANES - dh)
    n_near = min(npc, CMP_NEAR)
    assert n_near == npc or CMP_STRIDE * (n_near - SUB_Q // CMP_STRIDE + 1) - (CMP_LEN - 1) >= FAR_DIST
    qi = jnp.arange(SUB_Q)[:, None]
    cc = jnp.arange(npc - n_near, npc)[None, :]
    dd = qi - (CMP_LEN - 1) + CMP_STRIDE * (front - cc)
    nb = _rel_bias(rel_table, dd).reshape(g, NSA_R, SUB_Q, n_near) * LOG2E - b_far[..., None]
    nb = jnp.where(dd[None, None] >= 0, nb, NEG).reshape(g, NSA_R * SUB_Q, n_near)
    diff = jnp.arange(2 * npc)[:, None] - front - (SEL_LEN // CMP_STRIDE) * jnp.arange(nblk)[None, :]
    pool = jnp.where((diff == -1) | (diff == 3), 1.0, jnp.where((diff >= 0) & (diff <= 2), 2.0, 0.0))
    return pl.pallas_call(
        functools.partial(_nsa_cmp_kernel, n_tile=n_tile),
        grid=(g, b, n_tile + 1),
        in_specs=[pl.BlockSpec((1, CMP_TQ, NSA_R * dh), lambda gi, bi, i: (bi, jnp.minimum(i, n_tile - 1), gi)),
                  pl.BlockSpec((1, NSA_R * SUB_Q, LANES - dh), lambda gi, bi, i: (gi, 0, 0)),
                  pl.BlockSpec((1, 1, 2 * npc, LANES), lambda gi, bi, i: (bi, gi, 0, 0)),
                  pl.BlockSpec((1, 1, 2 * npc, dh), lambda gi, bi, i: (bi, gi, 0, 0)),
                  pl.BlockSpec((1, NSA_R * SUB_Q, n_near), lambda gi, bi, i: (gi, 0, 0)),
                  pl.BlockSpec((2 * npc, nblk), lambda gi, bi, i: (0, 0))],
        out_specs=[pl.BlockSpec((1, CMP_TQ, NSA_R * dh), lambda gi, bi, i: (bi, jnp.minimum(i, n_tile - 1), gi)),
                   pl.BlockSpec((1, 1, CMP_TQ, nblk), lambda gi, bi, i: (bi, gi, jnp.maximum(i - 1, 0), 0))],
        out_shape=[jax.ShapeDtypeStruct((b, s, N_HEADS * dh), BF16),
                   jax.ShapeDtypeStruct((b, g, s, nblk), BF16)],
        scratch_shapes=[pltpu.VMEM((CMP_TQ, nblk), F32)],
        compiler_params=_params("parallel", "parallel", "arbitrary"),
        name="nsa_cmp_select",
    )(proj, fq, kpad, vpad, nb, pool.astype(F32))


SEL_T = 256
SEL_WIDE = 8
SEL_NEAR = -(-(FAR_DIST + SEL_T - 1) // SEL_T)


def _nsa_sel_kernel(q_ref, fq_ref, mb_ref, k_ref, v_ref, bt_ref, o_ref, qp_ref, m_ref, acc_ref):
    i = pl.program_id(2)
    rows = NSA_R * SEL_T
    fw = (k_ref.shape[3] - LANES)
    kh = fw // (SEL_T // SEL_LEN)
    qs = (_stack_heads(q_ref[0]).astype(F32) * (HEAD_DIM ** -0.5 * LOG2E)).astype(BF16)
    mb = mb_ref[0, 0]
    for half in range(2):
        mbh = jnp.concatenate([mb[:, half * fw:(half + 1) * fw]] * NSA_R, axis=0)
        qp_ref[half] = jnp.concatenate([qs, fq_ref[0], mbh], axis=1)
    m_ref[...] = jnp.full_like(m_ref, NEG)
    acc_ref[...] = jnp.zeros_like(acc_ref)

    def update(kt, n_tile, half, bias):
        k0 = pl.multiple_of(kt * SEL_T, SEL_T)
        kk = k_ref[0, 0, pl.ds(k0, n_tile * SEL_T), :]
        vv = v_ref[0, 0, pl.ds(k0, n_tile * SEL_T), :]
        s = _dot_nt(qp_ref[half], kk)
        if bias is not None:
            s = s + bias
        m_prev = m_ref[...]
        m_new = jnp.maximum(m_prev, jnp.max(s, axis=1, keepdims=True))
        alpha = jnp.exp2(m_prev - m_new)
        p = jnp.exp2(s - jnp.concatenate([m_new] * (n_tile * SEL_T // LANES), axis=1))
        acc_ref[...] = acc_ref[...] * alpha + _dot(p.astype(BF16), vv)
        m_ref[...] = m_new

    def far_region(lo, hi, half):
        n = jnp.maximum(hi - lo, 0)

        def wide(j, carry):
            update(lo + SEL_WIDE * j, SEL_WIDE, half, None)
            return carry

        lax.fori_loop(0, n // SEL_WIDE, wide, 0)
        piece = SEL_WIDE // 2
        while piece >= 1:
            @pl.when((n & piece) != 0)
            def _(piece=piece):
                update(lo + (n & ~(2 * piece - 1)), piece, half, None)
            piece //= 2

    n_far = jnp.maximum(i - SEL_NEAR + 1, 0)
    far_region(0, jnp.minimum(n_far, kh), 0)
    far_region(kh, n_far, 1)

    def near(e):
        kt = i - e
        update(kt, 1, jnp.where(kt < kh, 0, 1), bt_ref[0, :, e * SEL_T:(e + 1) * SEL_T])

    @pl.when(i >= SEL_NEAR - 1)
    def _():
        for e in range(SEL_NEAR - 1, -1, -1):
            near(e)

    @pl.when(i < SEL_NEAR - 1)
    def _():
        for e in range(SEL_NEAR - 2, -1, -1):
            @pl.when(i - e >= 0)
            def _(e=e):
                near(e)

    acc = acc_ref[...]
    o = acc[:, :HEAD_DIM] / acc[:, HEAD_DIM:HEAD_DIM + 1]
    o_ref[0] = _unstack_heads(o, SEL_T).astype(o_ref.dtype)


def _nsa_sel(proj, ks, vs, mb, rel_table):
    b, s, _ = proj.shape
    g, dh = NSA_G, HEAD_DIM
    nblk = s // SEL_LEN
    fw = nblk // 2
    rows = NSA_R * SEL_T
    pos = jnp.arange(s)
    onehot = (((pos // SEL_LEN) % fw)[:, None] == jnp.arange(fw)[None, :]).astype(BF16)
    ones2 = jnp.zeros((s, LANES - dh), BF16).at[:, :2].set(1.0)
    kp = jnp.concatenate([ks, jnp.broadcast_to(ones2, (b, g, s, LANES - dh)),
                          jnp.broadcast_to(onehot, (b, g, s, fw))], axis=-1)
    ones1 = jnp.zeros((s, LANES - dh), BF16).at[:, :1].set(1.0)
    vp = jnp.concatenate([vs, jnp.broadcast_to(ones1, (b, g, s, LANES - dh))], axis=-1)
    b_far = rel_table[REL_BUCKETS - 1].reshape(g, NSA_R) * LOG2E
    hi = b_far.astype(BF16)
    lo = (b_far - hi.astype(F32)).astype(BF16)
    fq = jnp.zeros((g, NSA_R, SEL_T, LANES - dh), BF16)
    fq = fq.at[:, :, :, 0].set(hi[:, :, None]).at[:, :, :, 1].set(lo[:, :, None]).reshape(g, rows, LANES - dh)
    qi = jnp.arange(SEL_T)[:, None]
    cc = jnp.arange(SEL_NEAR * SEL_T)[None, :]
    dd = (cc // SEL_T) * SEL_T + qi - (cc % SEL_T)
    bias = _rel_bias(rel_table, dd).reshape(g, NSA_R, SEL_T, -1) * LOG2E - b_far[:, :, None, None]
    bt = jnp.where(dd[None, None] >= 0, bias, NEG).reshape(g, rows, SEL_NEAR * SEL_T)
    return pl.pallas_call(
        _nsa_sel_kernel,
        grid=(g, b, s // SEL_T),
        in_specs=[pl.BlockSpec((1, SEL_T, NSA_R * dh), lambda gi, bi, i: (bi, i, gi)),
                  pl.BlockSpec((1, rows, LANES - dh), lambda gi, bi, i: (gi, 0, 0)),
                  pl.BlockSpec((1, 1, SEL_T, nblk), lambda gi, bi, i: (bi, gi, i, 0)),
                  pl.BlockSpec((1, 1, s, LANES + fw), lambda gi, bi, i: (bi, gi, 0, 0)),
                  pl.BlockSpec((1, 1, s, LANES), lambda gi, bi, i: (bi, gi, 0, 0)),
                  pl.BlockSpec((1, rows, SEL_NEAR * SEL_T), lambda gi, bi, i: (gi, 0, 0),
                               pipeline_mode=pl.Buffered(1))],
        out_specs=pl.BlockSpec((1, SEL_T, NSA_R * dh), lambda gi, bi, i: (bi, i, gi)),
        out_shape=jax.ShapeDtypeStruct((b, s, N_HEADS * dh), BF16),
        scratch_shapes=[pltpu.VMEM((2, rows, LANES + fw), BF16), pltpu.VMEM((rows, LANES), F32),
                        pltpu.VMEM((rows, LANES), F32)],
        compiler_params=_params("parallel", "parallel", "parallel"),
        name="nsa_selected",
    )(proj, fq, mb, kp, vp, bt)


WIN_T = 512


def _softmax2_rows(s):
    m = jnp.max(s, axis=1, keepdims=True)
    e = jnp.exp2(s - m)
    den = jnp.sum(e, axis=1, keepdims=True)
    return e, 1.0 / den, m * (1.0 / LOG2E) + jnp.log(den)


def _nsa_win_kernel(q_ref, kp_ref, kc_ref, vp_ref, vc_ref, bt_ref, o_ref):
    i = pl.program_id(2)
    back = WIN_T
    span = back + SUB_Q
    n_sub, rows = WIN_T // SUB_Q, NSA_R * SUB_Q

    def body(first_tile):
        kcat = jnp.concatenate([kp_ref[0, 0], kc_ref[0, 0]], axis=0)
        vcat = jnp.concatenate([vp_ref[0, 0], vc_ref[0, 0]], axis=0)
        col = lax.broadcasted_iota(jnp.int32, (rows, span), 1)
        scores = []
        for j in range(n_sub):
            qs = _stack_heads(q_ref[0, j * SUB_Q:(j + 1) * SUB_Q, :]).astype(F32) * (HEAD_DIM ** -0.5 * LOG2E)
            s = _dot_nt(qs.astype(BF16), kcat[j * SUB_Q:j * SUB_Q + span]) + bt_ref[0]
            if first_tile:
                s = jnp.where(col >= back - j * SUB_Q, s, NEG)
            scores.append(s)
        e, inv, _ = _softmax2_rows(jnp.concatenate(scores, axis=0))
        e = e.astype(BF16)
        for j in range(n_sub):
            o = _dot(e[j * rows:(j + 1) * rows], vcat[j * SUB_Q:j * SUB_Q + span]) * inv[j * rows:(j + 1) * rows]
            o_ref[0, j * SUB_Q:(j + 1) * SUB_Q, :] = _unstack_heads(o, SUB_Q).astype(o_ref.dtype)

    @pl.when(i == 0)
    def _():
        body(True)

    @pl.when(i > 0)
    def _():
        body(False)


def _nsa_win(proj, kw, vw, rel_table):
    b, s, _ = proj.shape
    g, dh = NSA_G, HEAD_DIM
    span = WIN_T + SUB_Q
    qi = jnp.arange(SUB_Q)[:, None]
    cc = jnp.arange(span)[None, :]
    dd = qi + WIN_T - cc
    bias = _rel_bias(rel_table, dd)
    bt = jnp.where((dd[None] >= 0) & (dd[None] <= WIN_LEN - 1), bias * LOG2E, NEG).reshape(g, NSA_R * SUB_Q, span)
    prev = lambda gi, bi, i: (bi, gi, jnp.maximum(i - 1, 0), 0)
    cur = lambda gi, bi, i: (bi, gi, i, 0)
    return pl.pallas_call(
        _nsa_win_kernel,
        grid=(g, b, s // WIN_T),
        in_specs=[pl.BlockSpec((1, WIN_T, NSA_R * dh), lambda gi, bi, i: (bi, i, gi)),
                  pl.BlockSpec((1, 1, WIN_T, dh), prev), pl.BlockSpec((1, 1, WIN_T, dh), cur),
                  pl.BlockSpec((1, 1, WIN_T, dh), prev), pl.BlockSpec((1, 1, WIN_T, dh), cur),
                  pl.BlockSpec((1, NSA_R * SUB_Q, span), lambda gi, bi, i: (gi, 0, 0))],
        out_specs=pl.BlockSpec((1, WIN_T, NSA_R * dh), lambda gi, bi, i: (bi, i, gi)),
        out_shape=jax.ShapeDtypeStruct((b, s, N_HEADS * dh), BF16),
        compiler_params=_params("parallel", "parallel", "parallel"),
        name="nsa_window",
    )(proj, kw, kw, vw, vw, bt)


def _nsa_combine_kernel(oc_ref, os_ref, ow_ref, g_ref, e_ref, o_ref):
    d = o_ref.shape[2]
    sg = jax.nn.sigmoid(g_ref[0].astype(F32))
    ge = _dot_hi_exact_rhs(sg, e_ref[...])
    o = (ge[:, :d] * oc_ref[0].astype(F32) + ge[:, d:2 * d] * os_ref[0].astype(F32)
         + ge[:, 2 * d:] * ow_ref[0].astype(F32))
    o_ref[0] = o.astype(o_ref.dtype)


def _nsa_combine(o_cmp, o_sel, o_win, proj, gate_col, tm=512):
    b, s, d = o_cmp.shape
    c = jnp.arange(LANES)[:, None]
    n = jnp.arange(3 * d)[None, :]
    expand = (c == (n // d) * N_HEADS + (n % d) // HEAD_DIM).astype(BF16)
    spec = pl.BlockSpec((1, tm, d), lambda bi, i: (bi, i, 0))
    return pl.pallas_call(
        _nsa_combine_kernel,
        grid=(b, s // tm),
        in_specs=[spec, spec, spec,
                  pl.BlockSpec((1, tm, LANES), lambda bi, i: (bi, i, gate_col // LANES)),
                  pl.BlockSpec((LANES, 3 * d), lambda bi, i: (0, 0))],
        out_specs=spec,
        out_shape=jax.ShapeDtypeStruct((b, s, d), BF16),
        compiler_params=_params("parallel", "parallel"),
        name="nsa_combine",
    )(o_cmp, o_sel, o_win, proj, expand)


def _nsa_mixer(x, scale, shift, w_in, pe, w1, b1, w2, rel_table):
    b, s, _ = x.shape
    hd, gd = N_HEADS * HEAD_DIM, NSA_G * HEAD_DIM
    gate_col = hd + 6 * gd
    n_in = gate_col + LANES
    proj = _linear_in(x, scale, shift, _pad_cols(w_in, n_in).astype(BF16), tn=n_in)

    def group_major(t):
        return t.reshape(b, s, NSA_G, HEAD_DIM).transpose(0, 2, 1, 3)

    kc, vc, ks, vs, kw, vw = (proj[:, :, hd + n * gd: hd + (n + 1) * gd] for n in range(6))
    cmp = _compress(jnp.stack([kc, vc]), pe.reshape(2, -1), w1, b1, w2)
    o_cmp, mb = _nsa_cmp(proj, cmp[0], cmp[1], rel_table)
    o_sel = _nsa_sel(proj, group_major(ks), group_major(vs), mb, rel_table)
    o_win = _nsa_win(proj, group_major(kw), group_major(vw), rel_table)
    return _nsa_combine(o_cmp, o_sel, o_win, proj, gate_col)


DIL_BACK = 128
DIL_W = 256


def _dil_kernel(q_ref, kp_ref, kc_ref, vp_ref, vc_ref, bt_ref, o_ref, l_ref):
    t = pl.program_id(3)
    tq = q_ref.shape[1]
    span = DIL_BACK + SUB_Q
    kcat = jnp.concatenate([kp_ref[0, tq - DIL_BACK:, :], kc_ref[0]], axis=0)
    vcat = jnp.concatenate([vp_ref[0, tq - DIL_BACK:, :], vc_ref[0]], axis=0)
    lane_q = lax.broadcasted_iota(jnp.int32, (SUB_Q, DIL_W), 1) // HEAD_DIM
    lane_k = lax.broadcasted_iota(jnp.int32, (span, DIL_W), 1) // HEAD_DIM
    col = lax.broadcasted_iota(jnp.int32, (SUB_Q, span), 1)
    n_sub, n_head = tq // SUB_Q, DIL_W // HEAD_DIM
    scores = []
    for j in range(n_sub):
        qj = (q_ref[0, j * SUB_Q:(j + 1) * SUB_Q, :].astype(F32) * (HEAD_DIM ** -0.5 * LOG2E)).astype(BF16)
        kk = kcat[j * SUB_Q:j * SUB_Q + span]
        for hh in range(n_head):
            s = _dot_nt(jnp.where(lane_q == hh, qj, 0.0).astype(BF16), kk) + bt_ref[hh]
            if j == 0:
                s = jnp.where((col >= DIL_BACK) | (t > 0), s, NEG)
            scores.append(s)
    e_all, inv_all, lse_all = _softmax2_rows(jnp.concatenate(scores, axis=0))
    e_all = e_all.astype(BF16)
    for j in range(n_sub):
        vv = vcat[j * SUB_Q:j * SUB_Q + span]
        o = jnp.zeros((SUB_Q, DIL_W), F32)
        inv = jnp.zeros((SUB_Q, DIL_W), F32)
        lse = jnp.zeros((SUB_Q, DIL_W), F32)
        for hh in range(n_head):
            r0 = (j * n_head + hh) * SUB_Q
            o = o + _dot(e_all[r0:r0 + SUB_Q], jnp.where(lane_k == hh, vv, 0.0).astype(BF16))
            inv = jnp.where(lane_q == hh, inv_all[r0:r0 + SUB_Q], inv)
            lse = jnp.where(lane_q == hh, lse_all[r0:r0 + SUB_Q], lse)
        o_ref[0, j * SUB_Q:(j + 1) * SUB_Q, :] = (o * inv).astype(o_ref.dtype)
        l_ref[0, j * SUB_Q:(j + 1) * SUB_Q, :] = lse


def _dil_group(view, window, dil, rel_table):
    b, l, n_all = view.shape
    hd = N_HEADS * HEAD_DIM
    assert window // dil == DIL_BACK and n_all == dil * 3 * hd
    tq = min(1024, l)
    assert l % tq == 0 and tq >= DIL_BACK
    hpb = hd // DIL_W
    qi = jnp.arange(SUB_Q)[:, None]
    cc = jnp.arange(DIL_BACK + SUB_Q)[None, :]
    dd = qi + DIL_BACK - cc
    bt = jnp.where((dd[None] >= 0) & (dd[None] <= DIL_BACK), _rel_bias(rel_table, dd * dil) * LOG2E, NEG)

    def spec(comp, prev):
        def index(bi, r, hp, t):
            return (bi, jnp.maximum(t - 1, 0) if prev else t, (r * 3 + comp) * hpb + hp)
        return pl.BlockSpec((1, tq, DIL_W), index)

    out_spec = pl.BlockSpec((1, tq, DIL_W), lambda bi, r, hp, t: (bi, t, r * hpb + hp))
    return pl.pallas_call(
        _dil_kernel,
        grid=(b, dil, hpb, l // tq),
        in_specs=[spec(0, False), spec(1, True), spec(1, False), spec(2, True), spec(2, False),
                  pl.BlockSpec((DIL_W // HEAD_DIM, SUB_Q, DIL_BACK + SUB_Q), lambda bi, r, hp, t: (hp, 0, 0))],
        out_specs=[out_spec, out_spec],
        out_shape=[jax.ShapeDtypeStruct((b, l, dil * hd), BF16), jax.ShapeDtypeStruct((b, l, dil * hd), F32)],
        compiler_params=_params("parallel", "parallel", "parallel", "parallel"),
        name=f"dilated_attention_{dil}",
    )(view, view, view, view, view, bt)


def _dil_merge_kernel(*refs, dils):
    n = len(dils)
    o_refs, l_refs, out_ref, scratch = refs[:n], refs[n:2 * n], refs[2 * n], refs[2 * n + 1:]
    tm, d = out_ref.shape[1], out_ref.shape[2]

    def token_major(ref, dil, buf):
        if dil == 1:
            return ref[0].astype(F32)
        for r in range(dil):
            for c in range(d // LANES):
                buf[c, pl.ds(r, tm // dil, stride=dil), :] = (
                    ref[0, :, r * d + c * LANES:r * d + (c + 1) * LANES].astype(F32))
        return jnp.concatenate([buf[c] for c in range(d // LANES)], axis=1)

    outs = [token_major(o_refs[g], dils[g], scratch[2 * g]) for g in range(n)]
    lses = [token_major(l_refs[g], dils[g], scratch[2 * g + 1]) for g in range(n)]
    m = functools.reduce(jnp.maximum, lses)
    es = [jnp.exp(l - m) for l in lses]
    inv = 1.0 / functools.reduce(jnp.add, es)
    out_ref[0] = functools.reduce(jnp.add, [e * inv * o for e, o in zip(es, outs)]).astype(out_ref.dtype)


def _dil_merge(outs, lses, dils, tm=512):
    b = outs[0].shape[0]
    d = N_HEADS * HEAD_DIM
    s = outs[0].shape[1] * dils[0]
    specs = [pl.BlockSpec((1, tm // dil, dil * d), lambda bi, i: (bi, i, 0)) for dil in dils]
    return pl.pallas_call(
        functools.partial(_dil_merge_kernel, dils=tuple(dils)),
        grid=(b, s // tm),
        in_specs=specs + specs,
        out_specs=pl.BlockSpec((1, tm, d), lambda bi, i: (bi, i, 0)),
        out_shape=jax.ShapeDtypeStruct((b, s, d), BF16),
        scratch_shapes=[pltpu.VMEM((d // LANES, tm, LANES), F32) for _ in range(2 * len(dils))],
        compiler_params=_params("parallel", "parallel"),
        name="dilated_merge",
    )(*outs, *lses)


def _dil_mixer(x, scale, shift, w_in, rel_table):
    hd = N_HEADS * HEAD_DIM
    w = w_in.astype(BF16)
    outs, lses, dils = [], [], []
    for gi, (window, dil) in enumerate(DIL_GROUPS):
        view = _linear_in(x, scale, shift, w[:, gi * 3 * hd:(gi + 1) * 3 * hd], tn=3 * hd, dil=dil)
        o, lse = _dil_group(view, window, dil, rel_table)
        outs.append(o)
        lses.append(lse)
        dils.append(dil)
    return _dil_merge(outs, lses, dils)


def kernel(x, c, rel_table, mod_w, mod_b, ln_g, ln_b, gla_w_in, gla_w_a2, gla_b_a, gla_norm_g, gla_w_o,
           nsa_w_in, nsa_cmp_pe, nsa_cmp_w1, nsa_cmp_b1, nsa_cmp_w2, nsa_w_o, dil_w_in, dil_w_o,
           ffn_w_up, ffn_conv_w, ffn_conv_b, ffn_w_down):
    shift, scale, gate = _modulation(c, mod_w, mod_b)
    for i in range(DEPTH):
        kind, j = i % N_MIXERS, i // N_MIXERS
        m = 2 * i
        if kind == 0:
            a = _gla_mixer(x, scale[m], shift[m], gla_w_in[j], gla_w_a2[j], gla_b_a[j], gla_norm_g[j])
            w_o = gla_w_o[j]
        elif kind == 1:
            a = _nsa_mixer(x, scale[m], shift[m], nsa_w_in[j], nsa_cmp_pe[j], nsa_cmp_w1[j], nsa_cmp_b1[j],
                           nsa_cmp_w2[j], rel_table)
            w_o = nsa_w_o[j]
        else:
            a = _dil_mixer(x, scale[m], shift[m], dil_w_in[j], rel_table)
            w_o = dil_w_o[j]
        x = _out_norm(a, w_o.astype(BF16), x, gate[m], ln_g[i, 0], ln_b[i, 0])
        a = _ffn_up(x, scale[m + 1], shift[m + 1], ffn_w_up[i].astype(BF16), ffn_conv_w[i], ffn_conv_b[i])
        x = _out_norm(a, ffn_w_down[i].astype(BF16), x, gate[m + 1], ln_g[i, 1], ln_b[i, 1])
    return x
```
